```python
import math
import jax, jax.numpy as jnp
from jax import lax
import numpy as np


D_MODEL = 2048
BATCH = 1
SEQ = 16384
DEPTH = 2

N_MIXERS = 2
RMS_EPS = 1e-6

SSM_WIDTH = D_MODEL
SSM_GROUP = 16
SSM_GROUPS = SSM_WIDTH // SSM_GROUP
SSM_STATE = 64
SSM_CHUNK = 128
DT_MIN = 1e-3
DT_MAX = 1e-1

NSA_HEADS = 16
NSA_HEAD_DIM = 128
NSA_KV_GROUPS = 4
NSA_REP = NSA_HEADS // NSA_KV_GROUPS
NSA_KV_WIDTH = NSA_KV_GROUPS * NSA_HEAD_DIM
N_BRANCH = 3
NSA_IN_WIDTH = NSA_HEADS * NSA_HEAD_DIM + 2 * N_BRANCH * NSA_KV_WIDTH + N_BRANCH * NSA_HEADS
CMP_LEN = 32
CMP_STRIDE = 16
SEL_LEN = 64
SEL_TOPN = 16
WIN = 512
Q_BLOCK = 128

PEER_HEADS = 8
PEER_NKEYS = 128
PEER_EXPERTS = PEER_NKEYS ** 2
PEER_TOPK = 16
PEER_QDIM = 256
PEER_HALF = PEER_QDIM // 2
PEER_CHUNK = 128

kernel_name = 'hybrid_s5_nsa_peer_adaln'


def rmsnorm(x, g):
    xf = x.astype(jnp.float32)
    xf = xf * lax.rsqrt(jnp.mean(xf * xf, axis=-1, keepdims=True) + RMS_EPS)
    return xf.astype(x.dtype) * g


def modulate(h, shift, scale):
    return h * (1 + scale[:, None, :]) + shift[:, None, :]


def masked_softmax(s, mask, axis=-1):
    s = jnp.where(mask, s.astype(jnp.float32), -1e30)
    m = jnp.max(s, axis=axis, keepdims=True)
    p = jnp.exp(s - m) * mask
    return p / jnp.maximum(jnp.sum(p, axis=axis, keepdims=True), 1e-30)


def _lin_rec_combine(e1, e2):
    a1r, a1i, b1r, b1i = e1
    a2r, a2i, b2r, b2i = e2
    return (a1r * a2r - a1i * a2i,
            a1r * a2i + a1i * a2r,
            a2r * b1r - a2i * b1i + b2r,
            a2r * b1i + a2i * b1r + b2i)


def s5_scan(u, lam_re, lam_im, log_dt, b_re, b_im, c_re, c_im, d_skip):
    f32 = jnp.float32
    lam_re, lam_im = lam_re.astype(f32), lam_im.astype(f32)
    b_re, b_im, c_re, c_im = (t.astype(f32) for t in (b_re, b_im, c_re, c_im))
    d_skip = d_skip.astype(f32)
    dt = jnp.exp(log_dt.astype(f32))[:, None]
    mag = jnp.exp(lam_re * dt)
    ar, ai = mag * jnp.cos(lam_im * dt), mag * jnp.sin(lam_im * dt)
    den = lam_re * lam_re + lam_im * lam_im
    coef_r = ((ar - 1.0) * lam_re + ai * lam_im) / den
    coef_i = (ai * lam_re - (ar - 1.0) * lam_im) / den
    bb_r = coef_r[..., None] * b_re - coef_i[..., None] * b_im
    bb_i = coef_r[..., None] * b_im + coef_i[..., None] * b_re
    Bsz, L, G, P = u.shape
    n_chunks = L // SSM_CHUNK
    uc = u.reshape(Bsz, n_chunks, SSM_CHUNK, G, P).transpose(1, 0, 2, 3, 4)
    a_r = jnp.broadcast_to(ar, (Bsz, SSM_CHUNK, G, SSM_STATE))
    a_i = jnp.broadcast_to(ai, (Bsz, SSM_CHUNK, G, SSM_STATE))

    def step(carry, u_blk):
        hr, hi = carry
        bu_r = jnp.einsum('btgp,gnp->btgn', u_blk, bb_r)
        bu_i = jnp.einsum('btgp,gnp->btgn', u_blk, bb_i)
        cr, ci, sr, si = lax.associative_scan(_lin_rec_combine, (a_r, a_i, bu_r, bu_i), axis=1)
        sr = sr + cr * hr[:, None] - ci * hi[:, None]
        si = si + cr * hi[:, None] + ci * hr[:, None]
        y = jnp.einsum('gpn,btgn->btgp', c_re, sr) - jnp.einsum('gpn,btgn->btgp', c_im, si)
        y = y + d_skip * u_blk
        return (sr[:, -1], si[:, -1]), y

    init = (jnp.zeros((Bsz, G, SSM_STATE), f32), jnp.zeros((Bsz, G, SSM_STATE), f32))
    _, ys = lax.scan(step, init, uc)
    return ys.transpose(1, 0, 2, 3, 4).reshape(Bsz, L, G, P)


def s5_mixer(h, w_in, lam_re, lam_im, log_dt, b_re, b_im, c_re, c_im, d_skip, w_out):
    Bsz, L, _ = h.shape
    u = (h @ w_in).reshape(Bsz, L, SSM_GROUPS, SSM_GROUP)
    y = s5_scan(u.astype(jnp.float32), lam_re, lam_im, log_dt, b_re, b_im, c_re, c_im, d_skip)
    y = jax.nn.gelu(y.reshape(Bsz, L, SSM_WIDTH).astype(h.dtype))
    a, b = jnp.split(y @ w_out, 2, axis=-1)
    return a * jax.nn.sigmoid(b)


def compress_blocks(kv, cmp_idx, pos, w1, w2):
    blk = kv[:, cmp_idx] + pos[None, None, :, None, :]
    hid = jax.nn.gelu(jnp.einsum('bnlgd,ldh->bngh', blk, w1))
    return jnp.einsum('bngh,he->bnge', hid, w2)


def nsa_mixer(h, w_in, k_pos, k_w1, k_w2, v_pos, v_w1, v_w2, w_out):
    Bsz, L, _ = h.shape
    G, R, DH = NSA_KV_GROUPS, NSA_REP, NSA_HEAD_DIM
    proj = h @ w_in
    sizes = [NSA_HEADS * DH] + [NSA_KV_WIDTH] * (2 * N_BRANCH) + [N_BRANCH * NSA_HEADS]
    q, kc, vc, ks, vs, kw, vw, gl = jnp.split(proj, np.cumsum(sizes)[:-1].tolist(), axis=-1)
    q = q.reshape(Bsz, L, G, R, DH)
    kc, vc, ks, vs, kw, vw = (t.reshape(Bsz, L, G, DH) for t in (kc, vc, ks, vs, kw, vw))
    gates = jax.nn.sigmoid(gl).reshape(Bsz, L, G, R, N_BRANCH)
    scale = DH ** -0.5

    n_cmp = (L - CMP_LEN) // CMP_STRIDE + 1
    cmp_idx = np.arange(n_cmp)[:, None] * CMP_STRIDE + np.arange(CMP_LEN)[None, :]
    k_cmp = compress_blocks(kc, cmp_idx, k_pos, k_w1, k_w2)
    v_cmp = compress_blocks(vc, cmp_idx, v_pos, v_w1, v_w2)
    cmp_end = jnp.asarray(np.arange(n_cmp) * CMP_STRIDE + CMP_LEN - 1)

    n_sel = L // SEL_LEN
    ratio, span = SEL_LEN // CMP_STRIDE, CMP_LEN // CMP_STRIDE
    agg = np.zeros((n_cmp, n_sel), np.float32)
    jj = np.arange(n_sel)
    for m in range(ratio):
        for n in range(span):
            ii = ratio * jj + m - n
            ok = (ii >= 0) & (ii < n_cmp)
            agg[ii[ok], jj[ok]] += 1.0
    agg = jnp.asarray(agg)
    top_n = min(SEL_TOPN, n_sel)

    ks_blk = ks.reshape(Bsz, n_sel, SEL_LEN, G, DH).transpose(0, 3, 1, 2, 4)
    vs_blk = vs.reshape(Bsz, n_sel, SEL_LEN, G, DH).transpose(0, 3, 1, 2, 4)
    kw_pad = jnp.pad(kw, ((0, 0), (WIN, 0), (0, 0), (0, 0)))
    vw_pad = jnp.pad(vw, ((0, 0), (WIN, 0), (0, 0), (0, 0)))
    b_ix = jnp.arange(Bsz)[:, None, None, None]
    g_ix = jnp.arange(G)[None, None, :, None]
    blk_ids = jnp.arange(n_sel)
    in_blk = jnp.arange(SEL_LEN)
    win_off = jnp.arange(Q_BLOCK + WIN)

    def query_block(bi):
        t0 = bi * Q_BLOCK
        tpos = t0 + jnp.arange(Q_BLOCK)
        qb = lax.dynamic_slice_in_dim(q, t0, Q_BLOCK, axis=1)
        gb = lax.dynamic_slice_in_dim(gates, t0, Q_BLOCK, axis=1)
        s_c = jnp.einsum('btgrd,bigd->bgrti', qb, k_cmp) * scale
        p_c = masked_softmax(s_c, cmp_end[None, :] <= tpos[:, None])
        o_c = jnp.einsum('bgrti,bigd->btgrd', p_c.astype(v_cmp.dtype), v_cmp)
        imp = jnp.einsum('bgrti,ij->btgj', p_c, agg)
        cur = tpos // SEL_LEN
        forced = (blk_ids[None] == 0) | (blk_ids[None] == cur[:, None]) | (blk_ids[None] == cur[:, None] - 1)
        valid = blk_ids[None] * SEL_LEN <= tpos[:, None]
        score = jnp.where(forced[None, :, None, :], 1e9, jnp.where(valid[None, :, None, :], imp, -1e9))
        _, sel = lax.top_k(score, top_n)
        k_g = ks_blk[b_ix, g_ix, sel]
        v_g = vs_blk[b_ix, g_ix, sel]
        s_s = jnp.einsum('btgrd,btgnsd->btgrns', qb, k_g) * scale
        tok = sel[..., None] * SEL_LEN + in_blk
        mask_s = (tok <= tpos[None, :, None, None, None])[:, :, :, None]
        p_s = masked_softmax(s_s, mask_s, axis=(-2, -1))
        o_s = jnp.einsum('btgrns,btgnsd->btgrd', p_s.astype(v_g.dtype), v_g)
        kwb = lax.dynamic_slice_in_dim(kw_pad, t0, Q_BLOCK + WIN, axis=1)
        vwb = lax.dynamic_slice_in_dim(vw_pad, t0, Q_BLOCK + WIN, axis=1)
        spos = t0 - WIN + win_off
        diff = tpos[:, None] - spos[None, :]
        mask_w = (diff >= 0) & (diff < WIN) & (spos[None, :] >= 0)
        s_w = jnp.einsum('btgrd,bsgd->bgrts', qb, kwb) * scale
        p_w = masked_softmax(s_w, mask_w)
        o_w = jnp.einsum('bgrts,bsgd->btgrd', p_w.astype(vwb.dtype), vwb)
        o = gb[..., 0:1] * o_c + gb[..., 1:2] * o_s + gb[..., 2:3] * o_w
        return o.reshape(Bsz, Q_BLOCK, NSA_HEADS * DH)

    o = lax.map(query_block, jnp.arange(L // Q_BLOCK))
    o = o.transpose(1, 0, 2, 3).reshape(Bsz, L, NSA_HEADS * DH)
    return o @ w_out


def peer_ffn(h, w_q, sub_keys, u_tab, v_tab):
    Bsz, L, D = h.shape
    T = Bsz * L
    xt = h.reshape(T, D)
    q = (xt @ w_q).reshape(T, PEER_HEADS, 2, PEER_HALF)
    s = jnp.einsum('thcd,hckd->thck', q, sub_keys).astype(jnp.float32)
    s1, i1 = lax.top_k(s[:, :, 0], PEER_TOPK)
    s2, i2 = lax.top_k(s[:, :, 1], PEER_TOPK)
    cand_s = (s1[..., :, None] + s2[..., None, :]).reshape(T, PEER_HEADS, PEER_TOPK * PEER_TOPK)
    cand_e = (i1[..., :, None] * PEER_NKEYS + i2[..., None, :]).reshape(T, PEER_HEADS, PEER_TOPK * PEER_TOPK)
    top_s, top_pos = lax.top_k(cand_s, PEER_TOPK)
    experts = jnp.take_along_axis(cand_e, top_pos, axis=-1)
    gate = jax.nn.softmax(top_s, axis=-1).astype(h.dtype)
    E = PEER_HEADS * PEER_TOPK
    n_chunks = T // PEER_CHUNK

    def expert_chunk(args):
        xc, ec, gc = args
        u = u_tab[ec]
        act = jax.nn.gelu(jnp.einsum('cd,ced->ce', xc, u))
        return jnp.einsum('ce,ced->cd', gc * act, v_tab[ec])

    y = lax.map(expert_chunk, (xt.reshape(n_chunks, PEER_CHUNK, D),
                               experts.reshape(n_chunks, PEER_CHUNK, E),
                               gate.reshape(n_chunks, PEER_CHUNK, E)))
    return y.reshape(Bsz, L, D)


def setup_inputs(seed: int = 0) -> dict:
    key = jax.random.key(seed)
    ks = jax.random.split(key, 32)
    f32 = jnp.float32

    def nrm(k, shape, s):
        return s * jax.random.normal(k, shape, f32)

    n_ssm = (DEPTH + 1) // 2
    n_nsa = DEPTH // 2
    D, G, N, P, DH = D_MODEL, SSM_GROUPS, SSM_STATE, SSM_GROUP, NSA_HEAD_DIM
    return {
        'x': nrm(ks[0], (BATCH, SEQ, D), 1.0),
        'c': nrm(ks[1], (BATCH, D), 1.0),
        'ada_w': nrm(ks[2], (DEPTH, D, 6 * D), 0.5 * D ** -0.5),
        'ada_b': nrm(ks[3], (DEPTH, 6 * D), 0.02),
        'norm_mix': 1.0 + nrm(ks[4], (DEPTH, D), 0.02),
        'norm_ffn': 1.0 + nrm(ks[5], (DEPTH, D), 0.02),
        'norm_final': 1.0 + nrm(ks[6], (D,), 0.02),
        'ssm_w_in': nrm(ks[7], (n_ssm, D, SSM_WIDTH), D ** -0.5),
        'ssm_lambda_re': -0.5 + nrm(ks[8], (n_ssm, G, N), 0.01),
        'ssm_lambda_im': math.pi * jnp.arange(N, dtype=f32) + nrm(ks[9], (n_ssm, G, N), 0.01),
        'ssm_log_dt': jax.random.uniform(ks[10], (n_ssm, G), f32, math.log(DT_MIN), math.log(DT_MAX)),
        'ssm_b_re': nrm(ks[11], (n_ssm, G, N, P), (2 * P) ** -0.5),
        'ssm_b_im': nrm(ks[12], (n_ssm, G, N, P), (2 * P) ** -0.5),
        'ssm_c_re': nrm(ks[13], (n_ssm, G, P, N), 2 ** -0.5),
        'ssm_c_im': nrm(ks[14], (n_ssm, G, P, N), 2 ** -0.5),
        'ssm_d': nrm(ks[15], (n_ssm, G, P), 0.5),
        'ssm_w_out': nrm(ks[16], (n_ssm, SSM_WIDTH, 2 * D), SSM_WIDTH ** -0.5),
        'nsa_w_in': nrm(ks[17], (n_nsa, D, NSA_IN_WIDTH), D ** -0.5),
        'nsa_cmp_k_pos': nrm(ks[18], (n_nsa, CMP_LEN, DH), 0.1),
        'nsa_cmp_k_w1': nrm(ks[19], (n_nsa, CMP_LEN, DH, DH), (CMP_LEN * DH) ** -0.5),
        'nsa_cmp_k_w2': nrm(ks[20], (n_nsa, DH, DH), DH ** -0.5),
        'nsa_cmp_v_pos': nrm(ks[21], (n_nsa, CMP_LEN, DH), 0.1),
        'nsa_cmp_v_w1': nrm(ks[22], (n_nsa, CMP_LEN, DH, DH), (CMP_LEN * DH) ** -0.5),
        'nsa_cmp_v_w2': nrm(ks[23], (n_nsa, DH, DH), DH ** -0.5),
        'nsa_w_out': nrm(ks[24], (n_nsa, NSA_HEADS * DH, D), (NSA_HEADS * DH) ** -0.5),
        'peer_w_q': nrm(ks[25], (DEPTH, D, PEER_HEADS * PEER_QDIM), D ** -0.5),
        'peer_sub_keys': nrm(ks[26], (DEPTH, PEER_HEADS, 2, PEER_NKEYS, PEER_HALF), PEER_HALF ** -0.5),
        'peer_u': nrm(ks[27], (DEPTH, PEER_EXPERTS, D), D ** -0.5),
        'peer_v': nrm(ks[28], (DEPTH, PEER_EXPERTS, D), 1.0),
    }


def reference(x, c, ada_w, ada_b, norm_mix, norm_ffn, norm_final,
              ssm_w_in, ssm_lambda_re, ssm_lambda_im, ssm_log_dt, ssm_b_re, ssm_b_im,
              ssm_c_re, ssm_c_im, ssm_d, ssm_w_out,
              nsa_w_in, nsa_cmp_k_pos, nsa_cmp_k_w1, nsa_cmp_k_w2,
              nsa_cmp_v_pos, nsa_cmp_v_w1, nsa_cmp_v_w2, nsa_w_out,
              peer_w_q, peer_sub_keys, peer_u, peer_v):
    cond = jax.nn.silu(c)
    for i in range(DEPTH):
        mod = cond @ ada_w[i] + ada_b[i]
        sh1, sc1, g1, sh2, sc2, g2 = jnp.split(mod, 6, axis=-1)
        h = modulate(rmsnorm(x, norm_mix[i]), sh1, sc1)
        j = i // N_MIXERS
        if i % N_MIXERS == 0:
            y = s5_mixer(h, ssm_w_in[j], ssm_lambda_re[j], ssm_lambda_im[j], ssm_log_dt[j],
                         ssm_b_re[j], ssm_b_im[j], ssm_c_re[j], ssm_c_im[j], ssm_d[j], ssm_w_out[j])
        else:
            y = nsa_mixer(h, nsa_w_in[j], nsa_cmp_k_pos[j], nsa_cmp_k_w1[j], nsa_cmp_k_w2[j],
                          nsa_cmp_v_pos[j], nsa_cmp_v_w1[j], nsa_cmp_v_w2[j], nsa_w_out[j])
        x = x + g1[:, None, :] * y
        h = modulate(rmsnorm(x, norm_ffn[i]), sh2, sc2)
        x = x + g2[:, None, :] * peer_ffn(h, peer_w_q[i], peer_sub_keys[i], peer_u[i], peer_v[i])
    return rmsnorm(x, norm_final)
```

```python
import functools
import math

import numpy as np
import jax
import jax.numpy as jnp
from jax import lax
from jax.experimental import pallas as pl
from jax.experimental.pallas import tpu as pltpu

F32 = jnp.float32
BF16 = jnp.bfloat16

RMS_EPS = 1e-6

SSM_GROUP = 16
SSM_STATE = 64
SSM_BLOCK_GROUPS = 8
S5_SUBSEQ = 8
S5_CHUNK = 128
S5_LANE_BLOCK = 1024

NSA_HEADS = 16
NSA_HEAD_DIM = 128
NSA_KV_GROUPS = 4
NSA_REP = NSA_HEADS // NSA_KV_GROUPS
N_BRANCH = 3
CMP_LEN = 32
CMP_STRIDE = 16
SEL_LEN = 64
SEL_TOPN = 16
WIN = 512
Q_BLOCK = 128
SEL_KV_TILE = 512
MASK_PENALTY = -(2.0 ** 30)
ONEHOT_PERIOD = 128

PEER_HEADS = 8
PEER_NKEYS = 128
PEER_TOPK = 16
PEER_HALF = 128
RANK_NONE = 255.0

VMEM_LIMIT = 56 * 1024 * 1024
NT_DIMS = (((1,), (1,)), ((), ()))


def _cparams(n_axes):
    return pltpu.CompilerParams(
        dimension_semantics=("arbitrary",) * n_axes, vmem_limit_bytes=VMEM_LIMIT)


def _norm_mod(x, g, sh, sc):
    ms = jnp.mean(x * x, axis=-1, keepdims=True)
    xn = x * lax.rsqrt(ms + RMS_EPS)
    return (xn * g) * (1.0 + sc) + sh


def _adaln_kernel(c_ref, w_ref, b_ref, o_ref):
    c = c_ref[...]
    cond = c * jax.nn.sigmoid(c)
    o_ref[0] = jnp.dot(cond.astype(BF16), w_ref[0].astype(BF16),
                       preferred_element_type=F32) + b_ref[0]


def _adaln(c, ada_w, ada_b):
    depth, d, n = ada_w.shape
    tn = 1024
    c8 = jnp.broadcast_to(c, (8, d))
    out = pl.pallas_call(
        _adaln_kernel,
        grid=(depth, n // tn),
        in_specs=[pl.BlockSpec((8, d), lambda i, j: (0, 0)),
                  pl.BlockSpec((1, d, tn), lambda i, j: (i, 0, j)),
                  pl.BlockSpec((1, 1, tn), lambda i, j: (i, 0, j))],
        out_specs=pl.BlockSpec((1, 8, tn), lambda i, j: (i, 0, j)),
        out_shape=jax.ShapeDtypeStruct((depth, 8, n), F32),
        compiler_params=_cparams(2),
    )(c8, ada_w, ada_b.reshape(depth, 1, n))
    return out[:, 0:1, :]


def _nmm_kernel(x_ref, g_ref, sh_ref, sc_ref, w_ref, cs_ref, o_ref, h_scr, *, act):
    @pl.when(pl.program_id(1) == 0)
    def _():
        h_scr[...] = _norm_mod(x_ref[...], g_ref[...], sh_ref[...], sc_ref[...]).astype(BF16)

    acc = jnp.dot(h_scr[...], w_ref[...], preferred_element_type=F32)
    if act == "sigmoid":
        acc = jax.nn.sigmoid(acc)
    else:
        acc = acc * cs_ref[...]
    o_ref[...] = acc.astype(o_ref.dtype)


def _norm_mod_matmul(x, g, sh, sc, w, colscale, out_dtype, act="scale", tm=512, tn=1024):
    t, d = x.shape
    n = w.shape[1]
    tn = min(tn, n)
    row = lambda i, j: (0, 0)
    return pl.pallas_call(
        functools.partial(_nmm_kernel, act=act),
        grid=(t // tm, n // tn),
        in_specs=[pl.BlockSpec((tm, d), lambda i, j: (i, 0)),
                  pl.BlockSpec((1, d), row), pl.BlockSpec((1, d), row), pl.BlockSpec((1, d), row),
                  pl.BlockSpec((d, tn), lambda i, j: (0, j)),
                  pl.BlockSpec((1, tn), lambda i, j: (0, j))],
        out_specs=pl.BlockSpec((tm, tn), lambda i, j: (i, j)),
        out_shape=jax.ShapeDtypeStruct((t, n), out_dtype),
        scratch_shapes=[pltpu.VMEM((tm, d), BF16)],
        compiler_params=_cparams(2),
    )(x, g, sh, sc, w, colscale)


def _mm_res_kernel(a_ref, w_ref, x_ref, g_ref, o_ref):
    y = jnp.dot(a_ref[...], w_ref[...], preferred_element_type=F32)
    o_ref[...] = x_ref[...] + g_ref[...] * y


def _mm_glu_res_kernel(a_ref, wa_ref, wb_ref, x_ref, g_ref, o_ref):
    a = a_ref[...]
    ya = jnp.dot(a, wa_ref[...], preferred_element_type=F32)
    yb = jnp.dot(a, wb_ref[...], preferred_element_type=F32)
    o_ref[...] = x_ref[...] + g_ref[...] * (ya * jax.nn.sigmoid(yb))


def _matmul_residual(a, w, x, gate, glu, tm=512, tn=1024):
    t, k = a.shape
    n = x.shape[1]
    nb = n // tn
    a_spec = pl.BlockSpec((tm, k), lambda i, j: (i, 0))
    w_spec = pl.BlockSpec((k, tn), lambda i, j: (0, j))
    tail = [pl.BlockSpec((tm, tn), lambda i, j: (i, j)), pl.BlockSpec((1, tn), lambda i, j: (0, j))]
    if glu:
        kern = _mm_glu_res_kernel
        in_specs = [a_spec, w_spec, pl.BlockSpec((k, tn), lambda i, j: (0, j + nb))] + tail
        args = (a, w, w, x, gate)
    else:
        kern = _mm_res_kernel
        in_specs = [a_spec, w_spec] + tail
        args = (a, w, x, gate)
    return pl.pallas_call(
        kern,
        grid=(t // tm, nb),
        in_specs=in_specs,
        out_specs=pl.BlockSpec((tm, tn), lambda i, j: (i, j)),
        out_shape=jax.ShapeDtypeStruct((t, n), F32),
        compiler_params=_cparams(2),
    )(*args)


def _s5_param_kernel(lre_ref, lim_ref, ldt_ref, bre_ref, bim_ref, ar_ref, ai_ref, bbr_ref, bbi_ref):
    lre = lre_ref[...]
    lim = lim_ref[...]
    dt = jnp.exp(ldt_ref[...])
    mag = jnp.exp(lre * dt)
    ar = mag * jnp.cos(lim * dt)
    ai = mag * jnp.sin(lim * dt)
    den = lre * lre + lim * lim
    cr = ((ar - 1.0) * lre + ai * lim) / den
    ci = (ai * lre - (ar - 1.0) * lim) / den
    ar_ref[...] = ar
    ai_ref[...] = ai
    bbr_ref[...] = cr * bre_ref[...] - ci * bim_ref[...]
    bbi_ref[...] = cr * bim_ref[...] + ci * bre_ref[...]


def _s5_params(lam_re, lam_im, log_dt, b_re, b_im):
    g, n, p = b_re.shape
    rep = lambda a: jnp.repeat(a, p, axis=1)
    shp = jax.ShapeDtypeStruct((g, n * p), F32)
    ar, ai, bbr, bbi = pl.pallas_call(
        _s5_param_kernel, out_shape=(shp, shp, shp, shp),
    )(rep(lam_re), rep(lam_im), log_dt.reshape(g, 1), b_re.reshape(g, n * p), b_im.reshape(g, n * p))
    return ar[:, ::p], ai[:, ::p], bbr.reshape(g, n, p), bbi.reshape(g, n, p)


def _s5_scan_kernel(u_ref, a_ref, bbr_ref, bbi_ref, cr_ref, cin_ref, d_ref, o_ref,
                    bur, bui, hin_r, hin_i, car_r, car_i, apr, api):
    t, width = u_ref.shape
    sub = t // S5_SUBSEQ
    lanes = bur.shape[1]
    nblk = bbr_ref.shape[0]
    kin = width // nblk
    kst = lanes // nblk

    @pl.when(pl.program_id(0) == 0)
    def _init():
        car_r[...] = jnp.zeros_like(car_r)
        car_i[...] = jnp.zeros_like(car_i)
        ar = a_ref[0:1, :]
        ai = a_ref[1:2, :]
        pr, pi = ar, ai
        apr[0:1, :] = pr
        api[0:1, :] = pi
        for i in range(1, sub):
            pr, pi = pr * ar - pi * ai, pr * ai + pi * ar
            apr[i:i + 1, :] = pr
            api[i:i + 1, :] = pi

    for k in range(nblk):
        uk = u_ref[:, kin * k:kin * (k + 1)].astype(BF16)
        bur[:, kst * k:kst * (k + 1)] = jnp.dot(uk, bbr_ref[k], preferred_element_type=F32)
        bui[:, kst * k:kst * (k + 1)] = jnp.dot(uk, bbi_ref[k], preferred_element_type=F32)

    lb = S5_LANE_BLOCK
    for b in range(lanes // lb):
        sl = slice(b * lb, (b + 1) * lb)
        ar = jnp.broadcast_to(a_ref[0:1, sl], (S5_SUBSEQ, lb))
        ai = jnp.broadcast_to(a_ref[1:2, sl], (S5_SUBSEQ, lb))
        sr = jnp.zeros((S5_SUBSEQ, lb), F32)
        si = jnp.zeros((S5_SUBSEQ, lb), F32)
        for i in range(sub):
            rows = slice(S5_SUBSEQ * i, S5_SUBSEQ * (i + 1))
            sr, si = (ar * sr - ai * si + bur[rows, sl], ar * si + ai * sr + bui[rows, sl])
            bur[rows, sl] = sr
            bui[rows, sl] = si
        asr = apr[sub - 1:sub, sl]
        asi = api[sub - 1:sub, sl]
        hr = car_r[0:1, sl]
        hi = car_i[0:1, sl]
        for j in range(S5_SUBSEQ):
            hin_r[j:j + 1, sl] = hr
            hin_i[j:j + 1, sl] = hi
            er = sr[j:j + 1, :]
            ei = si[j:j + 1, :]
            hr, hi = er + asr * hr - asi * hi, ei + asr * hi + asi * hr
        car_r[0:1, sl] = hr
        car_i[0:1, sl] = hi
        hinr = hin_r[:, sl]
        hini = hin_i[:, sl]
        for i in range(sub):
            rows = slice(S5_SUBSEQ * i, S5_SUBSEQ * (i + 1))
            pr = jnp.broadcast_to(apr[i:i + 1, sl], (S5_SUBSEQ, lb))
            pi = jnp.broadcast_to(api[i:i + 1, sl], (S5_SUBSEQ, lb))
            bur[rows, sl] = bur[rows, sl] + pr * hinr - pi * hini
            bui[rows, sl] = bui[rows, sl] + pr * hini + pi * hinr

    for k in range(nblk):
        sr = bur[:, kst * k:kst * (k + 1)].astype(BF16)
        si = bui[:, kst * k:kst * (k + 1)].astype(BF16)
        y = (jnp.dot(sr, cr_ref[k], preferred_element_type=F32)
             + jnp.dot(si, cin_ref[k], preferred_element_type=F32))
        cols = slice(kin * k, kin * (k + 1))
        y = y + d_ref[:, cols] * u_ref[:, cols]
        o_ref[:, cols] = jax.nn.gelu(y).astype(o_ref.dtype)


def _s5_scan(u_p, ar, ai, bb_r, bb_i, c_re, c_im, d_skip):
    t_all, width = u_p.shape
    g, n, p = bb_r.shape
    bg = SSM_BLOCK_GROUPS
    nblk = g // bg
    eye = jnp.eye(bg, dtype=F32)
    blk_b = lambda bb: jnp.einsum('kgnp,gh->kgphn', bb.reshape(nblk, bg, n, p), eye
                                  ).reshape(nblk, bg * p, bg * n).astype(BF16)
    blk_c = lambda cc: jnp.einsum('kgpn,gh->kgnhp', cc.reshape(nblk, bg, p, n), eye
                                  ).reshape(nblk, bg * n, bg * p).astype(BF16)
    lanes = g * n
    a8 = jnp.zeros((8, lanes), F32).at[0].set(ar.reshape(lanes)).at[1].set(ai.reshape(lanes))
    t = S5_CHUNK
    sub = t // S5_SUBSEQ
    full3 = lambda c: (0, 0, 0)
    wspec_b = pl.BlockSpec((nblk, bg * p, bg * n), full3)
    wspec_c = pl.BlockSpec((nblk, bg * n, bg * p), full3)
    return pl.pallas_call(
        _s5_scan_kernel,
        grid=(t_all // t,),
        in_specs=[pl.BlockSpec((t, width), lambda c: (c, 0)),
                  pl.BlockSpec((8, lanes), lambda c: (0, 0)),
                  wspec_b, wspec_b, wspec_c, wspec_c,
                  pl.BlockSpec((1, width), lambda c: (0, 0))],
        out_specs=pl.BlockSpec((t, width), lambda c: (c, 0)),
        out_shape=jax.ShapeDtypeStruct((t_all, width), BF16),
        scratch_shapes=[pltpu.VMEM((t, lanes), F32), pltpu.VMEM((t, lanes), F32),
                        pltpu.VMEM((8, lanes), F32), pltpu.VMEM((8, lanes), F32),
                        pltpu.VMEM((8, lanes), F32), pltpu.VMEM((8, lanes), F32),
                        pltpu.VMEM((sub, lanes), F32), pltpu.VMEM((sub, lanes), F32)],
        compiler_params=_cparams(1),
    )(u_p, a8, blk_b(bb_r), blk_b(bb_i), blk_c(c_re), blk_c(-c_im), d_skip.reshape(1, width))


def _s5_rows_to_subseq(x, inverse=False):
    t_all, d = x.shape
    sub = S5_CHUNK // S5_SUBSEQ
    shape = (t_all // S5_CHUNK, sub, S5_SUBSEQ, d) if inverse else (t_all // S5_CHUNK, S5_SUBSEQ, sub, d)
    return x.reshape(shape).transpose(0, 2, 1, 3).reshape(t_all, d)


def _s5_layer(x, g, sh, sc, gate, w_in, lam_re, lam_im, log_dt, b_re, b_im, c_re, c_im, d_skip, w_out):
    d = x.shape[1]
    x_p = _s5_rows_to_subseq(x)
    ones = jnp.ones((1, w_in.shape[1]), F32)
    u_p = _norm_mod_matmul(x_p, g, sh, sc, w_in.astype(BF16), ones, F32)
    ar, ai, bb_r, bb_i = _s5_params(lam_re, lam_im, log_dt, b_re, b_im)
    gy_p = _s5_scan(u_p, ar, ai, bb_r, bb_i, c_re, c_im, d_skip)
    xn_p = _matmul_residual(gy_p, w_out.astype(BF16), x_p, gate, glu=True)
    return _s5_rows_to_subseq(xn_p, inverse=True)


def _cmp_kernel(x_ref, w1a_ref, w1b_ref, pos_ref, w1_ref, w2_ref, o_ref, acc_a, acc_b):
    l = pl.program_id(2)

    @pl.when(l == 0)
    def _():
        acc_a[...] = jnp.zeros_like(acc_a)
        acc_b[...] = jnp.zeros_like(acc_b)

    x = x_ref[...]
    acc_a[...] += jnp.dot(x, w1a_ref[0, 0], preferred_element_type=F32)
    acc_b[...] += jnp.dot(x, w1b_ref[0, 0], preferred_element_type=F32)

    @pl.when(l == pl.num_programs(2) - 1)
    def _():
        m = acc_a.shape[0]
        posc = jnp.zeros((8, NSA_HEAD_DIM), F32)
        for ll in range(CMP_LEN):
            prow = jnp.broadcast_to(pos_ref[0, ll:ll + 1, :], (8, NSA_HEAD_DIM)).astype(BF16)
            posc = posc + jnp.dot(prow, w1_ref[0, ll], preferred_element_type=F32)
        pre = acc_a[...] + pltpu.roll(acc_b[...], m - 1, 0) + posc[0:1, :]
        hid = jax.nn.gelu(pre)
        o_ref[0, 0] = jnp.dot(hid.astype(BF16), w2_ref[0], preferred_element_type=F32).astype(o_ref.dtype)


def _compress(qkv, kv_col0, pos, w1, w2):
    l_all, c = qkv.shape
    half = CMP_LEN // 2
    m = l_all // half
    cb = c // NSA_HEAD_DIM
    x2 = qkv.reshape(m, half * c)
    g = NSA_KV_GROUPS
    dh = NSA_HEAD_DIM
    return pl.pallas_call(
        _cmp_kernel,
        grid=(2, g, half),
        in_specs=[pl.BlockSpec((m, dh), lambda s, gg, l: (0, l * cb + kv_col0 + g * s + gg)),
                  pl.BlockSpec((1, 1, dh, dh), lambda s, gg, l: (s, l, 0, 0)),
                  pl.BlockSpec((1, 1, dh, dh), lambda s, gg, l: (s, l + half, 0, 0)),
                  pl.BlockSpec((1, CMP_LEN, dh), lambda s, gg, l: (s, 0, 0)),
                  pl.BlockSpec((1, CMP_LEN, dh, dh), lambda s, gg, l: (s, 0, 0, 0)),
                  pl.BlockSpec((1, dh, dh), lambda s, gg, l: (s, 0, 0))],
        out_specs=pl.BlockSpec((1, 1, m, dh), lambda s, gg, l: (s, gg, 0, 0)),
        out_shape=jax.ShapeDtypeStruct((2, g, m, dh), BF16),
        scratch_shapes=[pltpu.VMEM((m, dh), F32), pltpu.VMEM((m, dh), F32)],
        compiler_params=_cparams(3),
    )(x2, w1, w1, pos, w1, w2)


def _softmax_rows(s, mask):
    s = jnp.where(mask, s, -1e30)
    m = jnp.max(s, axis=-1, keepdims=True)
    p = jnp.exp(s - m) * mask.astype(F32)
    return p / jnp.maximum(jnp.sum(p, axis=-1, keepdims=True), 1e-30)


def _nsa_attn_kernel(q_ref, gt_ref, kc_ref, vc_ref, agg_ref, ks_ref, vs_ref, kw_ref, vw_ref, oh_ref,
                     o_ref, pen_scr):
    b = pl.program_id(1)
    qb = q_ref.shape[0]
    dh = NSA_HEAD_DIM
    rep = NSA_REP
    rows = rep * qb
    n_cmp = kc_ref.shape[2]
    n_sel = agg_ref.shape[1]
    t0 = b * qb

    q_blk = q_ref[...]
    qs = jnp.concatenate([q_blk[:, r * dh:(r + 1) * dh] for r in range(rep)], axis=0)
    tpos = t0 + lax.broadcasted_iota(jnp.int32, (rows, 1), 0) % qb

    kc = kc_ref[0, 0]
    s_c = lax.dot_general(qs, kc, NT_DIMS, preferred_element_type=F32)
    cmp_end = lax.broadcasted_iota(jnp.int32, (1, n_cmp), 1) * CMP_STRIDE + (CMP_LEN - 1)
    p_c = _softmax_rows(s_c, cmp_end <= tpos)
    o_c = jnp.dot(p_c.astype(BF16), vc_ref[0, 0], preferred_element_type=F32)

    psum = p_c[0:qb]
    for r in range(1, rep):
        psum = psum + p_c[r * qb:(r + 1) * qb]
    p_hi = psum.astype(BF16)
    p_lo = (psum - p_hi.astype(F32)).astype(BF16)
    agg = agg_ref[...]
    imp = (jnp.dot(p_hi, agg, preferred_element_type=F32)
           + jnp.dot(p_lo, agg, preferred_element_type=F32))
    tq = tpos[0:qb]
    blk = lax.broadcasted_iota(jnp.int32, (qb, n_sel), 1)
    cur = tq // SEL_LEN
    forced = (blk == 0) | (blk == cur) | (blk == cur - 1)
    valid = blk * SEL_LEN <= tq
    score = jnp.where(forced, 1e9, jnp.where(valid, imp, -1e9))
    chosen = jnp.zeros((qb, n_sel), jnp.bool_)
    for _ in range(min(SEL_TOPN, n_sel)):
        mx = jnp.max(score, axis=-1, keepdims=True)
        idx = jnp.min(jnp.where(score == mx, blk, n_sel), axis=-1, keepdims=True)
        hit = blk == idx
        chosen = chosen | hit
        score = jnp.where(hit, -jnp.inf, score)
    pen = jnp.where(chosen, 0.0, MASK_PENALTY).astype(BF16)
    n_half = pen_scr.shape[0]
    for hh in range(n_half):
        if n_sel >= ONEHOT_PERIOD:
            ph = pen[:, hh * ONEHOT_PERIOD:(hh + 1) * ONEHOT_PERIOD]
        else:
            ph = jnp.concatenate(
                [pen, jnp.full((qb, ONEHOT_PERIOD - n_sel), MASK_PENALTY, BF16)], axis=1)
        pen_scr[hh] = jnp.concatenate([ph] * rep, axis=0)

    kt = SEL_KV_TILE
    period_keys = ONEHOT_PERIOD * SEL_LEN

    def sel_step(i, carry, causal):
        m_run, l_run, acc = carry
        k0 = pl.multiple_of(i * kt, kt)
        half = k0 // period_keys
        e0 = pl.multiple_of(k0 % period_keys, kt)
        k_aug = jnp.concatenate([ks_ref[pl.ds(k0, kt), :], oh_ref[pl.ds(e0, kt), :]], axis=1)
        q_aug = jnp.concatenate([qs, pen_scr[half]], axis=1)
        s = lax.dot_general(q_aug, k_aug, NT_DIMS, preferred_element_type=F32)
        if causal:
            kpos = k0 + lax.broadcasted_iota(jnp.int32, (1, kt), 1)
            s = jnp.where(kpos <= tpos, s, MASK_PENALTY)
        m_new = jnp.maximum(m_run, jnp.max(s, axis=-1, keepdims=True))
        alpha = jnp.exp(m_run - m_new)
        p = jnp.exp(s - m_new)
        l_new = alpha * l_run + jnp.sum(p, axis=-1, keepdims=True)
        acc = alpha * acc + jnp.dot(p.astype(BF16), vs_ref[pl.ds(k0, kt), :],
                                    preferred_element_type=F32)
        return m_new, l_new, acc

    n_full = t0 // kt
    init = (jnp.full((rows, 1), -1e30, F32), jnp.zeros((rows, 1), F32), jnp.zeros((rows, dh), F32))
    carry = lax.fori_loop(0, n_full, lambda i, c: sel_step(i, c, False), init)
    _, l_s, acc_s = sel_step(n_full, carry, True)
    o_s = acc_s / l_s

    wlen = WIN + qb
    w0 = pl.multiple_of(jnp.maximum(t0 - WIN, 0), qb)
    s_w = lax.dot_general(qs, kw_ref[pl.ds(w0, wlen), :], NT_DIMS, preferred_element_type=F32)
    diff = tpos - (w0 + lax.broadcasted_iota(jnp.int32, (1, wlen), 1))
    p_w = _softmax_rows(s_w, (diff >= 0) & (diff < WIN))
    o_w = jnp.dot(p_w.astype(BF16), vw_ref[pl.ds(w0, wlen), :], preferred_element_type=F32)

    gt = gt_ref[...]
    for r in range(rep):
        rs = slice(r * qb, (r + 1) * qb)
        o = (gt[:, r:r + 1] * o_c[rs]
             + gt[:, rep + r:rep + r + 1] * o_s[rs]
             + gt[:, 2 * rep + r:2 * rep + r + 1] * o_w[rs])
        o_ref[:, r * dh:(r + 1) * dh] = o.astype(o_ref.dtype)


def _nsa_agg(n_cmp_pad, n_cmp, n_sel):
    ratio, span = SEL_LEN // CMP_STRIDE, CMP_LEN // CMP_STRIDE
    agg = np.zeros((n_cmp_pad, n_sel), np.float32)
    jj = np.arange(n_sel)
    for m in range(ratio):
        for n in range(span):
            ii = ratio * jj + m - n
            ok = (ii >= 0) & (ii < n_cmp)
            agg[ii[ok], jj[ok]] += 1.0
    return jnp.asarray(agg, dtype=BF16)


def _nsa_attention(qkv, gates, kv_cmp):
    l_all = qkv.shape[0]
    dh = NSA_HEAD_DIM
    g = NSA_KV_GROUPS
    qb = Q_BLOCK
    qw = NSA_REP * dh
    n_cmp_pad = l_all // CMP_STRIDE
    n_cmp = (l_all - CMP_LEN) // CMP_STRIDE + 1
    n_sel = l_all // SEL_LEN
    agg = _nsa_agg(n_cmp_pad, n_cmp, n_sel)
    period_keys = min(ONEHOT_PERIOD * SEL_LEN, l_all)
    key = np.arange(period_keys)
    onehot = jnp.asarray((key[:, None] // SEL_LEN) % ONEHOT_PERIOD == np.arange(ONEHOT_PERIOD)[None, :],
                         dtype=BF16)
    n_half = max(n_sel // ONEHOT_PERIOD, 1)
    q_blocks = (NSA_HEADS * dh) // dh
    col = lambda base: (lambda gg, b: (0, q_blocks + base + gg))
    kv_spec = lambda base: pl.BlockSpec((l_all, dh), col(base))
    return pl.pallas_call(
        _nsa_attn_kernel,
        grid=(g, l_all // qb),
        in_specs=[pl.BlockSpec((qb, qw), lambda gg, b: (b, gg)),
                  pl.BlockSpec((qb, dh), lambda gg, b: (b, gg)),
                  pl.BlockSpec((1, 1, n_cmp_pad, dh), lambda gg, b: (0, gg, 0, 0)),
                  pl.BlockSpec((1, 1, n_cmp_pad, dh), lambda gg, b: (1, gg, 0, 0)),
                  pl.BlockSpec((n_cmp_pad, n_sel), lambda gg, b: (0, 0)),
                  kv_spec(2 * g), kv_spec(3 * g), kv_spec(4 * g), kv_spec(5 * g),
                  pl.BlockSpec((period_keys, ONEHOT_PERIOD), lambda gg, b: (0, 0))],
        out_specs=pl.BlockSpec((qb, qw), lambda gg, b: (b, gg)),
        out_shape=jax.ShapeDtypeStruct((l_all, NSA_HEADS * dh), BF16),
        scratch_shapes=[pltpu.VMEM((n_half, NSA_REP * qb, ONEHOT_PERIOD), BF16)],
        compiler_params=_cparams(2),
    )(qkv, gates, kv_cmp, kv_cmp, agg, qkv, qkv, qkv, qkv, onehot)


def _nsa_layer(x, g, sh, sc, gate, w_in, k_pos, k_w1, k_w2, v_pos, v_w1, v_w2, w_out):
    dh = NSA_HEAD_DIM
    qd = NSA_HEADS * dh
    kvd = 2 * N_BRANCH * NSA_KV_GROUPS * dh
    w_main = w_in[:, :qd + kvd].astype(BF16)
    colscale = jnp.concatenate([jnp.full((1, qd), dh ** -0.5, F32), jnp.ones((1, kvd), F32)], axis=1)
    qkv = _norm_mod_matmul(x, g, sh, sc, w_main, colscale, BF16)
    wg = w_in[:, qd + kvd:].reshape(-1, NSA_KV_GROUPS, NSA_REP, N_BRANCH).transpose(0, 1, 3, 2)
    wg = wg.reshape(-1, NSA_KV_GROUPS, N_BRANCH * NSA_REP)
    wg = jnp.pad(wg, ((0, 0), (0, 0), (0, dh - N_BRANCH * NSA_REP))).reshape(-1, NSA_KV_GROUPS * dh)
    gates = _norm_mod_matmul(x, g, sh, sc, wg.astype(BF16), jnp.ones((1, wg.shape[1]), F32), F32,
                             act="sigmoid")
    kv_cmp = _compress(qkv, qd // dh, jnp.stack([k_pos, v_pos]),
                       jnp.stack([k_w1, v_w1]).astype(BF16), jnp.stack([k_w2, v_w2]).astype(BF16))
    o = _nsa_attention(qkv, gates, kv_cmp)
    return _matmul_residual(o, w_out.astype(BF16), x, gate, glu=False)


def _peer_query_kernel(x_ref, g_ref, sh_ref, sc_ref, wq_ref, keys_ref, h_ref, st_ref):
    h = _norm_mod(x_ref[...], g_ref[...], sh_ref[...], sc_ref[...]).astype(BF16)
    h_ref[...] = h
    q = jnp.dot(h, wq_ref[...], preferred_element_type=F32).astype(BF16)
    for hc in range(keys_ref.shape[0]):
        rows = slice(hc * PEER_NKEYS, (hc + 1) * PEER_NKEYS)
        st_ref[rows, :] = lax.dot_general(keys_ref[hc], q[:, hc * PEER_HALF:(hc + 1) * PEER_HALF],
                                          NT_DIMS, preferred_element_type=F32)


def _peer_query(x, g, sh, sc, w_q, sub_keys, tm=512):
    t, d = x.shape
    nq = w_q.shape[1]
    keys = sub_keys.reshape(-1, PEER_NKEYS, PEER_HALF).astype(BF16)
    row = lambda i: (0, 0)
    return pl.pallas_call(
        _peer_query_kernel,
        grid=(t // tm,),
        in_specs=[pl.BlockSpec((tm, d), lambda i: (i, 0)),
                  pl.BlockSpec((1, d), row), pl.BlockSpec((1, d), row), pl.BlockSpec((1, d), row),
                  pl.BlockSpec((d, nq), row),
                  pl.BlockSpec(keys.shape, lambda i: (0, 0, 0))],
        out_specs=[pl.BlockSpec((tm, d), lambda i: (i, 0)),
                   pl.BlockSpec((keys.shape[0] * PEER_NKEYS, tm), lambda i: (0, i))],
        out_shape=[jax.ShapeDtypeStruct((t, d), BF16),
                   jax.ShapeDtypeStruct((keys.shape[0] * PEER_NKEYS, t), F32)],
        compiler_params=_cparams(1),
    )(x, g, sh, sc, w_q.astype(BF16), keys)


def _peer_cells():
    return [(a, b) for a in range(PEER_TOPK) for b in range(PEER_TOPK) if (a + 1) * (b + 1) <= PEER_TOPK]


def _top_ranks(s, iota_k):
    nk = s.shape[0]
    rank = jnp.full(s.shape, RANK_NONE, F32)
    vals = []
    cur = s
    for a in range(PEER_TOPK):
        v = jnp.max(cur, axis=0, keepdims=True)
        idx = jnp.min(jnp.where(cur == v, iota_k, nk), axis=0, keepdims=True)
        hit = iota_k == idx
        rank = jnp.where(hit, float(a), rank)
        cur = jnp.where(hit, -jnp.inf, cur)
        vals.append(v)
    return vals, rank


def _peer_route_kernel(st_ref, seg_ref, r2_ref, ln_ref, g1_ref, g2_ref):
    tn = st_ref.shape[1]
    nk = PEER_NKEYS
    cells = _peer_cells()
    n_cell = len(cells)
    n_pad = -(-n_cell // 8) * 8
    n_seg = seg_ref.shape[1]
    iota_k = lax.broadcasted_iota(jnp.int32, (nk, tn), 0)
    iota_c = lax.broadcasted_iota(jnp.int32, (n_pad, tn), 0)
    for h in range(PEER_HEADS):
        s1 = st_ref[(2 * h) * nk:(2 * h + 1) * nk, :]
        s2 = st_ref[(2 * h + 1) * nk:(2 * h + 2) * nk, :]
        v1, rank1 = _top_ranks(s1, iota_k)
        v2, rank2 = _top_ranks(s2, iota_k)
        cand = jnp.concatenate([v1[a] + v2[b] for a, b in cells]
                               + [jnp.full((n_pad - n_cell, tn), -jnp.inf, F32)], axis=0)
        top = v1[0] + v2[0]
        e_c = jnp.exp(cand - top)
        chosen = jnp.zeros((n_pad, tn), jnp.bool_)
        cur = cand
        for _ in range(PEER_TOPK):
            v = jnp.max(cur, axis=0, keepdims=True)
            idx = jnp.min(jnp.where(cur == v, iota_c, n_pad), axis=0, keepdims=True)
            hit = iota_c == idx
            chosen = chosen | hit
            cur = jnp.where(hit, -jnp.inf, cur)
        chosen_f = chosen.astype(F32)
        z = jnp.sum(chosen_f * e_c, axis=0, keepdims=True)
        chosen_pad = jnp.concatenate([chosen_f, jnp.zeros((n_seg - n_pad, tn), F32)], axis=0)
        rowlen = jnp.dot(seg_ref[...], chosen_pad.astype(BF16), preferred_element_type=F32)
        ln = jnp.zeros((nk, tn), F32)
        for a in range(PEER_TOPK):
            ln = jnp.where(rank1 == float(a), rowlen[a:a + 1, :], ln)
        rows = slice(h * nk, (h + 1) * nk)
        r2_ref[rows, :] = rank2.astype(r2_ref.dtype)
        ln_ref[rows, :] = ln.astype(ln_ref.dtype)
        g1_ref[rows, :] = (jnp.exp(s1 - v1[0]) / z).astype(g1_ref.dtype)
        g2_ref[rows, :] = jnp.exp(s2 - v2[0]).astype(g2_ref.dtype)


def _peer_route(st, tn=256):
    n_rows, t = st.shape
    cells = _peer_cells()
    seg = np.zeros((PEER_TOPK, PEER_NKEYS), np.float32)
    for c, (a, _) in enumerate(cells):
        seg[a, c] = 1.0
    out_rows = PEER_HEADS * PEER_NKEYS
    shp = jax.ShapeDtypeStruct((out_rows, t), BF16)
    shp_row = jax.ShapeDtypeStruct((out_rows, t), F32)
    spec = pl.BlockSpec((out_rows, tn), lambda i: (0, i))
    return pl.pallas_call(
        _peer_route_kernel,
        grid=(t // tn,),
        in_specs=[pl.BlockSpec((n_rows, tn), lambda i: (0, i)),
                  pl.BlockSpec((PEER_TOPK, PEER_NKEYS), lambda i: (0, 0))],
        out_specs=[spec, spec, spec, spec],
        out_shape=[shp, shp_row, shp_row, shp],
        compiler_params=_cparams(1),
    )(st, jnp.asarray(seg, dtype=BF16))


def _peer_expert_kernel(h_ref, u_ref, vt_ref, r2_ref, ln_ref, g1_ref, g2_ref, x_ref, gate_ref, nf_ref,
                        o_ref, acc_t, p_scr, *, final_norm):
    c = pl.program_id(1)
    ec = u_ref.shape[0]
    nk = PEER_NKEYS

    @pl.when(c == 0)
    def _():
        acc_t[...] = jnp.zeros_like(acc_t)

    act = jax.nn.gelu(lax.dot_general(u_ref[...], h_ref[...], NT_DIMS, preferred_element_type=F32))
    for ii in range(ec // nk):
        i_key = c * (ec // nk) + ii
        w = jnp.zeros((nk, act.shape[1]), F32)
        for h in range(PEER_HEADS):
            rows = slice(h * nk, (h + 1) * nk)
            ln_row = ln_ref[pl.ds(h * nk + i_key, 1), :].astype(F32)
            g1_row = g1_ref[pl.ds(h * nk + i_key, 1), :].astype(F32)
            w = w + jnp.where(r2_ref[rows, :].astype(F32) < ln_row,
                              g2_ref[rows, :].astype(F32), 0.0) * g1_row
        p_scr[ii * nk:(ii + 1) * nk, :] = (w * act[ii * nk:(ii + 1) * nk, :]).astype(BF16)
    acc_t[...] += jnp.dot(vt_ref[...], p_scr[...], preferred_element_type=F32)

    @pl.when(c == pl.num_programs(1) - 1)
    def _():
        xo = x_ref[...] + gate_ref[...] * acc_t[...].T
        if final_norm:
            ms = jnp.mean(xo * xo, axis=-1, keepdims=True)
            xo = (xo * lax.rsqrt(ms + RMS_EPS)) * nf_ref[...]
        o_ref[...] = xo


def _peer_experts(h, u_bf, vt_bf, route, x, gate, norm_final, final_norm, tm=512, ec=512):
    t, d = x.shape
    e = u_bf.shape[0]
    r_rows = route[0].shape[0]
    rspec = pl.BlockSpec((r_rows, tm), lambda i, c: (0, i))
    row = lambda i, c: (0, 0)
    return pl.pallas_call(
        functools.partial(_peer_expert_kernel, final_norm=final_norm),
        grid=(t // tm, e // ec),
        in_specs=[pl.BlockSpec((tm, d), lambda i, c: (i, 0)),
                  pl.BlockSpec((ec, d), lambda i, c: (c, 0)),
                  pl.BlockSpec((d, ec), lambda i, c: (0, c)),
                  rspec, rspec, rspec, rspec,
                  pl.BlockSpec((tm, d), lambda i, c: (i, 0)),
                  pl.BlockSpec((1, d), row), pl.BlockSpec((1, d), row)],
        out_specs=pl.BlockSpec((tm, d), lambda i, c: (i, 0)),
        out_shape=jax.ShapeDtypeStruct((t, d), F32),
        scratch_shapes=[pltpu.VMEM((d, tm), F32), pltpu.VMEM((ec, tm), BF16)],
        compiler_params=_cparams(2),
    )(h, u_bf, vt_bf, *route, x, gate, norm_final)


def _peer_layer(x, g, sh, sc, gate, w_q, sub_keys, u_tab, v_tab, norm_final, final_norm):
    h, st = _peer_query(x, g, sh, sc, w_q, sub_keys)
    route = _peer_route(st)
    return _peer_experts(h, u_tab.astype(BF16), v_tab.astype(BF16).T, route, x, gate,
                         norm_final, final_norm)


def kernel(x, c, ada_w, ada_b, norm_mix, norm_ffn, norm_final, ssm_w_in, ssm_lambda_re, ssm_lambda_im, ssm_log_dt, ssm_b_re, ssm_b_im, ssm_c_re, ssm_c_im, ssm_d, ssm_w_out, nsa_w_in, nsa_cmp_k_pos, nsa_cmp_k_w1, nsa_cmp_k_w2, nsa_cmp_v_pos, nsa_cmp_v_w1, nsa_cmp_v_w2, nsa_w_out, peer_w_q, peer_sub_keys, peer_u, peer_v):
    bsz, l_all, d = x.shape
    assert bsz == 1
    depth = ada_w.shape[0]
    mod = _adaln(c, ada_w, ada_b)
    xt = x.reshape(l_all, d)
    nf = norm_final.reshape(1, d)
    for i in range(depth):
        sh1, sc1, g1, sh2, sc2, g2 = [mod[i, :, k * d:(k + 1) * d] for k in range(6)]
        j = i // 2
        gm = norm_mix[i].reshape(1, d)
        if i % 2 == 0:
            xt = _s5_layer(xt, gm, sh1, sc1, g1, ssm_w_in[j], ssm_lambda_re[j], ssm_lambda_im[j],
                           ssm_log_dt[j], ssm_b_re[j], ssm_b_im[j], ssm_c_re[j], ssm_c_im[j],
                           ssm_d[j], ssm_w_out[j])
        else:
            xt = _nsa_layer(xt, gm, sh1, sc1, g1, nsa_w_in[j], nsa_cmp_k_pos[j], nsa_cmp_k_w1[j],
                            nsa_cmp_k_w2[j], nsa_cmp_v_pos[j], nsa_cmp_v_w1[j], nsa_cmp_v_w2[j],
                            nsa_w_out[j])
        xt = _peer_layer(xt, norm_ffn[i].reshape(1, d), sh2, sc2, g2, peer_w_q[i], peer_sub_keys[i],
                         peer_u[i], peer_v[i], nf, final_norm=(i == depth - 1))
    return xt.reshape(bsz, l_all, d)
```

```python
import functools
import math

import numpy as np
import jax
import jax.numpy as jnp
from jax import lax
from jax.experimental import pallas as pl
from jax.experimental.pallas import tpu as pltpu

F32 = jnp.float32
BF16 = jnp.bfloat16

RMS_EPS = 1e-6

SSM_GROUP = 16
SSM_STATE = 64
SSM_BLOCK_GROUPS = 8
S5_SUBSEQ = 8
S5_CHUNK = 128
S5_LANE_BLOCK = 1024

NSA_HEADS = 16
NSA_HEAD_DIM = 128
NSA_KV_GROUPS = 4
NSA_REP = NSA_HEADS // NSA_KV_GROUPS
N_BRANCH = 3
CMP_LEN = 32
CMP_STRIDE = 16
SEL_LEN = 64
SEL_TOPN = 16
WIN = 512
Q_BLOCK = 128
SEL_KV_TILE = 512
MASK_PENALTY = -(2.0 ** 30)
ONEHOT_PERIOD = 128

PEER_HEADS = 8
PEER_NKEYS = 128
PEER_TOPK = 16
PEER_HALF = 128
PEER_SUBCHUNK = 512
RANK_NONE = 255.0

VMEM_LIMIT = 56 * 1024 * 1024
NT_DIMS = (((1,), (1,)), ((), ()))


def _cparams(n_axes):
    return pltpu.CompilerParams(
        dimension_semantics=("arbitrary",) * n_axes, vmem_limit_bytes=VMEM_LIMIT)


def _norm_mod(x, g, sh, sc):
    ms = jnp.mean(x * x, axis=-1, keepdims=True)
    xn = x * lax.rsqrt(ms + RMS_EPS)
    return (xn * g) * (1.0 + sc) + sh


def _adaln_kernel(c_ref, w_ref, b_ref, o_ref):
    c = c_ref[...]
    cond = c * jax.nn.sigmoid(c)
    o_ref[0] = jnp.dot(cond.astype(BF16), w_ref[0].astype(BF16),
                       preferred_element_type=F32) + b_ref[0]


def _adaln(c, ada_w, ada_b):
    depth, d, n = ada_w.shape
    tn = 1024
    c8 = jnp.broadcast_to(c, (8, d))
    out = pl.pallas_call(
        _adaln_kernel,
        grid=(depth, n // tn),
        in_specs=[pl.BlockSpec((8, d), lambda i, j: (0, 0)),
                  pl.BlockSpec((1, d, tn), lambda i, j: (i, 0, j)),
                  pl.BlockSpec((1, 1, tn), lambda i, j: (i, 0, j))],
        out_specs=pl.BlockSpec((1, 8, tn), lambda i, j: (i, 0, j)),
        out_shape=jax.ShapeDtypeStruct((depth, 8, n), F32),
        compiler_params=_cparams(2),
    )(c8, ada_w, ada_b.reshape(depth, 1, n))
    return out[:, 0:1, :]


def _nmm_kernel(x_ref, g_ref, sh_ref, sc_ref, w_ref, cs_ref, o_ref, h_scr, *, act):
    @pl.when(pl.program_id(1) == 0)
    def _():
        h_scr[...] = _norm_mod(x_ref[...], g_ref[...], sh_ref[...], sc_ref[...]).astype(BF16)

    acc = jnp.dot(h_scr[...], w_ref[...], preferred_element_type=F32)
    if act == "sigmoid":
        acc = jax.nn.sigmoid(acc)
    else:
        acc = acc * cs_ref[...]
    o_ref[...] = acc.astype(o_ref.dtype)


def _norm_mod_matmul(x, g, sh, sc, w, colscale, out_dtype, act="scale", tm=512, tn=1024):
    t, d = x.shape
    n = w.shape[1]
    tn = min(tn, n)
    row = lambda i, j: (0, 0)
    return pl.pallas_call(
        functools.partial(_nmm_kernel, act=act),
        grid=(t // tm, n // tn),
        in_specs=[pl.BlockSpec((tm, d), lambda i, j: (i, 0)),
                  pl.BlockSpec((1, d), row), pl.BlockSpec((1, d), row), pl.BlockSpec((1, d), row),
                  pl.BlockSpec((d, tn), lambda i, j: (0, j)),
                  pl.BlockSpec((1, tn), lambda i, j: (0, j))],
        out_specs=pl.BlockSpec((tm, tn), lambda i, j: (i, j)),
        out_shape=jax.ShapeDtypeStruct((t, n), out_dtype),
        scratch_shapes=[pltpu.VMEM((tm, d), BF16)],
        compiler_params=_cparams(2),
    )(x, g, sh, sc, w, colscale)


def _mm_res_kernel(a_ref, w_ref, x_ref, g_ref, o_ref):
    y = jnp.dot(a_ref[...], w_ref[...], preferred_element_type=F32)
    o_ref[...] = x_ref[...] + g_ref[...] * y


def _mm_glu_res_kernel(a_ref, wa_ref, wb_ref, x_ref, g_ref, o_ref):
    a = a_ref[...]
    ya = jnp.dot(a, wa_ref[...], preferred_element_type=F32)
    yb = jnp.dot(a, wb_ref[...], preferred_element_type=F32)
    o_ref[...] = x_ref[...] + g_ref[...] * (ya * jax.nn.sigmoid(yb))


def _matmul_residual(a, w, x, gate, glu, tm=512, tn=1024):
    t, k = a.shape
    n = x.shape[1]
    nb = n // tn
    a_spec = pl.BlockSpec((tm, k), lambda i, j: (i, 0))
    w_spec = pl.BlockSpec((k, tn), lambda i, j: (0, j))
    tail = [pl.BlockSpec((tm, tn), lambda i, j: (i, j)), pl.BlockSpec((1, tn), lambda i, j: (0, j))]
    if glu:
        kern = _mm_glu_res_kernel
        in_specs = [a_spec, w_spec, pl.BlockSpec((k, tn), lambda i, j: (0, j + nb))] + tail
        args = (a, w, w, x, gate)
    else:
        kern = _mm_res_kernel
        in_specs = [a_spec, w_spec] + tail
        args = (a, w, x, gate)
    return pl.pallas_call(
        kern,
        grid=(t // tm, nb),
        in_specs=in_specs,
        out_specs=pl.BlockSpec((tm, tn), lambda i, j: (i, j)),
        out_shape=jax.ShapeDtypeStruct((t, n), F32),
        compiler_params=_cparams(2),
    )(*args)


def _s5_param_kernel(lre_ref, lim_ref, ldt_ref, bre_ref, bim_ref, ar_ref, ai_ref, bbr_ref, bbi_ref):
    lre = lre_ref[...]
    lim = lim_ref[...]
    dt = jnp.exp(ldt_ref[...])
    mag = jnp.exp(lre * dt)
    ar = mag * jnp.cos(lim * dt)
    ai = mag * jnp.sin(lim * dt)
    den = lre * lre + lim * lim
    cr = ((ar - 1.0) * lre + ai * lim) / den
    ci = (ai * lre - (ar - 1.0) * lim) / den
    ar_ref[...] = ar
    ai_ref[...] = ai
    bbr_ref[...] = cr * bre_ref[...] - ci * bim_ref[...]
    bbi_ref[...] = cr * bim_ref[...] + ci * bre_ref[...]


def _s5_params(lam_re, lam_im, log_dt, b_re, b_im):
    g, n, p = b_re.shape
    rep = lambda a: jnp.repeat(a, p, axis=1)
    shp = jax.ShapeDtypeStruct((g, n * p), F32)
    ar, ai, bbr, bbi = pl.pallas_call(
        _s5_param_kernel, out_shape=(shp, shp, shp, shp),
    )(rep(lam_re), rep(lam_im), log_dt.reshape(g, 1), b_re.reshape(g, n * p), b_im.reshape(g, n * p))
    return ar[:, ::p], ai[:, ::p], bbr.reshape(g, n, p), bbi.reshape(g, n, p)


def _s5_scan_kernel(u_ref, a_ref, bbr_ref, bbi_ref, cr_ref, cin_ref, d_ref, o_ref,
                    bur, bui, hin_r, hin_i, car_r, car_i, apr, api):
    t, width = u_ref.shape
    sub = t // S5_SUBSEQ
    lanes = bur.shape[1]
    nblk = bbr_ref.shape[0]
    kin = width // nblk
    kst = lanes // nblk

    @pl.when(pl.program_id(0) == 0)
    def _init():
        car_r[...] = jnp.zeros_like(car_r)
        car_i[...] = jnp.zeros_like(car_i)
        ar = a_ref[0:1, :]
        ai = a_ref[1:2, :]
        pr, pi = ar, ai
        apr[0:1, :] = pr
        api[0:1, :] = pi
        for i in range(1, sub):
            pr, pi = pr * ar - pi * ai, pr * ai + pi * ar
            apr[i:i + 1, :] = pr
            api[i:i + 1, :] = pi

    for k in range(nblk):
        uk = u_ref[:, kin * k:kin * (k + 1)].astype(BF16)
        bur[:, kst * k:kst * (k + 1)] = jnp.dot(uk, bbr_ref[k], preferred_element_type=F32)
        bui[:, kst * k:kst * (k + 1)] = jnp.dot(uk, bbi_ref[k], preferred_element_type=F32)

    lb = S5_LANE_BLOCK
    for b in range(lanes // lb):
        sl = slice(b * lb, (b + 1) * lb)
        ar = jnp.broadcast_to(a_ref[0:1, sl], (S5_SUBSEQ, lb))
        ai = jnp.broadcast_to(a_ref[1:2, sl], (S5_SUBSEQ, lb))
        sr = jnp.zeros((S5_SUBSEQ, lb), F32)
        si = jnp.zeros((S5_SUBSEQ, lb), F32)
        for i in range(sub):
            rows = slice(S5_SUBSEQ * i, S5_SUBSEQ * (i + 1))
            sr, si = (ar * sr - ai * si + bur[rows, sl], ar * si + ai * sr + bui[rows, sl])
            bur[rows, sl] = sr
            bui[rows, sl] = si
        asr = apr[sub - 1:sub, sl]
        asi = api[sub - 1:sub, sl]
        hr = car_r[0:1, sl]
        hi = car_i[0:1, sl]
        for j in range(S5_SUBSEQ):
            hin_r[j:j + 1, sl] = hr
            hin_i[j:j + 1, sl] = hi
            er = sr[j:j + 1, :]
            ei = si[j:j + 1, :]
            hr, hi = er + asr * hr - asi * hi, ei + asr * hi + asi * hr
        car_r[0:1, sl] = hr
        car_i[0:1, sl] = hi
        hinr = hin_r[:, sl]
        hini = hin_i[:, sl]
        for i in range(sub):
            rows = slice(S5_SUBSEQ * i, S5_SUBSEQ * (i + 1))
            pr = jnp.broadcast_to(apr[i:i + 1, sl], (S5_SUBSEQ, lb))
            pi = jnp.broadcast_to(api[i:i + 1, sl], (S5_SUBSEQ, lb))
            bur[rows, sl] = bur[rows, sl] + pr * hinr - pi * hini
            bui[rows, sl] = bui[rows, sl] + pr * hini + pi * hinr

    for k in range(nblk):
        sr = bur[:, kst * k:kst * (k + 1)].astype(BF16)
        si = bui[:, kst * k:kst * (k + 1)].astype(BF16)
        y = (jnp.dot(sr, cr_ref[k], preferred_element_type=F32)
             + jnp.dot(si, cin_ref[k], preferred_element_type=F32))
        cols = slice(kin * k, kin * (k + 1))
        y = y + d_ref[:, cols] * u_ref[:, cols]
        o_ref[:, cols] = jax.nn.gelu(y).astype(o_ref.dtype)


def _s5_scan(u_p, ar, ai, bb_r, bb_i, c_re, c_im, d_skip):
    t_all, width = u_p.shape
    g, n, p = bb_r.shape
    bg = SSM_BLOCK_GROUPS
    nblk = g // bg
    eye = jnp.eye(bg, dtype=F32)
    blk_b = lambda bb: jnp.einsum('kgnp,gh->kgphn', bb.reshape(nblk, bg, n, p), eye
                                  ).reshape(nblk, bg * p, bg * n).astype(BF16)
    blk_c = lambda cc: jnp.einsum('kgpn,gh->kgnhp', cc.reshape(nblk, bg, p, n), eye
                                  ).reshape(nblk, bg * n, bg * p).astype(BF16)
    lanes = g * n
    a8 = jnp.zeros((8, lanes), F32).at[0].set(ar.reshape(lanes)).at[1].set(ai.reshape(lanes))
    t = S5_CHUNK
    sub = t // S5_SUBSEQ
    full3 = lambda c: (0, 0, 0)
    wspec_b = pl.BlockSpec((nblk, bg * p, bg * n), full3)
    wspec_c = pl.BlockSpec((nblk, bg * n, bg * p), full3)
    return pl.pallas_call(
        _s5_scan_kernel,
        grid=(t_all // t,),
        in_specs=[pl.BlockSpec((t, width), lambda c: (c, 0)),
                  pl.BlockSpec((8, lanes), lambda c: (0, 0)),
                  wspec_b, wspec_b, wspec_c, wspec_c,
                  pl.BlockSpec((1, width), lambda c: (0, 0))],
        out_specs=pl.BlockSpec((t, width), lambda c: (c, 0)),
        out_shape=jax.ShapeDtypeStruct((t_all, width), BF16),
        scratch_shapes=[pltpu.VMEM((t, lanes), F32), pltpu.VMEM((t, lanes), F32),
                        pltpu.VMEM((8, lanes), F32), pltpu.VMEM((8, lanes), F32),
                        pltpu.VMEM((8, lanes), F32), pltpu.VMEM((8, lanes), F32),
                        pltpu.VMEM((sub, lanes), F32), pltpu.VMEM((sub, lanes), F32)],
        compiler_params=_cparams(1),
    )(u_p, a8, blk_b(bb_r), blk_b(bb_i), blk_c(c_re), blk_c(-c_im), d_skip.reshape(1, width))


def _s5_rows_to_subseq(x, inverse=False):
    t_all, d = x.shape
    sub = S5_CHUNK // S5_SUBSEQ
    shape = (t_all // S5_CHUNK, sub, S5_SUBSEQ, d) if inverse else (t_all // S5_CHUNK, S5_SUBSEQ, sub, d)
    return x.reshape(shape).transpose(0, 2, 1, 3).reshape(t_all, d)


def _s5_layer(x, g, sh, sc, gate, w_in, lam_re, lam_im, log_dt, b_re, b_im, c_re, c_im, d_skip, w_out):
    d = x.shape[1]
    x_p = _s5_rows_to_subseq(x)
    ones = jnp.ones((1, w_in.shape[1]), F32)
    u_p = _norm_mod_matmul(x_p, g, sh, sc, w_in.astype(BF16), ones, F32)
    ar, ai, bb_r, bb_i = _s5_params(lam_re, lam_im, log_dt, b_re, b_im)
    gy_p = _s5_scan(u_p, ar, ai, bb_r, bb_i, c_re, c_im, d_skip)
    xn_p = _matmul_residual(gy_p, w_out.astype(BF16), x_p, gate, glu=True)
    return _s5_rows_to_subseq(xn_p, inverse=True)


def _cmp_kernel(x_ref, w1a_ref, w1b_ref, pos_ref, w1_ref, w2_ref, o_ref, acc_a, acc_b):
    l = pl.program_id(2)

    @pl.when(l == 0)
    def _():
        acc_a[...] = jnp.zeros_like(acc_a)
        acc_b[...] = jnp.zeros_like(acc_b)

    x = x_ref[...]
    acc_a[...] += jnp.dot(x, w1a_ref[0, 0], preferred_element_type=F32)
    acc_b[...] += jnp.dot(x, w1b_ref[0, 0], preferred_element_type=F32)

    @pl.when(l == pl.num_programs(2) - 1)
    def _():
        m = acc_a.shape[0]
        posc = jnp.zeros((8, NSA_HEAD_DIM), F32)
        for ll in range(CMP_LEN):
            prow = jnp.broadcast_to(pos_ref[0, ll:ll + 1, :], (8, NSA_HEAD_DIM)).astype(BF16)
            posc = posc + jnp.dot(prow, w1_ref[0, ll], preferred_element_type=F32)
        pre = acc_a[...] + pltpu.roll(acc_b[...], m - 1, 0) + posc[0:1, :]
        hid = jax.nn.gelu(pre)
        o_ref[0, 0] = jnp.dot(hid.astype(BF16), w2_ref[0], preferred_element_type=F32).astype(o_ref.dtype)


def _compress(qkv, kv_col0, pos, w1, w2):
    l_all, c = qkv.shape
    half = CMP_LEN // 2
    m = l_all // half
    cb = c // NSA_HEAD_DIM
    x2 = qkv.reshape(m, half * c)
    g = NSA_KV_GROUPS
    dh = NSA_HEAD_DIM
    return pl.pallas_call(
        _cmp_kernel,
        grid=(2, g, half),
        in_specs=[pl.BlockSpec((m, dh), lambda s, gg, l: (0, l * cb + kv_col0 + g * s + gg)),
                  pl.BlockSpec((1, 1, dh, dh), lambda s, gg, l: (s, l, 0, 0)),
                  pl.BlockSpec((1, 1, dh, dh), lambda s, gg, l: (s, l + half, 0, 0)),
                  pl.BlockSpec((1, CMP_LEN, dh), lambda s, gg, l: (s, 0, 0)),
                  pl.BlockSpec((1, CMP_LEN, dh, dh), lambda s, gg, l: (s, 0, 0, 0)),
                  pl.BlockSpec((1, dh, dh), lambda s, gg, l: (s, 0, 0))],
        out_specs=pl.BlockSpec((1, 1, m, dh), lambda s, gg, l: (s, gg, 0, 0)),
        out_shape=jax.ShapeDtypeStruct((2, g, m, dh), BF16),
        scratch_shapes=[pltpu.VMEM((m, dh), F32), pltpu.VMEM((m, dh), F32)],
        compiler_params=_cparams(3),
    )(x2, w1, w1, pos, w1, w2)


def _softmax2_rows(s, mask):
    s = jnp.where(mask, s, -1e30)
    m = jnp.max(s, axis=-1, keepdims=True)
    p = jnp.exp2(s - m)
    return p, 1.0 / jnp.maximum(jnp.sum(p, axis=-1, keepdims=True), 1e-30)


def _nsa_attn_kernel(q_ref, gt_ref, kc_ref, vc_ref, agg_ref, ks_ref, vs_ref, kw_ref, vw_ref, oh_ref,
                     o_ref, qaug_scr):
    b = pl.program_id(1)
    qb = q_ref.shape[0]
    dh = NSA_HEAD_DIM
    rep = NSA_REP
    rows = rep * qb
    n_cmp = kc_ref.shape[2]
    n_sel = agg_ref.shape[1]
    t0 = b * qb

    q_blk = q_ref[...]
    qs = jnp.concatenate([q_blk[:, r * dh:(r + 1) * dh] for r in range(rep)], axis=0)
    tpos = t0 + lax.broadcasted_iota(jnp.int32, (rows, 1), 0) % qb

    kc = kc_ref[0, 0]
    s_c = lax.dot_general(qs, kc, NT_DIMS, preferred_element_type=F32)
    cmp_end = lax.broadcasted_iota(jnp.int32, (1, n_cmp), 1) * CMP_STRIDE + (CMP_LEN - 1)
    p_c, inv_c = _softmax2_rows(s_c, cmp_end <= tpos)
    p_c = p_c * jnp.where(tpos >= CMP_LEN - 1, inv_c, 0.0)
    o_c = jnp.dot(p_c.astype(BF16), vc_ref[0, 0], preferred_element_type=F32)

    psum = p_c[0:qb]
    for r in range(1, rep):
        psum = psum + p_c[r * qb:(r + 1) * qb]
    p_hi = psum.astype(BF16)
    p_lo = (psum - p_hi.astype(F32)).astype(BF16)
    agg = agg_ref[...]
    imp = (jnp.dot(p_hi, agg, preferred_element_type=F32)
           + jnp.dot(p_lo, agg, preferred_element_type=F32))
    imp_t = imp.T
    tq = t0 + lax.broadcasted_iota(jnp.int32, (1, qb), 1)
    blk = lax.broadcasted_iota(jnp.int32, (n_sel, qb), 0)
    blk_f = blk.astype(F32)
    cur = tq // SEL_LEN
    forced = (blk == 0) | (blk == cur) | (blk == cur - 1)
    valid = blk * SEL_LEN <= tq
    score = jnp.where(forced, 1e9, jnp.where(valid, imp_t, -1e9))
    pen_t = jnp.full((n_sel, qb), MASK_PENALTY, F32)
    for _ in range(min(SEL_TOPN, n_sel)):
        mx = jnp.max(score, axis=0, keepdims=True)
        idx = jnp.min(jnp.where(score == mx, blk_f, float(n_sel)), axis=0, keepdims=True)
        hit = blk_f == idx
        pen_t = jnp.where(hit, 0.0, pen_t)
        score = jnp.where(hit, -jnp.inf, score)
    pen = pen_t.T.astype(BF16)
    n_half = qaug_scr.shape[0]
    for hh in range(n_half):
        if n_sel >= ONEHOT_PERIOD:
            ph = pen[:, hh * ONEHOT_PERIOD:(hh + 1) * ONEHOT_PERIOD]
        else:
            ph = jnp.concatenate(
                [pen, jnp.full((qb, ONEHOT_PERIOD - n_sel), MASK_PENALTY, BF16)], axis=1)
        qaug_scr[hh] = jnp.concatenate([qs, jnp.concatenate([ph] * rep, axis=0)], axis=1)

    kt = SEL_KV_TILE
    period_keys = ONEHOT_PERIOD * SEL_LEN
    ones_col = (lax.broadcasted_iota(jnp.int32, (kt, dh), 1) == 0).astype(BF16)

    def sel_scores(i):
        k0 = pl.multiple_of(i * kt, kt)
        half = k0 // period_keys
        e0 = pl.multiple_of(k0 % period_keys, kt)
        k_aug = jnp.concatenate([ks_ref[pl.ds(k0, kt), :], oh_ref[pl.ds(e0, kt), :]], axis=1)
        return lax.dot_general(qaug_scr[half], k_aug, NT_DIMS, preferred_element_type=F32)

    def sel_update(i, s, m_run, acc):
        k0 = pl.multiple_of(i * kt, kt)
        m_new = jnp.maximum(m_run, jnp.max(s, axis=-1, keepdims=True))
        alpha = jnp.exp2(m_run - m_new)
        p = jnp.exp2(s - m_new)
        v_aug = jnp.concatenate([vs_ref[pl.ds(k0, kt), :], ones_col], axis=1)
        acc = alpha * acc + jnp.dot(p.astype(BF16), v_aug, preferred_element_type=F32)
        return m_new, acc

    def sel_step(i, carry):
        s, m_run, acc = carry
        s_next = sel_scores(i + 1)
        m_new, acc = sel_update(i, s, m_run, acc)
        return s_next, m_new, acc

    n_full = t0 // kt
    init = (sel_scores(0), jnp.full((rows, 1), -1e30, F32), jnp.zeros((rows, 2 * dh), F32))
    s_last, m_s, acc_s = lax.fori_loop(0, n_full, sel_step, init)
    kpos = n_full * kt + lax.broadcasted_iota(jnp.int32, (1, kt), 1)
    s_last = jnp.where(kpos <= tpos, s_last, MASK_PENALTY)
    _, acc_s = sel_update(n_full, s_last, m_s, acc_s)
    o_s = acc_s[:, 0:dh] * (1.0 / acc_s[:, dh:dh + 1])

    wlen = WIN + qb
    w0 = pl.multiple_of(jnp.maximum(t0 - WIN, 0), qb)
    s_w = lax.dot_general(qs, kw_ref[pl.ds(w0, wlen), :], NT_DIMS, preferred_element_type=F32)
    diff = tpos - (w0 + lax.broadcasted_iota(jnp.int32, (1, wlen), 1))
    p_w, inv_w = _softmax2_rows(s_w, (diff >= 0) & (diff < WIN))
    o_w = jnp.dot((p_w * inv_w).astype(BF16), vw_ref[pl.ds(w0, wlen), :], preferred_element_type=F32)

    gt = gt_ref[...]
    for r in range(rep):
        rs = slice(r * qb, (r + 1) * qb)
        o = (gt[:, r:r + 1] * o_c[rs]
             + gt[:, rep + r:rep + r + 1] * o_s[rs]
             + gt[:, 2 * rep + r:2 * rep + r + 1] * o_w[rs])
        o_ref[:, r * dh:(r + 1) * dh] = o.astype(o_ref.dtype)


def _nsa_agg(n_cmp_pad, n_cmp, n_sel):
    ratio, span = SEL_LEN // CMP_STRIDE, CMP_LEN // CMP_STRIDE
    agg = np.zeros((n_cmp_pad, n_sel), np.float32)
    jj = np.arange(n_sel)
    for m in range(ratio):
        for n in range(span):
            ii = ratio * jj + m - n
            ok = (ii >= 0) & (ii < n_cmp)
            agg[ii[ok], jj[ok]] += 1.0
    return jnp.asarray(agg, dtype=BF16)


def _nsa_attention(qkv, gates, kv_cmp):
    l_all = qkv.shape[0]
    dh = NSA_HEAD_DIM
    g = NSA_KV_GROUPS
    qb = Q_BLOCK
    qw = NSA_REP * dh
    n_cmp_pad = l_all // CMP_STRIDE
    n_cmp = (l_all - CMP_LEN) // CMP_STRIDE + 1
    n_sel = l_all // SEL_LEN
    agg = _nsa_agg(n_cmp_pad, n_cmp, n_sel)
    period_keys = min(ONEHOT_PERIOD * SEL_LEN, l_all)
    key = np.arange(period_keys)
    onehot = jnp.asarray((key[:, None] // SEL_LEN) % ONEHOT_PERIOD == np.arange(ONEHOT_PERIOD)[None, :],
                         dtype=BF16)
    n_half = max(n_sel // ONEHOT_PERIOD, 1)
    q_blocks = (NSA_HEADS * dh) // dh
    col = lambda base: (lambda gg, b: (0, q_blocks + base + gg))
    kv_spec = lambda base: pl.BlockSpec((l_all, dh), col(base))
    return pl.pallas_call(
        _nsa_attn_kernel,
        grid=(g, l_all // qb),
        in_specs=[pl.BlockSpec((qb, qw), lambda gg, b: (b, gg)),
                  pl.BlockSpec((qb, dh), lambda gg, b: (b, gg)),
                  pl.BlockSpec((1, 1, n_cmp_pad, dh), lambda gg, b: (0, gg, 0, 0)),
                  pl.BlockSpec((1, 1, n_cmp_pad, dh), lambda gg, b: (1, gg, 0, 0)),
                  pl.BlockSpec((n_cmp_pad, n_sel), lambda gg, b: (0, 0)),
                  kv_spec(2 * g), kv_spec(3 * g), kv_spec(4 * g), kv_spec(5 * g),
                  pl.BlockSpec((period_keys, ONEHOT_PERIOD), lambda gg, b: (0, 0))],
        out_specs=pl.BlockSpec((qb, qw), lambda gg, b: (b, gg)),
        out_shape=jax.ShapeDtypeStruct((l_all, NSA_HEADS * dh), BF16),
        scratch_shapes=[pltpu.VMEM((n_half, NSA_REP * qb, dh + ONEHOT_PERIOD), BF16)],
        compiler_params=_cparams(2),
    )(qkv, gates, kv_cmp, kv_cmp, agg, qkv, qkv, qkv, qkv, onehot)


def _nsa_layer(x, g, sh, sc, gate, w_in, k_pos, k_w1, k_w2, v_pos, v_w1, v_w2, w_out):
    dh = NSA_HEAD_DIM
    qd = NSA_HEADS * dh
    kvd = 2 * N_BRANCH * NSA_KV_GROUPS * dh
    w_main = w_in[:, :qd + kvd].astype(BF16)
    colscale = jnp.concatenate([jnp.full((1, qd), dh ** -0.5 * math.log2(math.e), F32),
                                jnp.ones((1, kvd), F32)], axis=1)
    qkv = _norm_mod_matmul(x, g, sh, sc, w_main, colscale, BF16)
    wg = w_in[:, qd + kvd:].reshape(-1, NSA_KV_GROUPS, NSA_REP, N_BRANCH).transpose(0, 1, 3, 2)
    wg = wg.reshape(-1, NSA_KV_GROUPS, N_BRANCH * NSA_REP)
    wg = jnp.pad(wg, ((0, 0), (0, 0), (0, dh - N_BRANCH * NSA_REP))).reshape(-1, NSA_KV_GROUPS * dh)
    gates = _norm_mod_matmul(x, g, sh, sc, wg.astype(BF16), jnp.ones((1, wg.shape[1]), F32), F32,
                             act="sigmoid")
    kv_cmp = _compress(qkv, qd // dh, jnp.stack([k_pos, v_pos]),
                       jnp.stack([k_w1, v_w1]).astype(BF16), jnp.stack([k_w2, v_w2]).astype(BF16))
    o = _nsa_attention(qkv, gates, kv_cmp)
    return _matmul_residual(o, w_out.astype(BF16), x, gate, glu=False)


def _peer_query_kernel(x_ref, g_ref, sh_ref, sc_ref, wq_ref, keys_ref, ht_ref, st_ref):
    h32 = _norm_mod(x_ref[...], g_ref[...], sh_ref[...], sc_ref[...])
    ht_ref[...] = h32.T.astype(BF16)
    q = jnp.dot(h32.astype(BF16), wq_ref[...], preferred_element_type=F32).astype(BF16)
    for hc in range(keys_ref.shape[0]):
        rows = slice(hc * PEER_NKEYS, (hc + 1) * PEER_NKEYS)
        st_ref[rows, :] = lax.dot_general(keys_ref[hc], q[:, hc * PEER_HALF:(hc + 1) * PEER_HALF],
                                          NT_DIMS, preferred_element_type=F32)


def _peer_query(x, g, sh, sc, w_q, sub_keys, tm=512):
    t, d = x.shape
    nq = w_q.shape[1]
    keys = sub_keys.reshape(-1, PEER_NKEYS, PEER_HALF).astype(BF16)
    row = lambda i: (0, 0)
    return pl.pallas_call(
        _peer_query_kernel,
        grid=(t // tm,),
        in_specs=[pl.BlockSpec((tm, d), lambda i: (i, 0)),
                  pl.BlockSpec((1, d), row), pl.BlockSpec((1, d), row), pl.BlockSpec((1, d), row),
                  pl.BlockSpec((d, nq), row),
                  pl.BlockSpec(keys.shape, lambda i: (0, 0, 0))],
        out_specs=[pl.BlockSpec((d, tm), lambda i: (0, i)),
                   pl.BlockSpec((keys.shape[0] * PEER_NKEYS, tm), lambda i: (0, i))],
        out_shape=[jax.ShapeDtypeStruct((d, t), BF16),
                   jax.ShapeDtypeStruct((keys.shape[0] * PEER_NKEYS, t), F32)],
        compiler_params=_cparams(1),
    )(x, g, sh, sc, w_q.astype(BF16), keys)


def _peer_cells():
    return [(a, b) for a in range(PEER_TOPK) for b in range(PEER_TOPK) if (a + 1) * (b + 1) <= PEER_TOPK]


def _top_ranks(s, iota_k):
    nk = s.shape[0]
    rank = jnp.full(s.shape, RANK_NONE, F32)
    vals = []
    cur = s
    for a in range(PEER_TOPK):
        v = jnp.max(cur, axis=0, keepdims=True)
        idx = jnp.min(jnp.where(cur == v, iota_k, nk), axis=0, keepdims=True)
        hit = iota_k == idx
        rank = jnp.where(hit, float(a), rank)
        cur = jnp.where(hit, -jnp.inf, cur)
        vals.append(v)
    return vals, rank


def _peer_route_kernel(st_ref, seg_ref, r2_ref, ln_ref, g1_ref, g2_ref):
    tn = st_ref.shape[1]
    nk = PEER_NKEYS
    cells = _peer_cells()
    n_cell = len(cells)
    n_pad = -(-n_cell // 8) * 8
    n_seg = seg_ref.shape[1]
    iota_k = lax.broadcasted_iota(jnp.int32, (nk, tn), 0)
    iota_c = lax.broadcasted_iota(jnp.int32, (n_pad, tn), 0)
    for h in range(PEER_HEADS):
        s1 = st_ref[(2 * h) * nk:(2 * h + 1) * nk, :]
        s2 = st_ref[(2 * h + 1) * nk:(2 * h + 2) * nk, :]
        v1, rank1 = _top_ranks(s1, iota_k)
        v2, rank2 = _top_ranks(s2, iota_k)
        cand = jnp.concatenate([v1[a] + v2[b] for a, b in cells]
                               + [jnp.full((n_pad - n_cell, tn), -jnp.inf, F32)], axis=0)
        top = v1[0] + v2[0]
        e_c = jnp.exp(cand - top)
        chosen = jnp.zeros((n_pad, tn), jnp.bool_)
        cur = cand
        for _ in range(PEER_TOPK):
            v = jnp.max(cur, axis=0, keepdims=True)
            idx = jnp.min(jnp.where(cur == v, iota_c, n_pad), axis=0, keepdims=True)
            hit = iota_c == idx
            chosen = chosen | hit
            cur = jnp.where(hit, -jnp.inf, cur)
        chosen_f = chosen.astype(F32)
        z = jnp.sum(chosen_f * e_c, axis=0, keepdims=True)
        chosen_pad = jnp.concatenate([chosen_f, jnp.zeros((n_seg - n_pad, tn), F32)], axis=0)
        rowlen = jnp.dot(seg_ref[...], chosen_pad.astype(BF16), preferred_element_type=F32)
        ln = jnp.zeros((nk, tn), F32)
        for a in range(PEER_TOPK):
            ln = jnp.where(rank1 == float(a), rowlen[a:a + 1, :], ln)
        rows = slice(h * nk, (h + 1) * nk)
        r2_ref[rows, :] = rank2.astype(r2_ref.dtype)
        ln_ref[h] = ln
        g1_ref[h] = jnp.exp(s1 - v1[0]) / z
        g2_ref[rows, :] = jnp.exp(s2 - v2[0]).astype(g2_ref.dtype)


def _peer_route(st, tn=256):
    n_rows, t = st.shape
    cells = _peer_cells()
    seg = np.zeros((PEER_TOPK, PEER_NKEYS), np.float32)
    for c, (a, _) in enumerate(cells):
        seg[a, c] = 1.0
    out_rows = PEER_HEADS * PEER_NKEYS
    shp = jax.ShapeDtypeStruct((out_rows, t), BF16)
    shp_row = jax.ShapeDtypeStruct((PEER_HEADS, PEER_NKEYS, t), F32)
    spec = pl.BlockSpec((out_rows, tn), lambda i: (0, i))
    spec_row = pl.BlockSpec((PEER_HEADS, PEER_NKEYS, tn), lambda i: (0, 0, i))
    return pl.pallas_call(
        _peer_route_kernel,
        grid=(t // tn,),
        in_specs=[pl.BlockSpec((n_rows, tn), lambda i: (0, i)),
                  pl.BlockSpec((PEER_TOPK, PEER_NKEYS), lambda i: (0, 0))],
        out_specs=[spec, spec_row, spec_row, spec],
        out_shape=[shp, shp_row, shp_row, shp],
        compiler_params=_cparams(1),
    )(st, jnp.asarray(seg, dtype=BF16))


def _peer_expert_kernel(h_ref, u_ref, vt_ref, r2_ref, ln_ref, g1_ref, g2_ref, x_ref, gate_ref, nf_ref,
                        o_ref, acc_t, p_scr, *, final_norm):
    c = pl.program_id(1)
    ec = u_ref.shape[0]
    nk = PEER_NKEYS
    sub = PEER_SUBCHUNK
    tm = h_ref.shape[1]

    @pl.when(c == 0)
    def _():
        acc_t[...] = jnp.zeros_like(acc_t)

    pre = [jnp.dot(u_ref[s * sub:(s + 1) * sub, :], h_ref[...], preferred_element_type=F32)
           for s in range(ec // sub)]
    zero = jnp.zeros((), BF16)
    y = None
    for s in range(ec // sub):
        act = jax.nn.gelu(pre[s])
        for ii in range(sub // nk):
            i_loc = s * (sub // nk) + ii
            w = jnp.zeros((nk, tm), BF16)
            for h in range(PEER_HEADS):
                rows = slice(h * nk, (h + 1) * nk)
                ln_row = ln_ref[h, i_loc:i_loc + 1, :].astype(BF16)
                g1_row = g1_ref[h, i_loc:i_loc + 1, :].astype(BF16)
                w = w + jnp.where(r2_ref[rows, :] < ln_row, g2_ref[rows, :], zero) * g1_row
            p_scr[i_loc * nk:(i_loc + 1) * nk, :] = w * act[ii * nk:(ii + 1) * nk, :].astype(BF16)
        ys = jnp.dot(vt_ref[:, s * sub:(s + 1) * sub], p_scr[s * sub:(s + 1) * sub, :],
                     preferred_element_type=F32)
        y = ys if y is None else y + ys
    acc_t[...] += y

    @pl.when(c == pl.num_programs(1) - 1)
    def _():
        xo = x_ref[...] + gate_ref[...] * acc_t[...].T
        if final_norm:
            ms = jnp.mean(xo * xo, axis=-1, keepdims=True)
            xo = (xo * lax.rsqrt(ms + RMS_EPS)) * nf_ref[...]
        o_ref[...] = xo


def _peer_experts(ht, u_bf, vt_bf, route, x, gate, norm_final, final_norm, tm=512, ec=1024):
    t, d = x.shape
    e = u_bf.shape[0]
    r2, ln, g1, g2 = route
    rspec = pl.BlockSpec((r2.shape[0], tm), lambda i, c: (0, i))
    kspec = pl.BlockSpec((PEER_HEADS, ec // PEER_NKEYS, tm), lambda i, c: (0, c, i))
    row = lambda i, c: (0, 0)
    return pl.pallas_call(
        functools.partial(_peer_expert_kernel, final_norm=final_norm),
        grid=(t // tm, e // ec),
        in_specs=[pl.BlockSpec((d, tm), lambda i, c: (0, i)),
                  pl.BlockSpec((ec, d), lambda i, c: (c, 0)),
                  pl.BlockSpec((d, ec), lambda i, c: (0, c)),
                  rspec, kspec, kspec, rspec,
                  pl.BlockSpec((tm, d), lambda i, c: (i, 0)),
                  pl.BlockSpec((1, d), row), pl.BlockSpec((1, d), row)],
        out_specs=pl.BlockSpec((tm, d), lambda i, c: (i, 0)),
        out_shape=jax.ShapeDtypeStruct((t, d), F32),
        scratch_shapes=[pltpu.VMEM((d, tm), F32), pltpu.VMEM((ec, tm), BF16)],
        compiler_params=_cparams(2),
    )(ht, u_bf, vt_bf, r2, ln, g1, g2, x, gate, norm_final)


def _peer_layer(x, g, sh, sc, gate, w_q, sub_keys, u_tab, v_tab, norm_final, final_norm):
    ht, st = _peer_query(x, g, sh, sc, w_q, sub_keys)
    route = _peer_route(st)
    return _peer_experts(ht, u_tab.astype(BF16), v_tab.astype(BF16).T, route, x, gate,
                         norm_final, final_norm)


def kernel(x, c, ada_w, ada_b, norm_mix, norm_ffn, norm_final, ssm_w_in, ssm_lambda_re, ssm_lambda_im, ssm_log_dt, ssm_b_re, ssm_b_im, ssm_c_re, ssm_c_im, ssm_d, ssm_w_out, nsa_w_in, nsa_cmp_k_pos, nsa_cmp_k_w1, nsa_cmp_k_w2, nsa_cmp_v_pos, nsa_cmp_v_w1, nsa_cmp_v_w2, nsa_w_out, peer_w_q, peer_sub_keys, peer_u, peer_v):
    bsz, l_all, d = x.shape
    assert bsz == 1
    depth = ada_w.shape[0]
    mod = _adaln(c, ada_w, ada_b)
    xt = x.reshape(l_all, d)
    nf = norm_final.reshape(1, d)
    for i in range(depth):
        sh1, sc1, g1, sh2, sc2, g2 = [mod[i, :, k * d:(k + 1) * d] for k in range(6)]
        j = i // 2
        gm = norm_mix[i].reshape(1, d)
        if i % 2 == 0:
            xt = _s5_layer(xt, gm, sh1, sc1, g1, ssm_w_in[j], ssm_lambda_re[j], ssm_lambda_im[j],
                           ssm_log_dt[j], ssm_b_re[j], ssm_b_im[j], ssm_c_re[j], ssm_c_im[j],
                           ssm_d[j], ssm_w_out[j])
        else:
            xt = _nsa_layer(xt, gm, sh1, sc1, g1, nsa_w_in[j], nsa_cmp_k_pos[j], nsa_cmp_k_w1[j],
                            nsa_cmp_k_w2[j], nsa_cmp_v_pos[j], nsa_cmp_v_w1[j], nsa_cmp_v_w2[j],
                            nsa_w_out[j])
        xt = _peer_layer(xt, norm_ffn[i].reshape(1, d), sh2, sc2, g2, peer_w_q[i], peer_sub_keys[i],
                         peer_u[i], peer_v[i], nf, final_norm=(i == depth - 1))
    return xt.reshape(bsz, l_all, d)
```

```python
import functools
import math

import numpy as np
import jax
import jax.numpy as jnp
from jax import lax
from jax.experimental import pallas as pl
from jax.experimental.pallas import tpu as pltpu

F32 = jnp.float32
BF16 = jnp.bfloat16

RMS_EPS = 1e-6

SSM_GROUP = 16
SSM_STATE = 64
SSM_BLOCK_GROUPS = 8
S5_SUBSEQ = 8
S5_CHUNK = 128
S5_LANE_BLOCK = 1024

NSA_HEADS = 16
NSA_HEAD_DIM = 128
NSA_KV_GROUPS = 4
NSA_REP = NSA_HEADS // NSA_KV_GROUPS
N_BRANCH = 3
CMP_LEN = 32
CMP_STRIDE = 16
SEL_LEN = 64
SEL_TOPN = 16
WIN = 512
Q_BLOCK = 128
SEL_KV_TILE = 512
MASK_PENALTY = -(2.0 ** 30)
ONEHOT_PERIOD = 128

PEER_HEADS = 8
PEER_NKEYS = 128
PEER_TOPK = 16
PEER_HALF = 128
PEER_SUBCHUNK = 512
RANK_NONE = 255.0

VMEM_LIMIT = 56 * 1024 * 1024
NT_DIMS = (((1,), (1,)), ((), ()))


def _cparams(n_axes):
    return pltpu.CompilerParams(
        dimension_semantics=("arbitrary",) * n_axes, vmem_limit_bytes=VMEM_LIMIT)


def _norm_mod(x, g, sh, sc):
    ms = jnp.mean(x * x, axis=-1, keepdims=True)
    xn = x * lax.rsqrt(ms + RMS_EPS)
    return (xn * g) * (1.0 + sc) + sh


def _adaln_kernel(c_ref, w_ref, b_ref, o_ref):
    c = c_ref[...]
    cond = c * jax.nn.sigmoid(c)
    o_ref[0] = jnp.dot(cond.astype(BF16), w_ref[0].astype(BF16),
                       preferred_element_type=F32) + b_ref[0]


def _adaln(c, ada_w, ada_b):
    depth, d, n = ada_w.shape
    tn = 1024
    c8 = jnp.broadcast_to(c, (8, d))
    out = pl.pallas_call(
        _adaln_kernel,
        grid=(depth, n // tn),
        in_specs=[pl.BlockSpec((8, d), lambda i, j: (0, 0)),
                  pl.BlockSpec((1, d, tn), lambda i, j: (i, 0, j)),
                  pl.BlockSpec((1, 1, tn), lambda i, j: (i, 0, j))],
        out_specs=pl.BlockSpec((1, 8, tn), lambda i, j: (i, 0, j)),
        out_shape=jax.ShapeDtypeStruct((depth, 8, n), F32),
        compiler_params=_cparams(2),
    )(c8, ada_w, ada_b.reshape(depth, 1, n))
    return out[:, 0:1, :]


def _nmm_kernel(x_ref, g_ref, sh_ref, sc_ref, w_ref, cs_ref, o_ref, h_scr, *, act):
    @pl.when(pl.program_id(1) == 0)
    def _():
        h_scr[...] = _norm_mod(x_ref[...], g_ref[...], sh_ref[...], sc_ref[...]).astype(BF16)

    acc = jnp.dot(h_scr[...], w_ref[...], preferred_element_type=F32)
    if act == "sigmoid":
        acc = jax.nn.sigmoid(acc)
    else:
        acc = acc * cs_ref[...]
    o_ref[...] = acc.astype(o_ref.dtype)


def _norm_mod_matmul(x, g, sh, sc, w, colscale, out_dtype, act="scale", tm=512, tn=1024):
    t, d = x.shape
    n = w.shape[1]
    tn = min(tn, n)
    row = lambda i, j: (0, 0)
    return pl.pallas_call(
        functools.partial(_nmm_kernel, act=act),
        grid=(t // tm, n // tn),
        in_specs=[pl.BlockSpec((tm, d), lambda i, j: (i, 0)),
                  pl.BlockSpec((1, d), row), pl.BlockSpec((1, d), row), pl.BlockSpec((1, d), row),
                  pl.BlockSpec((d, tn), lambda i, j: (0, j)),
                  pl.BlockSpec((1, tn), lambda i, j: (0, j))],
        out_specs=pl.BlockSpec((tm, tn), lambda i, j: (i, j)),
        out_shape=jax.ShapeDtypeStruct((t, n), out_dtype),
        scratch_shapes=[pltpu.VMEM((tm, d), BF16)],
        compiler_params=_cparams(2),
    )(x, g, sh, sc, w, colscale)


def _mm_res_kernel(a_ref, w_ref, x_ref, g_ref, o_ref):
    y = jnp.dot(a_ref[...], w_ref[...], preferred_element_type=F32)
    o_ref[...] = x_ref[...] + g_ref[...] * y


def _mm_glu_res_kernel(a_ref, wa_ref, wb_ref, x_ref, g_ref, o_ref):
    a = a_ref[...]
    ya = jnp.dot(a, wa_ref[...], preferred_element_type=F32)
    yb = jnp.dot(a, wb_ref[...], preferred_element_type=F32)
    o_ref[...] = x_ref[...] + g_ref[...] * (ya * jax.nn.sigmoid(yb))


def _matmul_residual(a, w, x, gate, glu, tm=512, tn=1024):
    t, k = a.shape
    n = x.shape[1]
    nb = n // tn
    a_spec = pl.BlockSpec((tm, k), lambda i, j: (i, 0))
    w_spec = pl.BlockSpec((k, tn), lambda i, j: (0, j))
    tail = [pl.BlockSpec((tm, tn), lambda i, j: (i, j)), pl.BlockSpec((1, tn), lambda i, j: (0, j))]
    if glu:
        kern = _mm_glu_res_kernel
        in_specs = [a_spec, w_spec, pl.BlockSpec((k, tn), lambda i, j: (0, j + nb))] + tail
        args = (a, w, w, x, gate)
    else:
        kern = _mm_res_kernel
        in_specs = [a_spec, w_spec] + tail
        args = (a, w, x, gate)
    return pl.pallas_call(
        kern,
        grid=(t // tm, nb),
        in_specs=in_specs,
        out_specs=pl.BlockSpec((tm, tn), lambda i, j: (i, j)),
        out_shape=jax.ShapeDtypeStruct((t, n), F32),
        compiler_params=_cparams(2),
    )(*args)


def _s5_param_kernel(lre_ref, lim_ref, ldt_ref, bre_ref, bim_ref, ar_ref, ai_ref, bbr_ref, bbi_ref):
    lre = lre_ref[...]
    lim = lim_ref[...]
    dt = jnp.exp(ldt_ref[...])
    mag = jnp.exp(lre * dt)
    ar = mag * jnp.cos(lim * dt)
    ai = mag * jnp.sin(lim * dt)
    den = lre * lre + lim * lim
    cr = ((ar - 1.0) * lre + ai * lim) / den
    ci = (ai * lre - (ar - 1.0) * lim) / den
    ar_ref[...] = ar
    ai_ref[...] = ai
    bbr_ref[...] = cr * bre_ref[...] - ci * bim_ref[...]
    bbi_ref[...] = cr * bim_ref[...] + ci * bre_ref[...]


def _s5_params(lam_re, lam_im, log_dt, b_re, b_im):
    g, n, p = b_re.shape
    rep = lambda a: jnp.repeat(a, p, axis=1)
    shp = jax.ShapeDtypeStruct((g, n * p), F32)
    ar, ai, bbr, bbi = pl.pallas_call(
        _s5_param_kernel, out_shape=(shp, shp, shp, shp),
    )(rep(lam_re), rep(lam_im), log_dt.reshape(g, 1), b_re.reshape(g, n * p), b_im.reshape(g, n * p))
    return ar[:, ::p], ai[:, ::p], bbr.reshape(g, n, p), bbi.reshape(g, n, p)


def _s5_scan_kernel(u_ref, a_ref, bbr_ref, bbi_ref, cr_ref, cin_ref, d_ref, o_ref,
                    bur, bui, hin_r, hin_i, car_r, car_i, apr, api):
    t, width = u_ref.shape
    sub = t // S5_SUBSEQ
    lanes = bur.shape[1]
    nblk = bbr_ref.shape[0]
    kin = width // nblk
    kst = lanes // nblk

    @pl.when(pl.program_id(0) == 0)
    def _init():
        car_r[...] = jnp.zeros_like(car_r)
        car_i[...] = jnp.zeros_like(car_i)
        ar = a_ref[0:1, :]
        ai = a_ref[1:2, :]
        pr, pi = ar, ai
        apr[0:1, :] = pr
        api[0:1, :] = pi
        for i in range(1, sub):
            pr, pi = pr * ar - pi * ai, pr * ai + pi * ar
            apr[i:i + 1, :] = pr
            api[i:i + 1, :] = pi

    for k in range(nblk):
        uk = u_ref[:, kin * k:kin * (k + 1)].astype(BF16)
        bur[:, kst * k:kst * (k + 1)] = jnp.dot(uk, bbr_ref[k], preferred_element_type=F32)
        bui[:, kst * k:kst * (k + 1)] = jnp.dot(uk, bbi_ref[k], preferred_element_type=F32)

    lb = S5_LANE_BLOCK
    for b in range(lanes // lb):
        sl = slice(b * lb, (b + 1) * lb)
        ar = jnp.broadcast_to(a_ref[0:1, sl], (S5_SUBSEQ, lb))
        ai = jnp.broadcast_to(a_ref[1:2, sl], (S5_SUBSEQ, lb))
        sr = jnp.zeros((S5_SUBSEQ, lb), F32)
        si = jnp.zeros((S5_SUBSEQ, lb), F32)
        for i in range(sub):
            rows = slice(S5_SUBSEQ * i, S5_SUBSEQ * (i + 1))
            sr, si = (ar * sr - ai * si + bur[rows, sl], ar * si + ai * sr + bui[rows, sl])
            bur[rows, sl] = sr
            bui[rows, sl] = si
        asr = apr[sub - 1:sub, sl]
        asi = api[sub - 1:sub, sl]
        hr = car_r[0:1, sl]
        hi = car_i[0:1, sl]
        for j in range(S5_SUBSEQ):
            hin_r[j:j + 1, sl] = hr
            hin_i[j:j + 1, sl] = hi
            er = sr[j:j + 1, :]
            ei = si[j:j + 1, :]
            hr, hi = er + asr * hr - asi * hi, ei + asr * hi + asi * hr
        car_r[0:1, sl] = hr
        car_i[0:1, sl] = hi
        hinr = hin_r[:, sl]
        hini = hin_i[:, sl]
        for i in range(sub):
            rows = slice(S5_SUBSEQ * i, S5_SUBSEQ * (i + 1))
            pr = jnp.broadcast_to(apr[i:i + 1, sl], (S5_SUBSEQ, lb))
            pi = jnp.broadcast_to(api[i:i + 1, sl], (S5_SUBSEQ, lb))
            bur[rows, sl] = bur[rows, sl] + pr * hinr - pi * hini
            bui[rows, sl] = bui[rows, sl] + pr * hini + pi * hinr

    for k in range(nblk):
        sr = bur[:, kst * k:kst * (k + 1)].astype(BF16)
        si = bui[:, kst * k:kst * (k + 1)].astype(BF16)
        y = (jnp.dot(sr, cr_ref[k], preferred_element_type=F32)
             + jnp.dot(si, cin_ref[k], preferred_element_type=F32))
        cols = slice(kin * k, kin * (k + 1))
        y = y + d_ref[:, cols] * u_ref[:, cols]
        o_ref[:, cols] = jax.nn.gelu(y).astype(o_ref.dtype)


def _s5_scan(u_p, ar, ai, bb_r, bb_i, c_re, c_im, d_skip):
    t_all, width = u_p.shape
    g, n, p = bb_r.shape
    bg = SSM_BLOCK_GROUPS
    nblk = g // bg
    eye = jnp.eye(bg, dtype=F32)
    blk_b = lambda bb: jnp.einsum('kgnp,gh->kgphn', bb.reshape(nblk, bg, n, p), eye
                                  ).reshape(nblk, bg * p, bg * n).astype(BF16)
    blk_c = lambda cc: jnp.einsum('kgpn,gh->kgnhp', cc.reshape(nblk, bg, p, n), eye
                                  ).reshape(nblk, bg * n, bg * p).astype(BF16)
    lanes = g * n
    a8 = jnp.zeros((8, lanes), F32).at[0].set(ar.reshape(lanes)).at[1].set(ai.reshape(lanes))
    t = S5_CHUNK
    sub = t // S5_SUBSEQ
    full3 = lambda c: (0, 0, 0)
    wspec_b = pl.BlockSpec((nblk, bg * p, bg * n), full3)
    wspec_c = pl.BlockSpec((nblk, bg * n, bg * p), full3)
    return pl.pallas_call(
        _s5_scan_kernel,
        grid=(t_all // t,),
        in_specs=[pl.BlockSpec((t, width), lambda c: (c, 0)),
                  pl.BlockSpec((8, lanes), lambda c: (0, 0)),
                  wspec_b, wspec_b, wspec_c, wspec_c,
                  pl.BlockSpec((1, width), lambda c: (0, 0))],
        out_specs=pl.BlockSpec((t, width), lambda c: (c, 0)),
        out_shape=jax.ShapeDtypeStruct((t_all, width), BF16),
        scratch_shapes=[pltpu.VMEM((t, lanes), F32), pltpu.VMEM((t, lanes), F32),
                        pltpu.VMEM((8, lanes), F32), pltpu.VMEM((8, lanes), F32),
                        pltpu.VMEM((8, lanes), F32), pltpu.VMEM((8, lanes), F32),
                        pltpu.VMEM((sub, lanes), F32), pltpu.VMEM((sub, lanes), F32)],
        compiler_params=_cparams(1),
    )(u_p, a8, blk_b(bb_r), blk_b(bb_i), blk_c(c_re), blk_c(-c_im), d_skip.reshape(1, width))


def _s5_rows_to_subseq(x, inverse=False):
    t_all, d = x.shape
    sub = S5_CHUNK // S5_SUBSEQ
    shape = (t_all // S5_CHUNK, sub, S5_SUBSEQ, d) if inverse else (t_all // S5_CHUNK, S5_SUBSEQ, sub, d)
    return x.reshape(shape).transpose(0, 2, 1, 3).reshape(t_all, d)


def _s5_layer(x, g, sh, sc, gate, w_in, lam_re, lam_im, log_dt, b_re, b_im, c_re, c_im, d_skip, w_out):
    d = x.shape[1]
    x_p = _s5_rows_to_subseq(x)
    ones = jnp.ones((1, w_in.shape[1]), F32)
    u_p = _norm_mod_matmul(x_p, g, sh, sc, w_in.astype(BF16), ones, F32)
    ar, ai, bb_r, bb_i = _s5_params(lam_re, lam_im, log_dt, b_re, b_im)
    gy_p = _s5_scan(u_p, ar, ai, bb_r, bb_i, c_re, c_im, d_skip)
    xn_p = _matmul_residual(gy_p, w_out.astype(BF16), x_p, gate, glu=True)
    return _s5_rows_to_subseq(xn_p, inverse=True)


def _cmp_kernel(x_ref, w1a_ref, w1b_ref, pos_ref, w1_ref, w2_ref, o_ref, acc_a, acc_b):
    l = pl.program_id(2)

    @pl.when(l == 0)
    def _():
        acc_a[...] = jnp.zeros_like(acc_a)
        acc_b[...] = jnp.zeros_like(acc_b)

    x = x_ref[...]
    acc_a[...] += jnp.dot(x, w1a_ref[0, 0], preferred_element_type=F32)
    acc_b[...] += jnp.dot(x, w1b_ref[0, 0], preferred_element_type=F32)

    @pl.when(l == pl.num_programs(2) - 1)
    def _():
        m = acc_a.shape[0]
        posc = jnp.zeros((8, NSA_HEAD_DIM), F32)
        for ll in range(CMP_LEN):
            prow = jnp.broadcast_to(pos_ref[0, ll:ll + 1, :], (8, NSA_HEAD_DIM)).astype(BF16)
            posc = posc + jnp.dot(prow, w1_ref[0, ll], preferred_element_type=F32)
        pre = acc_a[...] + pltpu.roll(acc_b[...], m - 1, 0) + posc[0:1, :]
        hid = jax.nn.gelu(pre)
        o_ref[0, 0] = jnp.dot(hid.astype(BF16), w2_ref[0], preferred_element_type=F32).astype(o_ref.dtype)


def _compress(qkv, kv_col0, pos, w1, w2):
    l_all, c = qkv.shape
    half = CMP_LEN // 2
    m = l_all // half
    cb = c // NSA_HEAD_DIM
    x2 = qkv.reshape(m, half * c)
    g = NSA_KV_GROUPS
    dh = NSA_HEAD_DIM
    return pl.pallas_call(
        _cmp_kernel,
        grid=(2, g, half),
        in_specs=[pl.BlockSpec((m, dh), lambda s, gg, l: (0, l * cb + kv_col0 + g * s + gg)),
                  pl.BlockSpec((1, 1, dh, dh), lambda s, gg, l: (s, l, 0, 0)),
                  pl.BlockSpec((1, 1, dh, dh), lambda s, gg, l: (s, l + half, 0, 0)),
                  pl.BlockSpec((1, CMP_LEN, dh), lambda s, gg, l: (s, 0, 0)),
                  pl.BlockSpec((1, CMP_LEN, dh, dh), lambda s, gg, l: (s, 0, 0, 0)),
                  pl.BlockSpec((1, dh, dh), lambda s, gg, l: (s, 0, 0))],
        out_specs=pl.BlockSpec((1, 1, m, dh), lambda s, gg, l: (s, gg, 0, 0)),
        out_shape=jax.ShapeDtypeStruct((2, g, m, dh), BF16),
        scratch_shapes=[pltpu.VMEM((m, dh), F32), pltpu.VMEM((m, dh), F32)],
        compiler_params=_cparams(3),
    )(x2, w1, w1, pos, w1, w2)


def _softmax2_rows(s, mask):
    s = jnp.where(mask, s, -1e30)
    m = jnp.max(s, axis=-1, keepdims=True)
    p = jnp.exp2(s - m)
    return p, 1.0 / jnp.maximum(jnp.sum(p, axis=-1, keepdims=True), 1e-30)


def _nsa_attn_kernel(q_ref, gt_ref, kc_ref, vc_ref, agg_ref, ksa_ref, vsa_ref, kw_ref, vw_ref,
                     o_ref, qaug_scr, s_a, s_b, acc_scr, m_scr):
    b = pl.program_id(1)
    qb = q_ref.shape[0]
    dh = NSA_HEAD_DIM
    rep = NSA_REP
    rows = rep * qb
    n_cmp = kc_ref.shape[2]
    n_sel = agg_ref.shape[1]
    t0 = b * qb

    q_blk = q_ref[...]
    qs = jnp.concatenate([q_blk[:, r * dh:(r + 1) * dh] for r in range(rep)], axis=0)
    tpos = t0 + lax.broadcasted_iota(jnp.int32, (rows, 1), 0) % qb

    kc = kc_ref[0, 0]
    s_c = lax.dot_general(qs, kc, NT_DIMS, preferred_element_type=F32)
    cmp_end = lax.broadcasted_iota(jnp.int32, (1, n_cmp), 1) * CMP_STRIDE + (CMP_LEN - 1)
    p_c, inv_c = _softmax2_rows(s_c, cmp_end <= tpos)
    p_c = p_c * jnp.where(tpos >= CMP_LEN - 1, inv_c, 0.0)
    o_c = jnp.dot(p_c.astype(BF16), vc_ref[0, 0], preferred_element_type=F32)

    psum = p_c[0:qb]
    for r in range(1, rep):
        psum = psum + p_c[r * qb:(r + 1) * qb]
    p_hi = psum.astype(BF16)
    p_lo = (psum - p_hi.astype(F32)).astype(BF16)
    agg = agg_ref[...]
    imp = (jnp.dot(p_hi, agg, preferred_element_type=F32)
           + jnp.dot(p_lo, agg, preferred_element_type=F32))
    imp_t = imp.T
    tq = t0 + lax.broadcasted_iota(jnp.int32, (1, qb), 1)
    blk = lax.broadcasted_iota(jnp.int32, (n_sel, qb), 0)
    blk_f = blk.astype(F32)
    cur = tq // SEL_LEN
    forced = (blk == 0) | (blk == cur) | (blk == cur - 1)
    valid = blk * SEL_LEN <= tq
    score = jnp.where(forced, 1e9, jnp.where(valid, imp_t, -1e9))
    pen_t = jnp.full((n_sel, qb), MASK_PENALTY, F32)
    for _ in range(min(SEL_TOPN, n_sel)):
        mx = jnp.max(score, axis=0, keepdims=True)
        idx = jnp.min(jnp.where(score == mx, blk_f, float(n_sel)), axis=0, keepdims=True)
        hit = blk_f == idx
        pen_t = jnp.where(hit, 0.0, pen_t)
        score = jnp.where(hit, -jnp.inf, score)
    pen = pen_t.T.astype(BF16)
    n_half = qaug_scr.shape[0]
    for hh in range(n_half):
        if n_sel >= ONEHOT_PERIOD:
            ph = pen[:, hh * ONEHOT_PERIOD:(hh + 1) * ONEHOT_PERIOD]
        else:
            ph = jnp.concatenate(
                [pen, jnp.full((qb, ONEHOT_PERIOD - n_sel), MASK_PENALTY, BF16)], axis=1)
        qaug_scr[hh] = jnp.concatenate([qs, jnp.concatenate([ph] * rep, axis=0)], axis=1)

    kt = SEL_KV_TILE
    period_keys = ONEHOT_PERIOD * SEL_LEN

    def sel_scores(i, s_out):
        k0 = pl.multiple_of(i * kt, kt)
        s_out[...] = lax.dot_general(qaug_scr[k0 // period_keys], ksa_ref[pl.ds(k0, kt), :], NT_DIMS,
                                     preferred_element_type=F32)

    def sel_update(i, s_in, causal):
        k0 = pl.multiple_of(i * kt, kt)
        s = s_in[...]
        if causal:
            kpos = k0 + lax.broadcasted_iota(jnp.int32, (1, kt), 1)
            s = jnp.where(kpos <= tpos, s, MASK_PENALTY)
        m_run = m_scr[...]
        m_new = jnp.maximum(m_run, jnp.max(s, axis=-1, keepdims=True))
        p = jnp.exp2(s - m_new)
        acc_scr[...] = (jnp.exp2(m_run - m_new) * acc_scr[...]
                        + jnp.dot(p.astype(BF16), vsa_ref[pl.ds(k0, kt), :], preferred_element_type=F32))
        m_scr[...] = m_new

    def sel_pair(j, carry):
        sel_scores(2 * j + 1, s_b)
        sel_update(2 * j, s_a, False)
        sel_scores(2 * j + 2, s_a)
        sel_update(2 * j + 1, s_b, False)
        return carry

    n_full = t0 // kt
    m_scr[...] = jnp.full(m_scr.shape, -1e30, F32)
    acc_scr[...] = jnp.zeros_like(acc_scr)
    sel_scores(0, s_a)
    lax.fori_loop(0, n_full // 2, sel_pair, 0)
    odd = n_full % 2 == 1

    @pl.when(odd)
    def _():
        sel_scores(n_full, s_b)
        sel_update(n_full - 1, s_a, False)
        sel_update(n_full, s_b, True)

    @pl.when(jnp.logical_not(odd))
    def _():
        sel_update(n_full, s_a, True)

    acc_s = acc_scr[...]
    o_s = acc_s[:, 0:dh] * (1.0 / acc_s[:, dh:dh + 1])

    wlen = WIN + qb
    w0 = pl.multiple_of(jnp.maximum(t0 - WIN, 0), qb)
    s_w = lax.dot_general(qs, kw_ref[pl.ds(w0, wlen), :], NT_DIMS, preferred_element_type=F32)
    diff = tpos - (w0 + lax.broadcasted_iota(jnp.int32, (1, wlen), 1))
    p_w, inv_w = _softmax2_rows(s_w, (diff >= 0) & (diff < WIN))
    o_w = jnp.dot((p_w * inv_w).astype(BF16), vw_ref[pl.ds(w0, wlen), :], preferred_element_type=F32)

    gt = gt_ref[...]
    for r in range(rep):
        rs = slice(r * qb, (r + 1) * qb)
        o = (gt[:, r:r + 1] * o_c[rs]
             + gt[:, rep + r:rep + r + 1] * o_s[rs]
             + gt[:, 2 * rep + r:2 * rep + r + 1] * o_w[rs])
        o_ref[:, r * dh:(r + 1) * dh] = o.astype(o_ref.dtype)


def _nsa_agg(n_cmp_pad, n_cmp, n_sel):
    ratio, span = SEL_LEN // CMP_STRIDE, CMP_LEN // CMP_STRIDE
    agg = np.zeros((n_cmp_pad, n_sel), np.float32)
    jj = np.arange(n_sel)
    for m in range(ratio):
        for n in range(span):
            ii = ratio * jj + m - n
            ok = (ii >= 0) & (ii < n_cmp)
            agg[ii[ok], jj[ok]] += 1.0
    return jnp.asarray(agg, dtype=BF16)


def _nsa_attention(qkv, gates, kv_cmp):
    l_all = qkv.shape[0]
    dh = NSA_HEAD_DIM
    g = NSA_KV_GROUPS
    qb = Q_BLOCK
    qw = NSA_REP * dh
    n_cmp_pad = l_all // CMP_STRIDE
    n_cmp = (l_all - CMP_LEN) // CMP_STRIDE + 1
    n_sel = l_all // SEL_LEN
    agg = _nsa_agg(n_cmp_pad, n_cmp, n_sel)
    n_half = max(n_sel // ONEHOT_PERIOD, 1)
    q_blocks = (NSA_HEADS * dh) // dh
    key = np.arange(l_all)
    onehot = jnp.asarray((key[:, None] // SEL_LEN) % ONEHOT_PERIOD == np.arange(ONEHOT_PERIOD)[None, :],
                         dtype=BF16)
    ones_col = jnp.asarray(np.arange(dh)[None, :] == 0, dtype=BF16)
    grouped = lambda base: qkv[:, (q_blocks + base) * dh:(q_blocks + base + g) * dh].reshape(l_all, g, dh)
    ks_aug = jnp.concatenate([grouped(2 * g), jnp.broadcast_to(onehot[:, None, :], (l_all, g, ONEHOT_PERIOD))],
                             axis=-1).reshape(l_all, g * (dh + ONEHOT_PERIOD))
    vs_aug = jnp.concatenate([grouped(3 * g), jnp.broadcast_to(ones_col[:, None, :], (l_all, g, dh))],
                             axis=-1).reshape(l_all, g * 2 * dh)
    once = dict(pipeline_mode=pl.Buffered(1))
    kv_spec = lambda base: pl.BlockSpec((l_all, dh), lambda gg, b: (0, q_blocks + base + gg), **once)
    aug_spec = pl.BlockSpec((l_all, 2 * dh), lambda gg, b: (0, gg), **once)
    rows = NSA_REP * qb
    return pl.pallas_call(
        _nsa_attn_kernel,
        grid=(g, l_all // qb),
        in_specs=[pl.BlockSpec((qb, qw), lambda gg, b: (b, gg)),
                  pl.BlockSpec((qb, dh), lambda gg, b: (b, gg)),
                  pl.BlockSpec((1, 1, n_cmp_pad, dh), lambda gg, b: (0, gg, 0, 0)),
                  pl.BlockSpec((1, 1, n_cmp_pad, dh), lambda gg, b: (1, gg, 0, 0)),
                  pl.BlockSpec((n_cmp_pad, n_sel), lambda gg, b: (0, 0)),
                  aug_spec, aug_spec, kv_spec(4 * g), kv_spec(5 * g)],
        out_specs=pl.BlockSpec((qb, qw), lambda gg, b: (b, gg)),
        out_shape=jax.ShapeDtypeStruct((l_all, NSA_HEADS * dh), BF16),
        scratch_shapes=[pltpu.VMEM((n_half, rows, dh + ONEHOT_PERIOD), BF16),
                        pltpu.VMEM((rows, SEL_KV_TILE), F32), pltpu.VMEM((rows, SEL_KV_TILE), F32),
                        pltpu.VMEM((rows, 2 * dh), F32), pltpu.VMEM((rows, 1), F32)],
        compiler_params=_cparams(2),
    )(qkv, gates, kv_cmp, kv_cmp, agg, ks_aug, vs_aug, qkv, qkv)


def _nsa_layer(x, g, sh, sc, gate, w_in, k_pos, k_w1, k_w2, v_pos, v_w1, v_w2, w_out):
    dh = NSA_HEAD_DIM
    qd = NSA_HEADS * dh
    kvd = 2 * N_BRANCH * NSA_KV_GROUPS * dh
    w_main = w_in[:, :qd + kvd].astype(BF16)
    colscale = jnp.concatenate([jnp.full((1, qd), dh ** -0.5 * math.log2(math.e), F32),
                                jnp.ones((1, kvd), F32)], axis=1)
    qkv = _norm_mod_matmul(x, g, sh, sc, w_main, colscale, BF16)
    wg = w_in[:, qd + kvd:].reshape(-1, NSA_KV_GROUPS, NSA_REP, N_BRANCH).transpose(0, 1, 3, 2)
    wg = wg.reshape(-1, NSA_KV_GROUPS, N_BRANCH * NSA_REP)
    wg = jnp.pad(wg, ((0, 0), (0, 0), (0, dh - N_BRANCH * NSA_REP))).reshape(-1, NSA_KV_GROUPS * dh)
    gates = _norm_mod_matmul(x, g, sh, sc, wg.astype(BF16), jnp.ones((1, wg.shape[1]), F32), F32,
                             act="sigmoid")
    kv_cmp = _compress(qkv, qd // dh, jnp.stack([k_pos, v_pos]),
                       jnp.stack([k_w1, v_w1]).astype(BF16), jnp.stack([k_w2, v_w2]).astype(BF16))
    o = _nsa_attention(qkv, gates, kv_cmp)
    return _matmul_residual(o, w_out.astype(BF16), x, gate, glu=False)


def _peer_query_kernel(x_ref, g_ref, sh_ref, sc_ref, wq_ref, keys_ref, ht_ref, st_ref):
    h32 = _norm_mod(x_ref[...], g_ref[...], sh_ref[...], sc_ref[...])
    ht_ref[...] = h32.T.astype(BF16)
    q = jnp.dot(h32.astype(BF16), wq_ref[...], preferred_element_type=F32).astype(BF16)
    for hc in range(keys_ref.shape[0]):
        rows = slice(hc * PEER_NKEYS, (hc + 1) * PEER_NKEYS)
        st_ref[rows, :] = lax.dot_general(keys_ref[hc], q[:, hc * PEER_HALF:(hc + 1) * PEER_HALF],
                                          NT_DIMS, preferred_element_type=F32)


def _peer_query(x, g, sh, sc, w_q, sub_keys, tm=512):
    t, d = x.shape
    nq = w_q.shape[1]
    keys = sub_keys.reshape(-1, PEER_NKEYS, PEER_HALF).astype(BF16)
    row = lambda i: (0, 0)
    return pl.pallas_call(
        _peer_query_kernel,
        grid=(t // tm,),
        in_specs=[pl.BlockSpec((tm, d), lambda i: (i, 0)),
                  pl.BlockSpec((1, d), row), pl.BlockSpec((1, d), row), pl.BlockSpec((1, d), row),
                  pl.BlockSpec((d, nq), row),
                  pl.BlockSpec(keys.shape, lambda i: (0, 0, 0))],
        out_specs=[pl.BlockSpec((d, tm), lambda i: (0, i)),
                   pl.BlockSpec((keys.shape[0] * PEER_NKEYS, tm), lambda i: (0, i))],
        out_shape=[jax.ShapeDtypeStruct((d, t), BF16),
                   jax.ShapeDtypeStruct((keys.shape[0] * PEER_NKEYS, t), F32)],
        compiler_params=_cparams(1),
    )(x, g, sh, sc, w_q.astype(BF16), keys)


def _peer_cells():
    return [(a, b) for a in range(PEER_TOPK) for b in range(PEER_TOPK) if (a + 1) * (b + 1) <= PEER_TOPK]


def _top_ranks(s, iota_k):
    nk = s.shape[0]
    rank = jnp.full(s.shape, RANK_NONE, F32)
    vals = []
    cur = s
    for a in range(PEER_TOPK):
        v = jnp.max(cur, axis=0, keepdims=True)
        idx = jnp.min(jnp.where(cur == v, iota_k, nk), axis=0, keepdims=True)
        hit = iota_k == idx
        rank = jnp.where(hit, float(a), rank)
        cur = jnp.where(hit, -jnp.inf, cur)
        vals.append(v)
    return vals, rank


def _peer_route_kernel(st_ref, seg_ref, r2_ref, ln_ref, g1_ref, g2_ref):
    tn = st_ref.shape[1]
    nk = PEER_NKEYS
    cells = _peer_cells()
    n_cell = len(cells)
    n_pad = -(-n_cell // 8) * 8
    n_seg = seg_ref.shape[1]
    iota_k = lax.broadcasted_iota(jnp.int32, (nk, tn), 0)
    iota_c = lax.broadcasted_iota(jnp.int32, (n_pad, tn), 0)
    for h in range(PEER_HEADS):
        s1 = st_ref[(2 * h) * nk:(2 * h + 1) * nk, :]
        s2 = st_ref[(2 * h + 1) * nk:(2 * h + 2) * nk, :]
        v1, rank1 = _top_ranks(s1, iota_k)
        v2, rank2 = _top_ranks(s2, iota_k)
        cand = jnp.concatenate([v1[a] + v2[b] for a, b in cells]
                               + [jnp.full((n_pad - n_cell, tn), -jnp.inf, F32)], axis=0)
        top = v1[0] + v2[0]
        e_c = jnp.exp(cand - top)
        chosen = jnp.zeros((n_pad, tn), jnp.bool_)
        cur = cand
        for _ in range(PEER_TOPK):
            v = jnp.max(cur, axis=0, keepdims=True)
            idx = jnp.min(jnp.where(cur == v, iota_c, n_pad), axis=0, keepdims=True)
            hit = iota_c == idx
            chosen = chosen | hit
            cur = jnp.where(hit, -jnp.inf, cur)
        chosen_f = chosen.astype(F32)
        z = jnp.sum(chosen_f * e_c, axis=0, keepdims=True)
        chosen_pad = jnp.concatenate([chosen_f, jnp.zeros((n_seg - n_pad, tn), F32)], axis=0)
        rowlen = jnp.dot(seg_ref[...], chosen_pad.astype(BF16), preferred_element_type=F32)
        ln = jnp.zeros((nk, tn), F32)
        for a in range(PEER_TOPK):
            ln = jnp.where(rank1 == float(a), rowlen[a:a + 1, :], ln)
        rows = slice(h * nk, (h + 1) * nk)
        r2_ref[rows, :] = rank2.astype(r2_ref.dtype)
        ln_ref[h] = ln
        g1_ref[h] = jnp.exp(s1 - v1[0]) / z
        g2_ref[rows, :] = jnp.exp(s2 - v2[0]).astype(g2_ref.dtype)


def _peer_route(st, tn=256):
    n_rows, t = st.shape
    cells = _peer_cells()
    seg = np.zeros((PEER_TOPK, PEER_NKEYS), np.float32)
    for c, (a, _) in enumerate(cells):
        seg[a, c] = 1.0
    out_rows = PEER_HEADS * PEER_NKEYS
    shp = jax.ShapeDtypeStruct((out_rows, t), BF16)
    shp_row = jax.ShapeDtypeStruct((PEER_HEADS, PEER_NKEYS, t), F32)
    spec = pl.BlockSpec((out_rows, tn), lambda i: (0, i))
    spec_row = pl.BlockSpec((PEER_HEADS, PEER_NKEYS, tn), lambda i: (0, 0, i))
    return pl.pallas_call(
        _peer_route_kernel,
        grid=(t // tn,),
        in_specs=[pl.BlockSpec((n_rows, tn), lambda i: (0, i)),
                  pl.BlockSpec((PEER_TOPK, PEER_NKEYS), lambda i: (0, 0))],
        out_specs=[spec, spec_row, spec_row, spec],
        out_shape=[shp, shp_row, shp_row, shp],
        compiler_params=_cparams(1),
    )(st, jnp.asarray(seg, dtype=BF16))


def _peer_expert_kernel(h_ref, u_ref, vt_ref, r2_ref, ln_ref, g1_ref, g2_ref, x_ref, gate_ref, nf_ref,
                        o_ref, acc_t, p_scr, *, final_norm):
    c = pl.program_id(1)
    ec = u_ref.shape[0]
    nk = PEER_NKEYS
    sub = PEER_SUBCHUNK
    tm = h_ref.shape[1]

    @pl.when(c == 0)
    def _():
        acc_t[...] = jnp.zeros_like(acc_t)

    pre = [jnp.dot(u_ref[s * sub:(s + 1) * sub, :], h_ref[...], preferred_element_type=F32)
           for s in range(ec // sub)]
    zero = jnp.zeros((), BF16)
    y = None
    for s in range(ec // sub):
        act = jax.nn.gelu(pre[s])
        for ii in range(sub // nk):
            i_loc = s * (sub // nk) + ii
            w = jnp.zeros((nk, tm), BF16)
            for h in range(PEER_HEADS):
                rows = slice(h * nk, (h + 1) * nk)
                ln_row = ln_ref[h, i_loc:i_loc + 1, :].astype(BF16)
                g1_row = g1_ref[h, i_loc:i_loc + 1, :].astype(BF16)
                w = w + jnp.where(r2_ref[rows, :] < ln_row, g2_ref[rows, :], zero) * g1_row
            p_scr[i_loc * nk:(i_loc + 1) * nk, :] = w * act[ii * nk:(ii + 1) * nk, :].astype(BF16)
        ys = jnp.dot(vt_ref[:, s * sub:(s + 1) * sub], p_scr[s * sub:(s + 1) * sub, :],
                     preferred_element_type=F32)
        y = ys if y is None else y + ys
    acc_t[...] += y

    @pl.when(c == pl.num_programs(1) - 1)
    def _():
        xo = x_ref[...] + gate_ref[...] * acc_t[...].T
        if final_norm:
            ms = jnp.mean(xo * xo, axis=-1, keepdims=True)
            xo = (xo * lax.rsqrt(ms + RMS_EPS)) * nf_ref[...]
        o_ref[...] = xo


def _peer_experts(ht, u_bf, vt_bf, route, x, gate, norm_final, final_norm, tm=512, ec=1024):
    t, d = x.shape
    e = u_bf.shape[0]
    r2, ln, g1, g2 = route
    rspec = pl.BlockSpec((r2.shape[0], tm), lambda i, c: (0, i))
    kspec = pl.BlockSpec((PEER_HEADS, ec // PEER_NKEYS, tm), lambda i, c: (0, c, i))
    row = lambda i, c: (0, 0)
    return pl.pallas_call(
        functools.partial(_peer_expert_kernel, final_norm=final_norm),
        grid=(t // tm, e // ec),
        in_specs=[pl.BlockSpec((d, tm), lambda i, c: (0, i)),
                  pl.BlockSpec((ec, d), lambda i, c: (c, 0)),
                  pl.BlockSpec((d, ec), lambda i, c: (0, c)),
                  rspec, kspec, kspec, rspec,
                  pl.BlockSpec((tm, d), lambda i, c: (i, 0)),
                  pl.BlockSpec((1, d), row), pl.BlockSpec((1, d), row)],
        out_specs=pl.BlockSpec((tm, d), lambda i, c: (i, 0)),
        out_shape=jax.ShapeDtypeStruct((t, d), F32),
        scratch_shapes=[pltpu.VMEM((d, tm), F32), pltpu.VMEM((ec, tm), BF16)],
        compiler_params=_cparams(2),
    )(ht, u_bf, vt_bf, r2, ln, g1, g2, x, gate, norm_final)


def _peer_layer(x, g, sh, sc, gate, w_q, sub_keys, u_tab, v_tab, norm_final, final_norm):
    ht, st = _peer_query(x, g, sh, sc, w_q, sub_keys)
    route = _peer_route(st)
    return _peer_experts(ht, u_tab.astype(BF16), v_tab.astype(BF16).T, route, x, gate,
                         norm_final, final_norm)


def kernel(x, c, ada_w, ada_b, norm_mix, norm_ffn, norm_final, ssm_w_in, ssm_lambda_re, ssm_lambda_im, ssm_log_dt, ssm_b_re, ssm_b_im, ssm_c_re, ssm_c_im, ssm_d, ssm_w_out, nsa_w_in, nsa_cmp_k_pos, nsa_cmp_k_w1, nsa_cmp_k_w2, nsa_cmp_v_pos, nsa_cmp_v_w1, nsa_cmp_v_w2, nsa_w_out, peer_w_q, peer_sub_keys, peer_u, peer_v):
    bsz, l_all, d = x.shape
    assert bsz == 1
    depth = ada_w.shape[0]
    mod = _adaln(c, ada_w, ada_b)
    xt = x.reshape(l_all, d)
    nf = norm_final.reshape(1, d)
    for i in range(depth):
        sh1, sc1, g1, sh2, sc2, g2 = [mod[i, :, k * d:(k + 1) * d] for k in range(6)]
        j = i // 2
        gm = norm_mix[i].reshape(1, d)
        if i % 2 == 0:
            xt = _s5_layer(xt, gm, sh1, sc1, g1, ssm_w_in[j], ssm_lambda_re[j], ssm_lambda_im[j],
                           ssm_log_dt[j], ssm_b_re[j], ssm_b_im[j], ssm_c_re[j], ssm_c_im[j],
                           ssm_d[j], ssm_w_out[j])
        else:
            xt = _nsa_layer(xt, gm, sh1, sc1, g1, nsa_w_in[j], nsa_cmp_k_pos[j], nsa_cmp_k_w1[j],
                            nsa_cmp_k_w2[j], nsa_cmp_v_pos[j], nsa_cmp_v_w1[j], nsa_cmp_v_w2[j],
                            nsa_w_out[j])
        xt = _peer_layer(xt, norm_ffn[i].reshape(1, d), sh2, sc2, g2, peer_w_q[i], peer_sub_keys[i],
                         peer_u[i], peer_v[i], nf, final_norm=(i == depth - 1))
    return xt.reshape(bsz, l_all, d)
```

```python
import functools
import math

import numpy as np
import jax
import jax.numpy as jnp
from jax import lax
from jax.experimental import pallas as pl
from jax.experimental.pallas import tpu as pltpu

F32 = jnp.float32
BF16 = jnp.bfloat16

RMS_EPS = 1e-6

SSM_GROUP = 16
SSM_STATE = 64
SSM_BLOCK_GROUPS = 8
S5_SUBSEQ = 8
S5_CHUNK = 128
S5_LANE_BLOCK = 1024

NSA_HEADS = 16
NSA_HEAD_DIM = 128
NSA_KV_GROUPS = 4
NSA_REP = NSA_HEADS // NSA_KV_GROUPS
N_BRANCH = 3
CMP_LEN = 32
CMP_STRIDE = 16
SEL_LEN = 64
SEL_TOPN = 16
WIN = 512
Q_BLOCK = 128
SEL_KV_TILE = 512
MASK_PENALTY = -(2.0 ** 30)
ONEHOT_PERIOD = 128

PEER_HEADS = 8
PEER_NKEYS = 128
PEER_TOPK = 16
PEER_HALF = 128
PEER_EXPERT_CHUNK = 1024
PEER_LAST_IS_EVEN = (PEER_NKEYS ** 2 // PEER_EXPERT_CHUNK) % 2 == 0
RANK_NONE = 255.0

VMEM_LIMIT = 56 * 1024 * 1024
NT_DIMS = (((1,), (1,)), ((), ()))


def _cparams(n_axes):
    return pltpu.CompilerParams(
        dimension_semantics=("arbitrary",) * n_axes, vmem_limit_bytes=VMEM_LIMIT)


def _norm_mod(x, g, sh, sc):
    ms = jnp.mean(x * x, axis=-1, keepdims=True)
    xn = x * lax.rsqrt(ms + RMS_EPS)
    return (xn * g) * (1.0 + sc) + sh


def _adaln_kernel(c_ref, w_ref, b_ref, o_ref):
    c = c_ref[...]
    cond = c * jax.nn.sigmoid(c)
    o_ref[0] = jnp.dot(cond.astype(BF16), w_ref[0].astype(BF16),
                       preferred_element_type=F32) + b_ref[0]


def _adaln(c, ada_w, ada_b):
    depth, d, n = ada_w.shape
    tn = 1024
    c8 = jnp.broadcast_to(c, (8, d))
    out = pl.pallas_call(
        _adaln_kernel,
        grid=(depth, n // tn),
        in_specs=[pl.BlockSpec((8, d), lambda i, j: (0, 0)),
                  pl.BlockSpec((1, d, tn), lambda i, j: (i, 0, j)),
                  pl.BlockSpec((1, 1, tn), lambda i, j: (i, 0, j))],
        out_specs=pl.BlockSpec((1, 8, tn), lambda i, j: (i, 0, j)),
        out_shape=jax.ShapeDtypeStruct((depth, 8, n), F32),
        compiler_params=_cparams(2),
    )(c8, ada_w, ada_b.reshape(depth, 1, n))
    return out[:, 0:1, :]


def _nmm_kernel(x_ref, g_ref, sh_ref, sc_ref, w_ref, cs_ref, o_ref, h_scr, *, act):
    @pl.when(pl.program_id(1) == 0)
    def _():
        h_scr[...] = _norm_mod(x_ref[...], g_ref[...], sh_ref[...], sc_ref[...]).astype(BF16)

    acc = jnp.dot(h_scr[...], w_ref[...], preferred_element_type=F32)
    if act == "sigmoid":
        acc = jax.nn.sigmoid(acc)
    else:
        acc = acc * cs_ref[...]
    o_ref[...] = acc.astype(o_ref.dtype)


def _norm_mod_matmul(x, g, sh, sc, w, colscale, out_dtype, act="scale", tm=512, tn=1024):
    t, d = x.shape
    n = w.shape[1]
    tn = min(tn, n)
    row = lambda i, j: (0, 0)
    return pl.pallas_call(
        functools.partial(_nmm_kernel, act=act),
        grid=(t // tm, n // tn),
        in_specs=[pl.BlockSpec((tm, d), lambda i, j: (i, 0)),
                  pl.BlockSpec((1, d), row), pl.BlockSpec((1, d), row), pl.BlockSpec((1, d), row),
                  pl.BlockSpec((d, tn), lambda i, j: (0, j)),
                  pl.BlockSpec((1, tn), lambda i, j: (0, j))],
        out_specs=pl.BlockSpec((tm, tn), lambda i, j: (i, j)),
        out_shape=jax.ShapeDtypeStruct((t, n), out_dtype),
        scratch_shapes=[pltpu.VMEM((tm, d), BF16)],
        compiler_params=_cparams(2),
    )(x, g, sh, sc, w, colscale)


def _mm_res_kernel(a_ref, w_ref, x_ref, g_ref, o_ref):
    y = jnp.dot(a_ref[...], w_ref[...], preferred_element_type=F32)
    o_ref[...] = x_ref[...] + g_ref[...] * y


def _mm_glu_res_kernel(a_ref, wa_ref, wb_ref, x_ref, g_ref, o_ref):
    a = a_ref[...]
    ya = jnp.dot(a, wa_ref[...], preferred_element_type=F32)
    yb = jnp.dot(a, wb_ref[...], preferred_element_type=F32)
    o_ref[...] = x_ref[...] + g_ref[...] * (ya * jax.nn.sigmoid(yb))


def _matmul_residual(a, w, x, gate, glu, tm=512, tn=1024):
    t, k = a.shape
    n = x.shape[1]
    nb = n // tn
    a_spec = pl.BlockSpec((tm, k), lambda i, j: (i, 0))
    w_spec = pl.BlockSpec((k, tn), lambda i, j: (0, j))
    tail = [pl.BlockSpec((tm, tn), lambda i, j: (i, j)), pl.BlockSpec((1, tn), lambda i, j: (0, j))]
    if glu:
        kern = _mm_glu_res_kernel
        in_specs = [a_spec, w_spec, pl.BlockSpec((k, tn), lambda i, j: (0, j + nb))] + tail
        args = (a, w, w, x, gate)
    else:
        kern = _mm_res_kernel
        in_specs = [a_spec, w_spec] + tail
        args = (a, w, x, gate)
    return pl.pallas_call(
        kern,
        grid=(t // tm, nb),
        in_specs=in_specs,
        out_specs=pl.BlockSpec((tm, tn), lambda i, j: (i, j)),
        out_shape=jax.ShapeDtypeStruct((t, n), F32),
        compiler_params=_cparams(2),
    )(*args)


def _s5_param_kernel(lre_ref, lim_ref, ldt_ref, bre_ref, bim_ref, ar_ref, ai_ref, bbr_ref, bbi_ref):
    lre = lre_ref[...]
    lim = lim_ref[...]
    dt = jnp.exp(ldt_ref[...])
    mag = jnp.exp(lre * dt)
    ar = mag * jnp.cos(lim * dt)
    ai = mag * jnp.sin(lim * dt)
    den = lre * lre + lim * lim
    cr = ((ar - 1.0) * lre + ai * lim) / den
    ci = (ai * lre - (ar - 1.0) * lim) / den
    ar_ref[...] = ar
    ai_ref[...] = ai
    bbr_ref[...] = cr * bre_ref[...] - ci * bim_ref[...]
    bbi_ref[...] = cr * bim_ref[...] + ci * bre_ref[...]


def _s5_params(lam_re, lam_im, log_dt, b_re, b_im):
    g, n, p = b_re.shape
    rep = lambda a: jnp.repeat(a, p, axis=1)
    shp = jax.ShapeDtypeStruct((g, n * p), F32)
    ar, ai, bbr, bbi = pl.pallas_call(
        _s5_param_kernel, out_shape=(shp, shp, shp, shp),
    )(rep(lam_re), rep(lam_im), log_dt.reshape(g, 1), b_re.reshape(g, n * p), b_im.reshape(g, n * p))
    return ar[:, ::p], ai[:, ::p], bbr.reshape(g, n, p), bbi.reshape(g, n, p)


def _s5_scan_kernel(u_ref, a_ref, bbr_ref, bbi_ref, cr_ref, cin_ref, d_ref, o_ref,
                    bur, bui, hin_r, hin_i, car_r, car_i, apr, api):
    t, width = u_ref.shape
    sub = t // S5_SUBSEQ
    lanes = bur.shape[1]
    nblk = bbr_ref.shape[0]
    kin = width // nblk
    kst = lanes // nblk

    @pl.when(pl.program_id(0) == 0)
    def _init():
        car_r[...] = jnp.zeros_like(car_r)
        car_i[...] = jnp.zeros_like(car_i)
        ar = a_ref[0:1, :]
        ai = a_ref[1:2, :]
        pr, pi = ar, ai
        apr[0:1, :] = pr
        api[0:1, :] = pi
        for i in range(1, sub):
            pr, pi = pr * ar - pi * ai, pr * ai + pi * ar
            apr[i:i + 1, :] = pr
            api[i:i + 1, :] = pi

    for k in range(nblk):
        uk = u_ref[:, kin * k:kin * (k + 1)].astype(BF16)
        bur[:, kst * k:kst * (k + 1)] = jnp.dot(uk, bbr_ref[k], preferred_element_type=F32)
        bui[:, kst * k:kst * (k + 1)] = jnp.dot(uk, bbi_ref[k], preferred_element_type=F32)

    lb = S5_LANE_BLOCK
    for b in range(lanes // lb):
        sl = slice(b * lb, (b + 1) * lb)
        ar = jnp.broadcast_to(a_ref[0:1, sl], (S5_SUBSEQ, lb))
        ai = jnp.broadcast_to(a_ref[1:2, sl], (S5_SUBSEQ, lb))
        sr = jnp.zeros((S5_SUBSEQ, lb), F32)
        si = jnp.zeros((S5_SUBSEQ, lb), F32)
        for i in range(sub):
            rows = slice(S5_SUBSEQ * i, S5_SUBSEQ * (i + 1))
            sr, si = (ar * sr - ai * si + bur[rows, sl], ar * si + ai * sr + bui[rows, sl])
            bur[rows, sl] = sr
            bui[rows, sl] = si
        asr = apr[sub - 1:sub, sl]
        asi = api[sub - 1:sub, sl]
        hr = car_r[0:1, sl]
        hi = car_i[0:1, sl]
        for j in range(S5_SUBSEQ):
            hin_r[j:j + 1, sl] = hr
            hin_i[j:j + 1, sl] = hi
            er = sr[j:j + 1, :]
            ei = si[j:j + 1, :]
            hr, hi = er + asr * hr - asi * hi, ei + asr * hi + asi * hr
        car_r[0:1, sl] = hr
        car_i[0:1, sl] = hi
        hinr = hin_r[:, sl]
        hini = hin_i[:, sl]
        for i in range(sub):
            rows = slice(S5_SUBSEQ * i, S5_SUBSEQ * (i + 1))
            pr = jnp.broadcast_to(apr[i:i + 1, sl], (S5_SUBSEQ, lb))
            pi = jnp.broadcast_to(api[i:i + 1, sl], (S5_SUBSEQ, lb))
            bur[rows, sl] = bur[rows, sl] + pr * hinr - pi * hini
            bui[rows, sl] = bui[rows, sl] + pr * hini + pi * hinr

    for k in range(nblk):
        sr = bur[:, kst * k:kst * (k + 1)].astype(BF16)
        si = bui[:, kst * k:kst * (k + 1)].astype(BF16)
        y = (jnp.dot(sr, cr_ref[k], preferred_element_type=F32)
             + jnp.dot(si, cin_ref[k], preferred_element_type=F32))
        cols = slice(kin * k, kin * (k + 1))
        y = y + d_ref[:, cols] * u_ref[:, cols]
        o_ref[:, cols] = jax.nn.gelu(y).astype(o_ref.dtype)


def _s5_scan(u_p, ar, ai, bb_r, bb_i, c_re, c_im, d_skip):
    t_all, width = u_p.shape
    g, n, p = bb_r.shape
    bg = SSM_BLOCK_GROUPS
    nblk = g // bg
    eye = jnp.eye(bg, dtype=F32)
    blk_b = lambda bb: jnp.einsum('kgnp,gh->kgphn', bb.reshape(nblk, bg, n, p), eye
                                  ).reshape(nblk, bg * p, bg * n).astype(BF16)
    blk_c = lambda cc: jnp.einsum('kgpn,gh->kgnhp', cc.reshape(nblk, bg, p, n), eye
                                  ).reshape(nblk, bg * n, bg * p).astype(BF16)
    lanes = g * n
    a8 = jnp.zeros((8, lanes), F32).at[0].set(ar.reshape(lanes)).at[1].set(ai.reshape(lanes))
    t = S5_CHUNK
    sub = t // S5_SUBSEQ
    full3 = lambda c: (0, 0, 0)
    wspec_b = pl.BlockSpec((nblk, bg * p, bg * n), full3)
    wspec_c = pl.BlockSpec((nblk, bg * n, bg * p), full3)
    return pl.pallas_call(
        _s5_scan_kernel,
        grid=(t_all // t,),
        in_specs=[pl.BlockSpec((t, width), lambda c: (c, 0)),
                  pl.BlockSpec((8, lanes), lambda c: (0, 0)),
                  wspec_b, wspec_b, wspec_c, wspec_c,
                  pl.BlockSpec((1, width), lambda c: (0, 0))],
        out_specs=pl.BlockSpec((t, width), lambda c: (c, 0)),
        out_shape=jax.ShapeDtypeStruct((t_all, width), BF16),
        scratch_shapes=[pltpu.VMEM((t, lanes), F32), pltpu.VMEM((t, lanes), F32),
                        pltpu.VMEM((8, lanes), F32), pltpu.VMEM((8, lanes), F32),
                        pltpu.VMEM((8, lanes), F32), pltpu.VMEM((8, lanes), F32),
                        pltpu.VMEM((sub, lanes), F32), pltpu.VMEM((sub, lanes), F32)],
        compiler_params=_cparams(1),
    )(u_p, a8, blk_b(bb_r), blk_b(bb_i), blk_c(c_re), blk_c(-c_im), d_skip.reshape(1, width))


def _s5_rows_to_subseq(x, inverse=False):
    t_all, d = x.shape
    sub = S5_CHUNK // S5_SUBSEQ
    shape = (t_all // S5_CHUNK, sub, S5_SUBSEQ, d) if inverse else (t_all // S5_CHUNK, S5_SUBSEQ, sub, d)
    return x.reshape(shape).transpose(0, 2, 1, 3).reshape(t_all, d)


def _s5_layer(x, g, sh, sc, gate, w_in, lam_re, lam_im, log_dt, b_re, b_im, c_re, c_im, d_skip, w_out):
    d = x.shape[1]
    x_p = _s5_rows_to_subseq(x)
    ones = jnp.ones((1, w_in.shape[1]), F32)
    u_p = _norm_mod_matmul(x_p, g, sh, sc, w_in.astype(BF16), ones, F32)
    ar, ai, bb_r, bb_i = _s5_params(lam_re, lam_im, log_dt, b_re, b_im)
    gy_p = _s5_scan(u_p, ar, ai, bb_r, bb_i, c_re, c_im, d_skip)
    xn_p = _matmul_residual(gy_p, w_out.astype(BF16), x_p, gate, glu=True)
    return _s5_rows_to_subseq(xn_p, inverse=True)


def _cmp_kernel(x_ref, w1a_ref, w1b_ref, pos_ref, w1_ref, w2_ref, o_ref, acc_a, acc_b):
    l = pl.program_id(2)

    @pl.when(l == 0)
    def _():
        acc_a[...] = jnp.zeros_like(acc_a)
        acc_b[...] = jnp.zeros_like(acc_b)

    x = x_ref[...]
    acc_a[...] += jnp.dot(x, w1a_ref[0, 0], preferred_element_type=F32)
    acc_b[...] += jnp.dot(x, w1b_ref[0, 0], preferred_element_type=F32)

    @pl.when(l == pl.num_programs(2) - 1)
    def _():
        m = acc_a.shape[0]
        posc = jnp.zeros((8, NSA_HEAD_DIM), F32)
        for ll in range(CMP_LEN):
            prow = jnp.broadcast_to(pos_ref[0, ll:ll + 1, :], (8, NSA_HEAD_DIM)).astype(BF16)
            posc = posc + jnp.dot(prow, w1_ref[0, ll], preferred_element_type=F32)
        pre = acc_a[...] + pltpu.roll(acc_b[...], m - 1, 0) + posc[0:1, :]
        hid = jax.nn.gelu(pre)
        o_ref[0, 0] = jnp.dot(hid.astype(BF16), w2_ref[0], preferred_element_type=F32).astype(o_ref.dtype)


def _compress(qkv, kv_col0, pos, w1, w2):
    l_all, c = qkv.shape
    half = CMP_LEN // 2
    m = l_all // half
    cb = c // NSA_HEAD_DIM
    x2 = qkv.reshape(m, half * c)
    g = NSA_KV_GROUPS
    dh = NSA_HEAD_DIM
    return pl.pallas_call(
        _cmp_kernel,
        grid=(2, g, half),
        in_specs=[pl.BlockSpec((m, dh), lambda s, gg, l: (0, l * cb + kv_col0 + g * s + gg)),
                  pl.BlockSpec((1, 1, dh, dh), lambda s, gg, l: (s, l, 0, 0)),
                  pl.BlockSpec((1, 1, dh, dh), lambda s, gg, l: (s, l + half, 0, 0)),
                  pl.BlockSpec((1, CMP_LEN, dh), lambda s, gg, l: (s, 0, 0)),
                  pl.BlockSpec((1, CMP_LEN, dh, dh), lambda s, gg, l: (s, 0, 0, 0)),
                  pl.BlockSpec((1, dh, dh), lambda s, gg, l: (s, 0, 0))],
        out_specs=pl.BlockSpec((1, 1, m, dh), lambda s, gg, l: (s, gg, 0, 0)),
        out_shape=jax.ShapeDtypeStruct((2, g, m, dh), BF16),
        scratch_shapes=[pltpu.VMEM((m, dh), F32), pltpu.VMEM((m, dh), F32)],
        compiler_params=_cparams(3),
    )(x2, w1, w1, pos, w1, w2)


def _softmax2_rows(s, mask):
    s = jnp.where(mask, s, -1e30)
    m = jnp.max(s, axis=-1, keepdims=True)
    p = jnp.exp2(s - m)
    return p, 1.0 / jnp.maximum(jnp.sum(p, axis=-1, keepdims=True), 1e-30)


def _nsa_attn_kernel(q_ref, gt_ref, kc_ref, vc_ref, agg_ref, ksa_ref, vsa_ref, kw_ref, vw_ref,
                     o_ref, qaug_scr, s_a, s_b, acc_scr, m_scr):
    b = pl.program_id(1)
    qb = q_ref.shape[0]
    dh = NSA_HEAD_DIM
    rep = NSA_REP
    rows = rep * qb
    n_cmp = kc_ref.shape[2]
    n_sel = agg_ref.shape[1]
    t0 = b * qb

    q_blk = q_ref[...]
    qs = jnp.concatenate([q_blk[:, r * dh:(r + 1) * dh] for r in range(rep)], axis=0)
    tpos = t0 + lax.broadcasted_iota(jnp.int32, (rows, 1), 0) % qb

    kc = kc_ref[0, 0]
    s_c = lax.dot_general(qs, kc, NT_DIMS, preferred_element_type=F32)
    cmp_end = lax.broadcasted_iota(jnp.int32, (1, n_cmp), 1) * CMP_STRIDE + (CMP_LEN - 1)
    p_c, inv_c = _softmax2_rows(s_c, cmp_end <= tpos)
    p_c = p_c * jnp.where(tpos >= CMP_LEN - 1, inv_c, 0.0)
    o_c = jnp.dot(p_c.astype(BF16), vc_ref[0, 0], preferred_element_type=F32)

    psum = p_c[0:qb]
    for r in range(1, rep):
        psum = psum + p_c[r * qb:(r + 1) * qb]
    p_hi = psum.astype(BF16)
    p_lo = (psum - p_hi.astype(F32)).astype(BF16)
    agg = agg_ref[...]
    imp = (jnp.dot(p_hi, agg, preferred_element_type=F32)
           + jnp.dot(p_lo, agg, preferred_element_type=F32))
    imp_t = imp.T
    tq = t0 + lax.broadcasted_iota(jnp.int32, (1, qb), 1)
    blk = lax.broadcasted_iota(jnp.int32, (n_sel, qb), 0)
    blk_f = blk.astype(F32)
    cur = tq // SEL_LEN
    forced = (blk == 0) | (blk == cur) | (blk == cur - 1)
    valid = blk * SEL_LEN <= tq
    score = jnp.where(forced, 1e9, jnp.where(valid, imp_t, -1e9))
    pen_t = jnp.full((n_sel, qb), MASK_PENALTY, F32)
    for _ in range(min(SEL_TOPN, n_sel)):
        mx = jnp.max(score, axis=0, keepdims=True)
        idx = jnp.min(jnp.where(score == mx, blk_f, float(n_sel)), axis=0, keepdims=True)
        hit = blk_f == idx
        pen_t = jnp.where(hit, 0.0, pen_t)
        score = jnp.where(hit, -jnp.inf, score)
    pen = pen_t.T.astype(BF16)
    n_half = qaug_scr.shape[0]
    for hh in range(n_half):
        if n_sel >= ONEHOT_PERIOD:
            ph = pen[:, hh * ONEHOT_PERIOD:(hh + 1) * ONEHOT_PERIOD]
        else:
            ph = jnp.concatenate(
                [pen, jnp.full((qb, ONEHOT_PERIOD - n_sel), MASK_PENALTY, BF16)], axis=1)
        qaug_scr[hh] = jnp.concatenate([qs, jnp.concatenate([ph] * rep, axis=0)], axis=1)

    kt = SEL_KV_TILE
    period_keys = ONEHOT_PERIOD * SEL_LEN

    def sel_scores(i, s_out):
        k0 = pl.multiple_of(i * kt, kt)
        s_out[...] = lax.dot_general(qaug_scr[k0 // period_keys], ksa_ref[pl.ds(k0, kt), :], NT_DIMS,
                                     preferred_element_type=F32)

    def sel_update(i, s_in, causal):
        k0 = pl.multiple_of(i * kt, kt)
        s = s_in[...]
        if causal:
            kpos = k0 + lax.broadcasted_iota(jnp.int32, (1, kt), 1)
            s = jnp.where(kpos <= tpos, s, MASK_PENALTY)
        m_run = m_scr[...]
        m_new = jnp.maximum(m_run, jnp.max(s, axis=-1, keepdims=True))
        p = jnp.exp2(s - m_new)
        acc_scr[...] = (jnp.exp2(m_run - m_new) * acc_scr[...]
                        + jnp.dot(p.astype(BF16), vsa_ref[pl.ds(k0, kt), :], preferred_element_type=F32))
        m_scr[...] = m_new

    def sel_pair(j, carry):
        sel_scores(2 * j + 1, s_b)
        sel_update(2 * j, s_a, False)
        sel_scores(2 * j + 2, s_a)
        sel_update(2 * j + 1, s_b, False)
        return carry

    n_full = t0 // kt
    m_scr[...] = jnp.full(m_scr.shape, -1e30, F32)
    acc_scr[...] = jnp.zeros_like(acc_scr)
    sel_scores(0, s_a)
    lax.fori_loop(0, n_full // 2, sel_pair, 0)
    odd = n_full % 2 == 1

    @pl.when(odd)
    def _():
        sel_scores(n_full, s_b)
        sel_update(n_full - 1, s_a, False)
        sel_update(n_full, s_b, True)

    @pl.when(jnp.logical_not(odd))
    def _():
        sel_update(n_full, s_a, True)

    acc_s = acc_scr[...]
    o_s = acc_s[:, 0:dh] * (1.0 / acc_s[:, dh:dh + 1])

    wlen = WIN + qb
    w0 = pl.multiple_of(jnp.maximum(t0 - WIN, 0), qb)
    s_w = lax.dot_general(qs, kw_ref[pl.ds(w0, wlen), :], NT_DIMS, preferred_element_type=F32)
    diff = tpos - (w0 + lax.broadcasted_iota(jnp.int32, (1, wlen), 1))
    p_w, inv_w = _softmax2_rows(s_w, (diff >= 0) & (diff < WIN))
    o_w = jnp.dot((p_w * inv_w).astype(BF16), vw_ref[pl.ds(w0, wlen), :], preferred_element_type=F32)

    gt = gt_ref[...]
    for r in range(rep):
        rs = slice(r * qb, (r + 1) * qb)
        o = (gt[:, r:r + 1] * o_c[rs]
             + gt[:, rep + r:rep + r + 1] * o_s[rs]
             + gt[:, 2 * rep + r:2 * rep + r + 1] * o_w[rs])
        o_ref[:, r * dh:(r + 1) * dh] = o.astype(o_ref.dtype)


def _nsa_agg(n_cmp_pad, n_cmp, n_sel):
    ratio, span = SEL_LEN // CMP_STRIDE, CMP_LEN // CMP_STRIDE
    agg = np.zeros((n_cmp_pad, n_sel), np.float32)
    jj = np.arange(n_sel)
    for m in range(ratio):
        for n in range(span):
            ii = ratio * jj + m - n
            ok = (ii >= 0) & (ii < n_cmp)
            agg[ii[ok], jj[ok]] += 1.0
    return jnp.asarray(agg, dtype=BF16)


def _nsa_attention(qkv, gates, kv_cmp):
    l_all = qkv.shape[0]
    dh = NSA_HEAD_DIM
    g = NSA_KV_GROUPS
    qb = Q_BLOCK
    qw = NSA_REP * dh
    n_cmp_pad = l_all // CMP_STRIDE
    n_cmp = (l_all - CMP_LEN) // CMP_STRIDE + 1
    n_sel = l_all // SEL_LEN
    agg = _nsa_agg(n_cmp_pad, n_cmp, n_sel)
    n_half = max(n_sel // ONEHOT_PERIOD, 1)
    q_blocks = (NSA_HEADS * dh) // dh
    key = np.arange(l_all)
    onehot = jnp.asarray((key[:, None] // SEL_LEN) % ONEHOT_PERIOD == np.arange(ONEHOT_PERIOD)[None, :],
                         dtype=BF16)
    ones_col = jnp.asarray(np.arange(dh)[None, :] == 0, dtype=BF16)
    grouped = lambda base: qkv[:, (q_blocks + base) * dh:(q_blocks + base + g) * dh].reshape(l_all, g, dh)
    ks_aug = jnp.concatenate([grouped(2 * g), jnp.broadcast_to(onehot[:, None, :], (l_all, g, ONEHOT_PERIOD))],
                             axis=-1).reshape(l_all, g * (dh + ONEHOT_PERIOD))
    vs_aug = jnp.concatenate([grouped(3 * g), jnp.broadcast_to(ones_col[:, None, :], (l_all, g, dh))],
                             axis=-1).reshape(l_all, g * 2 * dh)
    once = dict(pipeline_mode=pl.Buffered(1))
    kv_spec = lambda base: pl.BlockSpec((l_all, dh), lambda gg, b: (0, q_blocks + base + gg), **once)
    aug_spec = pl.BlockSpec((l_all, 2 * dh), lambda gg, b: (0, gg), **once)
    rows = NSA_REP * qb
    return pl.pallas_call(
        _nsa_attn_kernel,
        grid=(g, l_all // qb),
        in_specs=[pl.BlockSpec((qb, qw), lambda gg, b: (b, gg)),
                  pl.BlockSpec((qb, dh), lambda gg, b: (b, gg)),
                  pl.BlockSpec((1, 1, n_cmp_pad, dh), lambda gg, b: (0, gg, 0, 0)),
                  pl.BlockSpec((1, 1, n_cmp_pad, dh), lambda gg, b: (1, gg, 0, 0)),
                  pl.BlockSpec((n_cmp_pad, n_sel), lambda gg, b: (0, 0)),
                  aug_spec, aug_spec, kv_spec(4 * g), kv_spec(5 * g)],
        out_specs=pl.BlockSpec((qb, qw), lambda gg, b: (b, gg)),
        out_shape=jax.ShapeDtypeStruct((l_all, NSA_HEADS * dh), BF16),
        scratch_shapes=[pltpu.VMEM((n_half, rows, dh + ONEHOT_PERIOD), BF16),
                        pltpu.VMEM((rows, SEL_KV_TILE), F32), pltpu.VMEM((rows, SEL_KV_TILE), F32),
                        pltpu.VMEM((rows, 2 * dh), F32), pltpu.VMEM((rows, 1), F32)],
        compiler_params=_cparams(2),
    )(qkv, gates, kv_cmp, kv_cmp, agg, ks_aug, vs_aug, qkv, qkv)


def _nsa_layer(x, g, sh, sc, gate, w_in, k_pos, k_w1, k_w2, v_pos, v_w1, v_w2, w_out):
    dh = NSA_HEAD_DIM
    qd = NSA_HEADS * dh
    kvd = 2 * N_BRANCH * NSA_KV_GROUPS * dh
    w_main = w_in[:, :qd + kvd].astype(BF16)
    colscale = jnp.concatenate([jnp.full((1, qd), dh ** -0.5 * math.log2(math.e), F32),
                                jnp.ones((1, kvd), F32)], axis=1)
    qkv = _norm_mod_matmul(x, g, sh, sc, w_main, colscale, BF16)
    wg = w_in[:, qd + kvd:].reshape(-1, NSA_KV_GROUPS, NSA_REP, N_BRANCH).transpose(0, 1, 3, 2)
    wg = wg.reshape(-1, NSA_KV_GROUPS, N_BRANCH * NSA_REP)
    wg = jnp.pad(wg, ((0, 0), (0, 0), (0, dh - N_BRANCH * NSA_REP))).reshape(-1, NSA_KV_GROUPS * dh)
    gates = _norm_mod_matmul(x, g, sh, sc, wg.astype(BF16), jnp.ones((1, wg.shape[1]), F32), F32,
                             act="sigmoid")
    kv_cmp = _compress(qkv, qd // dh, jnp.stack([k_pos, v_pos]),
                       jnp.stack([k_w1, v_w1]).astype(BF16), jnp.stack([k_w2, v_w2]).astype(BF16))
    o = _nsa_attention(qkv, gates, kv_cmp)
    return _matmul_residual(o, w_out.astype(BF16), x, gate, glu=False)


def _peer_query_kernel(x_ref, g_ref, sh_ref, sc_ref, wq_ref, keys_ref, ht_ref, st_ref):
    h32 = _norm_mod(x_ref[...], g_ref[...], sh_ref[...], sc_ref[...])
    ht_ref[...] = h32.T.astype(BF16)
    q = jnp.dot(h32.astype(BF16), wq_ref[...], preferred_element_type=F32).astype(BF16)
    for hc in range(keys_ref.shape[0]):
        rows = slice(hc * PEER_NKEYS, (hc + 1) * PEER_NKEYS)
        st_ref[rows, :] = lax.dot_general(keys_ref[hc], q[:, hc * PEER_HALF:(hc + 1) * PEER_HALF],
                                          NT_DIMS, preferred_element_type=F32)


def _peer_query(x, g, sh, sc, w_q, sub_keys, tm=512):
    t, d = x.shape
    nq = w_q.shape[1]
    keys = sub_keys.reshape(-1, PEER_NKEYS, PEER_HALF).astype(BF16)
    row = lambda i: (0, 0)
    return pl.pallas_call(
        _peer_query_kernel,
        grid=(t // tm,),
        in_specs=[pl.BlockSpec((tm, d), lambda i: (i, 0)),
                  pl.BlockSpec((1, d), row), pl.BlockSpec((1, d), row), pl.BlockSpec((1, d), row),
                  pl.BlockSpec((d, nq), row),
                  pl.BlockSpec(keys.shape, lambda i: (0, 0, 0))],
        out_specs=[pl.BlockSpec((d, tm), lambda i: (0, i)),
                   pl.BlockSpec((keys.shape[0] * PEER_NKEYS, tm), lambda i: (0, i))],
        out_shape=[jax.ShapeDtypeStruct((d, t), BF16),
                   jax.ShapeDtypeStruct((keys.shape[0] * PEER_NKEYS, t), F32)],
        compiler_params=_cparams(1),
    )(x, g, sh, sc, w_q.astype(BF16), keys)


def _peer_cells():
    return [(a, b) for a in range(PEER_TOPK) for b in range(PEER_TOPK) if (a + 1) * (b + 1) <= PEER_TOPK]


def _take_max(cur, iota, exact):
    v = jnp.max(cur, axis=0, keepdims=True)
    hit = cur == v
    if exact:
        idx = jnp.min(jnp.where(hit, iota, cur.shape[0]), axis=0, keepdims=True)
        hit = iota == idx
    return v, hit


def _top_ranks(s, iota_k, exact):
    rank = jnp.full(s.shape, RANK_NONE, F32)
    vals = []
    cur = s
    for a in range(PEER_TOPK):
        v, hit = _take_max(cur, iota_k, exact)
        rank = jnp.where(hit, float(a), rank)
        cur = jnp.where(hit, -jnp.inf, cur)
        vals.append(v)
    return vals, rank


def _peer_route_body(st_ref, seg_ref, r2_ref, ln_ref, g1_ref, g2_ref, exact):
    tn = st_ref.shape[1]
    nk = PEER_NKEYS
    cells = _peer_cells()
    n_cell = len(cells)
    n_pad = -(-n_cell // 8) * 8
    n_seg = seg_ref.shape[1]
    iota_k = lax.broadcasted_iota(jnp.int32, (nk, tn), 0)
    iota_c = lax.broadcasted_iota(jnp.int32, (n_pad, tn), 0)
    count = lambda m: jnp.sum(m.astype(F32), axis=0, keepdims=True)
    tied = jnp.zeros((1, tn), jnp.bool_)
    for h in range(PEER_HEADS):
        s1 = st_ref[(2 * h) * nk:(2 * h + 1) * nk, :]
        s2 = st_ref[(2 * h + 1) * nk:(2 * h + 2) * nk, :]
        v1, rank1 = _top_ranks(s1, iota_k, exact)
        v2, rank2 = _top_ranks(s2, iota_k, exact)
        cand = jnp.concatenate([v1[a] + v2[b] for a, b in cells]
                               + [jnp.full((n_pad - n_cell, tn), -jnp.inf, F32)], axis=0)
        top = v1[0] + v2[0]
        e_c = jnp.exp(cand - top)
        chosen = jnp.zeros((n_pad, tn), jnp.bool_)
        cur = cand
        for _ in range(PEER_TOPK):
            _, hit = _take_max(cur, iota_c, exact)
            chosen = chosen | hit
            cur = jnp.where(hit, -jnp.inf, cur)
        chosen_f = chosen.astype(F32)
        if not exact:
            k = float(PEER_TOPK)
            tied = (tied | (count(rank1 != RANK_NONE) != k) | (count(rank2 != RANK_NONE) != k)
                    | (jnp.sum(chosen_f, axis=0, keepdims=True) != k))
        z = jnp.sum(chosen_f * e_c, axis=0, keepdims=True)
        chosen_pad = jnp.concatenate([chosen_f, jnp.zeros((n_seg - n_pad, tn), F32)], axis=0)
        rowlen = jnp.dot(seg_ref[...], chosen_pad.astype(BF16), preferred_element_type=F32)
        ln = jnp.zeros((nk, tn), F32)
        for a in range(PEER_TOPK):
            ln = jnp.where(rank1 == float(a), rowlen[a:a + 1, :], ln)
        rows = slice(h * nk, (h + 1) * nk)
        r2_ref[rows, :] = rank2.astype(r2_ref.dtype)
        ln_ref[h] = ln
        g1_ref[h] = jnp.exp(s1 - v1[0]) / z
        g2_ref[rows, :] = jnp.exp(s2 - v2[0]).astype(g2_ref.dtype)
    return tied


def _peer_route_kernel(st_ref, seg_ref, r2_ref, ln_ref, g1_ref, g2_ref):
    refs = (st_ref, seg_ref, r2_ref, ln_ref, g1_ref, g2_ref)
    tied = _peer_route_body(*refs, exact=False)

    @pl.when(jnp.max(tied.astype(F32)) > 0.0)
    def _():
        _peer_route_body(*refs, exact=True)


def _peer_route(st, tn=256):
    n_rows, t = st.shape
    cells = _peer_cells()
    seg = np.zeros((PEER_TOPK, PEER_NKEYS), np.float32)
    for c, (a, _) in enumerate(cells):
        seg[a, c] = 1.0
    out_rows = PEER_HEADS * PEER_NKEYS
    shp = jax.ShapeDtypeStruct((out_rows, t), BF16)
    shp_row = jax.ShapeDtypeStruct((PEER_HEADS, PEER_NKEYS, t), F32)
    spec = pl.BlockSpec((out_rows, tn), lambda i: (0, i))
    spec_row = pl.BlockSpec((PEER_HEADS, PEER_NKEYS, tn), lambda i: (0, 0, i))
    return pl.pallas_call(
        _peer_route_kernel,
        grid=(t // tn,),
        in_specs=[pl.BlockSpec((n_rows, tn), lambda i: (0, i)),
                  pl.BlockSpec((PEER_TOPK, PEER_NKEYS), lambda i: (0, 0))],
        out_specs=[spec, spec_row, spec_row, spec],
        out_shape=[shp, shp_row, shp_row, shp],
        compiler_params=_cparams(1),
    )(st, jnp.asarray(seg, dtype=BF16))


def _peer_expert_kernel(h_ref, u_ref, vt_ref, r2_ref, ln_ref, g1_ref, g2_ref, x_ref, gate_ref, nf_ref,
                        o_ref, acc_t, p_scr, pre_a, pre_b, *, final_norm):
    c = pl.program_id(1)
    last = pl.num_programs(1) - 1
    ec = u_ref.shape[0]
    nk = PEER_NKEYS
    tm = h_ref.shape[1]
    zero = jnp.zeros((), BF16)

    def score(pre_out):
        pre_out[...] = jnp.dot(u_ref[...], h_ref[...], preferred_element_type=F32)

    def finish(pre_in):
        for ii in range(ec // nk):
            w = jnp.zeros((nk, tm), BF16)
            for h in range(PEER_HEADS):
                rows = slice(h * nk, (h + 1) * nk)
                ln_row = ln_ref[h, ii:ii + 1, :].astype(BF16)
                g1_row = g1_ref[h, ii:ii + 1, :].astype(BF16)
                w = w + jnp.where(r2_ref[rows, :] < ln_row, g2_ref[rows, :], zero) * g1_row
            act = jax.nn.gelu(pre_in[ii * nk:(ii + 1) * nk, :])
            p_scr[ii * nk:(ii + 1) * nk, :] = w * act.astype(BF16)
        acc_t[...] += jnp.dot(vt_ref[...], p_scr[...], preferred_element_type=F32)

    @pl.when(c == 0)
    def _():
        acc_t[...] = jnp.zeros_like(acc_t)
        score(pre_a)

    @pl.when((c > 0) & (c < last) & (c % 2 == 1))
    def _():
        score(pre_b)
        finish(pre_a)

    @pl.when((c > 0) & (c < last) & (c % 2 == 0))
    def _():
        score(pre_a)
        finish(pre_b)

    @pl.when(c == last)
    def _():
        finish(pre_b if PEER_LAST_IS_EVEN else pre_a)
        xo = x_ref[...] + gate_ref[...] * acc_t[...].T
        if final_norm:
            ms = jnp.mean(xo * xo, axis=-1, keepdims=True)
            xo = (xo * lax.rsqrt(ms + RMS_EPS)) * nf_ref[...]
        o_ref[...] = xo


def _peer_experts(ht, u_bf, vt_bf, route, x, gate, norm_final, final_norm, tm=512, ec=PEER_EXPERT_CHUNK):
    t, d = x.shape
    e = u_bf.shape[0]
    n_chunks = e // ec
    assert (n_chunks % 2 == 0) == PEER_LAST_IS_EVEN
    r2, ln, g1, g2 = route
    rspec = pl.BlockSpec((r2.shape[0], tm), lambda i, c: (0, i))
    prev = lambda c: jnp.maximum(c - 1, 0)
    kspec = pl.BlockSpec((PEER_HEADS, ec // PEER_NKEYS, tm), lambda i, c: (0, prev(c), i))
    row = lambda i, c: (0, 0)
    return pl.pallas_call(
        functools.partial(_peer_expert_kernel, final_norm=final_norm),
        grid=(t // tm, n_chunks + 1),
        in_specs=[pl.BlockSpec((d, tm), lambda i, c: (0, i)),
                  pl.BlockSpec((ec, d), lambda i, c: (jnp.minimum(c, n_chunks - 1), 0)),
                  pl.BlockSpec((d, ec), lambda i, c: (0, prev(c))),
                  rspec, kspec, kspec, rspec,
                  pl.BlockSpec((tm, d), lambda i, c: (i, 0)),
                  pl.BlockSpec((1, d), row), pl.BlockSpec((1, d), row)],
        out_specs=pl.BlockSpec((tm, d), lambda i, c: (i, 0)),
        out_shape=jax.ShapeDtypeStruct((t, d), F32),
        scratch_shapes=[pltpu.VMEM((d, tm), F32), pltpu.VMEM((ec, tm), BF16),
                        pltpu.VMEM((ec, tm), F32), pltpu.VMEM((ec, tm), F32)],
        compiler_params=_cparams(2),
    )(ht, u_bf, vt_bf, r2, ln, g1, g2, x, gate, norm_final)


def _peer_layer(x, g, sh, sc, gate, w_q, sub_keys, u_tab, v_tab, norm_final, final_norm):
    ht, st = _peer_query(x, g, sh, sc, w_q, sub_keys)
    route = _peer_route(st)
    return _peer_experts(ht, u_tab.astype(BF16), v_tab.astype(BF16).T, route, x, gate,
                         norm_final, final_norm)


def kernel(x, c, ada_w, ada_b, norm_mix, norm_ffn, norm_final, ssm_w_in, ssm_lambda_re, ssm_lambda_im, ssm_log_dt, ssm_b_re, ssm_b_im, ssm_c_re, ssm_c_im, ssm_d, ssm_w_out, nsa_w_in, nsa_cmp_k_pos, nsa_cmp_k_w1, nsa_cmp_k_w2, nsa_cmp_v_pos, nsa_cmp_v_w1, nsa_cmp_v_w2, nsa_w_out, peer_w_q, peer_sub_keys, peer_u, peer_v):
    bsz, l_all, d = x.shape
    assert bsz == 1
    depth = ada_w.shape[0]
    mod = _adaln(c, ada_w, ada_b)
    xt = x.reshape(l_all, d)
    nf = norm_final.reshape(1, d)
    for i in range(depth):
        sh1, sc1, g1, sh2, sc2, g2 = [mod[i, :, k * d:(k + 1) * d] for k in range(6)]
        j = i // 2
        gm = norm_mix[i].reshape(1, d)
        if i % 2 == 0:
            xt = _s5_layer(xt, gm, sh1, sc1, g1, ssm_w_in[j], ssm_lambda_re[j], ssm_lambda_im[j],
                           ssm_log_dt[j], ssm_b_re[j], ssm_b_im[j], ssm_c_re[j], ssm_c_im[j],
                           ssm_d[j], ssm_w_out[j])
        else:
            xt = _nsa_layer(xt, gm, sh1, sc1, g1, nsa_w_in[j], nsa_cmp_k_pos[j], nsa_cmp_k_w1[j],
                            nsa_cmp_k_w2[j], nsa_cmp_v_pos[j], nsa_cmp_v_w1[j], nsa_cmp_v_w2[j],
                            nsa_w_out[j])
        xt = _peer_layer(xt, norm_ffn[i].reshape(1, d), sh2, sc2, g2, peer_w_q[i], peer_sub_keys[i],
                         peer_u[i], peer_v[i], nf, final_norm=(i == depth - 1))
    return xt.reshape(bsz, l_all, d)
```

```python
import functools
import math

import numpy as np
import jax
import jax.numpy as jnp
from jax import lax
from jax.experimental import pallas as pl
from jax.experimental.pallas import tpu as pltpu

F32 = jnp.float32
BF16 = jnp.bfloat16

RMS_EPS = 1e-6

SSM_GROUP = 16
SSM_STATE = 64
SSM_BLOCK_GROUPS = 8
S5_SUBSEQ = 8
S5_CHUNK = 128
S5_LANE_BLOCK = 1024

NSA_HEADS = 16
NSA_HEAD_DIM = 128
NSA_KV_GROUPS = 4
NSA_REP = NSA_HEADS // NSA_KV_GROUPS
N_BRANCH = 3
CMP_LEN = 32
CMP_STRIDE = 16
SEL_LEN = 64
SEL_TOPN = 16
WIN = 512
Q_BLOCK = 128
SEL_KV_TILE = 512
MASK_PENALTY = -(2.0 ** 30)
ONEHOT_PERIOD = 128

PEER_HEADS = 8
PEER_NKEYS = 128
PEER_TOPK = 16
PEER_HALF = 128
PEER_EXPERT_CHUNK = 1024
PEER_LAST_IS_EVEN = (PEER_NKEYS ** 2 // PEER_EXPERT_CHUNK) % 2 == 0
RANK_NONE = 255.0

VMEM_LIMIT = 56 * 1024 * 1024
NT_DIMS = (((1,), (1,)), ((), ()))


def _cparams(n_axes):
    return pltpu.CompilerParams(
        dimension_semantics=("arbitrary",) * n_axes, vmem_limit_bytes=VMEM_LIMIT)


def _norm_mod(x, g, sh, sc):
    ms = jnp.mean(x * x, axis=-1, keepdims=True)
    xn = x * lax.rsqrt(ms + RMS_EPS)
    return (xn * g) * (1.0 + sc) + sh


def _adaln_kernel(c_ref, w_ref, b_ref, o_ref):
    c = c_ref[...]
    cond = c * jax.nn.sigmoid(c)
    o_ref[0] = jnp.dot(cond.astype(BF16), w_ref[0].astype(BF16),
                       preferred_element_type=F32) + b_ref[0]


def _adaln(c, ada_w, ada_b):
    depth, d, n = ada_w.shape
    tn = 1024
    c8 = jnp.broadcast_to(c, (8, d))
    out = pl.pallas_call(
        _adaln_kernel,
        grid=(depth, n // tn),
        in_specs=[pl.BlockSpec((8, d), lambda i, j: (0, 0)),
                  pl.BlockSpec((1, d, tn), lambda i, j: (i, 0, j)),
                  pl.BlockSpec((1, 1, tn), lambda i, j: (i, 0, j))],
        out_specs=pl.BlockSpec((1, 8, tn), lambda i, j: (i, 0, j)),
        out_shape=jax.ShapeDtypeStruct((depth, 8, n), F32),
        compiler_params=_cparams(2),
    )(c8, ada_w, ada_b.reshape(depth, 1, n))
    return out[:, 0:1, :]


def _nmm_kernel(x_ref, g_ref, sh_ref, sc_ref, w_ref, cs_ref, o_ref, h_scr, *, act):
    @pl.when(pl.program_id(1) == 0)
    def _():
        h_scr[...] = _norm_mod(x_ref[...], g_ref[...], sh_ref[...], sc_ref[...]).astype(BF16)

    acc = jnp.dot(h_scr[...], w_ref[...], preferred_element_type=F32)
    if act == "sigmoid":
        acc = jax.nn.sigmoid(acc)
    else:
        acc = acc * cs_ref[...]
    o_ref[...] = acc.astype(o_ref.dtype)


def _norm_mod_matmul(x, g, sh, sc, w, colscale, out_dtype, act="scale", tm=512, tn=1024):
    t, d = x.shape
    n = w.shape[1]
    tn = min(tn, n)
    row = lambda i, j: (0, 0)
    return pl.pallas_call(
        functools.partial(_nmm_kernel, act=act),
        grid=(t // tm, n // tn),
        in_specs=[pl.BlockSpec((tm, d), lambda i, j: (i, 0)),
                  pl.BlockSpec((1, d), row), pl.BlockSpec((1, d), row), pl.BlockSpec((1, d), row),
                  pl.BlockSpec((d, tn), lambda i, j: (0, j)),
                  pl.BlockSpec((1, tn), lambda i, j: (0, j))],
        out_specs=pl.BlockSpec((tm, tn), lambda i, j: (i, j)),
        out_shape=jax.ShapeDtypeStruct((t, n), out_dtype),
        scratch_shapes=[pltpu.VMEM((tm, d), BF16)],
        compiler_params=_cparams(2),
    )(x, g, sh, sc, w, colscale)


def _mm_res_kernel(a_ref, w_ref, x_ref, g_ref, o_ref):
    y = jnp.dot(a_ref[...], w_ref[...], preferred_element_type=F32)
    o_ref[...] = x_ref[...] + g_ref[...] * y


def _mm_glu_res_kernel(a_ref, wa_ref, wb_ref, x_ref, g_ref, o_ref):
    a = a_ref[...]
    ya = jnp.dot(a, wa_ref[...], preferred_element_type=F32)
    yb = jnp.dot(a, wb_ref[...], preferred_element_type=F32)
    o_ref[...] = x_ref[...] + g_ref[...] * (ya * jax.nn.sigmoid(yb))


def _matmul_residual(a, w, x, gate, glu, tm=512, tn=1024):
    t, k = a.shape
    n = x.shape[1]
    nb = n // tn
    a_spec = pl.BlockSpec((tm, k), lambda i, j: (i, 0))
    w_spec = pl.BlockSpec((k, tn), lambda i, j: (0, j))
    tail = [pl.BlockSpec((tm, tn), lambda i, j: (i, j)), pl.BlockSpec((1, tn), lambda i, j: (0, j))]
    if glu:
        kern = _mm_glu_res_kernel
        in_specs = [a_spec, w_spec, pl.BlockSpec((k, tn), lambda i, j: (0, j + nb))] + tail
        args = (a, w, w, x, gate)
    else:
        kern = _mm_res_kernel
        in_specs = [a_spec, w_spec] + tail
        args = (a, w, x, gate)
    return pl.pallas_call(
        kern,
        grid=(t // tm, nb),
        in_specs=in_specs,
        out_specs=pl.BlockSpec((tm, tn), lambda i, j: (i, j)),
        out_shape=jax.ShapeDtypeStruct((t, n), F32),
        compiler_params=_cparams(2),
    )(*args)


def _s5_param_kernel(lre_ref, lim_ref, ldt_ref, bre_ref, bim_ref, ar_ref, ai_ref, bbr_ref, bbi_ref):
    lre = lre_ref[...]
    lim = lim_ref[...]
    dt = jnp.exp(ldt_ref[...])
    mag = jnp.exp(lre * dt)
    ar = mag * jnp.cos(lim * dt)
    ai = mag * jnp.sin(lim * dt)
    den = lre * lre + lim * lim
    cr = ((ar - 1.0) * lre + ai * lim) / den
    ci = (ai * lre - (ar - 1.0) * lim) / den
    ar_ref[...] = ar
    ai_ref[...] = ai
    bbr_ref[...] = cr * bre_ref[...] - ci * bim_ref[...]
    bbi_ref[...] = cr * bim_ref[...] + ci * bre_ref[...]


def _s5_params(lam_re, lam_im, log_dt, b_re, b_im):
    g, n, p = b_re.shape
    rep = lambda a: jnp.repeat(a, p, axis=1)
    shp = jax.ShapeDtypeStruct((g, n * p), F32)
    ar, ai, bbr, bbi = pl.pallas_call(
        _s5_param_kernel, out_shape=(shp, shp, shp, shp),
    )(rep(lam_re), rep(lam_im), log_dt.reshape(g, 1), b_re.reshape(g, n * p), b_im.reshape(g, n * p))
    return ar[:, ::p], ai[:, ::p], bbr.reshape(g, n, p), bbi.reshape(g, n, p)


def _s5_scan_kernel(u_ref, a_ref, bbr_ref, bbi_ref, cr_ref, cin_ref, d_ref, o_ref,
                    bur, bui, hin_r, hin_i, car_r, car_i, apr, api):
    t, width = u_ref.shape
    sub = t // S5_SUBSEQ
    lanes = bur.shape[1]
    nblk = bbr_ref.shape[0]
    kin = width // nblk
    kst = lanes // nblk

    @pl.when(pl.program_id(0) == 0)
    def _init():
        car_r[...] = jnp.zeros_like(car_r)
        car_i[...] = jnp.zeros_like(car_i)
        ar = a_ref[0:1, :]
        ai = a_ref[1:2, :]
        pr, pi = ar, ai
        apr[0:1, :] = pr
        api[0:1, :] = pi
        for i in range(1, sub):
            pr, pi = pr * ar - pi * ai, pr * ai + pi * ar
            apr[i:i + 1, :] = pr
            api[i:i + 1, :] = pi

    for k in range(nblk):
        uk = u_ref[:, kin * k:kin * (k + 1)].astype(BF16)
        bur[:, kst * k:kst * (k + 1)] = jnp.dot(uk, bbr_ref[k], preferred_element_type=F32)
        bui[:, kst * k:kst * (k + 1)] = jnp.dot(uk, bbi_ref[k], preferred_element_type=F32)

    lb = S5_LANE_BLOCK
    for b in range(lanes // lb):
        sl = slice(b * lb, (b + 1) * lb)
        ar = jnp.broadcast_to(a_ref[0:1, sl], (S5_SUBSEQ, lb))
        ai = jnp.broadcast_to(a_ref[1:2, sl], (S5_SUBSEQ, lb))
        sr = jnp.zeros((S5_SUBSEQ, lb), F32)
        si = jnp.zeros((S5_SUBSEQ, lb), F32)
        for i in range(sub):
            rows = slice(S5_SUBSEQ * i, S5_SUBSEQ * (i + 1))
            sr, si = (ar * sr - ai * si + bur[rows, sl], ar * si + ai * sr + bui[rows, sl])
            bur[rows, sl] = sr
            bui[rows, sl] = si
        asr = apr[sub - 1:sub, sl]
        asi = api[sub - 1:sub, sl]
        hr = car_r[0:1, sl]
        hi = car_i[0:1, sl]
        for j in range(S5_SUBSEQ):
            hin_r[j:j + 1, sl] = hr
            hin_i[j:j + 1, sl] = hi
            er = sr[j:j + 1, :]
            ei = si[j:j + 1, :]
            hr, hi = er + asr * hr - asi * hi, ei + asr * hi + asi * hr
        car_r[0:1, sl] = hr
        car_i[0:1, sl] = hi
        hinr = hin_r[:, sl]
        hini = hin_i[:, sl]
        for i in range(sub):
            rows = slice(S5_SUBSEQ * i, S5_SUBSEQ * (i + 1))
            pr = jnp.broadcast_to(apr[i:i + 1, sl], (S5_SUBSEQ, lb))
            pi = jnp.broadcast_to(api[i:i + 1, sl], (S5_SUBSEQ, lb))
            bur[rows, sl] = bur[rows, sl] + pr * hinr - pi * hini
            bui[rows, sl] = bui[rows, sl] + pr * hini + pi * hinr

    for k in range(nblk):
        sr = bur[:, kst * k:kst * (k + 1)].astype(BF16)
        si = bui[:, kst * k:kst * (k + 1)].astype(BF16)
        y = (jnp.dot(sr, cr_ref[k], preferred_element_type=F32)
             + jnp.dot(si, cin_ref[k], preferred_element_type=F32))
        cols = slice(kin * k, kin * (k + 1))
        y = y + d_ref[:, cols] * u_ref[:, cols]
        o_ref[:, cols] = jax.nn.gelu(y).astype(o_ref.dtype)


def _s5_scan(u_p, ar, ai, bb_r, bb_i, c_re, c_im, d_skip):
    t_all, width = u_p.shape
    g, n, p = bb_r.shape
    bg = SSM_BLOCK_GROUPS
    nblk = g // bg
    eye = jnp.eye(bg, dtype=F32)
    blk_b = lambda bb: jnp.einsum('kgnp,gh->kgphn', bb.reshape(nblk, bg, n, p), eye
                                  ).reshape(nblk, bg * p, bg * n).astype(BF16)
    blk_c = lambda cc: jnp.einsum('kgpn,gh->kgnhp', cc.reshape(nblk, bg, p, n), eye
                                  ).reshape(nblk, bg * n, bg * p).astype(BF16)
    lanes = g * n
    a8 = jnp.zeros((8, lanes), F32).at[0].set(ar.reshape(lanes)).at[1].set(ai.reshape(lanes))
    t = S5_CHUNK
    sub = t // S5_SUBSEQ
    full3 = lambda c: (0, 0, 0)
    wspec_b = pl.BlockSpec((nblk, bg * p, bg * n), full3)
    wspec_c = pl.BlockSpec((nblk, bg * n, bg * p), full3)
    return pl.pallas_call(
        _s5_scan_kernel,
        grid=(t_all // t,),
        in_specs=[pl.BlockSpec((t, width), lambda c: (c, 0)),
                  pl.BlockSpec((8, lanes), lambda c: (0, 0)),
                  wspec_b, wspec_b, wspec_c, wspec_c,
                  pl.BlockSpec((1, width), lambda c: (0, 0))],
        out_specs=pl.BlockSpec((t, width), lambda c: (c, 0)),
        out_shape=jax.ShapeDtypeStruct((t_all, width), BF16),
        scratch_shapes=[pltpu.VMEM((t, lanes), F32), pltpu.VMEM((t, lanes), F32),
                        pltpu.VMEM((8, lanes), F32), pltpu.VMEM((8, lanes), F32),
                        pltpu.VMEM((8, lanes), F32), pltpu.VMEM((8, lanes), F32),
                        pltpu.VMEM((sub, lanes), F32), pltpu.VMEM((sub, lanes), F32)],
        compiler_params=_cparams(1),
    )(u_p, a8, blk_b(bb_r), blk_b(bb_i), blk_c(c_re), blk_c(-c_im), d_skip.reshape(1, width))


def _s5_rows_to_subseq(x, inverse=False):
    t_all, d = x.shape
    sub = S5_CHUNK // S5_SUBSEQ
    shape = (t_all // S5_CHUNK, sub, S5_SUBSEQ, d) if inverse else (t_all // S5_CHUNK, S5_SUBSEQ, sub, d)
    return x.reshape(shape).transpose(0, 2, 1, 3).reshape(t_all, d)


def _s5_layer(x, g, sh, sc, gate, w_in, lam_re, lam_im, log_dt, b_re, b_im, c_re, c_im, d_skip, w_out):
    d = x.shape[1]
    x_p = _s5_rows_to_subseq(x)
    ones = jnp.ones((1, w_in.shape[1]), F32)
    u_p = _norm_mod_matmul(x_p, g, sh, sc, w_in.astype(BF16), ones, F32)
    ar, ai, bb_r, bb_i = _s5_params(lam_re, lam_im, log_dt, b_re, b_im)
    gy_p = _s5_scan(u_p, ar, ai, bb_r, bb_i, c_re, c_im, d_skip)
    xn_p = _matmul_residual(gy_p, w_out.astype(BF16), x_p, gate, glu=True)
    return _s5_rows_to_subseq(xn_p, inverse=True)


def _cmp_kernel(x_ref, w1a_ref, w1b_ref, pos_ref, w1_ref, w2_ref, o_ref, acc_a, acc_b):
    l = pl.program_id(2)

    @pl.when(l == 0)
    def _():
        acc_a[...] = jnp.zeros_like(acc_a)
        acc_b[...] = jnp.zeros_like(acc_b)

    x = x_ref[...]
    acc_a[...] += jnp.dot(x, w1a_ref[0, 0], preferred_element_type=F32)
    acc_b[...] += jnp.dot(x, w1b_ref[0, 0], preferred_element_type=F32)

    @pl.when(l == pl.num_programs(2) - 1)
    def _():
        m = acc_a.shape[0]
        posc = jnp.zeros((8, NSA_HEAD_DIM), F32)
        for ll in range(CMP_LEN):
            prow = jnp.broadcast_to(pos_ref[0, ll:ll + 1, :], (8, NSA_HEAD_DIM)).astype(BF16)
            posc = posc + jnp.dot(prow, w1_ref[0, ll], preferred_element_type=F32)
        pre = acc_a[...] + pltpu.roll(acc_b[...], m - 1, 0) + posc[0:1, :]
        hid = jax.nn.gelu(pre)
        o_ref[0, 0] = jnp.dot(hid.astype(BF16), w2_ref[0], preferred_element_type=F32).astype(o_ref.dtype)


def _compress(qkv, kv_col0, pos, w1, w2):
    l_all, c = qkv.shape
    half = CMP_LEN // 2
    m = l_all // half
    cb = c // NSA_HEAD_DIM
    x2 = qkv.reshape(m, half * c)
    g = NSA_KV_GROUPS
    dh = NSA_HEAD_DIM
    return pl.pallas_call(
        _cmp_kernel,
        grid=(2, g, half),
        in_specs=[pl.BlockSpec((m, dh), lambda s, gg, l: (0, l * cb + kv_col0 + g * s + gg)),
                  pl.BlockSpec((1, 1, dh, dh), lambda s, gg, l: (s, l, 0, 0)),
                  pl.BlockSpec((1, 1, dh, dh), lambda s, gg, l: (s, l + half, 0, 0)),
                  pl.BlockSpec((1, CMP_LEN, dh), lambda s, gg, l: (s, 0, 0)),
                  pl.BlockSpec((1, CMP_LEN, dh, dh), lambda s, gg, l: (s, 0, 0, 0)),
                  pl.BlockSpec((1, dh, dh), lambda s, gg, l: (s, 0, 0))],
        out_specs=pl.BlockSpec((1, 1, m, dh), lambda s, gg, l: (s, gg, 0, 0)),
        out_shape=jax.ShapeDtypeStruct((2, g, m, dh), BF16),
        scratch_shapes=[pltpu.VMEM((m, dh), F32), pltpu.VMEM((m, dh), F32)],
        compiler_params=_cparams(3),
    )(x2, w1, w1, pos, w1, w2)


def _softmax2_rows(s, mask):
    s = jnp.where(mask, s, -1e30)
    m = jnp.max(s, axis=-1, keepdims=True)
    p = jnp.exp2(s - m)
    return p, 1.0 / jnp.maximum(jnp.sum(p, axis=-1, keepdims=True), 1e-30)


def _nsa_attn_kernel(q_ref, gt_ref, kc_ref, vc_ref, agg_ref, ksa_ref, vsa_ref, kw_ref, vw_ref,
                     o_ref, qaug_scr, s_a, s_b, acc_scr, m_scr):
    b = pl.program_id(1)
    qb = q_ref.shape[0]
    dh = NSA_HEAD_DIM
    rep = NSA_REP
    rows = rep * qb
    n_cmp = kc_ref.shape[2]
    n_sel = agg_ref.shape[1]
    t0 = b * qb

    q_blk = q_ref[...]
    qs = jnp.concatenate([q_blk[:, r * dh:(r + 1) * dh] for r in range(rep)], axis=0)
    tpos = t0 + lax.broadcasted_iota(jnp.int32, (rows, 1), 0) % qb

    kc = kc_ref[0, 0]
    s_c = lax.dot_general(qs, kc, NT_DIMS, preferred_element_type=F32)
    cmp_end = lax.broadcasted_iota(jnp.int32, (1, n_cmp), 1) * CMP_STRIDE + (CMP_LEN - 1)
    p_c, inv_c = _softmax2_rows(s_c, cmp_end <= tpos)
    p_c = p_c * jnp.where(tpos >= CMP_LEN - 1, inv_c, 0.0)
    o_c = jnp.dot(p_c.astype(BF16), vc_ref[0, 0], preferred_element_type=F32)

    psum = p_c[0:qb]
    for r in range(1, rep):
        psum = psum + p_c[r * qb:(r + 1) * qb]
    p_hi = psum.astype(BF16)
    p_lo = (psum - p_hi.astype(F32)).astype(BF16)
    agg = agg_ref[...]
    imp = (jnp.dot(p_hi, agg, preferred_element_type=F32)
           + jnp.dot(p_lo, agg, preferred_element_type=F32))
    imp_t = imp.T
    tq = t0 + lax.broadcasted_iota(jnp.int32, (1, qb), 1)
    blk = lax.broadcasted_iota(jnp.int32, (n_sel, qb), 0)
    blk_f = blk.astype(F32)
    cur = tq // SEL_LEN
    forced = (blk == 0) | (blk == cur) | (blk == cur - 1)
    valid = blk * SEL_LEN <= tq
    score = jnp.where(forced, 1e9, jnp.where(valid, imp_t, -1e9))
    pen_t = jnp.full((n_sel, qb), MASK_PENALTY, F32)
    for _ in range(min(SEL_TOPN, n_sel)):
        mx = jnp.max(score, axis=0, keepdims=True)
        idx = jnp.min(jnp.where(score == mx, blk_f, float(n_sel)), axis=0, keepdims=True)
        hit = blk_f == idx
        pen_t = jnp.where(hit, 0.0, pen_t)
        score = jnp.where(hit, -jnp.inf, score)
    pen = pen_t.T.astype(BF16)
    n_half = qaug_scr.shape[0]
    for hh in range(n_half):
        if n_sel >= ONEHOT_PERIOD:
            ph = pen[:, hh * ONEHOT_PERIOD:(hh + 1) * ONEHOT_PERIOD]
        else:
            ph = jnp.concatenate(
                [pen, jnp.full((qb, ONEHOT_PERIOD - n_sel), MASK_PENALTY, BF16)], axis=1)
        qaug_scr[hh] = jnp.concatenate([qs, jnp.concatenate([ph] * rep, axis=0)], axis=1)

    kt = SEL_KV_TILE
    period_keys = ONEHOT_PERIOD * SEL_LEN

    def sel_scores(i, s_out):
        k0 = pl.multiple_of(i * kt, kt)
        s_out[...] = lax.dot_general(qaug_scr[k0 // period_keys], ksa_ref[pl.ds(k0, kt), :], NT_DIMS,
                                     preferred_element_type=F32)

    def sel_update(i, s_in, causal):
        k0 = pl.multiple_of(i * kt, kt)
        s = s_in[...]
        if causal:
            kpos = k0 + lax.broadcasted_iota(jnp.int32, (1, kt), 1)
            s = jnp.where(kpos <= tpos, s, MASK_PENALTY)
        m_run = m_scr[...]
        m_new = jnp.maximum(m_run, jnp.max(s, axis=-1, keepdims=True))
        p = jnp.exp2(s - m_new)
        acc_scr[...] = (jnp.exp2(m_run - m_new) * acc_scr[...]
                        + jnp.dot(p.astype(BF16), vsa_ref[pl.ds(k0, kt), :], preferred_element_type=F32))
        m_scr[...] = m_new

    def sel_pair(j, carry):
        sel_scores(2 * j + 1, s_b)
        sel_update(2 * j, s_a, False)
        sel_scores(2 * j + 2, s_a)
        sel_update(2 * j + 1, s_b, False)
        return carry

    n_full = t0 // kt
    m_scr[...] = jnp.full(m_scr.shape, -1e30, F32)
    acc_scr[...] = jnp.zeros_like(acc_scr)
    sel_scores(0, s_a)
    lax.fori_loop(0, n_full // 2, sel_pair, 0)
    odd = n_full % 2 == 1

    @pl.when(odd)
    def _():
        sel_scores(n_full, s_b)
        sel_update(n_full - 1, s_a, False)
        sel_update(n_full, s_b, True)

    @pl.when(jnp.logical_not(odd))
    def _():
        sel_update(n_full, s_a, True)

    acc_s = acc_scr[...]
    o_s = acc_s[:, 0:dh] * (1.0 / acc_s[:, dh:dh + 1])

    wlen = WIN + qb
    w0 = pl.multiple_of(jnp.maximum(t0 - WIN, 0), qb)
    s_w = lax.dot_general(qs, kw_ref[pl.ds(w0, wlen), :], NT_DIMS, preferred_element_type=F32)
    diff = tpos - (w0 + lax.broadcasted_iota(jnp.int32, (1, wlen), 1))
    p_w, inv_w = _softmax2_rows(s_w, (diff >= 0) & (diff < WIN))
    o_w = jnp.dot((p_w * inv_w).astype(BF16), vw_ref[pl.ds(w0, wlen), :], preferred_element_type=F32)

    gt = gt_ref[...]
    for r in range(rep):
        rs = slice(r * qb, (r + 1) * qb)
        o = (gt[:, r:r + 1] * o_c[rs]
             + gt[:, rep + r:rep + r + 1] * o_s[rs]
             + gt[:, 2 * rep + r:2 * rep + r + 1] * o_w[rs])
        o_ref[:, r * dh:(r + 1) * dh] = o.astype(o_ref.dtype)


def _nsa_agg(n_cmp_pad, n_cmp, n_sel):
    ratio, span = SEL_LEN // CMP_STRIDE, CMP_LEN // CMP_STRIDE
    agg = np.zeros((n_cmp_pad, n_sel), np.float32)
    jj = np.arange(n_sel)
    for m in range(ratio):
        for n in range(span):
            ii = ratio * jj + m - n
            ok = (ii >= 0) & (ii < n_cmp)
            agg[ii[ok], jj[ok]] += 1.0
    return jnp.asarray(agg, dtype=BF16)


def _nsa_attention(qkv, gates, kv_cmp):
    l_all = qkv.shape[0]
    dh = NSA_HEAD_DIM
    g = NSA_KV_GROUPS
    qb = Q_BLOCK
    qw = NSA_REP * dh
    n_cmp_pad = l_all // CMP_STRIDE
    n_cmp = (l_all - CMP_LEN) // CMP_STRIDE + 1
    n_sel = l_all // SEL_LEN
    agg = _nsa_agg(n_cmp_pad, n_cmp, n_sel)
    n_half = max(n_sel // ONEHOT_PERIOD, 1)
    q_blocks = (NSA_HEADS * dh) // dh
    key = np.arange(l_all)
    onehot = jnp.asarray((key[:, None] // SEL_LEN) % ONEHOT_PERIOD == np.arange(ONEHOT_PERIOD)[None, :],
                         dtype=BF16)
    ones_col = jnp.asarray(np.arange(dh)[None, :] == 0, dtype=BF16)
    grouped = lambda base: qkv[:, (q_blocks + base) * dh:(q_blocks + base + g) * dh].reshape(l_all, g, dh)
    ks_aug = jnp.concatenate([grouped(2 * g), jnp.broadcast_to(onehot[:, None, :], (l_all, g, ONEHOT_PERIOD))],
                             axis=-1).reshape(l_all, g * (dh + ONEHOT_PERIOD))
    vs_aug = jnp.concatenate([grouped(3 * g), jnp.broadcast_to(ones_col[:, None, :], (l_all, g, dh))],
                             axis=-1).reshape(l_all, g * 2 * dh)
    once = dict(pipeline_mode=pl.Buffered(1))
    kv_spec = lambda base: pl.BlockSpec((l_all, dh), lambda gg, b: (0, q_blocks + base + gg), **once)
    aug_spec = pl.BlockSpec((l_all, 2 * dh), lambda gg, b: (0, gg), **once)
    rows = NSA_REP * qb
    return pl.pallas_call(
        _nsa_attn_kernel,
        grid=(g, l_all // qb),
        in_specs=[pl.BlockSpec((qb, qw), lambda gg, b: (b, gg)),
                  pl.BlockSpec((qb, dh), lambda gg, b: (b, gg)),
                  pl.BlockSpec((1, 1, n_cmp_pad, dh), lambda gg, b: (0, gg, 0, 0)),
                  pl.BlockSpec((1, 1, n_cmp_pad, dh), lambda gg, b: (1, gg, 0, 0)),
                  pl.BlockSpec((n_cmp_pad, n_sel), lambda gg, b: (0, 0)),
                  aug_spec, aug_spec, kv_spec(4 * g), kv_spec(5 * g)],
        out_specs=pl.BlockSpec((qb, qw), lambda gg, b: (b, gg)),
        out_shape=jax.ShapeDtypeStruct((l_all, NSA_HEADS * dh), BF16),
        scratch_shapes=[pltpu.VMEM((n_half, rows, dh + ONEHOT_PERIOD), BF16),
                        pltpu.VMEM((rows, SEL_KV_TILE), F32), pltpu.VMEM((rows, SEL_KV_TILE), F32),
                        pltpu.VMEM((rows, 2 * dh), F32), pltpu.VMEM((rows, 1), F32)],
        compiler_params=_cparams(2),
    )(qkv, gates, kv_cmp, kv_cmp, agg, ks_aug, vs_aug, qkv, qkv)


def _nsa_layer(x, g, sh, sc, gate, w_in, k_pos, k_w1, k_w2, v_pos, v_w1, v_w2, w_out):
    dh = NSA_HEAD_DIM
    qd = NSA_HEADS * dh
    kvd = 2 * N_BRANCH * NSA_KV_GROUPS * dh
    w_main = w_in[:, :qd + kvd].astype(BF16)
    colscale = jnp.concatenate([jnp.full((1, qd), dh ** -0.5 * math.log2(math.e), F32),
                                jnp.ones((1, kvd), F32)], axis=1)
    qkv = _norm_mod_matmul(x, g, sh, sc, w_main, colscale, BF16)
    wg = w_in[:, qd + kvd:].reshape(-1, NSA_KV_GROUPS, NSA_REP, N_BRANCH).transpose(0, 1, 3, 2)
    wg = wg.reshape(-1, NSA_KV_GROUPS, N_BRANCH * NSA_REP)
    wg = jnp.pad(wg, ((0, 0), (0, 0), (0, dh - N_BRANCH * NSA_REP))).reshape(-1, NSA_KV_GROUPS * dh)
    gates = _norm_mod_matmul(x, g, sh, sc, wg.astype(BF16), jnp.ones((1, wg.shape[1]), F32), F32,
                             act="sigmoid")
    kv_cmp = _compress(qkv, qd // dh, jnp.stack([k_pos, v_pos]),
                       jnp.stack([k_w1, v_w1]).astype(BF16), jnp.stack([k_w2, v_w2]).astype(BF16))
    o = _nsa_attention(qkv, gates, kv_cmp)
    return _matmul_residual(o, w_out.astype(BF16), x, gate, glu=False)


def _peer_query_kernel(x_ref, g_ref, sh_ref, sc_ref, wq_ref, keys_ref, ht_ref, st_ref):
    h32 = _norm_mod(x_ref[...], g_ref[...], sh_ref[...], sc_ref[...])
    ht_ref[...] = h32.T.astype(BF16)
    q = jnp.dot(h32.astype(BF16), wq_ref[...], preferred_element_type=F32).astype(BF16)
    for hc in range(keys_ref.shape[0]):
        rows = slice(hc * PEER_NKEYS, (hc + 1) * PEER_NKEYS)
        st_ref[rows, :] = lax.dot_general(keys_ref[hc], q[:, hc * PEER_HALF:(hc + 1) * PEER_HALF],
                                          NT_DIMS, preferred_element_type=F32)


def _peer_query(x, g, sh, sc, w_q, sub_keys, tm=512):
    t, d = x.shape
    nq = w_q.shape[1]
    keys = sub_keys.reshape(-1, PEER_NKEYS, PEER_HALF).astype(BF16)
    row = lambda i: (0, 0)
    return pl.pallas_call(
        _peer_query_kernel,
        grid=(t // tm,),
        in_specs=[pl.BlockSpec((tm, d), lambda i: (i, 0)),
                  pl.BlockSpec((1, d), row), pl.BlockSpec((1, d), row), pl.BlockSpec((1, d), row),
                  pl.BlockSpec((d, nq), row),
                  pl.BlockSpec(keys.shape, lambda i: (0, 0, 0))],
        out_specs=[pl.BlockSpec((d, tm), lambda i: (0, i)),
                   pl.BlockSpec((keys.shape[0] * PEER_NKEYS, tm), lambda i: (0, i))],
        out_shape=[jax.ShapeDtypeStruct((d, t), BF16),
                   jax.ShapeDtypeStruct((keys.shape[0] * PEER_NKEYS, t), F32)],
        compiler_params=_cparams(1),
    )(x, g, sh, sc, w_q.astype(BF16), keys)


def _peer_cells():
    return [(a, b) for a in range(PEER_TOPK) for b in range(PEER_TOPK) if (a + 1) * (b + 1) <= PEER_TOPK]


def _take_max(cur, iota, exact):
    v = jnp.max(cur, axis=0, keepdims=True)
    hit = cur == v
    if exact:
        idx = jnp.min(jnp.where(hit, iota, cur.shape[0]), axis=0, keepdims=True)
        hit = iota == idx
    return v, hit


def _top_ranks(s, iota_k, exact):
    rank = jnp.full(s.shape, RANK_NONE, F32)
    vals = []
    cur = s
    for a in range(PEER_TOPK):
        v, hit = _take_max(cur, iota_k, exact)
        rank = jnp.where(hit, float(a), rank)
        cur = jnp.where(hit, -jnp.inf, cur)
        vals.append(v)
    return vals, rank


def _peer_route_body(st_ref, seg_ref, r2_ref, ln_ref, g1_ref, g2_ref, exact):
    tn = st_ref.shape[1]
    nk = PEER_NKEYS
    cells = _peer_cells()
    n_cell = len(cells)
    n_pad = -(-n_cell // 8) * 8
    n_seg = seg_ref.shape[1]
    iota_k = lax.broadcasted_iota(jnp.int32, (nk, tn), 0)
    iota_c = lax.broadcasted_iota(jnp.int32, (n_pad, tn), 0)
    count = lambda m: jnp.sum(m.astype(F32), axis=0, keepdims=True)
    tied = jnp.zeros((1, tn), jnp.bool_)
    for h in range(PEER_HEADS):
        s1 = st_ref[(2 * h) * nk:(2 * h + 1) * nk, :]
        s2 = st_ref[(2 * h + 1) * nk:(2 * h + 2) * nk, :]
        v1, rank1 = _top_ranks(s1, iota_k, exact)
        v2, rank2 = _top_ranks(s2, iota_k, exact)
        cand = jnp.concatenate([v1[a] + v2[b] for a, b in cells]
                               + [jnp.full((n_pad - n_cell, tn), -jnp.inf, F32)], axis=0)
        top = v1[0] + v2[0]
        e_c = jnp.exp(cand - top)
        chosen = jnp.zeros((n_pad, tn), jnp.bool_)
        cur = cand
        for _ in range(PEER_TOPK):
            _, hit = _take_max(cur, iota_c, exact)
            chosen = chosen | hit
            cur = jnp.where(hit, -jnp.inf, cur)
        chosen_f = chosen.astype(F32)
        if not exact:
            k = float(PEER_TOPK)
            tied = (tied | (count(rank1 != RANK_NONE) != k) | (count(rank2 != RANK_NONE) != k)
                    | (jnp.sum(chosen_f, axis=0, keepdims=True) != k))
        z = jnp.sum(chosen_f * e_c, axis=0, keepdims=True)
        chosen_pad = jnp.concatenate([chosen_f, jnp.zeros((n_seg - n_pad, tn), F32)], axis=0)
        rowlen = jnp.dot(seg_ref[...], chosen_pad.astype(BF16), preferred_element_type=F32)
        ln = jnp.zeros((nk, tn), F32)
        for a in range(PEER_TOPK):
            ln = jnp.where(rank1 == float(a), rowlen[a:a + 1, :], ln)
        rows = slice(h * nk, (h + 1) * nk)
        r2_ref[rows, :] = rank2.astype(r2_ref.dtype)
        ln_ref[h] = ln
        g1_ref[h] = jnp.exp(s1 - v1[0]) / z
        g2_ref[rows, :] = jnp.exp(s2 - v2[0]).astype(g2_ref.dtype)
    return tied


def _peer_route_kernel(st_ref, seg_ref, r2_ref, ln_ref, g1_ref, g2_ref):
    refs = (st_ref, seg_ref, r2_ref, ln_ref, g1_ref, g2_ref)
    tied = _peer_route_body(*refs, exact=False)

    @pl.when(jnp.max(tied.astype(F32)) > 0.0)
    def _():
        _peer_route_body(*refs, exact=True)


def _peer_route(st, tn=256):
    n_rows, t = st.shape
    cells = _peer_cells()
    seg = np.zeros((PEER_TOPK, PEER_NKEYS), np.float32)
    for c, (a, _) in enumerate(cells):
        seg[a, c] = 1.0
    out_rows = PEER_HEADS * PEER_NKEYS
    shp = jax.ShapeDtypeStruct((out_rows, t), BF16)
    shp_row = jax.ShapeDtypeStruct((PEER_HEADS, PEER_NKEYS, t), F32)
    spec = pl.BlockSpec((out_rows, tn), lambda i: (0, i))
    spec_row = pl.BlockSpec((PEER_HEADS, PEER_NKEYS, tn), lambda i: (0, 0, i))
    return pl.pallas_call(
        _peer_route_kernel,
        grid=(t // tn,),
        in_specs=[pl.BlockSpec((n_rows, tn), lambda i: (0, i)),
                  pl.BlockSpec((PEER_TOPK, PEER_NKEYS), lambda i: (0, 0))],
        out_specs=[spec, spec_row, spec_row, spec],
        out_shape=[shp, shp_row, shp_row, shp],
        compiler_params=_cparams(1),
    )(st, jnp.asarray(seg, dtype=BF16))


def _peer_expert_kernel(h_ref, u_ref, vt_ref, r2_ref, ln_ref, g1_ref, g2_ref, x_ref, gate_ref, nf_ref,
                        o_ref, acc_t, p_scr, pre_a, pre_b, *, final_norm):
    c = pl.program_id(1)
    last = pl.num_programs(1) - 1
    ec = u_ref.shape[0]
    nk = PEER_NKEYS
    tm = h_ref.shape[1]
    zero = jnp.zeros((), BF16)

    def score(pre_out):
        pre_out[...] = jnp.dot(u_ref[...], h_ref[...], preferred_element_type=F32)

    def finish(pre_in):
        for ii in range(ec // nk):
            w = jnp.zeros((nk, tm), BF16)
            for h in range(PEER_HEADS):
                rows = slice(h * nk, (h + 1) * nk)
                ln_row = ln_ref[h, ii:ii + 1, :].astype(BF16)
                g1_row = g1_ref[h, ii:ii + 1, :].astype(BF16)
                w = w + jnp.where(r2_ref[rows, :] < ln_row, g2_ref[rows, :], zero) * g1_row
            act = jax.nn.gelu(pre_in[ii * nk:(ii + 1) * nk, :])
            p_scr[ii * nk:(ii + 1) * nk, :] = w * act.astype(BF16)
        acc_t[...] += jnp.dot(vt_ref[0], p_scr[...], preferred_element_type=F32)

    @pl.when(c == 0)
    def _():
        acc_t[...] = jnp.zeros_like(acc_t)
        score(pre_a)

    @pl.when((c > 0) & (c < last) & (c % 2 == 1))
    def _():
        score(pre_b)
        finish(pre_a)

    @pl.when((c > 0) & (c < last) & (c % 2 == 0))
    def _():
        score(pre_a)
        finish(pre_b)

    @pl.when(c == last)
    def _():
        finish(pre_b if PEER_LAST_IS_EVEN else pre_a)
        xo = x_ref[...] + gate_ref[...] * acc_t[...].T
        if final_norm:
            ms = jnp.mean(xo * xo, axis=-1, keepdims=True)
            xo = (xo * lax.rsqrt(ms + RMS_EPS)) * nf_ref[...]
        o_ref[...] = xo


def _peer_experts(ht, u_bf, vt_bf, route, x, gate, norm_final, final_norm, tm=512, ec=PEER_EXPERT_CHUNK):
    t, d = x.shape
    e = u_bf.shape[0]
    n_chunks = e // ec
    assert (n_chunks % 2 == 0) == PEER_LAST_IS_EVEN
    r2, ln, g1, g2 = route
    rspec = pl.BlockSpec((r2.shape[0], tm), lambda i, c: (0, i))
    prev = lambda c: jnp.maximum(c - 1, 0)
    kspec = pl.BlockSpec((PEER_HEADS, ec // PEER_NKEYS, tm), lambda i, c: (0, prev(c), i))
    row = lambda i, c: (0, 0)
    return pl.pallas_call(
        functools.partial(_peer_expert_kernel, final_norm=final_norm),
        grid=(t // tm, n_chunks + 1),
        in_specs=[pl.BlockSpec((d, tm), lambda i, c: (0, i)),
                  pl.BlockSpec((ec, d), lambda i, c: (jnp.minimum(c, n_chunks - 1), 0)),
                  pl.BlockSpec((1, d, ec), lambda i, c: (prev(c), 0, 0)),
                  rspec, kspec, kspec, rspec,
                  pl.BlockSpec((tm, d), lambda i, c: (i, 0)),
                  pl.BlockSpec((1, d), row), pl.BlockSpec((1, d), row)],
        out_specs=pl.BlockSpec((tm, d), lambda i, c: (i, 0)),
        out_shape=jax.ShapeDtypeStruct((t, d), F32),
        scratch_shapes=[pltpu.VMEM((d, tm), F32), pltpu.VMEM((ec, tm), BF16),
                        pltpu.VMEM((ec, tm), F32), pltpu.VMEM((ec, tm), F32)],
        compiler_params=_cparams(2),
    )(ht, u_bf, vt_bf, r2, ln, g1, g2, x, gate, norm_final)


def _peer_layer(x, g, sh, sc, gate, w_q, sub_keys, u_tab, v_tab, norm_final, final_norm):
    ht, st = _peer_query(x, g, sh, sc, w_q, sub_keys)
    route = _peer_route(st)
    e, d = v_tab.shape
    vt = v_tab.astype(BF16).reshape(e // PEER_EXPERT_CHUNK, PEER_EXPERT_CHUNK, d).transpose(0, 2, 1)
    return _peer_experts(ht, u_tab.astype(BF16), vt, route, x, gate, norm_final, final_norm)


def kernel(x, c, ada_w, ada_b, norm_mix, norm_ffn, norm_final, ssm_w_in, ssm_lambda_re, ssm_lambda_im, ssm_log_dt, ssm_b_re, ssm_b_im, ssm_c_re, ssm_c_im, ssm_d, ssm_w_out, nsa_w_in, nsa_cmp_k_pos, nsa_cmp_k_w1, nsa_cmp_k_w2, nsa_cmp_v_pos, nsa_cmp_v_w1, nsa_cmp_v_w2, nsa_w_out, peer_w_q, peer_sub_keys, peer_u, peer_v):
    bsz, l_all, d = x.shape
    assert bsz == 1
    depth = ada_w.shape[0]
    mod = _adaln(c, ada_w, ada_b)
    xt = x.reshape(l_all, d)
    nf = norm_final.reshape(1, d)
    for i in range(depth):
        sh1, sc1, g1, sh2, sc2, g2 = [mod[i, :, k * d:(k + 1) * d] for k in range(6)]
        j = i // 2
        gm = norm_mix[i].reshape(1, d)
        if i % 2 == 0:
            xt = _s5_layer(xt, gm, sh1, sc1, g1, ssm_w_in[j], ssm_lambda_re[j], ssm_lambda_im[j],
                           ssm_log_dt[j], ssm_b_re[j], ssm_b_im[j], ssm_c_re[j], ssm_c_im[j],
                           ssm_d[j], ssm_w_out[j])
        else:
            xt = _nsa_layer(xt, gm, sh1, sc1, g1, nsa_w_in[j], nsa_cmp_k_pos[j], nsa_cmp_k_w1[j],
                            nsa_cmp_k_w2[j], nsa_cmp_v_pos[j], nsa_cmp_v_w1[j], nsa_cmp_v_w2[j],
                            nsa_w_out[j])
        xt = _peer_layer(xt, norm_ffn[i].reshape(1, d), sh2, sc2, g2, peer_w_q[i], peer_sub_keys[i],
                         peer_u[i], peer_v[i], nf, final_norm=(i == depth - 1))
    return xt.reshape(bsz, l_all, d)
```

```python
import functools
import math

import numpy as np
import jax
import jax.numpy as jnp
from jax import lax
from jax.experimental import pallas as pl
from jax.experimental.pallas import tpu as pltpu

F32 = jnp.float32
BF16 = jnp.bfloat16

RMS_EPS = 1e-6

SSM_GROUP = 16
SSM_STATE = 64
SSM_BLOCK_GROUPS = 8
S5_SUBSEQ = 8
S5_CHUNK = 128
S5_LANE_BLOCK = 1024

NSA_HEADS = 16
NSA_HEAD_DIM = 128
NSA_KV_GROUPS = 4
NSA_REP = NSA_HEADS // NSA_KV_GROUPS
N_BRANCH = 3
CMP_LEN = 32
CMP_STRIDE = 16
SEL_LEN = 64
SEL_TOPN = 16
WIN = 512
Q_BLOCK = 128
SEL_KV_TILE = 512
MASK_PENALTY = -(2.0 ** 30)
SEL_DEN_LIMIT = 1e30
ONEHOT_PERIOD = 128

PEER_HEADS = 8
PEER_NKEYS = 128
PEER_TOPK = 16
PEER_HALF = 128
PEER_EXPERT_CHUNK = 1024
PEER_LAST_IS_EVEN = (PEER_NKEYS ** 2 // PEER_EXPERT_CHUNK) % 2 == 0
RANK_NONE = 255.0

VMEM_LIMIT = 56 * 1024 * 1024
NT_DIMS = (((1,), (1,)), ((), ()))


def _cparams(n_axes):
    return pltpu.CompilerParams(
        dimension_semantics=("arbitrary",) * n_axes, vmem_limit_bytes=VMEM_LIMIT)


def _norm_mod(x, g, sh, sc):
    ms = jnp.mean(x * x, axis=-1, keepdims=True)
    xn = x * lax.rsqrt(ms + RMS_EPS)
    return (xn * g) * (1.0 + sc) + sh


def _adaln_kernel(c_ref, w_ref, b_ref, o_ref):
    c = c_ref[...]
    cond = c * jax.nn.sigmoid(c)
    o_ref[0] = jnp.dot(cond.astype(BF16), w_ref[0].astype(BF16),
                       preferred_element_type=F32) + b_ref[0]


def _adaln(c, ada_w, ada_b):
    depth, d, n = ada_w.shape
    tn = 1024
    c8 = jnp.broadcast_to(c, (8, d))
    out = pl.pallas_call(
        _adaln_kernel,
        grid=(depth, n // tn),
        in_specs=[pl.BlockSpec((8, d), lambda i, j: (0, 0)),
                  pl.BlockSpec((1, d, tn), lambda i, j: (i, 0, j)),
                  pl.BlockSpec((1, 1, tn), lambda i, j: (i, 0, j))],
        out_specs=pl.BlockSpec((1, 8, tn), lambda i, j: (i, 0, j)),
        out_shape=jax.ShapeDtypeStruct((depth, 8, n), F32),
        compiler_params=_cparams(2),
    )(c8, ada_w, ada_b.reshape(depth, 1, n))
    return out[:, 0:1, :]


def _nmm_kernel(x_ref, g_ref, sh_ref, sc_ref, w_ref, cs_ref, o_ref, h_scr, *, act):
    @pl.when(pl.program_id(1) == 0)
    def _():
        h_scr[...] = _norm_mod(x_ref[...], g_ref[...], sh_ref[...], sc_ref[...]).astype(BF16)

    acc = jnp.dot(h_scr[...], w_ref[...], preferred_element_type=F32)
    if act == "sigmoid":
        acc = jax.nn.sigmoid(acc)
    else:
        acc = acc * cs_ref[...]
    o_ref[...] = acc.astype(o_ref.dtype)


def _norm_mod_matmul(x, g, sh, sc, w, colscale, out_dtype, act="scale", tm=512, tn=1024):
    t, d = x.shape
    n = w.shape[1]
    tn = min(tn, n)
    row = lambda i, j: (0, 0)
    return pl.pallas_call(
        functools.partial(_nmm_kernel, act=act),
        grid=(t // tm, n // tn),
        in_specs=[pl.BlockSpec((tm, d), lambda i, j: (i, 0)),
                  pl.BlockSpec((1, d), row), pl.BlockSpec((1, d), row), pl.BlockSpec((1, d), row),
                  pl.BlockSpec((d, tn), lambda i, j: (0, j)),
                  pl.BlockSpec((1, tn), lambda i, j: (0, j))],
        out_specs=pl.BlockSpec((tm, tn), lambda i, j: (i, j)),
        out_shape=jax.ShapeDtypeStruct((t, n), out_dtype),
        scratch_shapes=[pltpu.VMEM((tm, d), BF16)],
        compiler_params=_cparams(2),
    )(x, g, sh, sc, w, colscale)


def _mm_res_kernel(a_ref, w_ref, x_ref, g_ref, o_ref):
    y = jnp.dot(a_ref[...], w_ref[...], preferred_element_type=F32)
    o_ref[...] = x_ref[...] + g_ref[...] * y


def _mm_glu_res_kernel(a_ref, wa_ref, wb_ref, x_ref, g_ref, o_ref):
    a = a_ref[...]
    ya = jnp.dot(a, wa_ref[...], preferred_element_type=F32)
    yb = jnp.dot(a, wb_ref[...], preferred_element_type=F32)
    o_ref[...] = x_ref[...] + g_ref[...] * (ya * jax.nn.sigmoid(yb))


def _matmul_residual(a, w, x, gate, glu, tm=512, tn=1024):
    t, k = a.shape
    n = x.shape[1]
    nb = n // tn
    a_spec = pl.BlockSpec((tm, k), lambda i, j: (i, 0))
    w_spec = pl.BlockSpec((k, tn), lambda i, j: (0, j))
    tail = [pl.BlockSpec((tm, tn), lambda i, j: (i, j)), pl.BlockSpec((1, tn), lambda i, j: (0, j))]
    if glu:
        kern = _mm_glu_res_kernel
        in_specs = [a_spec, w_spec, pl.BlockSpec((k, tn), lambda i, j: (0, j + nb))] + tail
        args = (a, w, w, x, gate)
    else:
        kern = _mm_res_kernel
        in_specs = [a_spec, w_spec] + tail
        args = (a, w, x, gate)
    return pl.pallas_call(
        kern,
        grid=(t // tm, nb),
        in_specs=in_specs,
        out_specs=pl.BlockSpec((tm, tn), lambda i, j: (i, j)),
        out_shape=jax.ShapeDtypeStruct((t, n), F32),
        compiler_params=_cparams(2),
    )(*args)


def _s5_param_kernel(lre_ref, lim_ref, ldt_ref, bre_ref, bim_ref, ar_ref, ai_ref, bbr_ref, bbi_ref):
    lre = lre_ref[...]
    lim = lim_ref[...]
    dt = jnp.exp(ldt_ref[...])
    mag = jnp.exp(lre * dt)
    ar = mag * jnp.cos(lim * dt)
    ai = mag * jnp.sin(lim * dt)
    den = lre * lre + lim * lim
    cr = ((ar - 1.0) * lre + ai * lim) / den
    ci = (ai * lre - (ar - 1.0) * lim) / den
    ar_ref[...] = ar
    ai_ref[...] = ai
    bbr_ref[...] = cr * bre_ref[...] - ci * bim_ref[...]
    bbi_ref[...] = cr * bim_ref[...] + ci * bre_ref[...]


def _s5_params(lam_re, lam_im, log_dt, b_re, b_im):
    g, n, p = b_re.shape
    rep = lambda a: jnp.repeat(a, p, axis=1)
    shp = jax.ShapeDtypeStruct((g, n * p), F32)
    ar, ai, bbr, bbi = pl.pallas_call(
        _s5_param_kernel, out_shape=(shp, shp, shp, shp),
    )(rep(lam_re), rep(lam_im), log_dt.reshape(g, 1), b_re.reshape(g, n * p), b_im.reshape(g, n * p))
    return ar[:, ::p], ai[:, ::p], bbr.reshape(g, n, p), bbi.reshape(g, n, p)


def _s5_scan_kernel(u_ref, a_ref, bbr_ref, bbi_ref, cr_ref, cin_ref, d_ref, o_ref,
                    bur, bui, hin_r, hin_i, car_r, car_i, apr, api):
    t, width = u_ref.shape
    sub = t // S5_SUBSEQ
    lanes = bur.shape[1]
    nblk = bbr_ref.shape[0]
    kin = width // nblk
    kst = lanes // nblk

    @pl.when(pl.program_id(0) == 0)
    def _init():
        car_r[...] = jnp.zeros_like(car_r)
        car_i[...] = jnp.zeros_like(car_i)
        ar = a_ref[0:1, :]
        ai = a_ref[1:2, :]
        pr, pi = ar, ai
        apr[0:1, :] = pr
        api[0:1, :] = pi
        for i in range(1, sub):
            pr, pi = pr * ar - pi * ai, pr * ai + pi * ar
            apr[i:i + 1, :] = pr
            api[i:i + 1, :] = pi

    for k in range(nblk):
        uk = u_ref[:, kin * k:kin * (k + 1)].astype(BF16)
        bur[:, kst * k:kst * (k + 1)] = jnp.dot(uk, bbr_ref[k], preferred_element_type=F32)
        bui[:, kst * k:kst * (k + 1)] = jnp.dot(uk, bbi_ref[k], preferred_element_type=F32)

    lb = S5_LANE_BLOCK
    for b in range(lanes // lb):
        sl = slice(b * lb, (b + 1) * lb)
        ar = jnp.broadcast_to(a_ref[0:1, sl], (S5_SUBSEQ, lb))
        ai = jnp.broadcast_to(a_ref[1:2, sl], (S5_SUBSEQ, lb))
        sr = jnp.zeros((S5_SUBSEQ, lb), F32)
        si = jnp.zeros((S5_SUBSEQ, lb), F32)
        for i in range(sub):
            rows = slice(S5_SUBSEQ * i, S5_SUBSEQ * (i + 1))
            sr, si = (ar * sr - ai * si + bur[rows, sl], ar * si + ai * sr + bui[rows, sl])
            bur[rows, sl] = sr
            bui[rows, sl] = si
        asr = apr[sub - 1:sub, sl]
        asi = api[sub - 1:sub, sl]
        hr = car_r[0:1, sl]
        hi = car_i[0:1, sl]
        for j in range(S5_SUBSEQ):
            hin_r[j:j + 1, sl] = hr
            hin_i[j:j + 1, sl] = hi
            er = sr[j:j + 1, :]
            ei = si[j:j + 1, :]
            hr, hi = er + asr * hr - asi * hi, ei + asr * hi + asi * hr
        car_r[0:1, sl] = hr
        car_i[0:1, sl] = hi
        hinr = hin_r[:, sl]
        hini = hin_i[:, sl]
        for i in range(sub):
            rows = slice(S5_SUBSEQ * i, S5_SUBSEQ * (i + 1))
            pr = jnp.broadcast_to(apr[i:i + 1, sl], (S5_SUBSEQ, lb))
            pi = jnp.broadcast_to(api[i:i + 1, sl], (S5_SUBSEQ, lb))
            bur[rows, sl] = bur[rows, sl] + pr * hinr - pi * hini
            bui[rows, sl] = bui[rows, sl] + pr * hini + pi * hinr

    for k in range(nblk):
        sr = bur[:, kst * k:kst * (k + 1)].astype(BF16)
        si = bui[:, kst * k:kst * (k + 1)].astype(BF16)
        y = (jnp.dot(sr, cr_ref[k], preferred_element_type=F32)
             + jnp.dot(si, cin_ref[k], preferred_element_type=F32))
        cols = slice(kin * k, kin * (k + 1))
        y = y + d_ref[:, cols] * u_ref[:, cols]
        o_ref[:, cols] = jax.nn.gelu(y).astype(o_ref.dtype)


def _s5_scan(u_p, ar, ai, bb_r, bb_i, c_re, c_im, d_skip):
    t_all, width = u_p.shape
    g, n, p = bb_r.shape
    bg = SSM_BLOCK_GROUPS
    nblk = g // bg
    eye = jnp.eye(bg, dtype=F32)
    blk_b = lambda bb: jnp.einsum('kgnp,gh->kgphn', bb.reshape(nblk, bg, n, p), eye
                                  ).reshape(nblk, bg * p, bg * n).astype(BF16)
    blk_c = lambda cc: jnp.einsum('kgpn,gh->kgnhp', cc.reshape(nblk, bg, p, n), eye
                                  ).reshape(nblk, bg * n, bg * p).astype(BF16)
    lanes = g * n
    a8 = jnp.zeros((8, lanes), F32).at[0].set(ar.reshape(lanes)).at[1].set(ai.reshape(lanes))
    t = S5_CHUNK
    sub = t // S5_SUBSEQ
    full3 = lambda c: (0, 0, 0)
    wspec_b = pl.BlockSpec((nblk, bg * p, bg * n), full3)
    wspec_c = pl.BlockSpec((nblk, bg * n, bg * p), full3)
    return pl.pallas_call(
        _s5_scan_kernel,
        grid=(t_all // t,),
        in_specs=[pl.BlockSpec((t, width), lambda c: (c, 0)),
                  pl.BlockSpec((8, lanes), lambda c: (0, 0)),
                  wspec_b, wspec_b, wspec_c, wspec_c,
                  pl.BlockSpec((1, width), lambda c: (0, 0))],
        out_specs=pl.BlockSpec((t, width), lambda c: (c, 0)),
        out_shape=jax.ShapeDtypeStruct((t_all, width), BF16),
        scratch_shapes=[pltpu.VMEM((t, lanes), F32), pltpu.VMEM((t, lanes), F32),
                        pltpu.VMEM((8, lanes), F32), pltpu.VMEM((8, lanes), F32),
                        pltpu.VMEM((8, lanes), F32), pltpu.VMEM((8, lanes), F32),
                        pltpu.VMEM((sub, lanes), F32), pltpu.VMEM((sub, lanes), F32)],
        compiler_params=_cparams(1),
    )(u_p, a8, blk_b(bb_r), blk_b(bb_i), blk_c(c_re), blk_c(-c_im), d_skip.reshape(1, width))


def _s5_rows_to_subseq(x, inverse=False):
    t_all, d = x.shape
    sub = S5_CHUNK // S5_SUBSEQ
    shape = (t_all // S5_CHUNK, sub, S5_SUBSEQ, d) if inverse else (t_all // S5_CHUNK, S5_SUBSEQ, sub, d)
    return x.reshape(shape).transpose(0, 2, 1, 3).reshape(t_all, d)


def _s5_layer(x, g, sh, sc, gate, w_in, lam_re, lam_im, log_dt, b_re, b_im, c_re, c_im, d_skip, w_out):
    d = x.shape[1]
    x_p = _s5_rows_to_subseq(x)
    ones = jnp.ones((1, w_in.shape[1]), F32)
    u_p = _norm_mod_matmul(x_p, g, sh, sc, w_in.astype(BF16), ones, F32)
    ar, ai, bb_r, bb_i = _s5_params(lam_re, lam_im, log_dt, b_re, b_im)
    gy_p = _s5_scan(u_p, ar, ai, bb_r, bb_i, c_re, c_im, d_skip)
    xn_p = _matmul_residual(gy_p, w_out.astype(BF16), x_p, gate, glu=True)
    return _s5_rows_to_subseq(xn_p, inverse=True)


def _cmp_kernel(x_ref, w1a_ref, w1b_ref, pos_ref, w1_ref, w2_ref, o_ref, acc_a, acc_b):
    l = pl.program_id(2)

    @pl.when(l == 0)
    def _():
        acc_a[...] = jnp.zeros_like(acc_a)
        acc_b[...] = jnp.zeros_like(acc_b)

    x = x_ref[...]
    acc_a[...] += jnp.dot(x, w1a_ref[0, 0], preferred_element_type=F32)
    acc_b[...] += jnp.dot(x, w1b_ref[0, 0], preferred_element_type=F32)

    @pl.when(l == pl.num_programs(2) - 1)
    def _():
        m = acc_a.shape[0]
        posc = jnp.zeros((8, NSA_HEAD_DIM), F32)
        for ll in range(CMP_LEN):
            prow = jnp.broadcast_to(pos_ref[0, ll:ll + 1, :], (8, NSA_HEAD_DIM)).astype(BF16)
            posc = posc + jnp.dot(prow, w1_ref[0, ll], preferred_element_type=F32)
        pre = acc_a[...] + pltpu.roll(acc_b[...], m - 1, 0) + posc[0:1, :]
        hid = jax.nn.gelu(pre)
        o_ref[0, 0] = jnp.dot(hid.astype(BF16), w2_ref[0], preferred_element_type=F32).astype(o_ref.dtype)


def _compress(qkv, kv_col0, pos, w1, w2):
    l_all, c = qkv.shape
    half = CMP_LEN // 2
    m = l_all // half
    cb = c // NSA_HEAD_DIM
    x2 = qkv.reshape(m, half * c)
    g = NSA_KV_GROUPS
    dh = NSA_HEAD_DIM
    return pl.pallas_call(
        _cmp_kernel,
        grid=(2, g, half),
        in_specs=[pl.BlockSpec((m, dh), lambda s, gg, l: (0, l * cb + kv_col0 + g * s + gg)),
                  pl.BlockSpec((1, 1, dh, dh), lambda s, gg, l: (s, l, 0, 0)),
                  pl.BlockSpec((1, 1, dh, dh), lambda s, gg, l: (s, l + half, 0, 0)),
                  pl.BlockSpec((1, CMP_LEN, dh), lambda s, gg, l: (s, 0, 0)),
                  pl.BlockSpec((1, CMP_LEN, dh, dh), lambda s, gg, l: (s, 0, 0, 0)),
                  pl.BlockSpec((1, dh, dh), lambda s, gg, l: (s, 0, 0))],
        out_specs=pl.BlockSpec((1, 1, m, dh), lambda s, gg, l: (s, gg, 0, 0)),
        out_shape=jax.ShapeDtypeStruct((2, g, m, dh), BF16),
        scratch_shapes=[pltpu.VMEM((m, dh), F32), pltpu.VMEM((m, dh), F32)],
        compiler_params=_cparams(3),
    )(x2, w1, w1, pos, w1, w2)


def _softmax2_rows(s, mask):
    s = jnp.where(mask, s, -1e30)
    m = jnp.max(s, axis=-1, keepdims=True)
    p = jnp.exp2(s - m)
    return p, 1.0 / jnp.maximum(jnp.sum(p, axis=-1, keepdims=True), 1e-30)


def _nsa_attn_kernel(q_ref, gt_ref, kc_ref, vc_ref, agg_ref, ksa_ref, vsa_ref, kw_ref, vw_ref,
                     o_ref, qaug_scr, s_a, s_b, acc_scr, m_scr, shift_scr):
    b = pl.program_id(1)
    qb = q_ref.shape[0]
    dh = NSA_HEAD_DIM
    rep = NSA_REP
    rows = rep * qb
    n_cmp = kc_ref.shape[2]
    n_sel = agg_ref.shape[1]
    t0 = b * qb

    q_blk = q_ref[...]
    qs = jnp.concatenate([q_blk[:, r * dh:(r + 1) * dh] for r in range(rep)], axis=0)
    tpos = t0 + lax.broadcasted_iota(jnp.int32, (rows, 1), 0) % qb

    kc = kc_ref[0, 0]
    s_c = lax.dot_general(qs, kc, NT_DIMS, preferred_element_type=F32)
    cmp_end = lax.broadcasted_iota(jnp.int32, (1, n_cmp), 1) * CMP_STRIDE + (CMP_LEN - 1)
    p_c, inv_c = _softmax2_rows(s_c, cmp_end <= tpos)
    p_c = p_c * jnp.where(tpos >= CMP_LEN - 1, inv_c, 0.0)
    o_c = jnp.dot(p_c.astype(BF16), vc_ref[0, 0], preferred_element_type=F32)

    psum = p_c[0:qb]
    for r in range(1, rep):
        psum = psum + p_c[r * qb:(r + 1) * qb]
    p_hi = psum.astype(BF16)
    p_lo = (psum - p_hi.astype(F32)).astype(BF16)
    agg = agg_ref[...]
    imp = (jnp.dot(p_hi, agg, preferred_element_type=F32)
           + jnp.dot(p_lo, agg, preferred_element_type=F32))
    imp_t = imp.T
    tq = t0 + lax.broadcasted_iota(jnp.int32, (1, qb), 1)
    blk = lax.broadcasted_iota(jnp.int32, (n_sel, qb), 0)
    blk_f = blk.astype(F32)
    cur = tq // SEL_LEN
    forced = (blk == 0) | (blk == cur) | (blk == cur - 1)
    valid = blk * SEL_LEN <= tq
    score = jnp.where(forced, 1e9, jnp.where(valid, imp_t, -1e9))
    pen_t = jnp.full((n_sel, qb), MASK_PENALTY, F32)
    for _ in range(min(SEL_TOPN, n_sel)):
        mx = jnp.max(score, axis=0, keepdims=True)
        idx = jnp.min(jnp.where(score == mx, blk_f, float(n_sel)), axis=0, keepdims=True)
        hit = blk_f == idx
        pen_t = jnp.where(hit, 0.0, pen_t)
        score = jnp.where(hit, -jnp.inf, score)
    pen = pen_t.T.astype(BF16)
    n_half = qaug_scr.shape[0]
    for hh in range(n_half):
        if n_sel >= ONEHOT_PERIOD:
            ph = pen[:, hh * ONEHOT_PERIOD:(hh + 1) * ONEHOT_PERIOD]
        else:
            ph = jnp.concatenate(
                [pen, jnp.full((qb, ONEHOT_PERIOD - n_sel), MASK_PENALTY, BF16)], axis=1)
        qaug_scr[hh] = jnp.concatenate([qs, jnp.concatenate([ph] * rep, axis=0)], axis=1)

    kt = SEL_KV_TILE
    period_keys = ONEHOT_PERIOD * SEL_LEN

    def sel_scores(i, s_out):
        k0 = pl.multiple_of(i * kt, kt)
        s_out[...] = lax.dot_general(qaug_scr[k0 // period_keys], ksa_ref[pl.ds(k0, kt), :], NT_DIMS,
                                     preferred_element_type=F32)

    def causal_mask(i, s):
        kpos = i * kt + lax.broadcasted_iota(jnp.int32, (1, kt), 1)
        return jnp.where(kpos <= tpos, s, MASK_PENALTY)

    def update_running_max(i, s_in, causal):
        k0 = pl.multiple_of(i * kt, kt)
        s = causal_mask(i, s_in[...]) if causal else s_in[...]
        m_run = m_scr[...]
        m_new = jnp.maximum(m_run, jnp.max(s, axis=-1, keepdims=True))
        p = jnp.exp2(s - m_new)
        acc_scr[...] = (jnp.exp2(m_run - m_new) * acc_scr[...]
                        + jnp.dot(p.astype(BF16), vsa_ref[pl.ds(k0, kt), :], preferred_element_type=F32))
        m_scr[...] = m_new

    def update_fixed_shift(i, s_in, causal):
        k0 = pl.multiple_of(i * kt, kt)
        s = causal_mask(i, s_in[...]) if causal else s_in[...]
        shift = shift_scr[...]
        p = jnp.concatenate([jnp.exp2(s[:, c * dh:(c + 1) * dh] - shift).astype(BF16)
                             for c in range(kt // dh)], axis=1)
        acc_scr[...] += jnp.dot(p, vsa_ref[pl.ds(k0, kt), :], preferred_element_type=F32)

    def sweep(update, first_tile_scored):
        def pair(j, carry):
            sel_scores(2 * j + 1, s_b)
            update(2 * j, s_a, False)
            sel_scores(2 * j + 2, s_a)
            update(2 * j + 1, s_b, False)
            return carry

        acc_scr[...] = jnp.zeros_like(acc_scr)
        if not first_tile_scored:
            sel_scores(0, s_a)
        lax.fori_loop(0, n_full // 2, pair, 0)
        odd = n_full % 2 == 1

        @pl.when(odd)
        def _():
            sel_scores(n_full, s_b)
            update(n_full - 1, s_a, False)
            update(n_full, s_b, True)

        @pl.when(jnp.logical_not(odd))
        def _():
            update(n_full, s_a, True)

    n_full = t0 // kt
    sel_scores(0, s_a)
    first = jnp.max(causal_mask(0, s_a[...]), axis=-1, keepdims=True)
    shift_scr[...] = jnp.broadcast_to(first, shift_scr.shape)
    sweep(update_fixed_shift, True)
    den = acc_scr[:, dh:dh + 1]
    overflowed = jnp.max(jnp.where((den > 0.0) & (den < SEL_DEN_LIMIT), 0.0, 1.0)) > 0.0

    @pl.when(overflowed)
    def _():
        m_scr[...] = jnp.full(m_scr.shape, -1e30, F32)
        sweep(update_running_max, False)

    acc_s = acc_scr[...]
    o_s = acc_s[:, 0:dh] * (1.0 / acc_s[:, dh:dh + 1])

    wlen = WIN + qb
    w0 = pl.multiple_of(jnp.maximum(t0 - WIN, 0), qb)
    s_w = lax.dot_general(qs, kw_ref[pl.ds(w0, wlen), :], NT_DIMS, preferred_element_type=F32)
    diff = tpos - (w0 + lax.broadcasted_iota(jnp.int32, (1, wlen), 1))
    p_w, inv_w = _softmax2_rows(s_w, (diff >= 0) & (diff < WIN))
    o_w = jnp.dot((p_w * inv_w).astype(BF16), vw_ref[pl.ds(w0, wlen), :], preferred_element_type=F32)

    gt = gt_ref[...]
    for r in range(rep):
        rs = slice(r * qb, (r + 1) * qb)
        o = (gt[:, r:r + 1] * o_c[rs]
             + gt[:, rep + r:rep + r + 1] * o_s[rs]
             + gt[:, 2 * rep + r:2 * rep + r + 1] * o_w[rs])
        o_ref[:, r * dh:(r + 1) * dh] = o.astype(o_ref.dtype)


def _nsa_agg(n_cmp_pad, n_cmp, n_sel):
    ratio, span = SEL_LEN // CMP_STRIDE, CMP_LEN // CMP_STRIDE
    agg = np.zeros((n_cmp_pad, n_sel), np.float32)
    jj = np.arange(n_sel)
    for m in range(ratio):
        for n in range(span):
            ii = ratio * jj + m - n
            ok = (ii >= 0) & (ii < n_cmp)
            agg[ii[ok], jj[ok]] += 1.0
    return jnp.asarray(agg, dtype=BF16)


def _nsa_attention(qkv, gates, kv_cmp):
    l_all = qkv.shape[0]
    dh = NSA_HEAD_DIM
    g = NSA_KV_GROUPS
    qb = Q_BLOCK
    qw = NSA_REP * dh
    n_cmp_pad = l_all // CMP_STRIDE
    n_cmp = (l_all - CMP_LEN) // CMP_STRIDE + 1
    n_sel = l_all // SEL_LEN
    agg = _nsa_agg(n_cmp_pad, n_cmp, n_sel)
    n_half = max(n_sel // ONEHOT_PERIOD, 1)
    q_blocks = (NSA_HEADS * dh) // dh
    key = np.arange(l_all)
    onehot = jnp.asarray((key[:, None] // SEL_LEN) % ONEHOT_PERIOD == np.arange(ONEHOT_PERIOD)[None, :],
                         dtype=BF16)
    ones_col = jnp.asarray(np.arange(dh)[None, :] == 0, dtype=BF16)
    grouped = lambda base: qkv[:, (q_blocks + base) * dh:(q_blocks + base + g) * dh].reshape(l_all, g, dh)
    ks_aug = jnp.concatenate([grouped(2 * g), jnp.broadcast_to(onehot[:, None, :], (l_all, g, ONEHOT_PERIOD))],
                             axis=-1).reshape(l_all, g * (dh + ONEHOT_PERIOD))
    vs_aug = jnp.concatenate([grouped(3 * g), jnp.broadcast_to(ones_col[:, None, :], (l_all, g, dh))],
                             axis=-1).reshape(l_all, g * 2 * dh)
    once = dict(pipeline_mode=pl.Buffered(1))
    kv_spec = lambda base: pl.BlockSpec((l_all, dh), lambda gg, b: (0, q_blocks + base + gg), **once)
    aug_spec = pl.BlockSpec((l_all, 2 * dh), lambda gg, b: (0, gg), **once)
    rows = NSA_REP * qb
    return pl.pallas_call(
        _nsa_attn_kernel,
        grid=(g, l_all // qb),
        in_specs=[pl.BlockSpec((qb, qw), lambda gg, b: (b, gg)),
                  pl.BlockSpec((qb, dh), lambda gg, b: (b, gg)),
                  pl.BlockSpec((1, 1, n_cmp_pad, dh), lambda gg, b: (0, gg, 0, 0)),
                  pl.BlockSpec((1, 1, n_cmp_pad, dh), lambda gg, b: (1, gg, 0, 0)),
                  pl.BlockSpec((n_cmp_pad, n_sel), lambda gg, b: (0, 0)),
                  aug_spec, aug_spec, kv_spec(4 * g), kv_spec(5 * g)],
        out_specs=pl.BlockSpec((qb, qw), lambda gg, b: (b, gg)),
        out_shape=jax.ShapeDtypeStruct((l_all, NSA_HEADS * dh), BF16),
        scratch_shapes=[pltpu.VMEM((n_half, rows, dh + ONEHOT_PERIOD), BF16),
                        pltpu.VMEM((rows, SEL_KV_TILE), F32), pltpu.VMEM((rows, SEL_KV_TILE), F32),
                        pltpu.VMEM((rows, 2 * dh), F32), pltpu.VMEM((rows, 1), F32),
                        pltpu.VMEM((rows, dh), F32)],
        compiler_params=_cparams(2),
    )(qkv, gates, kv_cmp, kv_cmp, agg, ks_aug, vs_aug, qkv, qkv)


def _nsa_layer(x, g, sh, sc, gate, w_in, k_pos, k_w1, k_w2, v_pos, v_w1, v_w2, w_out):
    dh = NSA_HEAD_DIM
    qd = NSA_HEADS * dh
    kvd = 2 * N_BRANCH * NSA_KV_GROUPS * dh
    w_main = w_in[:, :qd + kvd].astype(BF16)
    colscale = jnp.concatenate([jnp.full((1, qd), dh ** -0.5 * math.log2(math.e), F32),
                                jnp.ones((1, kvd), F32)], axis=1)
    qkv = _norm_mod_matmul(x, g, sh, sc, w_main, colscale, BF16)
    wg = w_in[:, qd + kvd:].reshape(-1, NSA_KV_GROUPS, NSA_REP, N_BRANCH).transpose(0, 1, 3, 2)
    wg = wg.reshape(-1, NSA_KV_GROUPS, N_BRANCH * NSA_REP)
    wg = jnp.pad(wg, ((0, 0), (0, 0), (0, dh - N_BRANCH * NSA_REP))).reshape(-1, NSA_KV_GROUPS * dh)
    gates = _norm_mod_matmul(x, g, sh, sc, wg.astype(BF16), jnp.ones((1, wg.shape[1]), F32), F32,
                             act="sigmoid")
    kv_cmp = _compress(qkv, qd // dh, jnp.stack([k_pos, v_pos]),
                       jnp.stack([k_w1, v_w1]).astype(BF16), jnp.stack([k_w2, v_w2]).astype(BF16))
    o = _nsa_attention(qkv, gates, kv_cmp)
    return _matmul_residual(o, w_out.astype(BF16), x, gate, glu=False)


def _peer_query_kernel(x_ref, g_ref, sh_ref, sc_ref, wq_ref, keys_ref, ht_ref, st_ref):
    h32 = _norm_mod(x_ref[...], g_ref[...], sh_ref[...], sc_ref[...])
    ht_ref[...] = h32.T.astype(BF16)
    q = jnp.dot(h32.astype(BF16), wq_ref[...], preferred_element_type=F32).astype(BF16)
    for hc in range(keys_ref.shape[0]):
        rows = slice(hc * PEER_NKEYS, (hc + 1) * PEER_NKEYS)
        st_ref[rows, :] = lax.dot_general(keys_ref[hc], q[:, hc * PEER_HALF:(hc + 1) * PEER_HALF],
                                          NT_DIMS, preferred_element_type=F32)


def _peer_query(x, g, sh, sc, w_q, sub_keys, tm=512):
    t, d = x.shape
    nq = w_q.shape[1]
    keys = sub_keys.reshape(-1, PEER_NKEYS, PEER_HALF).astype(BF16)
    row = lambda i: (0, 0)
    return pl.pallas_call(
        _peer_query_kernel,
        grid=(t // tm,),
        in_specs=[pl.BlockSpec((tm, d), lambda i: (i, 0)),
                  pl.BlockSpec((1, d), row), pl.BlockSpec((1, d), row), pl.BlockSpec((1, d), row),
                  pl.BlockSpec((d, nq), row),
                  pl.BlockSpec(keys.shape, lambda i: (0, 0, 0))],
        out_specs=[pl.BlockSpec((d, tm), lambda i: (0, i)),
                   pl.BlockSpec((keys.shape[0] * PEER_NKEYS, tm), lambda i: (0, i))],
        out_shape=[jax.ShapeDtypeStruct((d, t), BF16),
                   jax.ShapeDtypeStruct((keys.shape[0] * PEER_NKEYS, t), F32)],
        compiler_params=_cparams(1),
    )(x, g, sh, sc, w_q.astype(BF16), keys)


def _peer_cells():
    return [(a, b) for a in range(PEER_TOPK) for b in range(PEER_TOPK) if (a + 1) * (b + 1) <= PEER_TOPK]


def _take_max(cur, iota, exact):
    v = jnp.max(cur, axis=0, keepdims=True)
    hit = cur == v
    if exact:
        idx = jnp.min(jnp.where(hit, iota, cur.shape[0]), axis=0, keepdims=True)
        hit = iota == idx
    return v, hit


def _top_ranks(s, iota_k, exact):
    rank = jnp.full(s.shape, RANK_NONE, F32)
    vals = []
    cur = s
    for a in range(PEER_TOPK):
        v, hit = _take_max(cur, iota_k, exact)
        rank = jnp.where(hit, float(a), rank)
        cur = jnp.where(hit, -jnp.inf, cur)
        vals.append(v)
    return vals, rank


def _peer_route_body(st_ref, seg_ref, r2_ref, ln_ref, g1_ref, g2_ref, exact):
    tn = st_ref.shape[1]
    nk = PEER_NKEYS
    cells = _peer_cells()
    n_cell = len(cells)
    n_pad = -(-n_cell // 8) * 8
    n_seg = seg_ref.shape[1]
    iota_k = lax.broadcasted_iota(jnp.int32, (nk, tn), 0)
    iota_c = lax.broadcasted_iota(jnp.int32, (n_pad, tn), 0)
    count = lambda m: jnp.sum(m.astype(F32), axis=0, keepdims=True)
    tied = jnp.zeros((1, tn), jnp.bool_)
    for h in range(PEER_HEADS):
        s1 = st_ref[(2 * h) * nk:(2 * h + 1) * nk, :]
        s2 = st_ref[(2 * h + 1) * nk:(2 * h + 2) * nk, :]
        v1, rank1 = _top_ranks(s1, iota_k, exact)
        v2, rank2 = _top_ranks(s2, iota_k, exact)
        cand = jnp.concatenate([v1[a] + v2[b] for a, b in cells]
                               + [jnp.full((n_pad - n_cell, tn), -jnp.inf, F32)], axis=0)
        top = v1[0] + v2[0]
        e_c = jnp.exp(cand - top)
        chosen = jnp.zeros((n_pad, tn), jnp.bool_)
        cur = cand
        for _ in range(PEER_TOPK):
            _, hit = _take_max(cur, iota_c, exact)
            chosen = chosen | hit
            cur = jnp.where(hit, -jnp.inf, cur)
        chosen_f = chosen.astype(F32)
        if not exact:
            k = float(PEER_TOPK)
            tied = (tied | (count(rank1 != RANK_NONE) != k) | (count(rank2 != RANK_NONE) != k)
                    | (jnp.sum(chosen_f, axis=0, keepdims=True) != k))
        z = jnp.sum(chosen_f * e_c, axis=0, keepdims=True)
        chosen_pad = jnp.concatenate([chosen_f, jnp.zeros((n_seg - n_pad, tn), F32)], axis=0)
        rowlen = jnp.dot(seg_ref[...], chosen_pad.astype(BF16), preferred_element_type=F32)
        ln = jnp.zeros((nk, tn), F32)
        for a in range(PEER_TOPK):
            ln = jnp.where(rank1 == float(a), rowlen[a:a + 1, :], ln)
        rows = slice(h * nk, (h + 1) * nk)
        r2_ref[rows, :] = rank2.astype(r2_ref.dtype)
        ln_ref[h] = ln
        g1_ref[h] = jnp.exp(s1 - v1[0]) / z
        g2_ref[rows, :] = jnp.exp(s2 - v2[0]).astype(g2_ref.dtype)
    return tied


def _peer_route_kernel(st_ref, seg_ref, r2_ref, ln_ref, g1_ref, g2_ref):
    refs = (st_ref, seg_ref, r2_ref, ln_ref, g1_ref, g2_ref)
    tied = _peer_route_body(*refs, exact=False)

    @pl.when(jnp.max(tied.astype(F32)) > 0.0)
    def _():
        _peer_route_body(*refs, exact=True)


def _peer_route(st, tn=256):
    n_rows, t = st.shape
    cells = _peer_cells()
    seg = np.zeros((PEER_TOPK, PEER_NKEYS), np.float32)
    for c, (a, _) in enumerate(cells):
        seg[a, c] = 1.0
    out_rows = PEER_HEADS * PEER_NKEYS
    shp = jax.ShapeDtypeStruct((out_rows, t), BF16)
    shp_row = jax.ShapeDtypeStruct((PEER_HEADS, PEER_NKEYS, t), F32)
    spec = pl.BlockSpec((out_rows, tn), lambda i: (0, i))
    spec_row = pl.BlockSpec((PEER_HEADS, PEER_NKEYS, tn), lambda i: (0, 0, i))
    return pl.pallas_call(
        _peer_route_kernel,
        grid=(t // tn,),
        in_specs=[pl.BlockSpec((n_rows, tn), lambda i: (0, i)),
                  pl.BlockSpec((PEER_TOPK, PEER_NKEYS), lambda i: (0, 0))],
        out_specs=[spec, spec_row, spec_row, spec],
        out_shape=[shp, shp_row, shp_row, shp],
        compiler_params=_cparams(1),
    )(st, jnp.asarray(seg, dtype=BF16))


def _peer_expert_kernel(h_ref, u_ref, vt_ref, r2_ref, ln_ref, g1_ref, g2_ref, x_ref, gate_ref, nf_ref,
                        o_ref, acc_t, p_scr, pre_a, pre_b, *, final_norm):
    c = pl.program_id(1)
    last = pl.num_programs(1) - 1
    ec = u_ref.shape[0]
    nk = PEER_NKEYS
    tm = h_ref.shape[1]
    zero = jnp.zeros((), BF16)

    def score(pre_out):
        pre_out[...] = jnp.dot(u_ref[...], h_ref[...], preferred_element_type=F32)

    def finish(pre_in):
        for ii in range(ec // nk):
            w = jnp.zeros((nk, tm), BF16)
            for h in range(PEER_HEADS):
                rows = slice(h * nk, (h + 1) * nk)
                ln_row = ln_ref[h, ii:ii + 1, :].astype(BF16)
                g1_row = g1_ref[h, ii:ii + 1, :].astype(BF16)
                w = w + jnp.where(r2_ref[rows, :] < ln_row, g2_ref[rows, :], zero) * g1_row
            act = jax.nn.gelu(pre_in[ii * nk:(ii + 1) * nk, :])
            p_scr[ii * nk:(ii + 1) * nk, :] = w * act.astype(BF16)
        acc_t[...] += jnp.dot(vt_ref[0], p_scr[...], preferred_element_type=F32)

    @pl.when(c == 0)
    def _():
        acc_t[...] = jnp.zeros_like(acc_t)
        score(pre_a)

    @pl.when((c > 0) & (c < last) & (c % 2 == 1))
    def _():
        score(pre_b)
        finish(pre_a)

    @pl.when((c > 0) & (c < last) & (c % 2 == 0))
    def _():
        score(pre_a)
        finish(pre_b)

    @pl.when(c == last)
    def _():
        finish(pre_b if PEER_LAST_IS_EVEN else pre_a)
        xo = x_ref[...] + gate_ref[...] * acc_t[...].T
        if final_norm:
            ms = jnp.mean(xo * xo, axis=-1, keepdims=True)
            xo = (xo * lax.rsqrt(ms + RMS_EPS)) * nf_ref[...]
        o_ref[...] = xo


def _peer_experts(ht, u_bf, vt_bf, route, x, gate, norm_final, final_norm, tm=512, ec=PEER_EXPERT_CHUNK):
    t, d = x.shape
    e = u_bf.shape[0]
    n_chunks = e // ec
    assert (n_chunks % 2 == 0) == PEER_LAST_IS_EVEN
    r2, ln, g1, g2 = route
    rspec = pl.BlockSpec((r2.shape[0], tm), lambda i, c: (0, i))
    prev = lambda c: jnp.maximum(c - 1, 0)
    kspec = pl.BlockSpec((PEER_HEADS, ec // PEER_NKEYS, tm), lambda i, c: (0, prev(c), i))
    row = lambda i, c: (0, 0)
    return pl.pallas_call(
        functools.partial(_peer_expert_kernel, final_norm=final_norm),
        grid=(t // tm, n_chunks + 1),
        in_specs=[pl.BlockSpec((d, tm), lambda i, c: (0, i)),
                  pl.BlockSpec((ec, d), lambda i, c: (jnp.minimum(c, n_chunks - 1), 0)),
                  pl.BlockSpec((1, d, ec), lambda i, c: (prev(c), 0, 0)),
                  rspec, kspec, kspec, rspec,
                  pl.BlockSpec((tm, d), lambda i, c: (i, 0)),
                  pl.BlockSpec((1, d), row), pl.BlockSpec((1, d), row)],
        out_specs=pl.BlockSpec((tm, d), lambda i, c: (i, 0)),
        out_shape=jax.ShapeDtypeStruct((t, d), F32),
        scratch_shapes=[pltpu.VMEM((d, tm), F32), pltpu.VMEM((ec, tm), BF16),
                        pltpu.VMEM((ec, tm), F32), pltpu.VMEM((ec, tm), F32)],
        compiler_params=_cparams(2),
    )(ht, u_bf, vt_bf, r2, ln, g1, g2, x, gate, norm_final)


def _peer_layer(x, g, sh, sc, gate, w_q, sub_keys, u_tab, v_tab, norm_final, final_norm):
    ht, st = _peer_query(x, g, sh, sc, w_q, sub_keys)
    route = _peer_route(st)
    e, d = v_tab.shape
    vt = v_tab.astype(BF16).reshape(e // PEER_EXPERT_CHUNK, PEER_EXPERT_CHUNK, d).transpose(0, 2, 1)
    return _peer_experts(ht, u_tab.astype(BF16), vt, route, x, gate, norm_final, final_norm)


def kernel(x, c, ada_w, ada_b, norm_mix, norm_ffn, norm_final, ssm_w_in, ssm_lambda_re, ssm_lambda_im, ssm_log_dt, ssm_b_re, ssm_b_im, ssm_c_re, ssm_c_im, ssm_d, ssm_w_out, nsa_w_in, nsa_cmp_k_pos, nsa_cmp_k_w1, nsa_cmp_k_w2, nsa_cmp_v_pos, nsa_cmp_v_w1, nsa_cmp_v_w2, nsa_w_out, peer_w_q, peer_sub_keys, peer_u, peer_v):
    bsz, l_all, d = x.shape
    assert bsz == 1
    depth = ada_w.shape[0]
    mod = _adaln(c, ada_w, ada_b)
    xt = x.reshape(l_all, d)
    nf = norm_final.reshape(1, d)
    for i in range(depth):
        sh1, sc1, g1, sh2, sc2, g2 = [mod[i, :, k * d:(k + 1) * d] for k in range(6)]
        j = i // 2
        gm = norm_mix[i].reshape(1, d)
        if i % 2 == 0:
            xt = _s5_layer(xt, gm, sh1, sc1, g1, ssm_w_in[j], ssm_lambda_re[j], ssm_lambda_im[j],
                           ssm_log_dt[j], ssm_b_re[j], ssm_b_im[j], ssm_c_re[j], ssm_c_im[j],
                           ssm_d[j], ssm_w_out[j])
        else:
            xt = _nsa_layer(xt, gm, sh1, sc1, g1, nsa_w_in[j], nsa_cmp_k_pos[j], nsa_cmp_k_w1[j],
                            nsa_cmp_k_w2[j], nsa_cmp_v_pos[j], nsa_cmp_v_w1[j], nsa_cmp_v_w2[j],
                            nsa_w_out[j])
        xt = _peer_layer(xt, norm_ffn[i].reshape(1, d), sh2, sc2, g2, peer_w_q[i], peer_sub_keys[i],
                         peer_u[i], peer_v[i], nf, final_norm=(i == depth - 1))
    return xt.reshape(bsz, l_all, d)
```

```python
import functools
import math

import numpy as np
import jax
import jax.numpy as jnp
from jax import lax
from jax.experimental import pallas as pl
from jax.experimental.pallas import tpu as pltpu

F32 = jnp.float32
BF16 = jnp.bfloat16

RMS_EPS = 1e-6

SSM_GROUP = 16
SSM_STATE = 64
SSM_BLOCK_GROUPS = 8
S5_SUBSEQ = 8
S5_CHUNK = 128
S5_LANE_BLOCK = 1024

NSA_HEADS = 16
NSA_HEAD_DIM = 128
NSA_KV_GROUPS = 4
NSA_REP = NSA_HEADS // NSA_KV_GROUPS
N_BRANCH = 3
CMP_LEN = 32
CMP_STRIDE = 16
SEL_LEN = 64
SEL_TOPN = 16
WIN = 512
Q_BLOCK = 128
SEL_KV_TILE = 512
MASK_PENALTY = -(2.0 ** 30)
SEL_DEN_LIMIT = 1e30
ONEHOT_PERIOD = 128

PEER_HEADS = 8
PEER_NKEYS = 128
PEER_TOPK = 16
PEER_HALF = 128
PEER_EXPERT_CHUNK = 1024
PEER_LAST_IS_EVEN = (PEER_NKEYS ** 2 // PEER_EXPERT_CHUNK) % 2 == 0
RANK_NONE = 255.0

VMEM_LIMIT = 56 * 1024 * 1024
NT_DIMS = (((1,), (1,)), ((), ()))


def _cparams(n_axes):
    return pltpu.CompilerParams(
        dimension_semantics=("arbitrary",) * n_axes, vmem_limit_bytes=VMEM_LIMIT)


def _norm_mod(x, g, sh, sc):
    ms = jnp.mean(x * x, axis=-1, keepdims=True)
    xn = x * lax.rsqrt(ms + RMS_EPS)
    return (xn * g) * (1.0 + sc) + sh


def _adaln_kernel(c_ref, w_ref, b_ref, o_ref):
    c = c_ref[...]
    cond = c * jax.nn.sigmoid(c)
    o_ref[0] = jnp.dot(cond.astype(BF16), w_ref[0].astype(BF16),
                       preferred_element_type=F32) + b_ref[0]


def _adaln(c, ada_w, ada_b):
    depth, d, n = ada_w.shape
    tn = 1024
    c8 = jnp.broadcast_to(c, (8, d))
    out = pl.pallas_call(
        _adaln_kernel,
        grid=(depth, n // tn),
        in_specs=[pl.BlockSpec((8, d), lambda i, j: (0, 0)),
                  pl.BlockSpec((1, d, tn), lambda i, j: (i, 0, j)),
                  pl.BlockSpec((1, 1, tn), lambda i, j: (i, 0, j))],
        out_specs=pl.BlockSpec((1, 8, tn), lambda i, j: (i, 0, j)),
        out_shape=jax.ShapeDtypeStruct((depth, 8, n), F32),
        compiler_params=_cparams(2),
    )(c8, ada_w, ada_b.reshape(depth, 1, n))
    return out[:, 0:1, :]


def _nmm_kernel(x_ref, g_ref, sh_ref, sc_ref, w_ref, cs_ref, o_ref, h_scr, *, act):
    @pl.when(pl.program_id(1) == 0)
    def _():
        h_scr[...] = _norm_mod(x_ref[...], g_ref[...], sh_ref[...], sc_ref[...]).astype(BF16)

    acc = jnp.dot(h_scr[...], w_ref[...], preferred_element_type=F32)
    if act == "sigmoid":
        acc = jax.nn.sigmoid(acc)
    else:
        acc = acc * cs_ref[...]
    o_ref[...] = acc.astype(o_ref.dtype)


def _norm_mod_matmul(x, g, sh, sc, w, colscale, out_dtype, act="scale", tm=512, tn=1024):
    t, d = x.shape
    n = w.shape[1]
    tn = min(tn, n)
    row = lambda i, j: (0, 0)
    return pl.pallas_call(
        functools.partial(_nmm_kernel, act=act),
        grid=(t // tm, n // tn),
        in_specs=[pl.BlockSpec((tm, d), lambda i, j: (i, 0)),
                  pl.BlockSpec((1, d), row), pl.BlockSpec((1, d), row), pl.BlockSpec((1, d), row),
                  pl.BlockSpec((d, tn), lambda i, j: (0, j)),
                  pl.BlockSpec((1, tn), lambda i, j: (0, j))],
        out_specs=pl.BlockSpec((tm, tn), lambda i, j: (i, j)),
        out_shape=jax.ShapeDtypeStruct((t, n), out_dtype),
        scratch_shapes=[pltpu.VMEM((tm, d), BF16)],
        compiler_params=_cparams(2),
    )(x, g, sh, sc, w, colscale)


def _mm_res_kernel(a_ref, w_ref, x_ref, g_ref, o_ref):
    y = jnp.dot(a_ref[...], w_ref[...], preferred_element_type=F32)
    o_ref[...] = x_ref[...] + g_ref[...] * y


def _mm_glu_res_kernel(a_ref, wa_ref, wb_ref, x_ref, g_ref, o_ref):
    a = a_ref[...]
    ya = jnp.dot(a, wa_ref[...], preferred_element_type=F32)
    yb = jnp.dot(a, wb_ref[...], preferred_element_type=F32)
    o_ref[...] = x_ref[...] + g_ref[...] * (ya * jax.nn.sigmoid(yb))


def _matmul_residual(a, w, x, gate, glu, tm=512, tn=1024):
    t, k = a.shape
    n = x.shape[1]
    nb = n // tn
    a_spec = pl.BlockSpec((tm, k), lambda i, j: (i, 0))
    w_spec = pl.BlockSpec((k, tn), lambda i, j: (0, j))
    tail = [pl.BlockSpec((tm, tn), lambda i, j: (i, j)), pl.BlockSpec((1, tn), lambda i, j: (0, j))]
    if glu:
        kern = _mm_glu_res_kernel
        in_specs = [a_spec, w_spec, pl.BlockSpec((k, tn), lambda i, j: (0, j + nb))] + tail
        args = (a, w, w, x, gate)
    else:
        kern = _mm_res_kernel
        in_specs = [a_spec, w_spec] + tail
        args = (a, w, x, gate)
    return pl.pallas_call(
        kern,
        grid=(t // tm, nb),
        in_specs=in_specs,
        out_specs=pl.BlockSpec((tm, tn), lambda i, j: (i, j)),
        out_shape=jax.ShapeDtypeStruct((t, n), F32),
        compiler_params=_cparams(2),
    )(*args)


def _s5_param_kernel(lre_ref, lim_ref, ldt_ref, bre_ref, bim_ref, ar_ref, ai_ref, bbr_ref, bbi_ref):
    lre = lre_ref[...]
    lim = lim_ref[...]
    dt = jnp.exp(ldt_ref[...])
    mag = jnp.exp(lre * dt)
    ar = mag * jnp.cos(lim * dt)
    ai = mag * jnp.sin(lim * dt)
    den = lre * lre + lim * lim
    cr = ((ar - 1.0) * lre + ai * lim) / den
    ci = (ai * lre - (ar - 1.0) * lim) / den
    ar_ref[...] = ar
    ai_ref[...] = ai
    bbr_ref[...] = cr * bre_ref[...] - ci * bim_ref[...]
    bbi_ref[...] = cr * bim_ref[...] + ci * bre_ref[...]


def _s5_params(lam_re, lam_im, log_dt, b_re, b_im):
    g, n, p = b_re.shape
    rep = lambda a: jnp.repeat(a, p, axis=1)
    shp = jax.ShapeDtypeStruct((g, n * p), F32)
    ar, ai, bbr, bbi = pl.pallas_call(
        _s5_param_kernel, out_shape=(shp, shp, shp, shp),
    )(rep(lam_re), rep(lam_im), log_dt.reshape(g, 1), b_re.reshape(g, n * p), b_im.reshape(g, n * p))
    return ar[:, ::p], ai[:, ::p], bbr.reshape(g, n, p), bbi.reshape(g, n, p)


def _s5_scan_kernel(u_ref, a_ref, bbr_ref, bbi_ref, cr_ref, cin_ref, d_ref, o_ref,
                    bur, bui, hin_r, hin_i, car_r, car_i, apr, api):
    t, width = u_ref.shape
    sub = t // S5_SUBSEQ
    lanes = bur.shape[1]
    nblk = bbr_ref.shape[0]
    kin = width // nblk
    kst = lanes // nblk

    @pl.when(pl.program_id(0) == 0)
    def _init():
        car_r[...] = jnp.zeros_like(car_r)
        car_i[...] = jnp.zeros_like(car_i)
        ar = a_ref[0:1, :]
        ai = a_ref[1:2, :]
        pr, pi = ar, ai
        apr[0:1, :] = pr
        api[0:1, :] = pi
        for i in range(1, sub):
            pr, pi = pr * ar - pi * ai, pr * ai + pi * ar
            apr[i:i + 1, :] = pr
            api[i:i + 1, :] = pi

    for k in range(nblk):
        uk = u_ref[:, kin * k:kin * (k + 1)].astype(BF16)
        bur[:, kst * k:kst * (k + 1)] = jnp.dot(uk, bbr_ref[k], preferred_element_type=F32)
        bui[:, kst * k:kst * (k + 1)] = jnp.dot(uk, bbi_ref[k], preferred_element_type=F32)

    lb = S5_LANE_BLOCK
    for b in range(lanes // lb):
        sl = slice(b * lb, (b + 1) * lb)
        ar = jnp.broadcast_to(a_ref[0:1, sl], (S5_SUBSEQ, lb))
        ai = jnp.broadcast_to(a_ref[1:2, sl], (S5_SUBSEQ, lb))
        sr = jnp.zeros((S5_SUBSEQ, lb), F32)
        si = jnp.zeros((S5_SUBSEQ, lb), F32)
        for i in range(sub):
            rows = slice(S5_SUBSEQ * i, S5_SUBSEQ * (i + 1))
            sr, si = (ar * sr - ai * si + bur[rows, sl], ar * si + ai * sr + bui[rows, sl])
            bur[rows, sl] = sr
            bui[rows, sl] = si
        asr = apr[sub - 1:sub, sl]
        asi = api[sub - 1:sub, sl]
        hr = car_r[0:1, sl]
        hi = car_i[0:1, sl]
        for j in range(S5_SUBSEQ):
            hin_r[j:j + 1, sl] = hr
            hin_i[j:j + 1, sl] = hi
            er = sr[j:j + 1, :]
            ei = si[j:j + 1, :]
            hr, hi = er + asr * hr - asi * hi, ei + asr * hi + asi * hr
        car_r[0:1, sl] = hr
        car_i[0:1, sl] = hi
        hinr = hin_r[:, sl]
        hini = hin_i[:, sl]
        for i in range(sub):
            rows = slice(S5_SUBSEQ * i, S5_SUBSEQ * (i + 1))
            pr = jnp.broadcast_to(apr[i:i + 1, sl], (S5_SUBSEQ, lb))
            pi = jnp.broadcast_to(api[i:i + 1, sl], (S5_SUBSEQ, lb))
            bur[rows, sl] = bur[rows, sl] + pr * hinr - pi * hini
            bui[rows, sl] = bui[rows, sl] + pr * hini + pi * hinr

    for k in range(nblk):
        sr = bur[:, kst * k:kst * (k + 1)].astype(BF16)
        si = bui[:, kst * k:kst * (k + 1)].astype(BF16)
        y = (jnp.dot(sr, cr_ref[k], preferred_element_type=F32)
             + jnp.dot(si, cin_ref[k], preferred_element_type=F32))
        cols = slice(kin * k, kin * (k + 1))
        y = y + d_ref[:, cols] * u_ref[:, cols]
        o_ref[:, cols] = jax.nn.gelu(y).astype(o_ref.dtype)


def _s5_scan(u_p, ar, ai, bb_r, bb_i, c_re, c_im, d_skip):
    t_all, width = u_p.shape
    g, n, p = bb_r.shape
    bg = SSM_BLOCK_GROUPS
    nblk = g // bg
    eye = jnp.eye(bg, dtype=F32)
    blk_b = lambda bb: jnp.einsum('kgnp,gh->kgphn', bb.reshape(nblk, bg, n, p), eye
                                  ).reshape(nblk, bg * p, bg * n).astype(BF16)
    blk_c = lambda cc: jnp.einsum('kgpn,gh->kgnhp', cc.reshape(nblk, bg, p, n), eye
                                  ).reshape(nblk, bg * n, bg * p).astype(BF16)
    lanes = g * n
    a8 = jnp.zeros((8, lanes), F32).at[0].set(ar.reshape(lanes)).at[1].set(ai.reshape(lanes))
    t = S5_CHUNK
    sub = t // S5_SUBSEQ
    full3 = lambda c: (0, 0, 0)
    wspec_b = pl.BlockSpec((nblk, bg * p, bg * n), full3)
    wspec_c = pl.BlockSpec((nblk, bg * n, bg * p), full3)
    return pl.pallas_call(
        _s5_scan_kernel,
        grid=(t_all // t,),
        in_specs=[pl.BlockSpec((t, width), lambda c: (c, 0)),
                  pl.BlockSpec((8, lanes), lambda c: (0, 0)),
                  wspec_b, wspec_b, wspec_c, wspec_c,
                  pl.BlockSpec((1, width), lambda c: (0, 0))],
        out_specs=pl.BlockSpec((t, width), lambda c: (c, 0)),
        out_shape=jax.ShapeDtypeStruct((t_all, width), BF16),
        scratch_shapes=[pltpu.VMEM((t, lanes), F32), pltpu.VMEM((t, lanes), F32),
                        pltpu.VMEM((8, lanes), F32), pltpu.VMEM((8, lanes), F32),
                        pltpu.VMEM((8, lanes), F32), pltpu.VMEM((8, lanes), F32),
                        pltpu.VMEM((sub, lanes), F32), pltpu.VMEM((sub, lanes), F32)],
        compiler_params=_cparams(1),
    )(u_p, a8, blk_b(bb_r), blk_b(bb_i), blk_c(c_re), blk_c(-c_im), d_skip.reshape(1, width))


def _s5_rows_to_subseq(x, inverse=False):
    t_all, d = x.shape
    sub = S5_CHUNK // S5_SUBSEQ
    shape = (t_all // S5_CHUNK, sub, S5_SUBSEQ, d) if inverse else (t_all // S5_CHUNK, S5_SUBSEQ, sub, d)
    return x.reshape(shape).transpose(0, 2, 1, 3).reshape(t_all, d)


def _s5_layer(x, g, sh, sc, gate, w_in, lam_re, lam_im, log_dt, b_re, b_im, c_re, c_im, d_skip, w_out):
    d = x.shape[1]
    x_p = _s5_rows_to_subseq(x)
    ones = jnp.ones((1, w_in.shape[1]), F32)
    u_p = _norm_mod_matmul(x_p, g, sh, sc, w_in.astype(BF16), ones, F32)
    ar, ai, bb_r, bb_i = _s5_params(lam_re, lam_im, log_dt, b_re, b_im)
    gy_p = _s5_scan(u_p, ar, ai, bb_r, bb_i, c_re, c_im, d_skip)
    xn_p = _matmul_residual(gy_p, w_out.astype(BF16), x_p, gate, glu=True)
    return _s5_rows_to_subseq(xn_p, inverse=True)


def _cmp_kernel(x_ref, w1a_ref, w1b_ref, pos_ref, w1_ref, w2_ref, o_ref, acc_a, acc_b):
    l = pl.program_id(2)

    @pl.when(l == 0)
    def _():
        acc_a[...] = jnp.zeros_like(acc_a)
        acc_b[...] = jnp.zeros_like(acc_b)

    x = x_ref[...]
    acc_a[...] += jnp.dot(x, w1a_ref[0, 0], preferred_element_type=F32)
    acc_b[...] += jnp.dot(x, w1b_ref[0, 0], preferred_element_type=F32)

    @pl.when(l == pl.num_programs(2) - 1)
    def _():
        m = acc_a.shape[0]
        posc = jnp.zeros((8, NSA_HEAD_DIM), F32)
        for ll in range(CMP_LEN):
            prow = jnp.broadcast_to(pos_ref[0, ll:ll + 1, :], (8, NSA_HEAD_DIM)).astype(BF16)
            posc = posc + jnp.dot(prow, w1_ref[0, ll], preferred_element_type=F32)
        pre = acc_a[...] + pltpu.roll(acc_b[...], m - 1, 0) + posc[0:1, :]
        hid = jax.nn.gelu(pre)
        o_ref[0, 0] = jnp.dot(hid.astype(BF16), w2_ref[0], preferred_element_type=F32).astype(o_ref.dtype)


def _compress(qkv, kv_col0, pos, w1, w2):
    l_all, c = qkv.shape
    half = CMP_LEN // 2
    m = l_all // half
    cb = c // NSA_HEAD_DIM
    x2 = qkv.reshape(m, half * c)
    g = NSA_KV_GROUPS
    dh = NSA_HEAD_DIM
    return pl.pallas_call(
        _cmp_kernel,
        grid=(2, g, half),
        in_specs=[pl.BlockSpec((m, dh), lambda s, gg, l: (0, l * cb + kv_col0 + g * s + gg)),
                  pl.BlockSpec((1, 1, dh, dh), lambda s, gg, l: (s, l, 0, 0)),
                  pl.BlockSpec((1, 1, dh, dh), lambda s, gg, l: (s, l + half, 0, 0)),
                  pl.BlockSpec((1, CMP_LEN, dh), lambda s, gg, l: (s, 0, 0)),
                  pl.BlockSpec((1, CMP_LEN, dh, dh), lambda s, gg, l: (s, 0, 0, 0)),
                  pl.BlockSpec((1, dh, dh), lambda s, gg, l: (s, 0, 0))],
        out_specs=pl.BlockSpec((1, 1, m, dh), lambda s, gg, l: (s, gg, 0, 0)),
        out_shape=jax.ShapeDtypeStruct((2, g, m, dh), BF16),
        scratch_shapes=[pltpu.VMEM((m, dh), F32), pltpu.VMEM((m, dh), F32)],
        compiler_params=_cparams(3),
    )(x2, w1, w1, pos, w1, w2)


def _softmax2_rows(s, mask):
    s = jnp.where(mask, s, -1e30)
    m = jnp.max(s, axis=-1, keepdims=True)
    p = jnp.exp2(s - m)
    return p, 1.0 / jnp.maximum(jnp.sum(p, axis=-1, keepdims=True), 1e-30)


def _shifted_softmax(s, mask, shift):
    p = jnp.exp2(jnp.where(mask, s - shift, -1e30))
    return p, jnp.sum(p, axis=-1, keepdims=True)


def _any_out_of_range(den, needed):
    ok = (den > 0.0) & (den < SEL_DEN_LIMIT)
    return jnp.max(jnp.where(needed & jnp.logical_not(ok), 1.0, 0.0)) > 0.0


def _nsa_attn_kernel(*refs):
    redo = _nsa_attn_body(True, *refs)

    @pl.when(redo)
    def _():
        _nsa_attn_body(False, *refs)


def _nsa_attn_body(fast, q_ref, gt_ref, kc_ref, vc_ref, agg_ref, ksa_ref, vsa_ref, kw_ref, vw_ref,
                   o_ref, qaug_scr, s_a, s_b, acc_scr, m_scr, shift_scr):
    b = pl.program_id(1)
    qb = q_ref.shape[0]
    dh = NSA_HEAD_DIM
    rep = NSA_REP
    rows = rep * qb
    n_cmp = kc_ref.shape[2]
    n_sel = agg_ref.shape[1]
    top_n = min(SEL_TOPN, n_sel)
    t0 = pl.multiple_of(b * qb, qb)

    q_blk = q_ref[...]
    qs = jnp.concatenate([q_blk[:, r * dh:(r + 1) * dh] for r in range(rep)], axis=0)
    tpos = t0 + lax.broadcasted_iota(jnp.int32, (rows, 1), 0) % qb
    all_rows = jnp.ones((rows, 1), jnp.bool_)

    kc = kc_ref[0, 0]
    s_c = lax.dot_general(qs, kc, NT_DIMS, preferred_element_type=F32)
    cmp_end = lax.broadcasted_iota(jnp.int32, (1, n_cmp), 1) * CMP_STRIDE + (CMP_LEN - 1)
    has_c = tpos >= CMP_LEN - 1
    if fast:
        p_c, den_c = _shifted_softmax(s_c, cmp_end <= tpos, s_c[:, 0:1])
        redo = _any_out_of_range(den_c, has_c)
        inv_c = 1.0 / jnp.maximum(den_c, 1e-30)
    else:
        p_c, inv_c = _softmax2_rows(s_c, cmp_end <= tpos)
    p_c = p_c * jnp.where(has_c, inv_c, 0.0)
    o_c = jnp.dot(p_c.astype(BF16), vc_ref[0, 0], preferred_element_type=F32)

    psum = p_c[0:qb]
    for r in range(1, rep):
        psum = psum + p_c[r * qb:(r + 1) * qb]
    p_hi = psum.astype(BF16)
    p_lo = (psum - p_hi.astype(F32)).astype(BF16)
    agg = agg_ref[...]
    imp = (jnp.dot(p_hi, agg, preferred_element_type=F32)
           + jnp.dot(p_lo, agg, preferred_element_type=F32))
    imp_t = imp.T
    tq = t0 + lax.broadcasted_iota(jnp.int32, (1, qb), 1)
    blk = lax.broadcasted_iota(jnp.int32, (n_sel, qb), 0)
    blk_f = blk.astype(F32)
    cur = tq // SEL_LEN
    valid = blk * SEL_LEN <= tq
    if fast:
        score = jnp.where(valid, imp_t, -1.0 - blk_f)
        score = jnp.where(blk == cur - 1, 1e9, score)
        score = jnp.where(blk == cur, 2e9, score)
        score = jnp.where(blk == 0, 3e9, score)
    else:
        forced = (blk == 0) | (blk == cur) | (blk == cur - 1)
        score = jnp.where(forced, 1e9, jnp.where(valid, imp_t, -1e9))
    pen_t = jnp.full((n_sel, qb), MASK_PENALTY, F32)
    for _ in range(top_n):
        mx = jnp.max(score, axis=0, keepdims=True)
        hit = score == mx
        if not fast:
            idx = jnp.min(jnp.where(hit, blk_f, float(n_sel)), axis=0, keepdims=True)
            hit = blk_f == idx
        pen_t = jnp.where(hit, 0.0, pen_t)
        score = jnp.where(hit, -jnp.inf, score)
    if fast:
        taken = jnp.sum(jnp.where(pen_t == 0.0, 1.0, 0.0), axis=0, keepdims=True)
        redo = redo | (jnp.max(jnp.where(taken != float(top_n), 1.0, 0.0)) > 0.0)
    pen = pen_t.T.astype(BF16)
    n_half = qaug_scr.shape[0]
    for hh in range(n_half):
        if n_sel >= ONEHOT_PERIOD:
            ph = pen[:, hh * ONEHOT_PERIOD:(hh + 1) * ONEHOT_PERIOD]
        else:
            ph = jnp.concatenate(
                [pen, jnp.full((qb, ONEHOT_PERIOD - n_sel), MASK_PENALTY, BF16)], axis=1)
        qaug_scr[hh] = jnp.concatenate([qs, jnp.concatenate([ph] * rep, axis=0)], axis=1)

    kt = SEL_KV_TILE
    period_keys = ONEHOT_PERIOD * SEL_LEN

    def sel_scores(i, s_out):
        k0 = pl.multiple_of(i * kt, kt)
        s_out[...] = lax.dot_general(qaug_scr[k0 // period_keys], ksa_ref[pl.ds(k0, kt), :], NT_DIMS,
                                     preferred_element_type=F32)

    def causal_mask(i, s):
        kpos = i * kt + lax.broadcasted_iota(jnp.int32, (1, kt), 1)
        return jnp.where(kpos <= tpos, s, MASK_PENALTY)

    def update_running_max(i, s_in, causal):
        k0 = pl.multiple_of(i * kt, kt)
        s = causal_mask(i, s_in[...]) if causal else s_in[...]
        m_run = m_scr[...]
        m_new = jnp.maximum(m_run, jnp.max(s, axis=-1, keepdims=True))
        p = jnp.exp2(s - m_new)
        acc_scr[...] = (jnp.exp2(m_run - m_new) * acc_scr[...]
                        + jnp.dot(p.astype(BF16), vsa_ref[pl.ds(k0, kt), :], preferred_element_type=F32))
        m_scr[...] = m_new

    def update_fixed_shift(i, s_in, causal):
        k0 = pl.multiple_of(i * kt, kt)
        s = causal_mask(i, s_in[...]) if causal else s_in[...]
        shift = shift_scr[...]
        p = jnp.concatenate([jnp.exp2(s[:, c * dh:(c + 1) * dh] - shift).astype(BF16)
                             for c in range(kt // dh)], axis=1)
        acc_scr[...] += jnp.dot(p, vsa_ref[pl.ds(k0, kt), :], preferred_element_type=F32)

    def sweep(update):
        def pair(j, carry):
            sel_scores(2 * j + 1, s_b)
            update(2 * j, s_a, False)
            sel_scores(2 * j + 2, s_a)
            update(2 * j + 1, s_b, False)
            return carry

        acc_scr[...] = jnp.zeros_like(acc_scr)
        lax.fori_loop(0, n_full // 2, pair, 0)
        odd = n_full % 2 == 1

        @pl.when(odd)
        def _():
            sel_scores(n_full, s_b)
            update(n_full - 1, s_a, False)
            update(n_full, s_b, True)

        @pl.when(jnp.logical_not(odd))
        def _():
            update(n_full, s_a, True)

    n_full = t0 // kt
    sel_scores(0, s_a)
    if fast:
        first = jnp.max(causal_mask(0, s_a[...]), axis=-1, keepdims=True)
        shift_scr[...] = jnp.broadcast_to(first, shift_scr.shape)
        sweep(update_fixed_shift)
    else:
        m_scr[...] = jnp.full(m_scr.shape, -1e30, F32)
        sweep(update_running_max)
    acc_s = acc_scr[...]
    den_s = acc_s[:, dh:dh + 1]
    o_s = acc_s[:, 0:dh] * (1.0 / den_s)

    wlen = WIN + qb
    w0 = pl.multiple_of(jnp.maximum(t0 - WIN, 0), qb)
    s_w = lax.dot_general(qs, kw_ref[pl.ds(w0, wlen), :], NT_DIMS, preferred_element_type=F32)
    diff = tpos - (w0 + lax.broadcasted_iota(jnp.int32, (1, wlen), 1))
    mask_w = (diff >= 0) & (diff < WIN)
    if fast:
        k_diag = jnp.concatenate([kw_ref[pl.ds(t0, qb), :]] * rep, axis=0).astype(F32)
        diag = jnp.sum(qs.astype(F32) * k_diag, axis=-1, keepdims=True)
        p_w, den_w = _shifted_softmax(s_w, mask_w, diag)
        o_w = jnp.dot(p_w.astype(BF16), vw_ref[pl.ds(w0, wlen), :], preferred_element_type=F32)
        o_w = o_w * (1.0 / den_w)
        redo = redo | _any_out_of_range(den_s, all_rows) | _any_out_of_range(den_w, all_rows)
    else:
        p_w, inv_w = _softmax2_rows(s_w, mask_w)
        o_w = jnp.dot((p_w * inv_w).astype(BF16), vw_ref[pl.ds(w0, wlen), :], preferred_element_type=F32)

    gt = gt_ref[...]
    for r in range(rep):
        rs = slice(r * qb, (r + 1) * qb)
        o = (gt[:, r:r + 1] * o_c[rs]
             + gt[:, rep + r:rep + r + 1] * o_s[rs]
             + gt[:, 2 * rep + r:2 * rep + r + 1] * o_w[rs])
        o_ref[:, r * dh:(r + 1) * dh] = o.astype(o_ref.dtype)
    return redo if fast else None


def _nsa_agg(n_cmp_pad, n_cmp, n_sel):
    ratio, span = SEL_LEN // CMP_STRIDE, CMP_LEN // CMP_STRIDE
    agg = np.zeros((n_cmp_pad, n_sel), np.float32)
    jj = np.arange(n_sel)
    for m in range(ratio):
        for n in range(span):
            ii = ratio * jj + m - n
            ok = (ii >= 0) & (ii < n_cmp)
            agg[ii[ok], jj[ok]] += 1.0
    return jnp.asarray(agg, dtype=BF16)


def _nsa_attention(qkv, gates, kv_cmp):
    l_all = qkv.shape[0]
    dh = NSA_HEAD_DIM
    g = NSA_KV_GROUPS
    qb = Q_BLOCK
    qw = NSA_REP * dh
    n_cmp_pad = l_all // CMP_STRIDE
    n_cmp = (l_all - CMP_LEN) // CMP_STRIDE + 1
    n_sel = l_all // SEL_LEN
    agg = _nsa_agg(n_cmp_pad, n_cmp, n_sel)
    n_half = max(n_sel // ONEHOT_PERIOD, 1)
    q_blocks = (NSA_HEADS * dh) // dh
    key = np.arange(l_all)
    onehot = jnp.asarray((key[:, None] // SEL_LEN) % ONEHOT_PERIOD == np.arange(ONEHOT_PERIOD)[None, :],
                         dtype=BF16)
    ones_col = jnp.asarray(np.arange(dh)[None, :] == 0, dtype=BF16)
    grouped = lambda base: qkv[:, (q_blocks + base) * dh:(q_blocks + base + g) * dh].reshape(l_all, g, dh)
    ks_aug = jnp.concatenate([grouped(2 * g), jnp.broadcast_to(onehot[:, None, :], (l_all, g, ONEHOT_PERIOD))],
                             axis=-1).reshape(l_all, g * (dh + ONEHOT_PERIOD))
    vs_aug = jnp.concatenate([grouped(3 * g), jnp.broadcast_to(ones_col[:, None, :], (l_all, g, dh))],
                             axis=-1).reshape(l_all, g * 2 * dh)
    once = dict(pipeline_mode=pl.Buffered(1))
    kv_spec = lambda base: pl.BlockSpec((l_all, dh), lambda gg, b: (0, q_blocks + base + gg), **once)
    aug_spec = pl.BlockSpec((l_all, 2 * dh), lambda gg, b: (0, gg), **once)
    rows = NSA_REP * qb
    return pl.pallas_call(
        _nsa_attn_kernel,
        grid=(g, l_all // qb),
        in_specs=[pl.BlockSpec((qb, qw), lambda gg, b: (b, gg)),
                  pl.BlockSpec((qb, dh), lambda gg, b: (b, gg)),
                  pl.BlockSpec((1, 1, n_cmp_pad, dh), lambda gg, b: (0, gg, 0, 0)),
                  pl.BlockSpec((1, 1, n_cmp_pad, dh), lambda gg, b: (1, gg, 0, 0)),
                  pl.BlockSpec((n_cmp_pad, n_sel), lambda gg, b: (0, 0)),
                  aug_spec, aug_spec, kv_spec(4 * g), kv_spec(5 * g)],
        out_specs=pl.BlockSpec((qb, qw), lambda gg, b: (b, gg)),
        out_shape=jax.ShapeDtypeStruct((l_all, NSA_HEADS * dh), BF16),
        scratch_shapes=[pltpu.VMEM((n_half, rows, dh + ONEHOT_PERIOD), BF16),
                        pltpu.VMEM((rows, SEL_KV_TILE), F32), pltpu.VMEM((rows, SEL_KV_TILE), F32),
                        pltpu.VMEM((rows, 2 * dh), F32), pltpu.VMEM((rows, 1), F32),
                        pltpu.VMEM((rows, dh), F32)],
        compiler_params=_cparams(2),
    )(qkv, gates, kv_cmp, kv_cmp, agg, ks_aug, vs_aug, qkv, qkv)


def _nsa_layer(x, g, sh, sc, gate, w_in, k_pos, k_w1, k_w2, v_pos, v_w1, v_w2, w_out):
    dh = NSA_HEAD_DIM
    qd = NSA_HEADS * dh
    kvd = 2 * N_BRANCH * NSA_KV_GROUPS * dh
    w_main = w_in[:, :qd + kvd].astype(BF16)
    colscale = jnp.concatenate([jnp.full((1, qd), dh ** -0.5 * math.log2(math.e), F32),
                                jnp.ones((1, kvd), F32)], axis=1)
    qkv = _norm_mod_matmul(x, g, sh, sc, w_main, colscale, BF16)
    wg = w_in[:, qd + kvd:].reshape(-1, NSA_KV_GROUPS, NSA_REP, N_BRANCH).transpose(0, 1, 3, 2)
    wg = wg.reshape(-1, NSA_KV_GROUPS, N_BRANCH * NSA_REP)
    wg = jnp.pad(wg, ((0, 0), (0, 0), (0, dh - N_BRANCH * NSA_REP))).reshape(-1, NSA_KV_GROUPS * dh)
    gates = _norm_mod_matmul(x, g, sh, sc, wg.astype(BF16), jnp.ones((1, wg.shape[1]), F32), F32,
                             act="sigmoid")
    kv_cmp = _compress(qkv, qd // dh, jnp.stack([k_pos, v_pos]),
                       jnp.stack([k_w1, v_w1]).astype(BF16), jnp.stack([k_w2, v_w2]).astype(BF16))
    o = _nsa_attention(qkv, gates, kv_cmp)
    return _matmul_residual(o, w_out.astype(BF16), x, gate, glu=False)


def _peer_query_kernel(x_ref, g_ref, sh_ref, sc_ref, wq_ref, keys_ref, ht_ref, st_ref):
    h32 = _norm_mod(x_ref[...], g_ref[...], sh_ref[...], sc_ref[...])
    ht_ref[...] = h32.T.astype(BF16)
    q = jnp.dot(h32.astype(BF16), wq_ref[...], preferred_element_type=F32).astype(BF16)
    for hc in range(keys_ref.shape[0]):
        rows = slice(hc * PEER_NKEYS, (hc + 1) * PEER_NKEYS)
        st_ref[rows, :] = lax.dot_general(keys_ref[hc], q[:, hc * PEER_HALF:(hc + 1) * PEER_HALF],
                                          NT_DIMS, preferred_element_type=F32)


def _peer_query(x, g, sh, sc, w_q, sub_keys, tm=512):
    t, d = x.shape
    nq = w_q.shape[1]
    keys = sub_keys.reshape(-1, PEER_NKEYS, PEER_HALF).astype(BF16)
    row = lambda i: (0, 0)
    return pl.pallas_call(
        _peer_query_kernel,
        grid=(t // tm,),
        in_specs=[pl.BlockSpec((tm, d), lambda i: (i, 0)),
                  pl.BlockSpec((1, d), row), pl.BlockSpec((1, d), row), pl.BlockSpec((1, d), row),
                  pl.BlockSpec((d, nq), row),
                  pl.BlockSpec(keys.shape, lambda i: (0, 0, 0))],
        out_specs=[pl.BlockSpec((d, tm), lambda i: (0, i)),
                   pl.BlockSpec((keys.shape[0] * PEER_NKEYS, tm), lambda i: (0, i))],
        out_shape=[jax.ShapeDtypeStruct((d, t), BF16),
                   jax.ShapeDtypeStruct((keys.shape[0] * PEER_NKEYS, t), F32)],
        compiler_params=_cparams(1),
    )(x, g, sh, sc, w_q.astype(BF16), keys)


def _peer_cells():
    return [(a, b) for a in range(PEER_TOPK) for b in range(PEER_TOPK) if (a + 1) * (b + 1) <= PEER_TOPK]


def _take_max(cur, iota, exact):
    v = jnp.max(cur, axis=0, keepdims=True)
    hit = cur == v
    if exact:
        idx = jnp.min(jnp.where(hit, iota, cur.shape[0]), axis=0, keepdims=True)
        hit = iota == idx
    return v, hit


def _top_ranks(s, iota_k, exact):
    rank = jnp.full(s.shape, RANK_NONE, F32)
    vals = []
    cur = s
    for a in range(PEER_TOPK):
        v, hit = _take_max(cur, iota_k, exact)
        rank = jnp.where(hit, float(a), rank)
        cur = jnp.where(hit, -jnp.inf, cur)
        vals.append(v)
    return vals, rank


def _peer_route_body(st_ref, seg_ref, r2_ref, ln_ref, g1_ref, g2_ref, exact):
    tn = st_ref.shape[1]
    nk = PEER_NKEYS
    cells = _peer_cells()
    n_cell = len(cells)
    n_pad = -(-n_cell // 8) * 8
    n_seg = seg_ref.shape[1]
    iota_k = lax.broadcasted_iota(jnp.int32, (nk, tn), 0)
    iota_c = lax.broadcasted_iota(jnp.int32, (n_pad, tn), 0)
    count = lambda m: jnp.sum(m.astype(F32), axis=0, keepdims=True)
    tied = jnp.zeros((1, tn), jnp.bool_)
    for h in range(PEER_HEADS):
        s1 = st_ref[(2 * h) * nk:(2 * h + 1) * nk, :]
        s2 = st_ref[(2 * h + 1) * nk:(2 * h + 2) * nk, :]
        v1, rank1 = _top_ranks(s1, iota_k, exact)
        v2, rank2 = _top_ranks(s2, iota_k, exact)
        cand = jnp.concatenate([v1[a] + v2[b] for a, b in cells]
                               + [jnp.full((n_pad - n_cell, tn), -jnp.inf, F32)], axis=0)
        top = v1[0] + v2[0]
        e_c = jnp.exp(cand - top)
        chosen = jnp.zeros((n_pad, tn), jnp.bool_)
        cur = cand
        for _ in range(PEER_TOPK):
            _, hit = _take_max(cur, iota_c, exact)
            chosen = chosen | hit
            cur = jnp.where(hit, -jnp.inf, cur)
        chosen_f = chosen.astype(F32)
        if not exact:
            k = float(PEER_TOPK)
            tied = (tied | (count(rank1 != RANK_NONE) != k) | (count(rank2 != RANK_NONE) != k)
                    | (jnp.sum(chosen_f, axis=0, keepdims=True) != k))
        z = jnp.sum(chosen_f * e_c, axis=0, keepdims=True)
        chosen_pad = jnp.concatenate([chosen_f, jnp.zeros((n_seg - n_pad, tn), F32)], axis=0)
        rowlen = jnp.dot(seg_ref[...], chosen_pad.astype(BF16), preferred_element_type=F32)
        ln = jnp.zeros((nk, tn), F32)
        for a in range(PEER_TOPK):
            ln = jnp.where(rank1 == float(a), rowlen[a:a + 1, :], ln)
        rows = slice(h * nk, (h + 1) * nk)
        r2_ref[rows, :] = rank2.astype(r2_ref.dtype)
        ln_ref[h] = ln
        g1_ref[h] = jnp.exp(s1 - v1[0]) / z
        g2_ref[rows, :] = jnp.exp(s2 - v2[0]).astype(g2_ref.dtype)
    return tied


def _peer_route_kernel(st_ref, seg_ref, r2_ref, ln_ref, g1_ref, g2_ref):
    refs = (st_ref, seg_ref, r2_ref, ln_ref, g1_ref, g2_ref)
    tied = _peer_route_body(*refs, exact=False)

    @pl.when(jnp.max(tied.astype(F32)) > 0.0)
    def _():
        _peer_route_body(*refs, exact=True)


def _peer_route(st, tn=128):
    n_rows, t = st.shape
    cells = _peer_cells()
    seg = np.zeros((PEER_TOPK, PEER_NKEYS), np.float32)
    for c, (a, _) in enumerate(cells):
        seg[a, c] = 1.0
    out_rows = PEER_HEADS * PEER_NKEYS
    shp = jax.ShapeDtypeStruct((out_rows, t), BF16)
    shp_row = jax.ShapeDtypeStruct((PEER_HEADS, PEER_NKEYS, t), F32)
    spec = pl.BlockSpec((out_rows, tn), lambda i: (0, i))
    spec_row = pl.BlockSpec((PEER_HEADS, PEER_NKEYS, tn), lambda i: (0, 0, i))
    return pl.pallas_call(
        _peer_route_kernel,
        grid=(t // tn,),
        in_specs=[pl.BlockSpec((n_rows, tn), lambda i: (0, i)),
                  pl.BlockSpec((PEER_TOPK, PEER_NKEYS), lambda i: (0, 0))],
        out_specs=[spec, spec_row, spec_row, spec],
        out_shape=[shp, shp_row, shp_row, shp],
        compiler_params=_cparams(1),
    )(st, jnp.asarray(seg, dtype=BF16))


def _peer_expert_kernel(h_ref, u_ref, vt_ref, r2_ref, ln_ref, g1_ref, g2_ref, x_ref, gate_ref, nf_ref,
                        o_ref, acc_t, p_scr, pre_a, pre_b, *, final_norm):
    c = pl.program_id(1)
    last = pl.num_programs(1) - 1
    ec = u_ref.shape[0]
    nk = PEER_NKEYS
    tm = h_ref.shape[1]
    zero = jnp.zeros((), BF16)

    def score(pre_out):
        pre_out[...] = jnp.dot(u_ref[...], h_ref[...], preferred_element_type=F32)

    def finish(pre_in):
        for ii in range(ec // nk):
            w = jnp.zeros((nk, tm), BF16)
            for h in range(PEER_HEADS):
                rows = slice(h * nk, (h + 1) * nk)
                ln_row = ln_ref[h, ii:ii + 1, :].astype(BF16)
                g1_row = g1_ref[h, ii:ii + 1, :].astype(BF16)
                w = w + jnp.where(r2_ref[rows, :] < ln_row, g2_ref[rows, :], zero) * g1_row
            act = jax.nn.gelu(pre_in[ii * nk:(ii + 1) * nk, :])
            p_scr[ii * nk:(ii + 1) * nk, :] = w * act.astype(BF16)
        acc_t[...] += jnp.dot(vt_ref[0], p_scr[...], preferred_element_type=F32)

    @pl.when(c == 0)
    def _():
        acc_t[...] = jnp.zeros_like(acc_t)
        score(pre_a)

    @pl.when((c > 0) & (c < last) & (c % 2 == 1))
    def _():
        score(pre_b)
        finish(pre_a)

    @pl.when((c > 0) & (c < last) & (c % 2 == 0))
    def _():
        score(pre_a)
        finish(pre_b)

    @pl.when(c == last)
    def _():
        finish(pre_b if PEER_LAST_IS_EVEN else pre_a)
        xo = x_ref[...] + gate_ref[...] * acc_t[...].T
        if final_norm:
            ms = jnp.mean(xo * xo, axis=-1, keepdims=True)
            xo = (xo * lax.rsqrt(ms + RMS_EPS)) * nf_ref[...]
        o_ref[...] = xo


def _peer_experts(ht, u_bf, vt_bf, route, x, gate, norm_final, final_norm, tm=512, ec=PEER_EXPERT_CHUNK):
    t, d = x.shape
    e = u_bf.shape[0]
    n_chunks = e // ec
    assert (n_chunks % 2 == 0) == PEER_LAST_IS_EVEN
    r2, ln, g1, g2 = route
    rspec = pl.BlockSpec((r2.shape[0], tm), lambda i, c: (0, i))
    prev = lambda c: jnp.maximum(c - 1, 0)
    kspec = pl.BlockSpec((PEER_HEADS, ec // PEER_NKEYS, tm), lambda i, c: (0, prev(c), i))
    row = lambda i, c: (0, 0)
    return pl.pallas_call(
        functools.partial(_peer_expert_kernel, final_norm=final_norm),
        grid=(t // tm, n_chunks + 1),
        in_specs=[pl.BlockSpec((d, tm), lambda i, c: (0, i)),
                  pl.BlockSpec((ec, d), lambda i, c: (jnp.minimum(c, n_chunks - 1), 0)),
                  pl.BlockSpec((1, d, ec), lambda i, c: (prev(c), 0, 0)),
                  rspec, kspec, kspec, rspec,
                  pl.BlockSpec((tm, d), lambda i, c: (i, 0)),
                  pl.BlockSpec((1, d), row), pl.BlockSpec((1, d), row)],
        out_specs=pl.BlockSpec((tm, d), lambda i, c: (i, 0)),
        out_shape=jax.ShapeDtypeStruct((t, d), F32),
        scratch_shapes=[pltpu.VMEM((d, tm), F32), pltpu.VMEM((ec, tm), BF16),
                        pltpu.VMEM((ec, tm), F32), pltpu.VMEM((ec, tm), F32)],
        compiler_params=_cparams(2),
    )(ht, u_bf, vt_bf, r2, ln, g1, g2, x, gate, norm_final)


def _peer_layer(x, g, sh, sc, gate, w_q, sub_keys, u_tab, v_tab, norm_final, final_norm):
    ht, st = _peer_query(x, g, sh, sc, w_q, sub_keys)
    route = _peer_route(st)
    e, d = v_tab.shape
    vt = v_tab.astype(BF16).reshape(e // PEER_EXPERT_CHUNK, PEER_EXPERT_CHUNK, d).transpose(0, 2, 1)
    return _peer_experts(ht, u_tab.astype(BF16), vt, route, x, gate, norm_final, final_norm)


def kernel(x, c, ada_w, ada_b, norm_mix, norm_ffn, norm_final, ssm_w_in, ssm_lambda_re, ssm_lambda_im, ssm_log_dt, ssm_b_re, ssm_b_im, ssm_c_re, ssm_c_im, ssm_d, ssm_w_out, nsa_w_in, nsa_cmp_k_pos, nsa_cmp_k_w1, nsa_cmp_k_w2, nsa_cmp_v_pos, nsa_cmp_v_w1, nsa_cmp_v_w2, nsa_w_out, peer_w_q, peer_sub_keys, peer_u, peer_v):
    bsz, l_all, d = x.shape
    assert bsz == 1
    depth = ada_w.shape[0]
    mod = _adaln(c, ada_w, ada_b)
    xt = x.reshape(l_all, d)
    nf = norm_final.reshape(1, d)
    for i in range(depth):
        sh1, sc1, g1, sh2, sc2, g2 = [mod[i, :, k * d:(k + 1) * d] for k in range(6)]
        j = i // 2
        gm = norm_mix[i].reshape(1, d)
        if i % 2 == 0:
            xt = _s5_layer(xt, gm, sh1, sc1, g1, ssm_w_in[j], ssm_lambda_re[j], ssm_lambda_im[j],
                           ssm_log_dt[j], ssm_b_re[j], ssm_b_im[j], ssm_c_re[j], ssm_c_im[j],
                           ssm_d[j], ssm_w_out[j])
        else:
            xt = _nsa_layer(xt, gm, sh1, sc1, g1, nsa_w_in[j], nsa_cmp_k_pos[j], nsa_cmp_k_w1[j],
                            nsa_cmp_k_w2[j], nsa_cmp_v_pos[j], nsa_cmp_v_w1[j], nsa_cmp_v_w2[j],
                            nsa_w_out[j])
        xt = _peer_layer(xt, norm_ffn[i].reshape(1, d), sh2, sc2, g2, peer_w_q[i], peer_sub_keys[i],
                         peer_u[i], peer_v[i], nf, final_norm=(i == depth - 1))
    return xt.reshape(bsz, l_all, d)
```

```python
import functools
import math

import numpy as np
import jax
import jax.numpy as jnp
from jax import lax
from jax.experimental import pallas as pl
from jax.experimental.pallas import tpu as pltpu

F32 = jnp.float32
BF16 = jnp.bfloat16

RMS_EPS = 1e-6

SSM_GROUP = 16
SSM_STATE = 64
SSM_BLOCK_GROUPS = 8
S5_SUBSEQ = 8
S5_CHUNK = 128
S5_LANE_BLOCK = 1024

NSA_HEADS = 16
NSA_HEAD_DIM = 128
NSA_KV_GROUPS = 4
NSA_REP = NSA_HEADS // NSA_KV_GROUPS
N_BRANCH = 3
CMP_LEN = 32
CMP_STRIDE = 16
SEL_LEN = 64
SEL_TOPN = 16
WIN = 512
Q_BLOCK = 256
SEL_KV_TILE = 512
MASK_PENALTY = -(2.0 ** 30)
SEL_DEN_LIMIT = 1e30
ONEHOT_PERIOD = 128

PEER_HEADS = 8
PEER_NKEYS = 128
PEER_TOPK = 16
PEER_HALF = 128
PEER_EXPERT_CHUNK = 1024
PEER_LAST_IS_EVEN = (PEER_NKEYS ** 2 // PEER_EXPERT_CHUNK) % 2 == 0
RANK_NONE = 255.0

VMEM_LIMIT = 56 * 1024 * 1024
NT_DIMS = (((1,), (1,)), ((), ()))


def _cparams(n_axes):
    return pltpu.CompilerParams(
        dimension_semantics=("arbitrary",) * n_axes, vmem_limit_bytes=VMEM_LIMIT)


def _norm_mod(x, g, sh, sc):
    ms = jnp.mean(x * x, axis=-1, keepdims=True)
    xn = x * lax.rsqrt(ms + RMS_EPS)
    return (xn * g) * (1.0 + sc) + sh


def _adaln_kernel(c_ref, w_ref, b_ref, o_ref):
    c = c_ref[...]
    cond = c * jax.nn.sigmoid(c)
    o_ref[0] = jnp.dot(cond.astype(BF16), w_ref[0].astype(BF16),
                       preferred_element_type=F32) + b_ref[0]


def _adaln(c, ada_w, ada_b):
    depth, d, n = ada_w.shape
    tn = 1024
    c8 = jnp.broadcast_to(c, (8, d))
    out = pl.pallas_call(
        _adaln_kernel,
        grid=(depth, n // tn),
        in_specs=[pl.BlockSpec((8, d), lambda i, j: (0, 0)),
                  pl.BlockSpec((1, d, tn), lambda i, j: (i, 0, j)),
                  pl.BlockSpec((1, 1, tn), lambda i, j: (i, 0, j))],
        out_specs=pl.BlockSpec((1, 8, tn), lambda i, j: (i, 0, j)),
        out_shape=jax.ShapeDtypeStruct((depth, 8, n), F32),
        compiler_params=_cparams(2),
    )(c8, ada_w, ada_b.reshape(depth, 1, n))
    return out[:, 0:1, :]


def _nmm_kernel(x_ref, g_ref, sh_ref, sc_ref, w_ref, cs_ref, o_ref, h_scr, *, act):
    @pl.when(pl.program_id(1) == 0)
    def _():
        h_scr[...] = _norm_mod(x_ref[...], g_ref[...], sh_ref[...], sc_ref[...]).astype(BF16)

    acc = jnp.dot(h_scr[...], w_ref[...], preferred_element_type=F32)
    if act == "sigmoid":
        acc = jax.nn.sigmoid(acc)
    else:
        acc = acc * cs_ref[...]
    o_ref[...] = acc.astype(o_ref.dtype)


def _norm_mod_matmul(x, g, sh, sc, w, colscale, out_dtype, act="scale", tm=512, tn=1024):
    t, d = x.shape
    n = w.shape[1]
    tn = min(tn, n)
    row = lambda i, j: (0, 0)
    return pl.pallas_call(
        functools.partial(_nmm_kernel, act=act),
        grid=(t // tm, n // tn),
        in_specs=[pl.BlockSpec((tm, d), lambda i, j: (i, 0)),
                  pl.BlockSpec((1, d), row), pl.BlockSpec((1, d), row), pl.BlockSpec((1, d), row),
                  pl.BlockSpec((d, tn), lambda i, j: (0, j)),
                  pl.BlockSpec((1, tn), lambda i, j: (0, j))],
        out_specs=pl.BlockSpec((tm, tn), lambda i, j: (i, j)),
        out_shape=jax.ShapeDtypeStruct((t, n), out_dtype),
        scratch_shapes=[pltpu.VMEM((tm, d), BF16)],
        compiler_params=_cparams(2),
    )(x, g, sh, sc, w, colscale)


def _mm_res_kernel(a_ref, w_ref, x_ref, g_ref, o_ref):
    y = jnp.dot(a_ref[...], w_ref[...], preferred_element_type=F32)
    o_ref[...] = x_ref[...] + g_ref[...] * y


def _mm_glu_res_kernel(a_ref, wa_ref, wb_ref, x_ref, g_ref, o_ref):
    a = a_ref[...]
    ya = jnp.dot(a, wa_ref[...], preferred_element_type=F32)
    yb = jnp.dot(a, wb_ref[...], preferred_element_type=F32)
    o_ref[...] = x_ref[...] + g_ref[...] * (ya * jax.nn.sigmoid(yb))


def _matmul_residual(a, w, x, gate, glu, tm=512, tn=1024):
    t, k = a.shape
    n = x.shape[1]
    nb = n // tn
    a_spec = pl.BlockSpec((tm, k), lambda i, j: (i, 0))
    w_spec = pl.BlockSpec((k, tn), lambda i, j: (0, j))
    tail = [pl.BlockSpec((tm, tn), lambda i, j: (i, j)), pl.BlockSpec((1, tn), lambda i, j: (0, j))]
    if glu:
        kern = _mm_glu_res_kernel
        in_specs = [a_spec, w_spec, pl.BlockSpec((k, tn), lambda i, j: (0, j + nb))] + tail
        args = (a, w, w, x, gate)
    else:
        kern = _mm_res_kernel
        in_specs = [a_spec, w_spec] + tail
        args = (a, w, x, gate)
    return pl.pallas_call(
        kern,
        grid=(t // tm, nb),
        in_specs=in_specs,
        out_specs=pl.BlockSpec((tm, tn), lambda i, j: (i, j)),
        out_shape=jax.ShapeDtypeStruct((t, n), F32),
        compiler_params=_cparams(2),
    )(*args)


def _s5_param_kernel(lre_ref, lim_ref, ldt_ref, bre_ref, bim_ref, ar_ref, ai_ref, bbr_ref, bbi_ref):
    lre = lre_ref[...]
    lim = lim_ref[...]
    dt = jnp.exp(ldt_ref[...])
    mag = jnp.exp(lre * dt)
    ar = mag * jnp.cos(lim * dt)
    ai = mag * jnp.sin(lim * dt)
    den = lre * lre + lim * lim
    cr = ((ar - 1.0) * lre + ai * lim) / den
    ci = (ai * lre - (ar - 1.0) * lim) / den
    ar_ref[...] = ar
    ai_ref[...] = ai
    bbr_ref[...] = cr * bre_ref[...] - ci * bim_ref[...]
    bbi_ref[...] = cr * bim_ref[...] + ci * bre_ref[...]


def _s5_params(lam_re, lam_im, log_dt, b_re, b_im):
    g, n, p = b_re.shape
    rep = lambda a: jnp.repeat(a, p, axis=1)
    shp = jax.ShapeDtypeStruct((g, n * p), F32)
    ar, ai, bbr, bbi = pl.pallas_call(
        _s5_param_kernel, out_shape=(shp, shp, shp, shp),
    )(rep(lam_re), rep(lam_im), log_dt.reshape(g, 1), b_re.reshape(g, n * p), b_im.reshape(g, n * p))
    return ar[:, ::p], ai[:, ::p], bbr.reshape(g, n, p), bbi.reshape(g, n, p)


def _s5_scan_kernel(u_ref, a_ref, bbr_ref, bbi_ref, cr_ref, cin_ref, d_ref, o_ref,
                    bur, bui, hin_r, hin_i, car_r, car_i, apr, api):
    t, width = u_ref.shape
    sub = t // S5_SUBSEQ
    lanes = bur.shape[1]
    nblk = bbr_ref.shape[0]
    kin = width // nblk
    kst = lanes // nblk

    @pl.when(pl.program_id(0) == 0)
    def _init():
        car_r[...] = jnp.zeros_like(car_r)
        car_i[...] = jnp.zeros_like(car_i)
        ar = a_ref[0:1, :]
        ai = a_ref[1:2, :]
        pr, pi = ar, ai
        apr[0:1, :] = pr
        api[0:1, :] = pi
        for i in range(1, sub):
            pr, pi = pr * ar - pi * ai, pr * ai + pi * ar
            apr[i:i + 1, :] = pr
            api[i:i + 1, :] = pi

    for k in range(nblk):
        uk = u_ref[:, kin * k:kin * (k + 1)].astype(BF16)
        bur[:, kst * k:kst * (k + 1)] = jnp.dot(uk, bbr_ref[k], preferred_element_type=F32)
        bui[:, kst * k:kst * (k + 1)] = jnp.dot(uk, bbi_ref[k], preferred_element_type=F32)

    lb = S5_LANE_BLOCK
    for b in range(lanes // lb):
        sl = slice(b * lb, (b + 1) * lb)
        ar = jnp.broadcast_to(a_ref[0:1, sl], (S5_SUBSEQ, lb))
        ai = jnp.broadcast_to(a_ref[1:2, sl], (S5_SUBSEQ, lb))
        sr = jnp.zeros((S5_SUBSEQ, lb), F32)
        si = jnp.zeros((S5_SUBSEQ, lb), F32)
        for i in range(sub):
            rows = slice(S5_SUBSEQ * i, S5_SUBSEQ * (i + 1))
            sr, si = (ar * sr - ai * si + bur[rows, sl], ar * si + ai * sr + bui[rows, sl])
            bur[rows, sl] = sr
            bui[rows, sl] = si
        asr = apr[sub - 1:sub, sl]
        asi = api[sub - 1:sub, sl]
        hr = car_r[0:1, sl]
        hi = car_i[0:1, sl]
        for j in range(S5_SUBSEQ):
            hin_r[j:j + 1, sl] = hr
            hin_i[j:j + 1, sl] = hi
            er = sr[j:j + 1, :]
            ei = si[j:j + 1, :]
            hr, hi = er + asr * hr - asi * hi, ei + asr * hi + asi * hr
        car_r[0:1, sl] = hr
        car_i[0:1, sl] = hi
        hinr = hin_r[:, sl]
        hini = hin_i[:, sl]
        for i in range(sub):
            rows = slice(S5_SUBSEQ * i, S5_SUBSEQ * (i + 1))
            pr = jnp.broadcast_to(apr[i:i + 1, sl], (S5_SUBSEQ, lb))
            pi = jnp.broadcast_to(api[i:i + 1, sl], (S5_SUBSEQ, lb))
            bur[rows, sl] = bur[rows, sl] + pr * hinr - pi * hini
            bui[rows, sl] = bui[rows, sl] + pr * hini + pi * hinr

    for k in range(nblk):
        sr = bur[:, kst * k:kst * (k + 1)].astype(BF16)
        si = bui[:, kst * k:kst * (k + 1)].astype(BF16)
        y = (jnp.dot(sr, cr_ref[k], preferred_element_type=F32)
             + jnp.dot(si, cin_ref[k], preferred_element_type=F32))
        cols = slice(kin * k, kin * (k + 1))
        y = y + d_ref[:, cols] * u_ref[:, cols]
        o_ref[:, cols] = jax.nn.gelu(y).astype(o_ref.dtype)


def _s5_scan(u_p, ar, ai, bb_r, bb_i, c_re, c_im, d_skip):
    t_all, width = u_p.shape
    g, n, p = bb_r.shape
    bg = SSM_BLOCK_GROUPS
    nblk = g // bg
    eye = jnp.eye(bg, dtype=F32)
    blk_b = lambda bb: jnp.einsum('kgnp,gh->kgphn', bb.reshape(nblk, bg, n, p), eye
                                  ).reshape(nblk, bg * p, bg * n).astype(BF16)
    blk_c = lambda cc: jnp.einsum('kgpn,gh->kgnhp', cc.reshape(nblk, bg, p, n), eye
                                  ).reshape(nblk, bg * n, bg * p).astype(BF16)
    lanes = g * n
    a8 = jnp.zeros((8, lanes), F32).at[0].set(ar.reshape(lanes)).at[1].set(ai.reshape(lanes))
    t = S5_CHUNK
    sub = t // S5_SUBSEQ
    full3 = lambda c: (0, 0, 0)
    wspec_b = pl.BlockSpec((nblk, bg * p, bg * n), full3)
    wspec_c = pl.BlockSpec((nblk, bg * n, bg * p), full3)
    return pl.pallas_call(
        _s5_scan_kernel,
        grid=(t_all // t,),
        in_specs=[pl.BlockSpec((t, width), lambda c: (c, 0)),
                  pl.BlockSpec((8, lanes), lambda c: (0, 0)),
                  wspec_b, wspec_b, wspec_c, wspec_c,
                  pl.BlockSpec((1, width), lambda c: (0, 0))],
        out_specs=pl.BlockSpec((t, width), lambda c: (c, 0)),
        out_shape=jax.ShapeDtypeStruct((t_all, width), BF16),
        scratch_shapes=[pltpu.VMEM((t, lanes), F32), pltpu.VMEM((t, lanes), F32),
                        pltpu.VMEM((8, lanes), F32), pltpu.VMEM((8, lanes), F32),
                        pltpu.VMEM((8, lanes), F32), pltpu.VMEM((8, lanes), F32),
                        pltpu.VMEM((sub, lanes), F32), pltpu.VMEM((sub, lanes), F32)],
        compiler_params=_cparams(1),
    )(u_p, a8, blk_b(bb_r), blk_b(bb_i), blk_c(c_re), blk_c(-c_im), d_skip.reshape(1, width))


def _s5_rows_to_subseq(x, inverse=False):
    t_all, d = x.shape
    sub = S5_CHUNK // S5_SUBSEQ
    shape = (t_all // S5_CHUNK, sub, S5_SUBSEQ, d) if inverse else (t_all // S5_CHUNK, S5_SUBSEQ, sub, d)
    return x.reshape(shape).transpose(0, 2, 1, 3).reshape(t_all, d)


def _s5_layer(x, g, sh, sc, gate, w_in, lam_re, lam_im, log_dt, b_re, b_im, c_re, c_im, d_skip, w_out):
    d = x.shape[1]
    x_p = _s5_rows_to_subseq(x)
    ones = jnp.ones((1, w_in.shape[1]), F32)
    u_p = _norm_mod_matmul(x_p, g, sh, sc, w_in.astype(BF16), ones, F32)
    ar, ai, bb_r, bb_i = _s5_params(lam_re, lam_im, log_dt, b_re, b_im)
    gy_p = _s5_scan(u_p, ar, ai, bb_r, bb_i, c_re, c_im, d_skip)
    xn_p = _matmul_residual(gy_p, w_out.astype(BF16), x_p, gate, glu=True)
    return _s5_rows_to_subseq(xn_p, inverse=True)


def _cmp_kernel(x_ref, w1a_ref, w1b_ref, pos_ref, w1_ref, w2_ref, o_ref, acc_a, acc_b):
    l = pl.program_id(2)

    @pl.when(l == 0)
    def _():
        acc_a[...] = jnp.zeros_like(acc_a)
        acc_b[...] = jnp.zeros_like(acc_b)

    x = x_ref[...]
    acc_a[...] += jnp.dot(x, w1a_ref[0, 0], preferred_element_type=F32)
    acc_b[...] += jnp.dot(x, w1b_ref[0, 0], preferred_element_type=F32)

    @pl.when(l == pl.num_programs(2) - 1)
    def _():
        m = acc_a.shape[0]
        posc = jnp.zeros((8, NSA_HEAD_DIM), F32)
        for ll in range(CMP_LEN):
            prow = jnp.broadcast_to(pos_ref[0, ll:ll + 1, :], (8, NSA_HEAD_DIM)).astype(BF16)
            posc = posc + jnp.dot(prow, w1_ref[0, ll], preferred_element_type=F32)
        pre = acc_a[...] + pltpu.roll(acc_b[...], m - 1, 0) + posc[0:1, :]
        hid = jax.nn.gelu(pre)
        o_ref[0, 0] = jnp.dot(hid.astype(BF16), w2_ref[0], preferred_element_type=F32).astype(o_ref.dtype)


def _compress(qkv, kv_col0, pos, w1, w2):
    l_all, c = qkv.shape
    half = CMP_LEN // 2
    m = l_all // half
    cb = c // NSA_HEAD_DIM
    x2 = qkv.reshape(m, half * c)
    g = NSA_KV_GROUPS
    dh = NSA_HEAD_DIM
    return pl.pallas_call(
        _cmp_kernel,
        grid=(2, g, half),
        in_specs=[pl.BlockSpec((m, dh), lambda s, gg, l: (0, l * cb + kv_col0 + g * s + gg)),
                  pl.BlockSpec((1, 1, dh, dh), lambda s, gg, l: (s, l, 0, 0)),
                  pl.BlockSpec((1, 1, dh, dh), lambda s, gg, l: (s, l + half, 0, 0)),
                  pl.BlockSpec((1, CMP_LEN, dh), lambda s, gg, l: (s, 0, 0)),
                  pl.BlockSpec((1, CMP_LEN, dh, dh), lambda s, gg, l: (s, 0, 0, 0)),
                  pl.BlockSpec((1, dh, dh), lambda s, gg, l: (s, 0, 0))],
        out_specs=pl.BlockSpec((1, 1, m, dh), lambda s, gg, l: (s, gg, 0, 0)),
        out_shape=jax.ShapeDtypeStruct((2, g, m, dh), BF16),
        scratch_shapes=[pltpu.VMEM((m, dh), F32), pltpu.VMEM((m, dh), F32)],
        compiler_params=_cparams(3),
    )(x2, w1, w1, pos, w1, w2)


def _softmax2_rows(s, mask):
    s = jnp.where(mask, s, -1e30)
    m = jnp.max(s, axis=-1, keepdims=True)
    p = jnp.exp2(s - m)
    return p, 1.0 / jnp.maximum(jnp.sum(p, axis=-1, keepdims=True), 1e-30)


def _shifted_softmax(s, mask, shift):
    p = jnp.exp2(jnp.where(mask, s - shift, -1e30))
    return p, jnp.sum(p, axis=-1, keepdims=True)


def _any_out_of_range(den, needed):
    ok = (den > 0.0) & (den < SEL_DEN_LIMIT)
    return jnp.max(jnp.where(needed & jnp.logical_not(ok), 1.0, 0.0)) > 0.0


def _nsa_attn_kernel(*refs):
    redo = _nsa_attn_body(True, *refs)

    @pl.when(redo)
    def _():
        _nsa_attn_body(False, *refs)


def _nsa_attn_body(fast, q_ref, gt_ref, kc_ref, vc_ref, agg_ref, ksa_ref, vsa_ref, kw_ref, vw_ref,
                   o_ref, qaug_scr, s_a, s_b, acc_scr, m_scr, shift_scr):
    b = pl.program_id(1)
    qb = q_ref.shape[0]
    dh = NSA_HEAD_DIM
    rep = NSA_REP
    rows = rep * qb
    n_cmp = kc_ref.shape[2]
    n_sel = agg_ref.shape[1]
    top_n = min(SEL_TOPN, n_sel)
    t0 = pl.multiple_of(b * qb, qb)

    q_blk = q_ref[...]
    qs = jnp.concatenate([q_blk[:, r * dh:(r + 1) * dh] for r in range(rep)], axis=0)
    tpos = t0 + lax.broadcasted_iota(jnp.int32, (rows, 1), 0) % qb
    all_rows = jnp.ones((rows, 1), jnp.bool_)

    kc = kc_ref[0, 0]
    s_c = lax.dot_general(qs, kc, NT_DIMS, preferred_element_type=F32)
    cmp_end = lax.broadcasted_iota(jnp.int32, (1, n_cmp), 1) * CMP_STRIDE + (CMP_LEN - 1)
    has_c = tpos >= CMP_LEN - 1
    if fast:
        p_c, den_c = _shifted_softmax(s_c, cmp_end <= tpos, s_c[:, 0:1])
        redo = _any_out_of_range(den_c, has_c)
        inv_c = 1.0 / jnp.maximum(den_c, 1e-30)
    else:
        p_c, inv_c = _softmax2_rows(s_c, cmp_end <= tpos)
    p_c = p_c * jnp.where(has_c, inv_c, 0.0)
    o_c = jnp.dot(p_c.astype(BF16), vc_ref[0, 0], preferred_element_type=F32)

    psum = p_c[0:qb]
    for r in range(1, rep):
        psum = psum + p_c[r * qb:(r + 1) * qb]
    p_hi = psum.astype(BF16)
    p_lo = (psum - p_hi.astype(F32)).astype(BF16)
    agg = agg_ref[...]
    imp = (jnp.dot(p_hi, agg, preferred_element_type=F32)
           + jnp.dot(p_lo, agg, preferred_element_type=F32))
    imp_t = imp.T
    tq = t0 + lax.broadcasted_iota(jnp.int32, (1, qb), 1)
    blk = lax.broadcasted_iota(jnp.int32, (n_sel, qb), 0)
    blk_f = blk.astype(F32)
    cur = tq // SEL_LEN
    valid = blk * SEL_LEN <= tq
    if fast:
        score = jnp.where(valid, imp_t, -1.0 - blk_f)
        score = jnp.where(blk == cur - 1, 1e9, score)
        score = jnp.where(blk == cur, 2e9, score)
        score = jnp.where(blk == 0, 3e9, score)
    else:
        forced = (blk == 0) | (blk == cur) | (blk == cur - 1)
        score = jnp.where(forced, 1e9, jnp.where(valid, imp_t, -1e9))
    pen_t = jnp.full((n_sel, qb), MASK_PENALTY, F32)
    for _ in range(top_n):
        mx = jnp.max(score, axis=0, keepdims=True)
        hit = score == mx
        if not fast:
            idx = jnp.min(jnp.where(hit, blk_f, float(n_sel)), axis=0, keepdims=True)
            hit = blk_f == idx
        pen_t = jnp.where(hit, 0.0, pen_t)
        score = jnp.where(hit, -jnp.inf, score)
    if fast:
        taken = jnp.sum(jnp.where(pen_t == 0.0, 1.0, 0.0), axis=0, keepdims=True)
        redo = redo | (jnp.max(jnp.where(taken != float(top_n), 1.0, 0.0)) > 0.0)
    pen = pen_t.T.astype(BF16)
    n_half = qaug_scr.shape[0]
    for hh in range(n_half):
        if n_sel >= ONEHOT_PERIOD:
            ph = pen[:, hh * ONEHOT_PERIOD:(hh + 1) * ONEHOT_PERIOD]
        else:
            ph = jnp.concatenate(
                [pen, jnp.full((qb, ONEHOT_PERIOD - n_sel), MASK_PENALTY, BF16)], axis=1)
        qaug_scr[hh] = jnp.concatenate([qs, jnp.concatenate([ph] * rep, axis=0)], axis=1)

    kt = SEL_KV_TILE
    period_keys = ONEHOT_PERIOD * SEL_LEN

    def sel_scores(i, s_out):
        k0 = pl.multiple_of(i * kt, kt)
        s_out[...] = lax.dot_general(qaug_scr[k0 // period_keys], ksa_ref[pl.ds(k0, kt), :], NT_DIMS,
                                     preferred_element_type=F32)

    def causal_mask(i, s):
        kpos = i * kt + lax.broadcasted_iota(jnp.int32, (1, kt), 1)
        return jnp.where(kpos <= tpos, s, MASK_PENALTY)

    def update_running_max(i, s_in, causal):
        k0 = pl.multiple_of(i * kt, kt)
        s = causal_mask(i, s_in[...]) if causal else s_in[...]
        m_run = m_scr[...]
        m_new = jnp.maximum(m_run, jnp.max(s, axis=-1, keepdims=True))
        p = jnp.exp2(s - m_new)
        acc_scr[...] = (jnp.exp2(m_run - m_new) * acc_scr[...]
                        + jnp.dot(p.astype(BF16), vsa_ref[pl.ds(k0, kt), :], preferred_element_type=F32))
        m_scr[...] = m_new

    def update_fixed_shift(i, s_in, causal):
        k0 = pl.multiple_of(i * kt, kt)
        s = causal_mask(i, s_in[...]) if causal else s_in[...]
        shift = shift_scr[...]
        p = jnp.concatenate([jnp.exp2(s[:, c * dh:(c + 1) * dh] - shift).astype(BF16)
                             for c in range(kt // dh)], axis=1)
        acc_scr[...] += jnp.dot(p, vsa_ref[pl.ds(k0, kt), :], preferred_element_type=F32)

    def sweep(update):
        def pair(j, carry):
            sel_scores(2 * j + 1, s_b)
            update(2 * j, s_a, False)
            sel_scores(2 * j + 2, s_a)
            update(2 * j + 1, s_b, False)
            return carry

        acc_scr[...] = jnp.zeros_like(acc_scr)
        lax.fori_loop(0, n_full // 2, pair, 0)
        odd = n_full % 2 == 1

        @pl.when(odd)
        def _():
            sel_scores(n_full, s_b)
            update(n_full - 1, s_a, False)
            update(n_full, s_b, True)

        @pl.when(jnp.logical_not(odd))
        def _():
            update(n_full, s_a, True)

    n_full = t0 // kt
    sel_scores(0, s_a)
    if fast:
        first = jnp.max(causal_mask(0, s_a[...]), axis=-1, keepdims=True)
        shift_scr[...] = jnp.broadcast_to(first, shift_scr.shape)
        sweep(update_fixed_shift)
    else:
        m_scr[...] = jnp.full(m_scr.shape, -1e30, F32)
        sweep(update_running_max)
    acc_s = acc_scr[...]
    den_s = acc_s[:, dh:dh + 1]
    o_s = acc_s[:, 0:dh] * (1.0 / den_s)

    wlen = WIN + qb
    w0 = pl.multiple_of(jnp.maximum(t0 - WIN, 0), qb)
    s_w = lax.dot_general(qs, kw_ref[pl.ds(w0, wlen), :], NT_DIMS, preferred_element_type=F32)
    diff = tpos - (w0 + lax.broadcasted_iota(jnp.int32, (1, wlen), 1))
    mask_w = (diff >= 0) & (diff < WIN)
    if fast:
        k_diag = jnp.concatenate([kw_ref[pl.ds(t0, qb), :]] * rep, axis=0).astype(F32)
        diag = jnp.sum(qs.astype(F32) * k_diag, axis=-1, keepdims=True)
        p_w, den_w = _shifted_softmax(s_w, mask_w, diag)
        o_w = jnp.dot(p_w.astype(BF16), vw_ref[pl.ds(w0, wlen), :], preferred_element_type=F32)
        o_w = o_w * (1.0 / den_w)
        redo = redo | _any_out_of_range(den_s, all_rows) | _any_out_of_range(den_w, all_rows)
    else:
        p_w, inv_w = _softmax2_rows(s_w, mask_w)
        o_w = jnp.dot((p_w * inv_w).astype(BF16), vw_ref[pl.ds(w0, wlen), :], preferred_element_type=F32)

    gt = gt_ref[...]
    for r in range(rep):
        rs = slice(r * qb, (r + 1) * qb)
        o = (gt[:, r:r + 1] * o_c[rs]
             + gt[:, rep + r:rep + r + 1] * o_s[rs]
             + gt[:, 2 * rep + r:2 * rep + r + 1] * o_w[rs])
        o_ref[:, r * dh:(r + 1) * dh] = o.astype(o_ref.dtype)
    return redo if fast else None


def _nsa_agg(n_cmp_pad, n_cmp, n_sel):
    ratio, span = SEL_LEN // CMP_STRIDE, CMP_LEN // CMP_STRIDE
    agg = np.zeros((n_cmp_pad, n_sel), np.float32)
    jj = np.arange(n_sel)
    for m in range(ratio):
        for n in range(span):
            ii = ratio * jj + m - n
            ok = (ii >= 0) & (ii < n_cmp)
            agg[ii[ok], jj[ok]] += 1.0
    return jnp.asarray(agg, dtype=BF16)


def _nsa_attention(qkv, gates, kv_cmp):
    l_all = qkv.shape[0]
    dh = NSA_HEAD_DIM
    g = NSA_KV_GROUPS
    qb = Q_BLOCK
    qw = NSA_REP * dh
    n_cmp_pad = l_all // CMP_STRIDE
    n_cmp = (l_all - CMP_LEN) // CMP_STRIDE + 1
    n_sel = l_all // SEL_LEN
    agg = _nsa_agg(n_cmp_pad, n_cmp, n_sel)
    n_half = max(n_sel // ONEHOT_PERIOD, 1)
    q_blocks = (NSA_HEADS * dh) // dh
    key = np.arange(l_all)
    onehot = jnp.asarray((key[:, None] // SEL_LEN) % ONEHOT_PERIOD == np.arange(ONEHOT_PERIOD)[None, :],
                         dtype=BF16)
    ones_col = jnp.asarray(np.arange(dh)[None, :] == 0, dtype=BF16)
    grouped = lambda base: qkv[:, (q_blocks + base) * dh:(q_blocks + base + g) * dh].reshape(l_all, g, dh)
    ks_aug = jnp.concatenate([grouped(2 * g), jnp.broadcast_to(onehot[:, None, :], (l_all, g, ONEHOT_PERIOD))],
                             axis=-1).reshape(l_all, g * (dh + ONEHOT_PERIOD))
    vs_aug = jnp.concatenate([grouped(3 * g), jnp.broadcast_to(ones_col[:, None, :], (l_all, g, dh))],
                             axis=-1).reshape(l_all, g * 2 * dh)
    once = dict(pipeline_mode=pl.Buffered(1))
    kv_spec = lambda base: pl.BlockSpec((l_all, dh), lambda gg, b: (0, q_blocks + base + gg), **once)
    aug_spec = pl.BlockSpec((l_all, 2 * dh), lambda gg, b: (0, gg), **once)
    rows = NSA_REP * qb
    return pl.pallas_call(
        _nsa_attn_kernel,
        grid=(g, l_all // qb),
        in_specs=[pl.BlockSpec((qb, qw), lambda gg, b: (b, gg)),
                  pl.BlockSpec((qb, dh), lambda gg, b: (b, gg)),
                  pl.BlockSpec((1, 1, n_cmp_pad, dh), lambda gg, b: (0, gg, 0, 0)),
                  pl.BlockSpec((1, 1, n_cmp_pad, dh), lambda gg, b: (1, gg, 0, 0)),
                  pl.BlockSpec((n_cmp_pad, n_sel), lambda gg, b: (0, 0)),
                  aug_spec, aug_spec, kv_spec(4 * g), kv_spec(5 * g)],
        out_specs=pl.BlockSpec((qb, qw), lambda gg, b: (b, gg)),
        out_shape=jax.ShapeDtypeStruct((l_all, NSA_HEADS * dh), BF16),
        scratch_shapes=[pltpu.VMEM((n_half, rows, dh + ONEHOT_PERIOD), BF16),
                        pltpu.VMEM((rows, SEL_KV_TILE), F32), pltpu.VMEM((rows, SEL_KV_TILE), F32),
                        pltpu.VMEM((rows, 2 * dh), F32), pltpu.VMEM((rows, 1), F32),
                        pltpu.VMEM((rows, dh), F32)],
        compiler_params=_cparams(2),
    )(qkv, gates, kv_cmp, kv_cmp, agg, ks_aug, vs_aug, qkv, qkv)


def _nsa_layer(x, g, sh, sc, gate, w_in, k_pos, k_w1, k_w2, v_pos, v_w1, v_w2, w_out):
    dh = NSA_HEAD_DIM
    qd = NSA_HEADS * dh
    kvd = 2 * N_BRANCH * NSA_KV_GROUPS * dh
    w_main = w_in[:, :qd + kvd].astype(BF16)
    colscale = jnp.concatenate([jnp.full((1, qd), dh ** -0.5 * math.log2(math.e), F32),
                                jnp.ones((1, kvd), F32)], axis=1)
    qkv = _norm_mod_matmul(x, g, sh, sc, w_main, colscale, BF16)
    wg = w_in[:, qd + kvd:].reshape(-1, NSA_KV_GROUPS, NSA_REP, N_BRANCH).transpose(0, 1, 3, 2)
    wg = wg.reshape(-1, NSA_KV_GROUPS, N_BRANCH * NSA_REP)
    wg = jnp.pad(wg, ((0, 0), (0, 0), (0, dh - N_BRANCH * NSA_REP))).reshape(-1, NSA_KV_GROUPS * dh)
    gates = _norm_mod_matmul(x, g, sh, sc, wg.astype(BF16), jnp.ones((1, wg.shape[1]), F32), F32,
                             act="sigmoid")
    kv_cmp = _compress(qkv, qd // dh, jnp.stack([k_pos, v_pos]),
                       jnp.stack([k_w1, v_w1]).astype(BF16), jnp.stack([k_w2, v_w2]).astype(BF16))
    o = _nsa_attention(qkv, gates, kv_cmp)
    return _matmul_residual(o, w_out.astype(BF16), x, gate, glu=False)


def _peer_query_kernel(x_ref, g_ref, sh_ref, sc_ref, wq_ref, keys_ref, ht_ref, st_ref):
    h32 = _norm_mod(x_ref[...], g_ref[...], sh_ref[...], sc_ref[...])
    ht_ref[...] = h32.T.astype(BF16)
    q = jnp.dot(h32.astype(BF16), wq_ref[...], preferred_element_type=F32).astype(BF16)
    for hc in range(keys_ref.shape[0]):
        rows = slice(hc * PEER_NKEYS, (hc + 1) * PEER_NKEYS)
        st_ref[rows, :] = lax.dot_general(keys_ref[hc], q[:, hc * PEER_HALF:(hc + 1) * PEER_HALF],
                                          NT_DIMS, preferred_element_type=F32)


def _peer_query(x, g, sh, sc, w_q, sub_keys, tm=512):
    t, d = x.shape
    nq = w_q.shape[1]
    keys = sub_keys.reshape(-1, PEER_NKEYS, PEER_HALF).astype(BF16)
    row = lambda i: (0, 0)
    return pl.pallas_call(
        _peer_query_kernel,
        grid=(t // tm,),
        in_specs=[pl.BlockSpec((tm, d), lambda i: (i, 0)),
                  pl.BlockSpec((1, d), row), pl.BlockSpec((1, d), row), pl.BlockSpec((1, d), row),
                  pl.BlockSpec((d, nq), row),
                  pl.BlockSpec(keys.shape, lambda i: (0, 0, 0))],
        out_specs=[pl.BlockSpec((d, tm), lambda i: (0, i)),
                   pl.BlockSpec((keys.shape[0] * PEER_NKEYS, tm), lambda i: (0, i))],
        out_shape=[jax.ShapeDtypeStruct((d, t), BF16),
                   jax.ShapeDtypeStruct((keys.shape[0] * PEER_NKEYS, t), F32)],
        compiler_params=_cparams(1),
    )(x, g, sh, sc, w_q.astype(BF16), keys)


def _peer_cells():
    return [(a, b) for a in range(PEER_TOPK) for b in range(PEER_TOPK) if (a + 1) * (b + 1) <= PEER_TOPK]


def _take_max(cur, iota, exact):
    v = jnp.max(cur, axis=0, keepdims=True)
    hit = cur == v
    if exact:
        idx = jnp.min(jnp.where(hit, iota, cur.shape[0]), axis=0, keepdims=True)
        hit = iota == idx
    return v, hit


def _top_ranks(s, iota_k, exact):
    rank = jnp.full(s.shape, RANK_NONE, F32)
    vals = []
    cur = s
    for a in range(PEER_TOPK):
        v, hit = _take_max(cur, iota_k, exact)
        rank = jnp.where(hit, float(a), rank)
        cur = jnp.where(hit, -jnp.inf, cur)
        vals.append(v)
    return vals, rank


def _peer_route_body(st_ref, seg_ref, r2_ref, ln_ref, g1_ref, g2_ref, exact):
    tn = st_ref.shape[1]
    nk = PEER_NKEYS
    cells = _peer_cells()
    n_cell = len(cells)
    n_pad = -(-n_cell // 8) * 8
    n_seg = seg_ref.shape[1]
    iota_k = lax.broadcasted_iota(jnp.int32, (nk, tn), 0)
    iota_c = lax.broadcasted_iota(jnp.int32, (n_pad, tn), 0)
    count = lambda m: jnp.sum(m.astype(F32), axis=0, keepdims=True)
    tied = jnp.zeros((1, tn), jnp.bool_)
    for h in range(PEER_HEADS):
        s1 = st_ref[(2 * h) * nk:(2 * h + 1) * nk, :]
        s2 = st_ref[(2 * h + 1) * nk:(2 * h + 2) * nk, :]
        v1, rank1 = _top_ranks(s1, iota_k, exact)
        v2, rank2 = _top_ranks(s2, iota_k, exact)
        cand = jnp.concatenate([v1[a] + v2[b] for a, b in cells]
                               + [jnp.full((n_pad - n_cell, tn), -jnp.inf, F32)], axis=0)
        top = v1[0] + v2[0]
        e_c = jnp.exp(cand - top)
        chosen = jnp.zeros((n_pad, tn), jnp.bool_)
        cur = cand
        for _ in range(PEER_TOPK):
            _, hit = _take_max(cur, iota_c, exact)
            chosen = chosen | hit
            cur = jnp.where(hit, -jnp.inf, cur)
        chosen_f = chosen.astype(F32)
        if not exact:
            k = float(PEER_TOPK)
            tied = (tied | (count(rank1 != RANK_NONE) != k) | (count(rank2 != RANK_NONE) != k)
                    | (jnp.sum(chosen_f, axis=0, keepdims=True) != k))
        z = jnp.sum(chosen_f * e_c, axis=0, keepdims=True)
        chosen_pad = jnp.concatenate([chosen_f, jnp.zeros((n_seg - n_pad, tn), F32)], axis=0)
        rowlen = jnp.dot(seg_ref[...], chosen_pad.astype(BF16), preferred_element_type=F32)
        ln = jnp.zeros((nk, tn), F32)
        for a in range(PEER_TOPK):
            ln = jnp.where(rank1 == float(a), rowlen[a:a + 1, :], ln)
        rows = slice(h * nk, (h + 1) * nk)
        r2_ref[rows, :] = rank2.astype(r2_ref.dtype)
        ln_ref[h] = ln
        g1_ref[h] = jnp.exp(s1 - v1[0]) / z
        g2_ref[rows, :] = jnp.exp(s2 - v2[0]).astype(g2_ref.dtype)
    return tied


def _peer_route_kernel(st_ref, seg_ref, r2_ref, ln_ref, g1_ref, g2_ref):
    refs = (st_ref, seg_ref, r2_ref, ln_ref, g1_ref, g2_ref)
    tied = _peer_route_body(*refs, exact=False)

    @pl.when(jnp.max(tied.astype(F32)) > 0.0)
    def _():
        _peer_route_body(*refs, exact=True)


def _peer_route(st, tn=128):
    n_rows, t = st.shape
    cells = _peer_cells()
    seg = np.zeros((PEER_TOPK, PEER_NKEYS), np.float32)
    for c, (a, _) in enumerate(cells):
        seg[a, c] = 1.0
    out_rows = PEER_HEADS * PEER_NKEYS
    shp = jax.ShapeDtypeStruct((out_rows, t), BF16)
    shp_row = jax.ShapeDtypeStruct((PEER_HEADS, PEER_NKEYS, t), F32)
    spec = pl.BlockSpec((out_rows, tn), lambda i: (0, i))
    spec_row = pl.BlockSpec((PEER_HEADS, PEER_NKEYS, tn), lambda i: (0, 0, i))
    return pl.pallas_call(
        _peer_route_kernel,
        grid=(t // tn,),
        in_specs=[pl.BlockSpec((n_rows, tn), lambda i: (0, i)),
                  pl.BlockSpec((PEER_TOPK, PEER_NKEYS), lambda i: (0, 0))],
        out_specs=[spec, spec_row, spec_row, spec],
        out_shape=[shp, shp_row, shp_row, shp],
        compiler_params=_cparams(1),
    )(st, jnp.asarray(seg, dtype=BF16))


def _peer_expert_kernel(h_ref, u_ref, vt_ref, r2_ref, ln_ref, g1_ref, g2_ref, x_ref, gate_ref, nf_ref,
                        o_ref, acc_t, p_scr, pre_a, pre_b, *, final_norm):
    c = pl.program_id(1)
    last = pl.num_programs(1) - 1
    ec = u_ref.shape[0]
    nk = PEER_NKEYS
    tm = h_ref.shape[1]
    zero = jnp.zeros((), BF16)

    def score(pre_out):
        pre_out[...] = jnp.dot(u_ref[...], h_ref[...], preferred_element_type=F32)

    def finish(pre_in):
        for ii in range(ec // nk):
            w = jnp.zeros((nk, tm), BF16)
            for h in range(PEER_HEADS):
                rows = slice(h * nk, (h + 1) * nk)
                ln_row = ln_ref[h, ii:ii + 1, :].astype(BF16)
                g1_row = g1_ref[h, ii:ii + 1, :].astype(BF16)
                w = w + jnp.where(r2_ref[rows, :] < ln_row, g2_ref[rows, :], zero) * g1_row
            act = jax.nn.gelu(pre_in[ii * nk:(ii + 1) * nk, :])
            p_scr[ii * nk:(ii + 1) * nk, :] = w * act.astype(BF16)
        acc_t[...] += jnp.dot(vt_ref[0], p_scr[...], preferred_element_type=F32)

    @pl.when(c == 0)
    def _():
        acc_t[...] = jnp.zeros_like(acc_t)
        score(pre_a)

    @pl.when((c > 0) & (c < last) & (c % 2 == 1))
    def _():
        score(pre_b)
        finish(pre_a)

    @pl.when((c > 0) & (c < last) & (c % 2 == 0))
    def _():
        score(pre_a)
        finish(pre_b)

    @pl.when(c == last)
    def _():
        finish(pre_b if PEER_LAST_IS_EVEN else pre_a)
        xo = x_ref[...] + gate_ref[...] * acc_t[...].T
        if final_norm:
            ms = jnp.mean(xo * xo, axis=-1, keepdims=True)
            xo = (xo * lax.rsqrt(ms + RMS_EPS)) * nf_ref[...]
        o_ref[...] = xo


def _peer_experts(ht, u_bf, vt_bf, route, x, gate, norm_final, final_norm, tm=512, ec=PEER_EXPERT_CHUNK):
    t, d = x.shape
    e = u_bf.shape[0]
    n_chunks = e // ec
    assert (n_chunks % 2 == 0) == PEER_LAST_IS_EVEN
    r2, ln, g1, g2 = route
    rspec = pl.BlockSpec((r2.shape[0], tm), lambda i, c: (0, i))
    prev = lambda c: jnp.maximum(c - 1, 0)
    kspec = pl.BlockSpec((PEER_HEADS, ec // PEER_NKEYS, tm), lambda i, c: (0, prev(c), i))
    row = lambda i, c: (0, 0)
    return pl.pallas_call(
        functools.partial(_peer_expert_kernel, final_norm=final_norm),
        grid=(t // tm, n_chunks + 1),
        in_specs=[pl.BlockSpec((d, tm), lambda i, c: (0, i)),
                  pl.BlockSpec((ec, d), lambda i, c: (jnp.minimum(c, n_chunks - 1), 0)),
                  pl.BlockSpec((1, d, ec), lambda i, c: (prev(c), 0, 0)),
                  rspec, kspec, kspec, rspec,
                  pl.BlockSpec((tm, d), lambda i, c: (i, 0)),
                  pl.BlockSpec((1, d), row), pl.BlockSpec((1, d), row)],
        out_specs=pl.BlockSpec((tm, d), lambda i, c: (i, 0)),
        out_shape=jax.ShapeDtypeStruct((t, d), F32),
        scratch_shapes=[pltpu.VMEM((d, tm), F32), pltpu.VMEM((ec, tm), BF16),
                        pltpu.VMEM((ec, tm), F32), pltpu.VMEM((ec, tm), F32)],
        compiler_params=_cparams(2),
    )(ht, u_bf, vt_bf, r2, ln, g1, g2, x, gate, norm_final)


def _peer_layer(x, g, sh, sc, gate, w_q, sub_keys, u_tab, v_tab, norm_final, final_norm):
    ht, st = _peer_query(x, g, sh, sc, w_q, sub_keys)
    route = _peer_route(st)
    e, d = v_tab.shape
    vt = v_tab.astype(BF16).reshape(e // PEER_EXPERT_CHUNK, PEER_EXPERT_CHUNK, d).transpose(0, 2, 1)
    return _peer_experts(ht, u_tab.astype(BF16), vt, route, x, gate, norm_final, final_norm)


def kernel(x, c, ada_w, ada_b, norm_mix, norm_ffn, norm_final, ssm_w_in, ssm_lambda_re, ssm_lambda_im, ssm_log_dt, ssm_b_re, ssm_b_im, ssm_c_re, ssm_c_im, ssm_d, ssm_w_out, nsa_w_in, nsa_cmp_k_pos, nsa_cmp_k_w1, nsa_cmp_k_w2, nsa_cmp_v_pos, nsa_cmp_v_w1, nsa_cmp_v_w2, nsa_w_out, peer_w_q, peer_sub_keys, peer_u, peer_v):
    bsz, l_all, d = x.shape
    assert bsz == 1
    depth = ada_w.shape[0]
    mod = _adaln(c, ada_w, ada_b)
    xt = x.reshape(l_all, d)
    nf = norm_final.reshape(1, d)
    for i in range(depth):
        sh1, sc1, g1, sh2, sc2, g2 = [mod[i, :, k * d:(k + 1) * d] for k in range(6)]
        j = i // 2
        gm = norm_mix[i].reshape(1, d)
        if i % 2 == 0:
            xt = _s5_layer(xt, gm, sh1, sc1, g1, ssm_w_in[j], ssm_lambda_re[j], ssm_lambda_im[j],
                           ssm_log_dt[j], ssm_b_re[j], ssm_b_im[j], ssm_c_re[j], ssm_c_im[j],
                           ssm_d[j], ssm_w_out[j])
        else:
            xt = _nsa_layer(xt, gm, sh1, sc1, g1, nsa_w_in[j], nsa_cmp_k_pos[j], nsa_cmp_k_w1[j],
                            nsa_cmp_k_w2[j], nsa_cmp_v_pos[j], nsa_cmp_v_w1[j], nsa_cmp_v_w2[j],
                            nsa_w_out[j])
        xt = _peer_layer(xt, norm_ffn[i].reshape(1, d), sh2, sc2, g2, peer_w_q[i], peer_sub_keys[i],
                         peer_u[i], peer_v[i], nf, final_norm=(i == depth - 1))
    return xt.reshape(bsz, l_all, d)
```

```python
import functools
import math

import numpy as np
import jax
import jax.numpy as jnp
from jax import lax
from jax.experimental import pallas as pl
from jax.experimental.pallas import tpu as pltpu

F32 = jnp.float32
BF16 = jnp.bfloat16

RMS_EPS = 1e-6

SSM_GROUP = 16
SSM_STATE = 64
SSM_BLOCK_GROUPS = 8
S5_SUBSEQ = 8
S5_CHUNK = 128
S5_LANE_BLOCK = 1024

NSA_HEADS = 16
NSA_HEAD_DIM = 128
NSA_KV_GROUPS = 4
NSA_REP = NSA_HEADS // NSA_KV_GROUPS
N_BRANCH = 3
CMP_LEN = 32
CMP_STRIDE = 16
SEL_LEN = 64
SEL_TOPN = 16
WIN = 512
Q_BLOCK = 256
SEL_KV_TILE = 512
MASK_PENALTY = -(2.0 ** 30)
SEL_DEN_LIMIT = 1e30
ONEHOT_PERIOD = 128

PEER_HEADS = 8
PEER_NKEYS = 128
PEER_TOPK = 16
PEER_HALF = 128
PEER_EXPERT_CHUNK = 1024
PEER_LAST_IS_EVEN = (PEER_NKEYS ** 2 // PEER_EXPERT_CHUNK) % 2 == 0
RANK_NONE = 255.0

VMEM_LIMIT = 56 * 1024 * 1024
NT_DIMS = (((1,), (1,)), ((), ()))


def _cparams(n_axes):
    return pltpu.CompilerParams(
        dimension_semantics=("arbitrary",) * n_axes, vmem_limit_bytes=VMEM_LIMIT)


def _norm_mod(x, g, sh, sc):
    ms = jnp.mean(x * x, axis=-1, keepdims=True)
    xn = x * lax.rsqrt(ms + RMS_EPS)
    return (xn * g) * (1.0 + sc) + sh


def _adaln_kernel(c_ref, w_ref, b_ref, o_ref):
    c = c_ref[...]
    cond = c * jax.nn.sigmoid(c)
    o_ref[0] = jnp.dot(cond.astype(BF16), w_ref[0].astype(BF16),
                       preferred_element_type=F32) + b_ref[0]


def _adaln(c, ada_w, ada_b):
    depth, d, n = ada_w.shape
    tn = 1024
    c8 = jnp.broadcast_to(c, (8, d))
    out = pl.pallas_call(
        _adaln_kernel,
        grid=(depth, n // tn),
        in_specs=[pl.BlockSpec((8, d), lambda i, j: (0, 0)),
                  pl.BlockSpec((1, d, tn), lambda i, j: (i, 0, j)),
                  pl.BlockSpec((1, 1, tn), lambda i, j: (i, 0, j))],
        out_specs=pl.BlockSpec((1, 8, tn), lambda i, j: (i, 0, j)),
        out_shape=jax.ShapeDtypeStruct((depth, 8, n), F32),
        compiler_params=_cparams(2),
    )(c8, ada_w, ada_b.reshape(depth, 1, n))
    return out[:, 0:1, :]


def _nmm_kernel(x_ref, g_ref, sh_ref, sc_ref, w_ref, cs_ref, o_ref, h_scr, *, act):
    @pl.when(pl.program_id(1) == 0)
    def _():
        h_scr[...] = _norm_mod(x_ref[...], g_ref[...], sh_ref[...], sc_ref[...]).astype(BF16)

    acc = jnp.dot(h_scr[...], w_ref[...], preferred_element_type=F32)
    if act == "sigmoid":
        acc = jax.nn.sigmoid(acc)
    else:
        acc = acc * cs_ref[...]
    o_ref[...] = acc.astype(o_ref.dtype)


def _norm_mod_matmul(x, g, sh, sc, w, colscale, out_dtype, act="scale", tm=512, tn=1024):
    t, d = x.shape
    n = w.shape[1]
    tn = min(tn, n)
    row = lambda i, j: (0, 0)
    return pl.pallas_call(
        functools.partial(_nmm_kernel, act=act),
        grid=(t // tm, n // tn),
        in_specs=[pl.BlockSpec((tm, d), lambda i, j: (i, 0)),
                  pl.BlockSpec((1, d), row), pl.BlockSpec((1, d), row), pl.BlockSpec((1, d), row),
                  pl.BlockSpec((d, tn), lambda i, j: (0, j)),
                  pl.BlockSpec((1, tn), lambda i, j: (0, j))],
        out_specs=pl.BlockSpec((tm, tn), lambda i, j: (i, j)),
        out_shape=jax.ShapeDtypeStruct((t, n), out_dtype),
        scratch_shapes=[pltpu.VMEM((tm, d), BF16)],
        compiler_params=_cparams(2),
    )(x, g, sh, sc, w, colscale)


def _mm_res_kernel(a_ref, w_ref, x_ref, g_ref, o_ref):
    y = jnp.dot(a_ref[...], w_ref[...], preferred_element_type=F32)
    o_ref[...] = x_ref[...] + g_ref[...] * y


def _mm_glu_res_kernel(a_ref, wa_ref, wb_ref, x_ref, g_ref, o_ref):
    a = a_ref[...]
    ya = jnp.dot(a, wa_ref[...], preferred_element_type=F32)
    yb = jnp.dot(a, wb_ref[...], preferred_element_type=F32)
    o_ref[...] = x_ref[...] + g_ref[...] * (ya * jax.nn.sigmoid(yb))


def _matmul_residual(a, w, x, gate, glu, tm=512, tn=1024):
    t, k = a.shape
    n = x.shape[1]
    nb = n // tn
    a_spec = pl.BlockSpec((tm, k), lambda i, j: (i, 0))
    w_spec = pl.BlockSpec((k, tn), lambda i, j: (0, j))
    tail = [pl.BlockSpec((tm, tn), lambda i, j: (i, j)), pl.BlockSpec((1, tn), lambda i, j: (0, j))]
    if glu:
        kern = _mm_glu_res_kernel
        in_specs = [a_spec, w_spec, pl.BlockSpec((k, tn), lambda i, j: (0, j + nb))] + tail
        args = (a, w, w, x, gate)
    else:
        kern = _mm_res_kernel
        in_specs = [a_spec, w_spec] + tail
        args = (a, w, x, gate)
    return pl.pallas_call(
        kern,
        grid=(t // tm, nb),
        in_specs=in_specs,
        out_specs=pl.BlockSpec((tm, tn), lambda i, j: (i, j)),
        out_shape=jax.ShapeDtypeStruct((t, n), F32),
        compiler_params=_cparams(2),
    )(*args)


def _s5_param_kernel(lre_ref, lim_ref, ldt_ref, bre_ref, bim_ref, ar_ref, ai_ref, bbr_ref, bbi_ref):
    lre = lre_ref[...]
    lim = lim_ref[...]
    dt = jnp.exp(ldt_ref[...])
    mag = jnp.exp(lre * dt)
    ar = mag * jnp.cos(lim * dt)
    ai = mag * jnp.sin(lim * dt)
    den = lre * lre + lim * lim
    cr = ((ar - 1.0) * lre + ai * lim) / den
    ci = (ai * lre - (ar - 1.0) * lim) / den
    ar_ref[...] = ar
    ai_ref[...] = ai
    bbr_ref[...] = cr * bre_ref[...] - ci * bim_ref[...]
    bbi_ref[...] = cr * bim_ref[...] + ci * bre_ref[...]


def _s5_params(lam_re, lam_im, log_dt, b_re, b_im):
    g, n, p = b_re.shape
    rep = lambda a: jnp.repeat(a, p, axis=1)
    shp = jax.ShapeDtypeStruct((g, n * p), F32)
    ar, ai, bbr, bbi = pl.pallas_call(
        _s5_param_kernel, out_shape=(shp, shp, shp, shp),
    )(rep(lam_re), rep(lam_im), log_dt.reshape(g, 1), b_re.reshape(g, n * p), b_im.reshape(g, n * p))
    return ar[:, ::p], ai[:, ::p], bbr.reshape(g, n, p), bbi.reshape(g, n, p)


def _s5_scan_kernel(u_ref, perm_ref, perm_t_ref, a_ref, bbr_ref, bbi_ref, cr_ref, cin_ref, d_ref, o_ref,
                    bur, bui, hin_r, hin_i, car_r, car_i, apr, api, up32, up16, gy16):
    t, width = u_ref.shape
    sub = t // S5_SUBSEQ
    lanes = bur.shape[1]
    nblk = bbr_ref.shape[0]
    kin = width // nblk
    kst = lanes // nblk

    @pl.when(pl.program_id(0) == 0)
    def _init():
        car_r[...] = jnp.zeros_like(car_r)
        car_i[...] = jnp.zeros_like(car_i)
        ar = a_ref[0:1, :]
        ai = a_ref[1:2, :]
        pr, pi = ar, ai
        apr[0:1, :] = pr
        api[0:1, :] = pi
        for i in range(1, sub):
            pr, pi = pr * ar - pi * ai, pr * ai + pi * ar
            apr[i:i + 1, :] = pr
            api[i:i + 1, :] = pi

    u = u_ref[...]
    u_hi = u.astype(BF16)
    u_lo = (u - u_hi.astype(F32)).astype(BF16)
    up_hi = jnp.dot(perm_ref[...], u_hi, preferred_element_type=F32)
    up32[...] = up_hi + jnp.dot(perm_ref[...], u_lo, preferred_element_type=F32)
    up16[...] = up_hi.astype(BF16)

    for k in range(nblk):
        uk = up16[:, kin * k:kin * (k + 1)]
        bur[:, kst * k:kst * (k + 1)] = jnp.dot(uk, bbr_ref[k], preferred_element_type=F32)
        bui[:, kst * k:kst * (k + 1)] = jnp.dot(uk, bbi_ref[k], preferred_element_type=F32)

    lb = S5_LANE_BLOCK
    for b in range(lanes // lb):
        sl = slice(b * lb, (b + 1) * lb)
        ar = jnp.broadcast_to(a_ref[0:1, sl], (S5_SUBSEQ, lb))
        ai = jnp.broadcast_to(a_ref[1:2, sl], (S5_SUBSEQ, lb))
        sr = jnp.zeros((S5_SUBSEQ, lb), F32)
        si = jnp.zeros((S5_SUBSEQ, lb), F32)
        for i in range(sub):
            rows = slice(S5_SUBSEQ * i, S5_SUBSEQ * (i + 1))
            sr, si = (ar * sr - ai * si + bur[rows, sl], ar * si + ai * sr + bui[rows, sl])
            bur[rows, sl] = sr
            bui[rows, sl] = si
        asr = apr[sub - 1:sub, sl]
        asi = api[sub - 1:sub, sl]
        hr = car_r[0:1, sl]
        hi = car_i[0:1, sl]
        for j in range(S5_SUBSEQ):
            hin_r[j:j + 1, sl] = hr
            hin_i[j:j + 1, sl] = hi
            er = sr[j:j + 1, :]
            ei = si[j:j + 1, :]
            hr, hi = er + asr * hr - asi * hi, ei + asr * hi + asi * hr
        car_r[0:1, sl] = hr
        car_i[0:1, sl] = hi
        hinr = hin_r[:, sl]
        hini = hin_i[:, sl]
        for i in range(sub):
            rows = slice(S5_SUBSEQ * i, S5_SUBSEQ * (i + 1))
            pr = jnp.broadcast_to(apr[i:i + 1, sl], (S5_SUBSEQ, lb))
            pi = jnp.broadcast_to(api[i:i + 1, sl], (S5_SUBSEQ, lb))
            bur[rows, sl] = bur[rows, sl] + pr * hinr - pi * hini
            bui[rows, sl] = bui[rows, sl] + pr * hini + pi * hinr

    for k in range(nblk):
        sr = bur[:, kst * k:kst * (k + 1)].astype(BF16)
        si = bui[:, kst * k:kst * (k + 1)].astype(BF16)
        y = (jnp.dot(sr, cr_ref[k], preferred_element_type=F32)
             + jnp.dot(si, cin_ref[k], preferred_element_type=F32))
        cols = slice(kin * k, kin * (k + 1))
        y = y + d_ref[:, cols] * up32[:, cols]
        gy16[:, cols] = jax.nn.gelu(y).astype(BF16)
    o_ref[...] = jnp.dot(perm_t_ref[...], gy16[...], preferred_element_type=F32).astype(o_ref.dtype)


def _s5_scan(u, ar, ai, bb_r, bb_i, c_re, c_im, d_skip):
    t_all, width = u.shape
    g, n, p = bb_r.shape
    bg = SSM_BLOCK_GROUPS
    nblk = g // bg
    eye = jnp.eye(bg, dtype=F32)
    blk_b = lambda bb: jnp.einsum('kgnp,gh->kgphn', bb.reshape(nblk, bg, n, p), eye
                                  ).reshape(nblk, bg * p, bg * n).astype(BF16)
    blk_c = lambda cc: jnp.einsum('kgpn,gh->kgnhp', cc.reshape(nblk, bg, p, n), eye
                                  ).reshape(nblk, bg * n, bg * p).astype(BF16)
    lanes = g * n
    a8 = jnp.zeros((8, lanes), F32).at[0].set(ar.reshape(lanes)).at[1].set(ai.reshape(lanes))
    t = S5_CHUNK
    sub = t // S5_SUBSEQ
    r = np.arange(t)
    perm = np.zeros((t, t), np.float32)
    perm[r, (r % S5_SUBSEQ) * sub + r // S5_SUBSEQ] = 1.0
    full3 = lambda c: (0, 0, 0)
    wspec_b = pl.BlockSpec((nblk, bg * p, bg * n), full3)
    wspec_c = pl.BlockSpec((nblk, bg * n, bg * p), full3)
    pspec = pl.BlockSpec((t, t), lambda c: (0, 0))
    return pl.pallas_call(
        _s5_scan_kernel,
        grid=(t_all // t,),
        in_specs=[pl.BlockSpec((t, width), lambda c: (c, 0)),
                  pspec, pspec,
                  pl.BlockSpec((8, lanes), lambda c: (0, 0)),
                  wspec_b, wspec_b, wspec_c, wspec_c,
                  pl.BlockSpec((1, width), lambda c: (0, 0))],
        out_specs=pl.BlockSpec((t, width), lambda c: (c, 0)),
        out_shape=jax.ShapeDtypeStruct((t_all, width), BF16),
        scratch_shapes=[pltpu.VMEM((t, lanes), F32), pltpu.VMEM((t, lanes), F32),
                        pltpu.VMEM((8, lanes), F32), pltpu.VMEM((8, lanes), F32),
                        pltpu.VMEM((8, lanes), F32), pltpu.VMEM((8, lanes), F32),
                        pltpu.VMEM((sub, lanes), F32), pltpu.VMEM((sub, lanes), F32),
                        pltpu.VMEM((t, width), F32), pltpu.VMEM((t, width), BF16),
                        pltpu.VMEM((t, width), BF16)],
        compiler_params=_cparams(1),
    )(u, jnp.asarray(perm, dtype=BF16), jnp.asarray(perm.T, dtype=BF16), a8, blk_b(bb_r), blk_b(bb_i),
      blk_c(c_re), blk_c(-c_im), d_skip.reshape(1, width))


def _s5_layer(x, g, sh, sc, gate, w_in, lam_re, lam_im, log_dt, b_re, b_im, c_re, c_im, d_skip, w_out):
    ones = jnp.ones((1, w_in.shape[1]), F32)
    u = _norm_mod_matmul(x, g, sh, sc, w_in.astype(BF16), ones, F32)
    ar, ai, bb_r, bb_i = _s5_params(lam_re, lam_im, log_dt, b_re, b_im)
    gy = _s5_scan(u, ar, ai, bb_r, bb_i, c_re, c_im, d_skip)
    return _matmul_residual(gy, w_out.astype(BF16), x, gate, glu=True)


def _cmp_kernel(x_ref, w1a_ref, w1b_ref, pos_ref, w1_ref, w2_ref, o_ref, acc_a, acc_b):
    l = pl.program_id(2)

    @pl.when(l == 0)
    def _():
        acc_a[...] = jnp.zeros_like(acc_a)
        acc_b[...] = jnp.zeros_like(acc_b)

    x = x_ref[...]
    acc_a[...] += jnp.dot(x, w1a_ref[0, 0], preferred_element_type=F32)
    acc_b[...] += jnp.dot(x, w1b_ref[0, 0], preferred_element_type=F32)

    @pl.when(l == pl.num_programs(2) - 1)
    def _():
        m = acc_a.shape[0]
        posc = jnp.zeros((8, NSA_HEAD_DIM), F32)
        for ll in range(CMP_LEN):
            prow = jnp.broadcast_to(pos_ref[0, ll:ll + 1, :], (8, NSA_HEAD_DIM)).astype(BF16)
            posc = posc + jnp.dot(prow, w1_ref[0, ll], preferred_element_type=F32)
        pre = acc_a[...] + pltpu.roll(acc_b[...], m - 1, 0) + posc[0:1, :]
        hid = jax.nn.gelu(pre)
        o_ref[0, 0] = jnp.dot(hid.astype(BF16), w2_ref[0], preferred_element_type=F32).astype(o_ref.dtype)


def _compress(qkv, kv_col0, pos, w1, w2):
    l_all, c = qkv.shape
    half = CMP_LEN // 2
    m = l_all // half
    cb = c // NSA_HEAD_DIM
    x2 = qkv.reshape(m, half * c)
    g = NSA_KV_GROUPS
    dh = NSA_HEAD_DIM
    return pl.pallas_call(
        _cmp_kernel,
        grid=(2, g, half),
        in_specs=[pl.BlockSpec((m, dh), lambda s, gg, l: (0, l * cb + kv_col0 + g * s + gg)),
                  pl.BlockSpec((1, 1, dh, dh), lambda s, gg, l: (s, l, 0, 0)),
                  pl.BlockSpec((1, 1, dh, dh), lambda s, gg, l: (s, l + half, 0, 0)),
                  pl.BlockSpec((1, CMP_LEN, dh), lambda s, gg, l: (s, 0, 0)),
                  pl.BlockSpec((1, CMP_LEN, dh, dh), lambda s, gg, l: (s, 0, 0, 0)),
                  pl.BlockSpec((1, dh, dh), lambda s, gg, l: (s, 0, 0))],
        out_specs=pl.BlockSpec((1, 1, m, dh), lambda s, gg, l: (s, gg, 0, 0)),
        out_shape=jax.ShapeDtypeStruct((2, g, m, dh), BF16),
        scratch_shapes=[pltpu.VMEM((m, dh), F32), pltpu.VMEM((m, dh), F32)],
        compiler_params=_cparams(3),
    )(x2, w1, w1, pos, w1, w2)


def _softmax2_rows(s, mask):
    s = jnp.where(mask, s, -1e30)
    m = jnp.max(s, axis=-1, keepdims=True)
    p = jnp.exp2(s - m)
    return p, 1.0 / jnp.maximum(jnp.sum(p, axis=-1, keepdims=True), 1e-30)


def _shifted_softmax(s, mask, shift):
    p = jnp.exp2(jnp.where(mask, s - shift, -1e30))
    return p, jnp.sum(p, axis=-1, keepdims=True)


def _any_out_of_range(den, needed):
    ok = (den > 0.0) & (den < SEL_DEN_LIMIT)
    return jnp.max(jnp.where(needed & jnp.logical_not(ok), 1.0, 0.0)) > 0.0


def _nsa_attn_kernel(*refs):
    redo = _nsa_attn_body(True, *refs)

    @pl.when(redo)
    def _():
        _nsa_attn_body(False, *refs)


def _nsa_attn_body(fast, q_ref, gt_ref, kc_ref, vc_ref, agg_ref, ksa_ref, vsa_ref, kw_ref, vw_ref,
                   o_ref, qaug_scr, s_a, s_b, acc_scr, m_scr, shift_scr):
    b = pl.program_id(1)
    qb = q_ref.shape[0]
    dh = NSA_HEAD_DIM
    rep = NSA_REP
    rows = rep * qb
    n_cmp = kc_ref.shape[2]
    n_sel = agg_ref.shape[1]
    top_n = min(SEL_TOPN, n_sel)
    t0 = pl.multiple_of(b * qb, qb)

    q_blk = q_ref[...]
    qs = jnp.concatenate([q_blk[:, r * dh:(r + 1) * dh] for r in range(rep)], axis=0)
    tpos = t0 + lax.broadcasted_iota(jnp.int32, (rows, 1), 0) % qb
    all_rows = jnp.ones((rows, 1), jnp.bool_)

    kc = kc_ref[0, 0]
    s_c = lax.dot_general(qs, kc, NT_DIMS, preferred_element_type=F32)
    cmp_end = lax.broadcasted_iota(jnp.int32, (1, n_cmp), 1) * CMP_STRIDE + (CMP_LEN - 1)
    has_c = tpos >= CMP_LEN - 1
    if fast:
        p_c, den_c = _shifted_softmax(s_c, cmp_end <= tpos, s_c[:, 0:1])
        redo = _any_out_of_range(den_c, has_c)
        inv_c = 1.0 / jnp.maximum(den_c, 1e-30)
    else:
        p_c, inv_c = _softmax2_rows(s_c, cmp_end <= tpos)
    p_c = p_c * jnp.where(has_c, inv_c, 0.0)
    o_c = jnp.dot(p_c.astype(BF16), vc_ref[0, 0], preferred_element_type=F32)

    psum = p_c[0:qb]
    for r in range(1, rep):
        psum = psum + p_c[r * qb:(r + 1) * qb]
    p_hi = psum.astype(BF16)
    p_lo = (psum - p_hi.astype(F32)).astype(BF16)
    agg = agg_ref[...]
    imp = (jnp.dot(p_hi, agg, preferred_element_type=F32)
           + jnp.dot(p_lo, agg, preferred_element_type=F32))
    imp_t = imp.T
    tq = t0 + lax.broadcasted_iota(jnp.int32, (1, qb), 1)
    blk = lax.broadcasted_iota(jnp.int32, (n_sel, qb), 0)
    blk_f = blk.astype(F32)
    cur = tq // SEL_LEN
    valid = blk * SEL_LEN <= tq
    if fast:
        score = jnp.where(valid, imp_t, -1.0 - blk_f)
        score = jnp.where(blk == cur - 1, 1e9, score)
        score = jnp.where(blk == cur, 2e9, score)
        score = jnp.where(blk == 0, 3e9, score)
    else:
        forced = (blk == 0) | (blk == cur) | (blk == cur - 1)
        score = jnp.where(forced, 1e9, jnp.where(valid, imp_t, -1e9))
    pen_t = jnp.full((n_sel, qb), MASK_PENALTY, F32)
    for _ in range(top_n):
        mx = jnp.max(score, axis=0, keepdims=True)
        hit = score == mx
        if not fast:
            idx = jnp.min(jnp.where(hit, blk_f, float(n_sel)), axis=0, keepdims=True)
            hit = blk_f == idx
        pen_t = jnp.where(hit, 0.0, pen_t)
        score = jnp.where(hit, -jnp.inf, score)
    if fast:
        taken = jnp.sum(jnp.where(pen_t == 0.0, 1.0, 0.0), axis=0, keepdims=True)
        redo = redo | (jnp.max(jnp.where(taken != float(top_n), 1.0, 0.0)) > 0.0)
    pen = pen_t.T.astype(BF16)
    n_half = qaug_scr.shape[0]
    for hh in range(n_half):
        if n_sel >= ONEHOT_PERIOD:
            ph = pen[:, hh * ONEHOT_PERIOD:(hh + 1) * ONEHOT_PERIOD]
        else:
            ph = jnp.concatenate(
                [pen, jnp.full((qb, ONEHOT_PERIOD - n_sel), MASK_PENALTY, BF16)], axis=1)
        qaug_scr[hh] = jnp.concatenate([qs, jnp.concatenate([ph] * rep, axis=0)], axis=1)

    kt = SEL_KV_TILE
    period_keys = ONEHOT_PERIOD * SEL_LEN

    def sel_scores(i, s_out):
        k0 = pl.multiple_of(i * kt, kt)
        s_out[...] = lax.dot_general(qaug_scr[k0 // period_keys], ksa_ref[pl.ds(k0, kt), :], NT_DIMS,
                                     preferred_element_type=F32)

    def causal_mask(i, s):
        kpos = i * kt + lax.broadcasted_iota(jnp.int32, (1, kt), 1)
        return jnp.where(kpos <= tpos, s, MASK_PENALTY)

    def update_running_max(i, s_in, causal):
        k0 = pl.multiple_of(i * kt, kt)
        s = causal_mask(i, s_in[...]) if causal else s_in[...]
        m_run = m_scr[...]
        m_new = jnp.maximum(m_run, jnp.max(s, axis=-1, keepdims=True))
        p = jnp.exp2(s - m_new)
        acc_scr[...] = (jnp.exp2(m_run - m_new) * acc_scr[...]
                        + jnp.dot(p.astype(BF16), vsa_ref[pl.ds(k0, kt), :], preferred_element_type=F32))
        m_scr[...] = m_new

    def update_fixed_shift(i, s_in, causal):
        k0 = pl.multiple_of(i * kt, kt)
        s = causal_mask(i, s_in[...]) if causal else s_in[...]
        shift = shift_scr[...]
        p = jnp.concatenate([jnp.exp2(s[:, c * dh:(c + 1) * dh] - shift).astype(BF16)
                             for c in range(kt // dh)], axis=1)
        acc_scr[...] += jnp.dot(p, vsa_ref[pl.ds(k0, kt), :], preferred_element_type=F32)

    def sweep(update):
        def pair(j, carry):
            sel_scores(2 * j + 1, s_b)
            update(2 * j, s_a, False)
            sel_scores(2 * j + 2, s_a)
            update(2 * j + 1, s_b, False)
            return carry

        acc_scr[...] = jnp.zeros_like(acc_scr)
        lax.fori_loop(0, n_full // 2, pair, 0)
        odd = n_full % 2 == 1

        @pl.when(odd)
        def _():
            sel_scores(n_full, s_b)
            update(n_full - 1, s_a, False)
            update(n_full, s_b, True)

        @pl.when(jnp.logical_not(odd))
        def _():
            update(n_full, s_a, True)

    n_full = t0 // kt
    sel_scores(0, s_a)
    if fast:
        first = jnp.max(causal_mask(0, s_a[...]), axis=-1, keepdims=True)
        shift_scr[...] = jnp.broadcast_to(first, shift_scr.shape)
        sweep(update_fixed_shift)
    else:
        m_scr[...] = jnp.full(m_scr.shape, -1e30, F32)
        sweep(update_running_max)
    acc_s = acc_scr[...]
    den_s = acc_s[:, dh:dh + 1]
    o_s = acc_s[:, 0:dh] * (1.0 / den_s)

    wlen = WIN + qb
    w0 = pl.multiple_of(jnp.maximum(t0 - WIN, 0), qb)
    s_w = lax.dot_general(qs, kw_ref[pl.ds(w0, wlen), :], NT_DIMS, preferred_element_type=F32)
    diff = tpos - (w0 + lax.broadcasted_iota(jnp.int32, (1, wlen), 1))
    mask_w = (diff >= 0) & (diff < WIN)
    if fast:
        k_diag = jnp.concatenate([kw_ref[pl.ds(t0, qb), :]] * rep, axis=0).astype(F32)
        diag = jnp.sum(qs.astype(F32) * k_diag, axis=-1, keepdims=True)
        p_w, den_w = _shifted_softmax(s_w, mask_w, diag)
        o_w = jnp.dot(p_w.astype(BF16), vw_ref[pl.ds(w0, wlen), :], preferred_element_type=F32)
        o_w = o_w * (1.0 / den_w)
        redo = redo | _any_out_of_range(den_s, all_rows) | _any_out_of_range(den_w, all_rows)
    else:
        p_w, inv_w = _softmax2_rows(s_w, mask_w)
        o_w = jnp.dot((p_w * inv_w).astype(BF16), vw_ref[pl.ds(w0, wlen), :], preferred_element_type=F32)

    gt = gt_ref[...]
    for r in range(rep):
        rs = slice(r * qb, (r + 1) * qb)
        o = (gt[:, r:r + 1] * o_c[rs]
             + gt[:, rep + r:rep + r + 1] * o_s[rs]
             + gt[:, 2 * rep + r:2 * rep + r + 1] * o_w[rs])
        o_ref[:, r * dh:(r + 1) * dh] = o.astype(o_ref.dtype)
    return redo if fast else None


def _nsa_agg(n_cmp_pad, n_cmp, n_sel):
    ratio, span = SEL_LEN // CMP_STRIDE, CMP_LEN // CMP_STRIDE
    agg = np.zeros((n_cmp_pad, n_sel), np.float32)
    jj = np.arange(n_sel)
    for m in range(ratio):
        for n in range(span):
            ii = ratio * jj + m - n
            ok = (ii >= 0) & (ii < n_cmp)
            agg[ii[ok], jj[ok]] += 1.0
    return jnp.asarray(agg, dtype=BF16)


def _nsa_attention(qkv, gates, kv_cmp):
    l_all = qkv.shape[0]
    dh = NSA_HEAD_DIM
    g = NSA_KV_GROUPS
    qb = Q_BLOCK
    qw = NSA_REP * dh
    n_cmp_pad = l_all // CMP_STRIDE
    n_cmp = (l_all - CMP_LEN) // CMP_STRIDE + 1
    n_sel = l_all // SEL_LEN
    agg = _nsa_agg(n_cmp_pad, n_cmp, n_sel)
    n_half = max(n_sel // ONEHOT_PERIOD, 1)
    q_blocks = (NSA_HEADS * dh) // dh
    key = np.arange(l_all)
    onehot = jnp.asarray((key[:, None] // SEL_LEN) % ONEHOT_PERIOD == np.arange(ONEHOT_PERIOD)[None, :],
                         dtype=BF16)
    ones_col = jnp.asarray(np.arange(dh)[None, :] == 0, dtype=BF16)
    grouped = lambda base: qkv[:, (q_blocks + base) * dh:(q_blocks + base + g) * dh].reshape(l_all, g, dh)
    ks_aug = jnp.concatenate([grouped(2 * g), jnp.broadcast_to(onehot[:, None, :], (l_all, g, ONEHOT_PERIOD))],
                             axis=-1).reshape(l_all, g * (dh + ONEHOT_PERIOD))
    vs_aug = jnp.concatenate([grouped(3 * g), jnp.broadcast_to(ones_col[:, None, :], (l_all, g, dh))],
                             axis=-1).reshape(l_all, g * 2 * dh)
    once = dict(pipeline_mode=pl.Buffered(1))
    kv_spec = lambda base: pl.BlockSpec((l_all, dh), lambda gg, b: (0, q_blocks + base + gg), **once)
    aug_spec = pl.BlockSpec((l_all, 2 * dh), lambda gg, b: (0, gg), **once)
    rows = NSA_REP * qb
    return pl.pallas_call(
        _nsa_attn_kernel,
        grid=(g, l_all // qb),
        in_specs=[pl.BlockSpec((qb, qw), lambda gg, b: (b, gg)),
                  pl.BlockSpec((qb, dh), lambda gg, b: (b, gg)),
                  pl.BlockSpec((1, 1, n_cmp_pad, dh), lambda gg, b: (0, gg, 0, 0)),
                  pl.BlockSpec((1, 1, n_cmp_pad, dh), lambda gg, b: (1, gg, 0, 0)),
                  pl.BlockSpec((n_cmp_pad, n_sel), lambda gg, b: (0, 0)),
                  aug_spec, aug_spec, kv_spec(4 * g), kv_spec(5 * g)],
        out_specs=pl.BlockSpec((qb, qw), lambda gg, b: (b, gg)),
        out_shape=jax.ShapeDtypeStruct((l_all, NSA_HEADS * dh), BF16),
        scratch_shapes=[pltpu.VMEM((n_half, rows, dh + ONEHOT_PERIOD), BF16),
                        pltpu.VMEM((rows, SEL_KV_TILE), F32), pltpu.VMEM((rows, SEL_KV_TILE), F32),
                        pltpu.VMEM((rows, 2 * dh), F32), pltpu.VMEM((rows, 1), F32),
                        pltpu.VMEM((rows, dh), F32)],
        compiler_params=_cparams(2),
    )(qkv, gates, kv_cmp, kv_cmp, agg, ks_aug, vs_aug, qkv, qkv)


def _nsa_layer(x, g, sh, sc, gate, w_in, k_pos, k_w1, k_w2, v_pos, v_w1, v_w2, w_out):
    dh = NSA_HEAD_DIM
    qd = NSA_HEADS * dh
    kvd = 2 * N_BRANCH * NSA_KV_GROUPS * dh
    w_main = w_in[:, :qd + kvd].astype(BF16)
    colscale = jnp.concatenate([jnp.full((1, qd), dh ** -0.5 * math.log2(math.e), F32),
                                jnp.ones((1, kvd), F32)], axis=1)
    qkv = _norm_mod_matmul(x, g, sh, sc, w_main, colscale, BF16)
    wg = w_in[:, qd + kvd:].reshape(-1, NSA_KV_GROUPS, NSA_REP, N_BRANCH).transpose(0, 1, 3, 2)
    wg = wg.reshape(-1, NSA_KV_GROUPS, N_BRANCH * NSA_REP)
    wg = jnp.pad(wg, ((0, 0), (0, 0), (0, dh - N_BRANCH * NSA_REP))).reshape(-1, NSA_KV_GROUPS * dh)
    gates = _norm_mod_matmul(x, g, sh, sc, wg.astype(BF16), jnp.ones((1, wg.shape[1]), F32), F32,
                             act="sigmoid")
    kv_cmp = _compress(qkv, qd // dh, jnp.stack([k_pos, v_pos]),
                       jnp.stack([k_w1, v_w1]).astype(BF16), jnp.stack([k_w2, v_w2]).astype(BF16))
    o = _nsa_attention(qkv, gates, kv_cmp)
    return _matmul_residual(o, w_out.astype(BF16), x, gate, glu=False)


def _peer_query_kernel(x_ref, g_ref, sh_ref, sc_ref, wq_ref, keys_ref, ht_ref, st_ref):
    h32 = _norm_mod(x_ref[...], g_ref[...], sh_ref[...], sc_ref[...])
    ht_ref[...] = h32.T.astype(BF16)
    q = jnp.dot(h32.astype(BF16), wq_ref[...], preferred_element_type=F32).astype(BF16)
    for hc in range(keys_ref.shape[0]):
        rows = slice(hc * PEER_NKEYS, (hc + 1) * PEER_NKEYS)
        st_ref[rows, :] = lax.dot_general(keys_ref[hc], q[:, hc * PEER_HALF:(hc + 1) * PEER_HALF],
                                          NT_DIMS, preferred_element_type=F32)


def _peer_query(x, g, sh, sc, w_q, sub_keys, tm=512):
    t, d = x.shape
    nq = w_q.shape[1]
    keys = sub_keys.reshape(-1, PEER_NKEYS, PEER_HALF).astype(BF16)
    row = lambda i: (0, 0)
    return pl.pallas_call(
        _peer_query_kernel,
        grid=(t // tm,),
        in_specs=[pl.BlockSpec((tm, d), lambda i: (i, 0)),
                  pl.BlockSpec((1, d), row), pl.BlockSpec((1, d), row), pl.BlockSpec((1, d), row),
                  pl.BlockSpec((d, nq), row),
                  pl.BlockSpec(keys.shape, lambda i: (0, 0, 0))],
        out_specs=[pl.BlockSpec((d, tm), lambda i: (0, i)),
                   pl.BlockSpec((keys.shape[0] * PEER_NKEYS, tm), lambda i: (0, i))],
        out_shape=[jax.ShapeDtypeStruct((d, t), BF16),
                   jax.ShapeDtypeStruct((keys.shape[0] * PEER_NKEYS, t), F32)],
        compiler_params=_cparams(1),
    )(x, g, sh, sc, w_q.astype(BF16), keys)


def _peer_cells():
    return [(a, b) for a in range(PEER_TOPK) for b in range(PEER_TOPK) if (a + 1) * (b + 1) <= PEER_TOPK]


def _take_max(cur, iota, exact):
    v = jnp.max(cur, axis=0, keepdims=True)
    hit = cur == v
    if exact:
        idx = jnp.min(jnp.where(hit, iota, cur.shape[0]), axis=0, keepdims=True)
        hit = iota == idx
    return v, hit


def _top_ranks(s, iota_k, exact):
    rank = jnp.full(s.shape, RANK_NONE, F32)
    vals = []
    cur = s
    for a in range(PEER_TOPK):
        v, hit = _take_max(cur, iota_k, exact)
        rank = jnp.where(hit, float(a), rank)
        cur = jnp.where(hit, -jnp.inf, cur)
        vals.append(v)
    return vals, rank


def _peer_route_body(st_ref, seg_ref, r2_ref, ln_ref, g1_ref, g2_ref, exact):
    tn = st_ref.shape[1]
    nk = PEER_NKEYS
    cells = _peer_cells()
    n_cell = len(cells)
    n_pad = -(-n_cell // 8) * 8
    n_seg = seg_ref.shape[1]
    iota_k = lax.broadcasted_iota(jnp.int32, (nk, tn), 0)
    iota_c = lax.broadcasted_iota(jnp.int32, (n_pad, tn), 0)
    count = lambda m: jnp.sum(m.astype(F32), axis=0, keepdims=True)
    tied = jnp.zeros((1, tn), jnp.bool_)
    for h in range(PEER_HEADS):
        s1 = st_ref[(2 * h) * nk:(2 * h + 1) * nk, :]
        s2 = st_ref[(2 * h + 1) * nk:(2 * h + 2) * nk, :]
        v1, rank1 = _top_ranks(s1, iota_k, exact)
        v2, rank2 = _top_ranks(s2, iota_k, exact)
        cand = jnp.concatenate([v1[a] + v2[b] for a, b in cells]
                               + [jnp.full((n_pad - n_cell, tn), -jnp.inf, F32)], axis=0)
        top = v1[0] + v2[0]
        e_c = jnp.exp(cand - top)
        chosen = jnp.zeros((n_pad, tn), jnp.bool_)
        cur = cand
        for _ in range(PEER_TOPK):
            _, hit = _take_max(cur, iota_c, exact)
            chosen = chosen | hit
            cur = jnp.where(hit, -jnp.inf, cur)
        chosen_f = chosen.astype(F32)
        if not exact:
            k = float(PEER_TOPK)
            tied = (tied | (count(rank1 != RANK_NONE) != k) | (count(rank2 != RANK_NONE) != k)
                    | (jnp.sum(chosen_f, axis=0, keepdims=True) != k))
        z = jnp.sum(chosen_f * e_c, axis=0, keepdims=True)
        chosen_pad = jnp.concatenate([chosen_f, jnp.zeros((n_seg - n_pad, tn), F32)], axis=0)
        rowlen = jnp.dot(seg_ref[...], chosen_pad.astype(BF16), preferred_element_type=F32)
        ln = jnp.zeros((nk, tn), F32)
        for a in range(PEER_TOPK):
            ln = jnp.where(rank1 == float(a), rowlen[a:a + 1, :], ln)
        rows = slice(h * nk, (h + 1) * nk)
        r2_ref[rows, :] = rank2.astype(r2_ref.dtype)
        ln_ref[h] = ln
        g1_ref[h] = jnp.exp(s1 - v1[0]) / z
        g2_ref[rows, :] = jnp.exp(s2 - v2[0]).astype(g2_ref.dtype)
    return tied


def _peer_route_kernel(st_ref, seg_ref, r2_ref, ln_ref, g1_ref, g2_ref):
    refs = (st_ref, seg_ref, r2_ref, ln_ref, g1_ref, g2_ref)
    tied = _peer_route_body(*refs, exact=False)

    @pl.when(jnp.max(tied.astype(F32)) > 0.0)
    def _():
        _peer_route_body(*refs, exact=True)


def _peer_route(st, tn=128):
    n_rows, t = st.shape
    cells = _peer_cells()
    seg = np.zeros((PEER_TOPK, PEER_NKEYS), np.float32)
    for c, (a, _) in enumerate(cells):
        seg[a, c] = 1.0
    out_rows = PEER_HEADS * PEER_NKEYS
    shp = jax.ShapeDtypeStruct((out_rows, t), BF16)
    shp_row = jax.ShapeDtypeStruct((PEER_HEADS, PEER_NKEYS, t), F32)
    spec = pl.BlockSpec((out_rows, tn), lambda i: (0, i))
    spec_row = pl.BlockSpec((PEER_HEADS, PEER_NKEYS, tn), lambda i: (0, 0, i))
    return pl.pallas_call(
        _peer_route_kernel,
        grid=(t // tn,),
        in_specs=[pl.BlockSpec((n_rows, tn), lambda i: (0, i)),
                  pl.BlockSpec((PEER_TOPK, PEER_NKEYS), lambda i: (0, 0))],
        out_specs=[spec, spec_row, spec_row, spec],
        out_shape=[shp, shp_row, shp_row, shp],
        compiler_params=_cparams(1),
    )(st, jnp.asarray(seg, dtype=BF16))


def _peer_expert_kernel(h_ref, u_ref, vt_ref, r2_ref, ln_ref, g1_ref, g2_ref, x_ref, gate_ref, nf_ref,
                        o_ref, acc_t, p_scr, pre_a, pre_b, *, final_norm):
    c = pl.program_id(1)
    last = pl.num_programs(1) - 1
    ec = u_ref.shape[0]
    nk = PEER_NKEYS
    tm = h_ref.shape[1]
    zero = jnp.zeros((), BF16)

    def score(pre_out):
        pre_out[...] = jnp.dot(u_ref[...], h_ref[...], preferred_element_type=F32)

    def finish(pre_in):
        for ii in range(ec // nk):
            w = jnp.zeros((nk, tm), BF16)
            for h in range(PEER_HEADS):
                rows = slice(h * nk, (h + 1) * nk)
                ln_row = ln_ref[h, ii:ii + 1, :].astype(BF16)
                g1_row = g1_ref[h, ii:ii + 1, :].astype(BF16)
                w = w + jnp.where(r2_ref[rows, :] < ln_row, g2_ref[rows, :], zero) * g1_row
            act = jax.nn.gelu(pre_in[ii * nk:(ii + 1) * nk, :])
            p_scr[ii * nk:(ii + 1) * nk, :] = w * act.astype(BF16)
        acc_t[...] += jnp.dot(vt_ref[0], p_scr[...], preferred_element_type=F32)

    @pl.when(c == 0)
    def _():
        acc_t[...] = jnp.zeros_like(acc_t)
        score(pre_a)

    @pl.when((c > 0) & (c < last) & (c % 2 == 1))
    def _():
        score(pre_b)
        finish(pre_a)

    @pl.when((c > 0) & (c < last) & (c % 2 == 0))
    def _():
        score(pre_a)
        finish(pre_b)

    @pl.when(c == last)
    def _():
        finish(pre_b if PEER_LAST_IS_EVEN else pre_a)
        xo = x_ref[...] + gate_ref[...] * acc_t[...].T
        if final_norm:
            ms = jnp.mean(xo * xo, axis=-1, keepdims=True)
            xo = (xo * lax.rsqrt(ms + RMS_EPS)) * nf_ref[...]
        o_ref[...] = xo


def _peer_experts(ht, u_bf, vt_bf, route, x, gate, norm_final, final_norm, tm=512, ec=PEER_EXPERT_CHUNK):
    t, d = x.shape
    e = u_bf.shape[0]
    n_chunks = e // ec
    assert (n_chunks % 2 == 0) == PEER_LAST_IS_EVEN
    r2, ln, g1, g2 = route
    rspec = pl.BlockSpec((r2.shape[0], tm), lambda i, c: (0, i))
    prev = lambda c: jnp.maximum(c - 1, 0)
    kspec = pl.BlockSpec((PEER_HEADS, ec // PEER_NKEYS, tm), lambda i, c: (0, prev(c), i))
    row = lambda i, c: (0, 0)
    return pl.pallas_call(
        functools.partial(_peer_expert_kernel, final_norm=final_norm),
        grid=(t // tm, n_chunks + 1),
        in_specs=[pl.BlockSpec((d, tm), lambda i, c: (0, i)),
                  pl.BlockSpec((ec, d), lambda i, c: (jnp.minimum(c, n_chunks - 1), 0)),
                  pl.BlockSpec((1, d, ec), lambda i, c: (prev(c), 0, 0)),
                  rspec, kspec, kspec, rspec,
                  pl.BlockSpec((tm, d), lambda i, c: (i, 0)),
                  pl.BlockSpec((1, d), row), pl.BlockSpec((1, d), row)],
        out_specs=pl.BlockSpec((tm, d), lambda i, c: (i, 0)),
        out_shape=jax.ShapeDtypeStruct((t, d), F32),
        scratch_shapes=[pltpu.VMEM((d, tm), F32), pltpu.VMEM((ec, tm), BF16),
                        pltpu.VMEM((ec, tm), F32), pltpu.VMEM((ec, tm), F32)],
        compiler_params=_cparams(2),
    )(ht, u_bf, vt_bf, r2, ln, g1, g2, x, gate, norm_final)


def _peer_layer(x, g, sh, sc, gate, w_q, sub_keys, u_tab, v_tab, norm_final, final_norm):
    ht, st = _peer_query(x, g, sh, sc, w_q, sub_keys)
    route = _peer_route(st)
    e, d = v_tab.shape
    vt = v_tab.astype(BF16).reshape(e // PEER_EXPERT_CHUNK, PEER_EXPERT_CHUNK, d).transpose(0, 2, 1)
    return _peer_experts(ht, u_tab.astype(BF16), vt, route, x, gate, norm_final, final_norm)


def kernel(x, c, ada_w, ada_b, norm_mix, norm_ffn, norm_final, ssm_w_in, ssm_lambda_re, ssm_lambda_im, ssm_log_dt, ssm_b_re, ssm_b_im, ssm_c_re, ssm_c_im, ssm_d, ssm_w_out, nsa_w_in, nsa_cmp_k_pos, nsa_cmp_k_w1, nsa_cmp_k_w2, nsa_cmp_v_pos, nsa_cmp_v_w1, nsa_cmp_v_w2, nsa_w_out, peer_w_q, peer_sub_keys, peer_u, peer_v):
    bsz, l_all, d = x.shape
    assert bsz == 1
    depth = ada_w.shape[0]
    mod = _adaln(c, ada_w, ada_b)
    xt = x.reshape(l_all, d)
    nf = norm_final.reshape(1, d)
    for i in range(depth):
        sh1, sc1, g1, sh2, sc2, g2 = [mod[i, :, k * d:(k + 1) * d] for k in range(6)]
        j = i // 2
        gm = norm_mix[i].reshape(1, d)
        if i % 2 == 0:
            xt = _s5_layer(xt, gm, sh1, sc1, g1, ssm_w_in[j], ssm_lambda_re[j], ssm_lambda_im[j],
                           ssm_log_dt[j], ssm_b_re[j], ssm_b_im[j], ssm_c_re[j], ssm_c_im[j],
                           ssm_d[j], ssm_w_out[j])
        else:
            xt = _nsa_layer(xt, gm, sh1, sc1, g1, nsa_w_in[j], nsa_cmp_k_pos[j], nsa_cmp_k_w1[j],
                            nsa_cmp_k_w2[j], nsa_cmp_v_pos[j], nsa_cmp_v_w1[j], nsa_cmp_v_w2[j],
                            nsa_w_out[j])
        xt = _peer_layer(xt, norm_ffn[i].reshape(1, d), sh2, sc2, g2, peer_w_q[i], peer_sub_keys[i],
                         peer_u[i], peer_v[i], nf, final_norm=(i == depth - 1))
    return xt.reshape(bsz, l_all, d)
```

```python
import functools
import math

import numpy as np
import jax
import jax.numpy as jnp
from jax import lax
from jax.experimental import pallas as pl
from jax.experimental.pallas import tpu as pltpu

F32 = jnp.float32
BF16 = jnp.bfloat16

RMS_EPS = 1e-6

SSM_GROUP = 16
SSM_STATE = 64
SSM_BLOCK_GROUPS = 8
S5_SUBSEQ = 8
S5_CHUNK = 128
S5_LANE_BLOCK = 1024

NSA_HEADS = 16
NSA_HEAD_DIM = 128
NSA_KV_GROUPS = 4
NSA_REP = NSA_HEADS // NSA_KV_GROUPS
N_BRANCH = 3
CMP_LEN = 32
CMP_STRIDE = 16
SEL_LEN = 64
SEL_TOPN = 16
WIN = 512
Q_BLOCK = 256
SEL_KV_TILE = 512
MASK_PENALTY = -(2.0 ** 30)
SEL_DEN_LIMIT = 1e30
ONEHOT_PERIOD = 128

PEER_HEADS = 8
PEER_NKEYS = 128
PEER_TOPK = 16
PEER_HALF = 128
PEER_EXPERT_CHUNK = 1024
PEER_LAST_IS_EVEN = (PEER_NKEYS ** 2 // PEER_EXPERT_CHUNK) % 2 == 0
RANK_NONE = 255.0
RANK_CODE = 2.0 ** 100
RANK_CODE_STEPS = 32.0

VMEM_LIMIT = 56 * 1024 * 1024
NT_DIMS = (((1,), (1,)), ((), ()))


def _cparams(n_axes):
    return pltpu.CompilerParams(
        dimension_semantics=("arbitrary",) * n_axes, vmem_limit_bytes=VMEM_LIMIT)


def _norm_mod(x, g, sh, sc):
    ms = jnp.mean(x * x, axis=-1, keepdims=True)
    xn = x * lax.rsqrt(ms + RMS_EPS)
    return (xn * g) * (1.0 + sc) + sh


def _adaln_kernel(c_ref, w_ref, b_ref, o_ref):
    c = c_ref[...]
    cond = c * jax.nn.sigmoid(c)
    o_ref[0] = jnp.dot(cond.astype(BF16), w_ref[0].astype(BF16),
                       preferred_element_type=F32) + b_ref[0]


def _adaln(c, ada_w, ada_b):
    depth, d, n = ada_w.shape
    tn = 1024
    c8 = jnp.broadcast_to(c, (8, d))
    out = pl.pallas_call(
        _adaln_kernel,
        grid=(depth, n // tn),
        in_specs=[pl.BlockSpec((8, d), lambda i, j: (0, 0)),
                  pl.BlockSpec((1, d, tn), lambda i, j: (i, 0, j)),
                  pl.BlockSpec((1, 1, tn), lambda i, j: (i, 0, j))],
        out_specs=pl.BlockSpec((1, 8, tn), lambda i, j: (i, 0, j)),
        out_shape=jax.ShapeDtypeStruct((depth, 8, n), F32),
        compiler_params=_cparams(2),
    )(c8, ada_w, ada_b.reshape(depth, 1, n))
    return out[:, 0:1, :]


def _nmm_kernel(x_ref, g_ref, sh_ref, sc_ref, w_ref, cs_ref, o_ref, h_scr, *, act):
    @pl.when(pl.program_id(1) == 0)
    def _():
        h_scr[...] = _norm_mod(x_ref[...], g_ref[...], sh_ref[...], sc_ref[...]).astype(BF16)

    acc = jnp.dot(h_scr[...], w_ref[...], preferred_element_type=F32)
    if act == "sigmoid":
        acc = jax.nn.sigmoid(acc)
    else:
        acc = acc * cs_ref[...]
    o_ref[...] = acc.astype(o_ref.dtype)


def _norm_mod_matmul(x, g, sh, sc, w, colscale, out_dtype, act="scale", tm=512, tn=1024):
    t, d = x.shape
    n = w.shape[1]
    tn = min(tn, n)
    row = lambda i, j: (0, 0)
    return pl.pallas_call(
        functools.partial(_nmm_kernel, act=act),
        grid=(t // tm, n // tn),
        in_specs=[pl.BlockSpec((tm, d), lambda i, j: (i, 0)),
                  pl.BlockSpec((1, d), row), pl.BlockSpec((1, d), row), pl.BlockSpec((1, d), row),
                  pl.BlockSpec((d, tn), lambda i, j: (0, j)),
                  pl.BlockSpec((1, tn), lambda i, j: (0, j))],
        out_specs=pl.BlockSpec((tm, tn), lambda i, j: (i, j)),
        out_shape=jax.ShapeDtypeStruct((t, n), out_dtype),
        scratch_shapes=[pltpu.VMEM((tm, d), BF16)],
        compiler_params=_cparams(2),
    )(x, g, sh, sc, w, colscale)


def _mm_res_kernel(a_ref, w_ref, x_ref, g_ref, o_ref):
    y = jnp.dot(a_ref[...], w_ref[...], preferred_element_type=F32)
    o_ref[...] = x_ref[...] + g_ref[...] * y


def _mm_glu_res_kernel(a_ref, wa_ref, wb_ref, x_ref, g_ref, o_ref):
    a = a_ref[...]
    ya = jnp.dot(a, wa_ref[...], preferred_element_type=F32)
    yb = jnp.dot(a, wb_ref[...], preferred_element_type=F32)
    o_ref[...] = x_ref[...] + g_ref[...] * (ya * jax.nn.sigmoid(yb))


def _matmul_residual(a, w, x, gate, glu, tm=512, tn=1024):
    t, k = a.shape
    n = x.shape[1]
    nb = n // tn
    a_spec = pl.BlockSpec((tm, k), lambda i, j: (i, 0))
    w_spec = pl.BlockSpec((k, tn), lambda i, j: (0, j))
    tail = [pl.BlockSpec((tm, tn), lambda i, j: (i, j)), pl.BlockSpec((1, tn), lambda i, j: (0, j))]
    if glu:
        kern = _mm_glu_res_kernel
        in_specs = [a_spec, w_spec, pl.BlockSpec((k, tn), lambda i, j: (0, j + nb))] + tail
        args = (a, w, w, x, gate)
    else:
        kern = _mm_res_kernel
        in_specs = [a_spec, w_spec] + tail
        args = (a, w, x, gate)
    return pl.pallas_call(
        kern,
        grid=(t // tm, nb),
        in_specs=in_specs,
        out_specs=pl.BlockSpec((tm, tn), lambda i, j: (i, j)),
        out_shape=jax.ShapeDtypeStruct((t, n), F32),
        compiler_params=_cparams(2),
    )(*args)


def _s5_param_kernel(lre_ref, lim_ref, ldt_ref, bre_ref, bim_ref, ar_ref, ai_ref, bbr_ref, bbi_ref):
    lre = lre_ref[...]
    lim = lim_ref[...]
    dt = jnp.exp(ldt_ref[...])
    mag = jnp.exp(lre * dt)
    ar = mag * jnp.cos(lim * dt)
    ai = mag * jnp.sin(lim * dt)
    den = lre * lre + lim * lim
    cr = ((ar - 1.0) * lre + ai * lim) / den
    ci = (ai * lre - (ar - 1.0) * lim) / den
    ar_ref[...] = ar
    ai_ref[...] = ai
    bbr_ref[...] = cr * bre_ref[...] - ci * bim_ref[...]
    bbi_ref[...] = cr * bim_ref[...] + ci * bre_ref[...]


def _s5_params(lam_re, lam_im, log_dt, b_re, b_im):
    g, n, p = b_re.shape
    rep = lambda a: jnp.repeat(a, p, axis=1)
    shp = jax.ShapeDtypeStruct((g, n * p), F32)
    ar, ai, bbr, bbi = pl.pallas_call(
        _s5_param_kernel, out_shape=(shp, shp, shp, shp),
    )(rep(lam_re), rep(lam_im), log_dt.reshape(g, 1), b_re.reshape(g, n * p), b_im.reshape(g, n * p))
    return ar[:, ::p], ai[:, ::p], bbr.reshape(g, n, p), bbi.reshape(g, n, p)


def _s5_scan_kernel(u_ref, a_ref, bbr_ref, bbi_ref, cr_ref, cin_ref, d_ref, o_ref,
                    bur, bui, hin_r, hin_i, car_r, car_i, apr, api):
    t, width = u_ref.shape
    sub = t // S5_SUBSEQ
    lanes = bur.shape[1]
    nblk = bbr_ref.shape[0]
    kin = width // nblk
    kst = lanes // nblk

    @pl.when(pl.program_id(0) == 0)
    def _init():
        car_r[...] = jnp.zeros_like(car_r)
        car_i[...] = jnp.zeros_like(car_i)
        ar = a_ref[0:1, :]
        ai = a_ref[1:2, :]
        pr, pi = ar, ai
        apr[0:1, :] = pr
        api[0:1, :] = pi
        for i in range(1, sub):
            pr, pi = pr * ar - pi * ai, pr * ai + pi * ar
            apr[i:i + 1, :] = pr
            api[i:i + 1, :] = pi

    for k in range(nblk):
        uk = u_ref[:, kin * k:kin * (k + 1)].astype(BF16)
        bur[:, kst * k:kst * (k + 1)] = jnp.dot(uk, bbr_ref[k], preferred_element_type=F32)
        bui[:, kst * k:kst * (k + 1)] = jnp.dot(uk, bbi_ref[k], preferred_element_type=F32)

    lb = S5_LANE_BLOCK
    for b in range(lanes // lb):
        sl = slice(b * lb, (b + 1) * lb)
        ar = jnp.broadcast_to(a_ref[0:1, sl], (S5_SUBSEQ, lb))
        ai = jnp.broadcast_to(a_ref[1:2, sl], (S5_SUBSEQ, lb))
        sr = jnp.zeros((S5_SUBSEQ, lb), F32)
        si = jnp.zeros((S5_SUBSEQ, lb), F32)
        for i in range(sub):
            rows = slice(S5_SUBSEQ * i, S5_SUBSEQ * (i + 1))
            sr, si = (ar * sr - ai * si + bur[rows, sl], ar * si + ai * sr + bui[rows, sl])
            bur[rows, sl] = sr
            bui[rows, sl] = si
        asr = apr[sub - 1:sub, sl]
        asi = api[sub - 1:sub, sl]
        hr = car_r[0:1, sl]
        hi = car_i[0:1, sl]
        for j in range(S5_SUBSEQ):
            hin_r[j:j + 1, sl] = hr
            hin_i[j:j + 1, sl] = hi
            er = sr[j:j + 1, :]
            ei = si[j:j + 1, :]
            hr, hi = er + asr * hr - asi * hi, ei + asr * hi + asi * hr
        car_r[0:1, sl] = hr
        car_i[0:1, sl] = hi
        hinr = hin_r[:, sl]
        hini = hin_i[:, sl]
        for i in range(sub):
            rows = slice(S5_SUBSEQ * i, S5_SUBSEQ * (i + 1))
            pr = jnp.broadcast_to(apr[i:i + 1, sl], (S5_SUBSEQ, lb))
            pi = jnp.broadcast_to(api[i:i + 1, sl], (S5_SUBSEQ, lb))
            bur[rows, sl] = bur[rows, sl] + pr * hinr - pi * hini
            bui[rows, sl] = bui[rows, sl] + pr * hini + pi * hinr

    for k in range(nblk):
        sr = bur[:, kst * k:kst * (k + 1)].astype(BF16)
        si = bui[:, kst * k:kst * (k + 1)].astype(BF16)
        y = (jnp.dot(sr, cr_ref[k], preferred_element_type=F32)
             + jnp.dot(si, cin_ref[k], preferred_element_type=F32))
        cols = slice(kin * k, kin * (k + 1))
        y = y + d_ref[:, cols] * u_ref[:, cols]
        o_ref[:, cols] = jax.nn.gelu(y).astype(o_ref.dtype)


def _s5_scan(u_p, ar, ai, bb_r, bb_i, c_re, c_im, d_skip):
    t_all, width = u_p.shape
    g, n, p = bb_r.shape
    bg = SSM_BLOCK_GROUPS
    nblk = g // bg
    eye = jnp.eye(bg, dtype=F32)
    blk_b = lambda bb: jnp.einsum('kgnp,gh->kgphn', bb.reshape(nblk, bg, n, p), eye
                                  ).reshape(nblk, bg * p, bg * n).astype(BF16)
    blk_c = lambda cc: jnp.einsum('kgpn,gh->kgnhp', cc.reshape(nblk, bg, p, n), eye
                                  ).reshape(nblk, bg * n, bg * p).astype(BF16)
    lanes = g * n
    a8 = jnp.zeros((8, lanes), F32).at[0].set(ar.reshape(lanes)).at[1].set(ai.reshape(lanes))
    t = S5_CHUNK
    sub = t // S5_SUBSEQ
    full3 = lambda c: (0, 0, 0)
    wspec_b = pl.BlockSpec((nblk, bg * p, bg * n), full3)
    wspec_c = pl.BlockSpec((nblk, bg * n, bg * p), full3)
    return pl.pallas_call(
        _s5_scan_kernel,
        grid=(t_all // t,),
        in_specs=[pl.BlockSpec((t, width), lambda c: (c, 0)),
                  pl.BlockSpec((8, lanes), lambda c: (0, 0)),
                  wspec_b, wspec_b, wspec_c, wspec_c,
                  pl.BlockSpec((1, width), lambda c: (0, 0))],
        out_specs=pl.BlockSpec((t, width), lambda c: (c, 0)),
        out_shape=jax.ShapeDtypeStruct((t_all, width), BF16),
        scratch_shapes=[pltpu.VMEM((t, lanes), F32), pltpu.VMEM((t, lanes), F32),
                        pltpu.VMEM((8, lanes), F32), pltpu.VMEM((8, lanes), F32),
                        pltpu.VMEM((8, lanes), F32), pltpu.VMEM((8, lanes), F32),
                        pltpu.VMEM((sub, lanes), F32), pltpu.VMEM((sub, lanes), F32)],
        compiler_params=_cparams(1),
    )(u_p, a8, blk_b(bb_r), blk_b(bb_i), blk_c(c_re), blk_c(-c_im), d_skip.reshape(1, width))


def _s5_rows_to_subseq(x, inverse=False):
    t_all, d = x.shape
    sub = S5_CHUNK // S5_SUBSEQ
    shape = (t_all // S5_CHUNK, sub, S5_SUBSEQ, d) if inverse else (t_all // S5_CHUNK, S5_SUBSEQ, sub, d)
    return x.reshape(shape).transpose(0, 2, 1, 3).reshape(t_all, d)


def _s5_layer(x, g, sh, sc, gate, w_in, lam_re, lam_im, log_dt, b_re, b_im, c_re, c_im, d_skip, w_out):
    d = x.shape[1]
    x_p = _s5_rows_to_subseq(x)
    ones = jnp.ones((1, w_in.shape[1]), F32)
    u_p = _norm_mod_matmul(x_p, g, sh, sc, w_in.astype(BF16), ones, F32)
    ar, ai, bb_r, bb_i = _s5_params(lam_re, lam_im, log_dt, b_re, b_im)
    gy_p = _s5_scan(u_p, ar, ai, bb_r, bb_i, c_re, c_im, d_skip)
    xn_p = _matmul_residual(gy_p, w_out.astype(BF16), x_p, gate, glu=True)
    return _s5_rows_to_subseq(xn_p, inverse=True)


def _cmp_kernel(x_ref, w1a_ref, w1b_ref, pos_ref, w1_ref, w2_ref, o_ref, acc_a, acc_b):
    l = pl.program_id(2)

    @pl.when(l == 0)
    def _():
        acc_a[...] = jnp.zeros_like(acc_a)
        acc_b[...] = jnp.zeros_like(acc_b)

    x = x_ref[...]
    acc_a[...] += jnp.dot(x, w1a_ref[0, 0], preferred_element_type=F32)
    acc_b[...] += jnp.dot(x, w1b_ref[0, 0], preferred_element_type=F32)

    @pl.when(l == pl.num_programs(2) - 1)
    def _():
        m = acc_a.shape[0]
        posc = jnp.zeros((8, NSA_HEAD_DIM), F32)
        for ll in range(CMP_LEN):
            prow = jnp.broadcast_to(pos_ref[0, ll:ll + 1, :], (8, NSA_HEAD_DIM)).astype(BF16)
            posc = posc + jnp.dot(prow, w1_ref[0, ll], preferred_element_type=F32)
        pre = acc_a[...] + pltpu.roll(acc_b[...], m - 1, 0) + posc[0:1, :]
        hid = jax.nn.gelu(pre)
        o_ref[0, 0] = jnp.dot(hid.astype(BF16), w2_ref[0], preferred_element_type=F32).astype(o_ref.dtype)


def _compress(qkv, kv_col0, pos, w1, w2):
    l_all, c = qkv.shape
    half = CMP_LEN // 2
    m = l_all // half
    cb = c // NSA_HEAD_DIM
    x2 = qkv.reshape(m, half * c)
    g = NSA_KV_GROUPS
    dh = NSA_HEAD_DIM
    return pl.pallas_call(
        _cmp_kernel,
        grid=(2, g, half),
        in_specs=[pl.BlockSpec((m, dh), lambda s, gg, l: (0, l * cb + kv_col0 + g * s + gg)),
                  pl.BlockSpec((1, 1, dh, dh), lambda s, gg, l: (s, l, 0, 0)),
                  pl.BlockSpec((1, 1, dh, dh), lambda s, gg, l: (s, l + half, 0, 0)),
                  pl.BlockSpec((1, CMP_LEN, dh), lambda s, gg, l: (s, 0, 0)),
                  pl.BlockSpec((1, CMP_LEN, dh, dh), lambda s, gg, l: (s, 0, 0, 0)),
                  pl.BlockSpec((1, dh, dh), lambda s, gg, l: (s, 0, 0))],
        out_specs=pl.BlockSpec((1, 1, m, dh), lambda s, gg, l: (s, gg, 0, 0)),
        out_shape=jax.ShapeDtypeStruct((2, g, m, dh), BF16),
        scratch_shapes=[pltpu.VMEM((m, dh), F32), pltpu.VMEM((m, dh), F32)],
        compiler_params=_cparams(3),
    )(x2, w1, w1, pos, w1, w2)


def _softmax2_rows(s, mask):
    s = jnp.where(mask, s, -1e30)
    m = jnp.max(s, axis=-1, keepdims=True)
    p = jnp.exp2(s - m)
    return p, 1.0 / jnp.maximum(jnp.sum(p, axis=-1, keepdims=True), 1e-30)


def _shifted_softmax(s, mask, shift):
    p = jnp.exp2(jnp.where(mask, s - shift, -1e30))
    return p, jnp.sum(p, axis=-1, keepdims=True)


def _any_out_of_range(den, needed):
    ok = (den > 0.0) & (den < SEL_DEN_LIMIT)
    return jnp.max(jnp.where(needed & jnp.logical_not(ok), 1.0, 0.0)) > 0.0


def _nsa_attn_kernel(*refs):
    redo = _nsa_attn_body(True, *refs)

    @pl.when(redo)
    def _():
        _nsa_attn_body(False, *refs)


def _nsa_attn_body(fast, q_ref, gt_ref, kc_ref, vc_ref, agg_ref, ksa_ref, vsa_ref, kw_ref, vw_ref,
                   o_ref, qaug_scr, s_a, s_b, acc_scr, m_scr, shift_scr):
    b = pl.program_id(1)
    qb = q_ref.shape[0]
    dh = NSA_HEAD_DIM
    rep = NSA_REP
    rows = rep * qb
    n_cmp = kc_ref.shape[2]
    n_sel = agg_ref.shape[1]
    top_n = min(SEL_TOPN, n_sel)
    t0 = pl.multiple_of(b * qb, qb)

    q_blk = q_ref[...]
    qs = jnp.concatenate([q_blk[:, r * dh:(r + 1) * dh] for r in range(rep)], axis=0)
    tpos = t0 + lax.broadcasted_iota(jnp.int32, (rows, 1), 0) % qb
    all_rows = jnp.ones((rows, 1), jnp.bool_)

    kc = kc_ref[0, 0]
    s_c = lax.dot_general(qs, kc, NT_DIMS, preferred_element_type=F32)
    cmp_end = lax.broadcasted_iota(jnp.int32, (1, n_cmp), 1) * CMP_STRIDE + (CMP_LEN - 1)
    has_c = tpos >= CMP_LEN - 1
    if fast:
        p_c, den_c = _shifted_softmax(s_c, cmp_end <= tpos, s_c[:, 0:1])
        redo = _any_out_of_range(den_c, has_c)
        inv_c = 1.0 / jnp.maximum(den_c, 1e-30)
    else:
        p_c, inv_c = _softmax2_rows(s_c, cmp_end <= tpos)
    p_c = p_c * jnp.where(has_c, inv_c, 0.0)
    o_c = jnp.dot(p_c.astype(BF16), vc_ref[0, 0], preferred_element_type=F32)

    psum = p_c[0:qb]
    for r in range(1, rep):
        psum = psum + p_c[r * qb:(r + 1) * qb]
    p_hi = psum.astype(BF16)
    p_lo = (psum - p_hi.astype(F32)).astype(BF16)
    agg = agg_ref[...]
    imp = (jnp.dot(p_hi, agg, preferred_element_type=F32)
           + jnp.dot(p_lo, agg, preferred_element_type=F32))
    imp_t = imp.T
    tq = t0 + lax.broadcasted_iota(jnp.int32, (1, qb), 1)
    blk = lax.broadcasted_iota(jnp.int32, (n_sel, qb), 0)
    blk_f = blk.astype(F32)
    cur = tq // SEL_LEN
    valid = blk * SEL_LEN <= tq
    if fast:
        score = jnp.where(valid, imp_t, -1.0 - blk_f)
        score = jnp.where(blk == cur - 1, 1e9, score)
        score = jnp.where(blk == cur, 2e9, score)
        score = jnp.where(blk == 0, 3e9, score)
    else:
        forced = (blk == 0) | (blk == cur) | (blk == cur - 1)
        score = jnp.where(forced, 1e9, jnp.where(valid, imp_t, -1e9))
    for _ in range(top_n):
        mx = jnp.max(score, axis=0, keepdims=True)
        hit = score == mx
        if not fast:
            idx = jnp.min(jnp.where(hit, blk_f, float(n_sel)), axis=0, keepdims=True)
            hit = blk_f == idx
        score = jnp.where(hit, -jnp.inf, score)
    took = score == -jnp.inf
    if fast:
        taken = jnp.sum(jnp.where(took, 1.0, 0.0), axis=0, keepdims=True)
        redo = redo | (jnp.max(jnp.where(taken != float(top_n), 1.0, 0.0)) > 0.0)
    pen = jnp.where(took, 0.0, MASK_PENALTY).T.astype(BF16)
    n_half = qaug_scr.shape[0]
    for hh in range(n_half):
        if n_sel >= ONEHOT_PERIOD:
            ph = pen[:, hh * ONEHOT_PERIOD:(hh + 1) * ONEHOT_PERIOD]
        else:
            ph = jnp.concatenate(
                [pen, jnp.full((qb, ONEHOT_PERIOD - n_sel), MASK_PENALTY, BF16)], axis=1)
        qaug_scr[hh] = jnp.concatenate([qs, jnp.concatenate([ph] * rep, axis=0)], axis=1)

    kt = SEL_KV_TILE
    period_keys = ONEHOT_PERIOD * SEL_LEN

    def sel_scores(i, s_out):
        k0 = pl.multiple_of(i * kt, kt)
        s_out[...] = lax.dot_general(qaug_scr[k0 // period_keys], ksa_ref[pl.ds(k0, kt), :], NT_DIMS,
                                     preferred_element_type=F32)

    def causal_mask(i, s):
        kpos = i * kt + lax.broadcasted_iota(jnp.int32, (1, kt), 1)
        return jnp.where(kpos <= tpos, s, MASK_PENALTY)

    def update_running_max(i, s_in, causal):
        k0 = pl.multiple_of(i * kt, kt)
        s = causal_mask(i, s_in[...]) if causal else s_in[...]
        m_run = m_scr[...]
        m_new = jnp.maximum(m_run, jnp.max(s, axis=-1, keepdims=True))
        p = jnp.exp2(s - m_new)
        acc_scr[...] = (jnp.exp2(m_run - m_new) * acc_scr[...]
                        + jnp.dot(p.astype(BF16), vsa_ref[pl.ds(k0, kt), :], preferred_element_type=F32))
        m_scr[...] = m_new

    def update_fixed_shift(i, s_in, causal):
        k0 = pl.multiple_of(i * kt, kt)
        s = causal_mask(i, s_in[...]) if causal else s_in[...]
        shift = shift_scr[...]
        p = jnp.concatenate([jnp.exp2(s[:, c * dh:(c + 1) * dh] - shift).astype(BF16)
                             for c in range(kt // dh)], axis=1)
        acc_scr[...] += jnp.dot(p, vsa_ref[pl.ds(k0, kt), :], preferred_element_type=F32)

    def sweep(update):
        def pair(j, carry):
            sel_scores(2 * j + 1, s_b)
            update(2 * j, s_a, False)
            sel_scores(2 * j + 2, s_a)
            update(2 * j + 1, s_b, False)
            return carry

        acc_scr[...] = jnp.zeros_like(acc_scr)
        lax.fori_loop(0, n_full // 2, pair, 0)
        odd = n_full % 2 == 1

        @pl.when(odd)
        def _():
            sel_scores(n_full, s_b)
            update(n_full - 1, s_a, False)
            update(n_full, s_b, True)

        @pl.when(jnp.logical_not(odd))
        def _():
            update(n_full, s_a, True)

    n_full = t0 // kt
    sel_scores(0, s_a)
    if fast:
        first = jnp.max(causal_mask(0, s_a[...]), axis=-1, keepdims=True)
        shift_scr[...] = jnp.broadcast_to(first, shift_scr.shape)
        sweep(update_fixed_shift)
    else:
        m_scr[...] = jnp.full(m_scr.shape, -1e30, F32)
        sweep(update_running_max)
    acc_s = acc_scr[...]
    den_s = acc_s[:, dh:dh + 1]
    o_s = acc_s[:, 0:dh] * (1.0 / den_s)

    wlen = WIN + qb
    w0 = pl.multiple_of(jnp.maximum(t0 - WIN, 0), qb)
    s_w = lax.dot_general(qs, kw_ref[pl.ds(w0, wlen), :], NT_DIMS, preferred_element_type=F32)
    diff = tpos - (w0 + lax.broadcasted_iota(jnp.int32, (1, wlen), 1))
    mask_w = (diff >= 0) & (diff < WIN)
    if fast:
        k_diag = jnp.concatenate([kw_ref[pl.ds(t0, qb), :]] * rep, axis=0).astype(F32)
        diag = jnp.sum(qs.astype(F32) * k_diag, axis=-1, keepdims=True)
        p_w, den_w = _shifted_softmax(s_w, mask_w, diag)
        o_w = jnp.dot(p_w.astype(BF16), vw_ref[pl.ds(w0, wlen), :], preferred_element_type=F32)
        o_w = o_w * (1.0 / den_w)
        redo = redo | _any_out_of_range(den_s, all_rows) | _any_out_of_range(den_w, all_rows)
    else:
        p_w, inv_w = _softmax2_rows(s_w, mask_w)
        o_w = jnp.dot((p_w * inv_w).astype(BF16), vw_ref[pl.ds(w0, wlen), :], preferred_element_type=F32)

    gt = gt_ref[...]
    for r in range(rep):
        rs = slice(r * qb, (r + 1) * qb)
        o = (gt[:, r:r + 1] * o_c[rs]
             + gt[:, rep + r:rep + r + 1] * o_s[rs]
             + gt[:, 2 * rep + r:2 * rep + r + 1] * o_w[rs])
        o_ref[:, r * dh:(r + 1) * dh] = o.astype(o_ref.dtype)
    return redo if fast else None


def _nsa_agg(n_cmp_pad, n_cmp, n_sel):
    ratio, span = SEL_LEN // CMP_STRIDE, CMP_LEN // CMP_STRIDE
    agg = np.zeros((n_cmp_pad, n_sel), np.float32)
    jj = np.arange(n_sel)
    for m in range(ratio):
        for n in range(span):
            ii = ratio * jj + m - n
            ok = (ii >= 0) & (ii < n_cmp)
            agg[ii[ok], jj[ok]] += 1.0
    return jnp.asarray(agg, dtype=BF16)


def _nsa_attention(qkv, gates, kv_cmp):
    l_all = qkv.shape[0]
    dh = NSA_HEAD_DIM
    g = NSA_KV_GROUPS
    qb = Q_BLOCK
    qw = NSA_REP * dh
    n_cmp_pad = l_all // CMP_STRIDE
    n_cmp = (l_all - CMP_LEN) // CMP_STRIDE + 1
    n_sel = l_all // SEL_LEN
    agg = _nsa_agg(n_cmp_pad, n_cmp, n_sel)
    n_half = max(n_sel // ONEHOT_PERIOD, 1)
    q_blocks = (NSA_HEADS * dh) // dh
    key = np.arange(l_all)
    onehot = jnp.asarray((key[:, None] // SEL_LEN) % ONEHOT_PERIOD == np.arange(ONEHOT_PERIOD)[None, :],
                         dtype=BF16)
    ones_col = jnp.asarray(np.arange(dh)[None, :] == 0, dtype=BF16)
    grouped = lambda base: qkv[:, (q_blocks + base) * dh:(q_blocks + base + g) * dh].reshape(l_all, g, dh)
    ks_aug = jnp.concatenate([grouped(2 * g), jnp.broadcast_to(onehot[:, None, :], (l_all, g, ONEHOT_PERIOD))],
                             axis=-1).reshape(l_all, g * (dh + ONEHOT_PERIOD))
    vs_aug = jnp.concatenate([grouped(3 * g), jnp.broadcast_to(ones_col[:, None, :], (l_all, g, dh))],
                             axis=-1).reshape(l_all, g * 2 * dh)
    once = dict(pipeline_mode=pl.Buffered(1))
    kv_spec = lambda base: pl.BlockSpec((l_all, dh), lambda gg, b: (0, q_blocks + base + gg), **once)
    aug_spec = pl.BlockSpec((l_all, 2 * dh), lambda gg, b: (0, gg), **once)
    rows = NSA_REP * qb
    return pl.pallas_call(
        _nsa_attn_kernel,
        grid=(g, l_all // qb),
        in_specs=[pl.BlockSpec((qb, qw), lambda gg, b: (b, gg)),
                  pl.BlockSpec((qb, dh), lambda gg, b: (b, gg)),
                  pl.BlockSpec((1, 1, n_cmp_pad, dh), lambda gg, b: (0, gg, 0, 0)),
                  pl.BlockSpec((1, 1, n_cmp_pad, dh), lambda gg, b: (1, gg, 0, 0)),
                  pl.BlockSpec((n_cmp_pad, n_sel), lambda gg, b: (0, 0)),
                  aug_spec, aug_spec, kv_spec(4 * g), kv_spec(5 * g)],
        out_specs=pl.BlockSpec((qb, qw), lambda gg, b: (b, gg)),
        out_shape=jax.ShapeDtypeStruct((l_all, NSA_HEADS * dh), BF16),
        scratch_shapes=[pltpu.VMEM((n_half, rows, dh + ONEHOT_PERIOD), BF16),
                        pltpu.VMEM((rows, SEL_KV_TILE), F32), pltpu.VMEM((rows, SEL_KV_TILE), F32),
                        pltpu.VMEM((rows, 2 * dh), F32), pltpu.VMEM((rows, 1), F32),
                        pltpu.VMEM((rows, dh), F32)],
        compiler_params=_cparams(2),
    )(qkv, gates, kv_cmp, kv_cmp, agg, ks_aug, vs_aug, qkv, qkv)


def _nsa_layer(x, g, sh, sc, gate, w_in, k_pos, k_w1, k_w2, v_pos, v_w1, v_w2, w_out):
    dh = NSA_HEAD_DIM
    qd = NSA_HEADS * dh
    kvd = 2 * N_BRANCH * NSA_KV_GROUPS * dh
    w_main = w_in[:, :qd + kvd].astype(BF16)
    colscale = jnp.concatenate([jnp.full((1, qd), dh ** -0.5 * math.log2(math.e), F32),
                                jnp.ones((1, kvd), F32)], axis=1)
    qkv = _norm_mod_matmul(x, g, sh, sc, w_main, colscale, BF16)
    wg = w_in[:, qd + kvd:].reshape(-1, NSA_KV_GROUPS, NSA_REP, N_BRANCH).transpose(0, 1, 3, 2)
    wg = wg.reshape(-1, NSA_KV_GROUPS, N_BRANCH * NSA_REP)
    wg = jnp.pad(wg, ((0, 0), (0, 0), (0, dh - N_BRANCH * NSA_REP))).reshape(-1, NSA_KV_GROUPS * dh)
    gates = _norm_mod_matmul(x, g, sh, sc, wg.astype(BF16), jnp.ones((1, wg.shape[1]), F32), F32,
                             act="sigmoid")
    kv_cmp = _compress(qkv, qd // dh, jnp.stack([k_pos, v_pos]),
                       jnp.stack([k_w1, v_w1]).astype(BF16), jnp.stack([k_w2, v_w2]).astype(BF16))
    o = _nsa_attention(qkv, gates, kv_cmp)
    return _matmul_residual(o, w_out.astype(BF16), x, gate, glu=False)


def _peer_query_kernel(x_ref, g_ref, sh_ref, sc_ref, wq_ref, keys_ref, ht_ref, st_ref):
    h32 = _norm_mod(x_ref[...], g_ref[...], sh_ref[...], sc_ref[...])
    ht_ref[...] = h32.T.astype(BF16)
    q = jnp.dot(h32.astype(BF16), wq_ref[...], preferred_element_type=F32).astype(BF16)
    for hc in range(keys_ref.shape[0]):
        rows = slice(hc * PEER_NKEYS, (hc + 1) * PEER_NKEYS)
        st_ref[rows, :] = lax.dot_general(keys_ref[hc], q[:, hc * PEER_HALF:(hc + 1) * PEER_HALF],
                                          NT_DIMS, preferred_element_type=F32)


def _peer_query(x, g, sh, sc, w_q, sub_keys, tm=512):
    t, d = x.shape
    nq = w_q.shape[1]
    keys = sub_keys.reshape(-1, PEER_NKEYS, PEER_HALF).astype(BF16)
    row = lambda i: (0, 0)
    return pl.pallas_call(
        _peer_query_kernel,
        grid=(t // tm,),
        in_specs=[pl.BlockSpec((tm, d), lambda i: (i, 0)),
                  pl.BlockSpec((1, d), row), pl.BlockSpec((1, d), row), pl.BlockSpec((1, d), row),
                  pl.BlockSpec((d, nq), row),
                  pl.BlockSpec(keys.shape, lambda i: (0, 0, 0))],
        out_specs=[pl.BlockSpec((d, tm), lambda i: (0, i)),
                   pl.BlockSpec((keys.shape[0] * PEER_NKEYS, tm), lambda i: (0, i))],
        out_shape=[jax.ShapeDtypeStruct((d, t), BF16),
                   jax.ShapeDtypeStruct((keys.shape[0] * PEER_NKEYS, t), F32)],
        compiler_params=_cparams(1),
    )(x, g, sh, sc, w_q.astype(BF16), keys)


def _peer_cells():
    return [(a, b) for a in range(PEER_TOPK) for b in range(PEER_TOPK) if (a + 1) * (b + 1) <= PEER_TOPK]


def _take_max(cur, iota, exact):
    v = jnp.max(cur, axis=0, keepdims=True)
    hit = cur == v
    if exact:
        idx = jnp.min(jnp.where(hit, iota, cur.shape[0]), axis=0, keepdims=True)
        hit = iota == idx
    return v, hit


def _top_ranks(s, iota_k, exact):
    vals = []
    cur = s
    if exact:
        rank = jnp.full(s.shape, RANK_NONE, F32)
        for a in range(PEER_TOPK):
            v, hit = _take_max(cur, iota_k, exact)
            rank = jnp.where(hit, float(a), rank)
            cur = jnp.where(hit, -jnp.inf, cur)
            vals.append(v)
        return vals, rank
    for a in range(PEER_TOPK):
        v, hit = _take_max(cur, iota_k, exact)
        cur = jnp.where(hit, -RANK_CODE * (1.0 + a / RANK_CODE_STEPS), cur)
        vals.append(v)
    rank = jnp.where(cur <= -RANK_CODE, (cur * (-1.0 / RANK_CODE) - 1.0) * RANK_CODE_STEPS, RANK_NONE)
    return vals, rank


def _peer_route_body(st_ref, seg_ref, r2_ref, ln_ref, g1_ref, g2_ref, exact):
    tn = st_ref.shape[1]
    nk = PEER_NKEYS
    cells = _peer_cells()
    n_cell = len(cells)
    n_pad = -(-n_cell // 8) * 8
    n_seg = seg_ref.shape[1]
    iota_k = lax.broadcasted_iota(jnp.int32, (nk, tn), 0)
    iota_c = lax.broadcasted_iota(jnp.int32, (n_pad, tn), 0)
    count = lambda m: jnp.sum(m.astype(F32), axis=0, keepdims=True)
    tied = jnp.zeros((1, tn), jnp.bool_)
    for h in range(PEER_HEADS):
        s1 = st_ref[(2 * h) * nk:(2 * h + 1) * nk, :]
        s2 = st_ref[(2 * h + 1) * nk:(2 * h + 2) * nk, :]
        v1, rank1 = _top_ranks(s1, iota_k, exact)
        v2, rank2 = _top_ranks(s2, iota_k, exact)
        cand = jnp.concatenate([v1[a] + v2[b] for a, b in cells]
                               + [jnp.full((n_pad - n_cell, tn), -jnp.inf, F32)], axis=0)
        top = v1[0] + v2[0]
        e_c = jnp.exp(cand - top)
        chosen = jnp.zeros((n_pad, tn), jnp.bool_)
        cur = cand
        for _ in range(PEER_TOPK):
            _, hit = _take_max(cur, iota_c, exact)
            chosen = chosen | hit
            cur = jnp.where(hit, -jnp.inf, cur)
        chosen_f = chosen.astype(F32)
        if not exact:
            k = float(PEER_TOPK)
            tied = (tied | (count(rank1 != RANK_NONE) != k) | (count(rank2 != RANK_NONE) != k)
                    | (jnp.sum(chosen_f, axis=0, keepdims=True) != k))
        z = jnp.sum(chosen_f * e_c, axis=0, keepdims=True)
        chosen_pad = jnp.concatenate([chosen_f, jnp.zeros((n_seg - n_pad, tn), F32)], axis=0)
        rowlen = jnp.dot(seg_ref[...], chosen_pad.astype(BF16), preferred_element_type=F32)
        ln = jnp.zeros((nk, tn), F32)
        for a in range(PEER_TOPK):
            ln = jnp.where(rank1 == float(a), rowlen[a:a + 1, :], ln)
        rows = slice(h * nk, (h + 1) * nk)
        r2_ref[rows, :] = rank2.astype(r2_ref.dtype)
        ln_ref[h] = ln
        g1_ref[h] = jnp.exp(s1 - v1[0]) / z
        g2_ref[rows, :] = jnp.exp(s2 - v2[0]).astype(g2_ref.dtype)
    return tied


def _peer_route_kernel(st_ref, seg_ref, r2_ref, ln_ref, g1_ref, g2_ref):
    refs = (st_ref, seg_ref, r2_ref, ln_ref, g1_ref, g2_ref)
    tied = _peer_route_body(*refs, exact=False)

    @pl.when(jnp.max(tied.astype(F32)) > 0.0)
    def _():
        _peer_route_body(*refs, exact=True)


def _peer_route(st, tn=128):
    n_rows, t = st.shape
    cells = _peer_cells()
    seg = np.zeros((PEER_TOPK, PEER_NKEYS), np.float32)
    for c, (a, _) in enumerate(cells):
        seg[a, c] = 1.0
    out_rows = PEER_HEADS * PEER_NKEYS
    shp = jax.ShapeDtypeStruct((out_rows, t), BF16)
    shp_row = jax.ShapeDtypeStruct((PEER_HEADS, PEER_NKEYS, t), F32)
    spec = pl.BlockSpec((out_rows, tn), lambda i: (0, i))
    spec_row = pl.BlockSpec((PEER_HEADS, PEER_NKEYS, tn), lambda i: (0, 0, i))
    return pl.pallas_call(
        _peer_route_kernel,
        grid=(t // tn,),
        in_specs=[pl.BlockSpec((n_rows, tn), lambda i: (0, i)),
                  pl.BlockSpec((PEER_TOPK, PEER_NKEYS), lambda i: (0, 0))],
        out_specs=[spec, spec_row, spec_row, spec],
        out_shape=[shp, shp_row, shp_row, shp],
        compiler_params=_cparams(1),
    )(st, jnp.asarray(seg, dtype=BF16))


def _peer_expert_kernel(h_ref, u_ref, vt_ref, r2_ref, ln_ref, g1_ref, g2_ref, x_ref, gate_ref, nf_ref,
                        o_ref, acc_t, p_scr, pre_a, pre_b, *, final_norm):
    c = pl.program_id(1)
    last = pl.num_programs(1) - 1
    ec = u_ref.shape[0]
    nk = PEER_NKEYS
    tm = h_ref.shape[1]
    zero = jnp.zeros((), BF16)

    def score(pre_out):
        pre_out[...] = jnp.dot(u_ref[...], h_ref[...], preferred_element_type=F32)

    def finish(pre_in):
        for ii in range(ec // nk):
            w = jnp.zeros((nk, tm), BF16)
            for h in range(PEER_HEADS):
                rows = slice(h * nk, (h + 1) * nk)
                ln_row = ln_ref[h, ii:ii + 1, :].astype(BF16)
                g1_row = g1_ref[h, ii:ii + 1, :].astype(BF16)
                w = w + jnp.where(r2_ref[rows, :] < ln_row, g2_ref[rows, :], zero) * g1_row
            act = jax.nn.gelu(pre_in[ii * nk:(ii + 1) * nk, :])
            p_scr[ii * nk:(ii + 1) * nk, :] = w * act.astype(BF16)
        acc_t[...] += jnp.dot(vt_ref[0], p_scr[...], preferred_element_type=F32)

    @pl.when(c == 0)
    def _():
        acc_t[...] = jnp.zeros_like(acc_t)
        score(pre_a)

    @pl.when((c > 0) & (c < last) & (c % 2 == 1))
    def _():
        score(pre_b)
        finish(pre_a)

    @pl.when((c > 0) & (c < last) & (c % 2 == 0))
    def _():
        score(pre_a)
        finish(pre_b)

    @pl.when(c == last)
    def _():
        finish(pre_b if PEER_LAST_IS_EVEN else pre_a)
        xo = x_ref[...] + gate_ref[...] * acc_t[...].T
        if final_norm:
            ms = jnp.mean(xo * xo, axis=-1, keepdims=True)
            xo = (xo * lax.rsqrt(ms + RMS_EPS)) * nf_ref[...]
        o_ref[...] = xo


def _peer_experts(ht, u_bf, vt_bf, route, x, gate, norm_final, final_norm, tm=512, ec=PEER_EXPERT_CHUNK):
    t, d = x.shape
    e = u_bf.shape[0]
    n_chunks = e // ec
    assert (n_chunks % 2 == 0) == PEER_LAST_IS_EVEN
    r2, ln, g1, g2 = route
    rspec = pl.BlockSpec((r2.shape[0], tm), lambda i, c: (0, i))
    prev = lambda c: jnp.maximum(c - 1, 0)
    kspec = pl.BlockSpec((PEER_HEADS, ec // PEER_NKEYS, tm), lambda i, c: (0, prev(c), i))
    row = lambda i, c: (0, 0)
    return pl.pallas_call(
        functools.partial(_peer_expert_kernel, final_norm=final_norm),
        grid=(t // tm, n_chunks + 1),
        in_specs=[pl.BlockSpec((d, tm), lambda i, c: (0, i)),
                  pl.BlockSpec((ec, d), lambda i, c: (jnp.minimum(c, n_chunks - 1), 0)),
                  pl.BlockSpec((1, d, ec), lambda i, c: (prev(c), 0, 0)),
                  rspec, kspec, kspec, rspec,
                  pl.BlockSpec((tm, d), lambda i, c: (i, 0)),
                  pl.BlockSpec((1, d), row), pl.BlockSpec((1, d), row)],
        out_specs=pl.BlockSpec((tm, d), lambda i, c: (i, 0)),
        out_shape=jax.ShapeDtypeStruct((t, d), F32),
        scratch_shapes=[pltpu.VMEM((d, tm), F32), pltpu.VMEM((ec, tm), BF16),
                        pltpu.VMEM((ec, tm), F32), pltpu.VMEM((ec, tm), F32)],
        compiler_params=_cparams(2),
    )(ht, u_bf, vt_bf, r2, ln, g1, g2, x, gate, norm_final)


def _peer_layer(x, g, sh, sc, gate, w_q, sub_keys, u_tab, v_tab, norm_final, final_norm):
    ht, st = _peer_query(x, g, sh, sc, w_q, sub_keys)
    route = _peer_route(st)
    e, d = v_tab.shape
    vt = v_tab.astype(BF16).reshape(e // PEER_EXPERT_CHUNK, PEER_EXPERT_CHUNK, d).transpose(0, 2, 1)
    return _peer_experts(ht, u_tab.astype(BF16), vt, route, x, gate, norm_final, final_norm)


def kernel(x, c, ada_w, ada_b, norm_mix, norm_ffn, norm_final, ssm_w_in, ssm_lambda_re, ssm_lambda_im, ssm_log_dt, ssm_b_re, ssm_b_im, ssm_c_re, ssm_c_im, ssm_d, ssm_w_out, nsa_w_in, nsa_cmp_k_pos, nsa_cmp_k_w1, nsa_cmp_k_w2, nsa_cmp_v_pos, nsa_cmp_v_w1, nsa_cmp_v_w2, nsa_w_out, peer_w_q, peer_sub_keys, peer_u, peer_v):
    bsz, l_all, d = x.shape
    assert bsz == 1
    depth = ada_w.shape[0]
    mod = _adaln(c, ada_w, ada_b)
    xt = x.reshape(l_all, d)
    nf = norm_final.reshape(1, d)
    for i in range(depth):
        sh1, sc1, g1, sh2, sc2, g2 = [mod[i, :, k * d:(k + 1) * d] for k in range(6)]
        j = i // 2
        gm = norm_mix[i].reshape(1, d)
        if i % 2 == 0:
            xt = _s5_layer(xt, gm, sh1, sc1, g1, ssm_w_in[j], ssm_lambda_re[j], ssm_lambda_im[j],
                           ssm_log_dt[j], ssm_b_re[j], ssm_b_im[j], ssm_c_re[j], ssm_c_im[j],
                           ssm_d[j], ssm_w_out[j])
        else:
            xt = _nsa_layer(xt, gm, sh1, sc1, g1, nsa_w_in[j], nsa_cmp_k_pos[j], nsa_cmp_k_w1[j],
                            nsa_cmp_k_w2[j], nsa_cmp_v_pos[j], nsa_cmp_v_w1[j], nsa_cmp_v_w2[j],
                            nsa_w_out[j])
        xt = _peer_layer(xt, norm_ffn[i].reshape(1, d), sh2, sc2, g2, peer_w_q[i], peer_sub_keys[i],
                         peer_u[i], peer_v[i], nf, final_norm=(i == depth - 1))
    return xt.reshape(bsz, l_all, d)
```

```python
import functools
import math

import numpy as np
import jax
import jax.numpy as jnp
from jax import lax
from jax.experimental import pallas as pl
from jax.experimental.pallas import tpu as pltpu

F32 = jnp.float32
BF16 = jnp.bfloat16

RMS_EPS = 1e-6

SSM_GROUP = 16
SSM_STATE = 64
SSM_BLOCK_GROUPS = 8
S5_SUBSEQ = 8
S5_CHUNK = 128
S5_LANE_BLOCK = 1024

NSA_HEADS = 16
NSA_HEAD_DIM = 128
NSA_KV_GROUPS = 4
NSA_REP = NSA_HEADS // NSA_KV_GROUPS
N_BRANCH = 3
CMP_LEN = 32
CMP_STRIDE = 16
SEL_LEN = 64
SEL_TOPN = 16
WIN = 512
Q_BLOCK = 256
SEL_KV_TILE = 512
MASK_PENALTY = -(2.0 ** 30)
SEL_DEN_LIMIT = 1e30
ONEHOT_PERIOD = 128

PEER_HEADS = 8
PEER_NKEYS = 128
PEER_TOPK = 16
PEER_HALF = 128
PEER_EXPERT_CHUNK = 1024
PEER_LAST_IS_EVEN = (PEER_NKEYS ** 2 // PEER_EXPERT_CHUNK) % 2 == 0
RANK_NONE = 255.0
RANK_CODE = 2.0 ** 100
RANK_CODE_STEPS = 32.0

VMEM_LIMIT = 56 * 1024 * 1024
NT_DIMS = (((1,), (1,)), ((), ()))


def _cparams(n_axes):
    return pltpu.CompilerParams(
        dimension_semantics=("arbitrary",) * n_axes, vmem_limit_bytes=VMEM_LIMIT)


def _norm_mod(x, g, sh, sc):
    ms = jnp.mean(x * x, axis=-1, keepdims=True)
    xn = x * lax.rsqrt(ms + RMS_EPS)
    return (xn * g) * (1.0 + sc) + sh


def _adaln_kernel(c_ref, w_ref, b_ref, o_ref):
    c = c_ref[...]
    cond = c * jax.nn.sigmoid(c)
    o_ref[0] = jnp.dot(cond.astype(BF16), w_ref[0].astype(BF16),
                       preferred_element_type=F32) + b_ref[0]


def _adaln(c, ada_w, ada_b):
    depth, d, n = ada_w.shape
    tn = 1024
    c8 = jnp.broadcast_to(c, (8, d))
    out = pl.pallas_call(
        _adaln_kernel,
        grid=(depth, n // tn),
        in_specs=[pl.BlockSpec((8, d), lambda i, j: (0, 0)),
                  pl.BlockSpec((1, d, tn), lambda i, j: (i, 0, j)),
                  pl.BlockSpec((1, 1, tn), lambda i, j: (i, 0, j))],
        out_specs=pl.BlockSpec((1, 8, tn), lambda i, j: (i, 0, j)),
        out_shape=jax.ShapeDtypeStruct((depth, 8, n), F32),
        compiler_params=_cparams(2),
    )(c8, ada_w, ada_b.reshape(depth, 1, n))
    return out[:, 0:1, :]


def _nmm_kernel(x_ref, g_ref, sh_ref, sc_ref, w_ref, cs_ref, o_ref, h_scr, *, act):
    @pl.when(pl.program_id(1) == 0)
    def _():
        h_scr[...] = _norm_mod(x_ref[...], g_ref[...], sh_ref[...], sc_ref[...]).astype(BF16)

    acc = jnp.dot(h_scr[...], w_ref[...], preferred_element_type=F32)
    if act == "sigmoid":
        acc = jax.nn.sigmoid(acc)
    else:
        acc = acc * cs_ref[...]
    o_ref[...] = acc.astype(o_ref.dtype)


def _norm_mod_matmul(x, g, sh, sc, w, colscale, out_dtype, act="scale", tm=1024, tn=1024):
    t, d = x.shape
    n = w.shape[1]
    tm, tn = min(tm, t), min(tn, n)
    row = lambda i, j: (0, 0)
    return pl.pallas_call(
        functools.partial(_nmm_kernel, act=act),
        grid=(t // tm, n // tn),
        in_specs=[pl.BlockSpec((tm, d), lambda i, j: (i, 0)),
                  pl.BlockSpec((1, d), row), pl.BlockSpec((1, d), row), pl.BlockSpec((1, d), row),
                  pl.BlockSpec((d, tn), lambda i, j: (0, j)),
                  pl.BlockSpec((1, tn), lambda i, j: (0, j))],
        out_specs=pl.BlockSpec((tm, tn), lambda i, j: (i, j)),
        out_shape=jax.ShapeDtypeStruct((t, n), out_dtype),
        scratch_shapes=[pltpu.VMEM((tm, d), BF16)],
        compiler_params=_cparams(2),
    )(x, g, sh, sc, w, colscale)


def _mm_res_kernel(a_ref, w_ref, x_ref, g_ref, o_ref):
    y = jnp.dot(a_ref[...], w_ref[...], preferred_element_type=F32)
    o_ref[...] = x_ref[...] + g_ref[...] * y


def _mm_glu_res_kernel(a_ref, wa_ref, wb_ref, x_ref, g_ref, o_ref):
    a = a_ref[...]
    ya = jnp.dot(a, wa_ref[...], preferred_element_type=F32)
    yb = jnp.dot(a, wb_ref[...], preferred_element_type=F32)
    o_ref[...] = x_ref[...] + g_ref[...] * (ya * jax.nn.sigmoid(yb))


def _matmul_residual(a, w, x, gate, glu, tm=1024, tn=1024):
    t, k = a.shape
    n = x.shape[1]
    tm = min(tm, t)
    nb = n // tn
    a_spec = pl.BlockSpec((tm, k), lambda i, j: (i, 0))
    w_spec = pl.BlockSpec((k, tn), lambda i, j: (0, j))
    tail = [pl.BlockSpec((tm, tn), lambda i, j: (i, j)), pl.BlockSpec((1, tn), lambda i, j: (0, j))]
    if glu:
        kern = _mm_glu_res_kernel
        in_specs = [a_spec, w_spec, pl.BlockSpec((k, tn), lambda i, j: (0, j + nb))] + tail
        args = (a, w, w, x, gate)
    else:
        kern = _mm_res_kernel
        in_specs = [a_spec, w_spec] + tail
        args = (a, w, x, gate)
    return pl.pallas_call(
        kern,
        grid=(t // tm, nb),
        in_specs=in_specs,
        out_specs=pl.BlockSpec((tm, tn), lambda i, j: (i, j)),
        out_shape=jax.ShapeDtypeStruct((t, n), F32),
        compiler_params=_cparams(2),
    )(*args)


def _s5_param_kernel(lre_ref, lim_ref, ldt_ref, bre_ref, bim_ref, ar_ref, ai_ref, bbr_ref, bbi_ref):
    lre = lre_ref[...]
    lim = lim_ref[...]
    dt = jnp.exp(ldt_ref[...])
    mag = jnp.exp(lre * dt)
    ar = mag * jnp.cos(lim * dt)
    ai = mag * jnp.sin(lim * dt)
    den = lre * lre + lim * lim
    cr = ((ar - 1.0) * lre + ai * lim) / den
    ci = (ai * lre - (ar - 1.0) * lim) / den
    ar_ref[...] = ar
    ai_ref[...] = ai
    bbr_ref[...] = cr * bre_ref[...] - ci * bim_ref[...]
    bbi_ref[...] = cr * bim_ref[...] + ci * bre_ref[...]


def _s5_params(lam_re, lam_im, log_dt, b_re, b_im):
    g, n, p = b_re.shape
    rep = lambda a: jnp.repeat(a, p, axis=1)
    shp = jax.ShapeDtypeStruct((g, n * p), F32)
    ar, ai, bbr, bbi = pl.pallas_call(
        _s5_param_kernel, out_shape=(shp, shp, shp, shp),
    )(rep(lam_re), rep(lam_im), log_dt.reshape(g, 1), b_re.reshape(g, n * p), b_im.reshape(g, n * p))
    return ar[:, ::p], ai[:, ::p], bbr.reshape(g, n, p), bbi.reshape(g, n, p)


def _s5_scan_kernel(u_ref, a_ref, bbr_ref, bbi_ref, cr_ref, cin_ref, d_ref, o_ref,
                    bur, bui, hin_r, hin_i, car_r, car_i, apr, api):
    t, width = u_ref.shape
    sub = t // S5_SUBSEQ
    lanes = bur.shape[1]
    nblk = bbr_ref.shape[0]
    kin = width // nblk
    kst = lanes // nblk

    @pl.when(pl.program_id(0) == 0)
    def _init():
        car_r[...] = jnp.zeros_like(car_r)
        car_i[...] = jnp.zeros_like(car_i)
        ar = a_ref[0:1, :]
        ai = a_ref[1:2, :]
        pr, pi = ar, ai
        apr[0:1, :] = pr
        api[0:1, :] = pi
        for i in range(1, sub):
            pr, pi = pr * ar - pi * ai, pr * ai + pi * ar
            apr[i:i + 1, :] = pr
            api[i:i + 1, :] = pi

    for k in range(nblk):
        uk = u_ref[:, kin * k:kin * (k + 1)].astype(BF16)
        bur[:, kst * k:kst * (k + 1)] = jnp.dot(uk, bbr_ref[k], preferred_element_type=F32)
        bui[:, kst * k:kst * (k + 1)] = jnp.dot(uk, bbi_ref[k], preferred_element_type=F32)

    lb = S5_LANE_BLOCK
    for b in range(lanes // lb):
        sl = slice(b * lb, (b + 1) * lb)
        ar = jnp.broadcast_to(a_ref[0:1, sl], (S5_SUBSEQ, lb))
        ai = jnp.broadcast_to(a_ref[1:2, sl], (S5_SUBSEQ, lb))
        sr = jnp.zeros((S5_SUBSEQ, lb), F32)
        si = jnp.zeros((S5_SUBSEQ, lb), F32)
        for i in range(sub):
            rows = slice(S5_SUBSEQ * i, S5_SUBSEQ * (i + 1))
            sr, si = (ar * sr - ai * si + bur[rows, sl], ar * si + ai * sr + bui[rows, sl])
            bur[rows, sl] = sr
            bui[rows, sl] = si
        asr = apr[sub - 1:sub, sl]
        asi = api[sub - 1:sub, sl]
        hr = car_r[0:1, sl]
        hi = car_i[0:1, sl]
        for j in range(S5_SUBSEQ):
            hin_r[j:j + 1, sl] = hr
            hin_i[j:j + 1, sl] = hi
            er = sr[j:j + 1, :]
            ei = si[j:j + 1, :]
            hr, hi = er + asr * hr - asi * hi, ei + asr * hi + asi * hr
        car_r[0:1, sl] = hr
        car_i[0:1, sl] = hi
        hinr = hin_r[:, sl]
        hini = hin_i[:, sl]
        for i in range(sub):
            rows = slice(S5_SUBSEQ * i, S5_SUBSEQ * (i + 1))
            pr = jnp.broadcast_to(apr[i:i + 1, sl], (S5_SUBSEQ, lb))
            pi = jnp.broadcast_to(api[i:i + 1, sl], (S5_SUBSEQ, lb))
            bur[rows, sl] = bur[rows, sl] + pr * hinr - pi * hini
            bui[rows, sl] = bui[rows, sl] + pr * hini + pi * hinr

    for k in range(nblk):
        sr = bur[:, kst * k:kst * (k + 1)].astype(BF16)
        si = bui[:, kst * k:kst * (k + 1)].astype(BF16)
        y = (jnp.dot(sr, cr_ref[k], preferred_element_type=F32)
             + jnp.dot(si, cin_ref[k], preferred_element_type=F32))
        cols = slice(kin * k, kin * (k + 1))
        y = y + d_ref[:, cols] * u_ref[:, cols]
        o_ref[:, cols] = jax.nn.gelu(y).astype(o_ref.dtype)


def _s5_scan(u_p, ar, ai, bb_r, bb_i, c_re, c_im, d_skip):
    t_all, width = u_p.shape
    g, n, p = bb_r.shape
    bg = SSM_BLOCK_GROUPS
    nblk = g // bg
    eye = jnp.eye(bg, dtype=F32)
    blk_b = lambda bb: jnp.einsum('kgnp,gh->kgphn', bb.reshape(nblk, bg, n, p), eye
                                  ).reshape(nblk, bg * p, bg * n).astype(BF16)
    blk_c = lambda cc: jnp.einsum('kgpn,gh->kgnhp', cc.reshape(nblk, bg, p, n), eye
                                  ).reshape(nblk, bg * n, bg * p).astype(BF16)
    lanes = g * n
    a8 = jnp.zeros((8, lanes), F32).at[0].set(ar.reshape(lanes)).at[1].set(ai.reshape(lanes))
    t = S5_CHUNK
    sub = t // S5_SUBSEQ
    full3 = lambda c: (0, 0, 0)
    wspec_b = pl.BlockSpec((nblk, bg * p, bg * n), full3)
    wspec_c = pl.BlockSpec((nblk, bg * n, bg * p), full3)
    return pl.pallas_call(
        _s5_scan_kernel,
        grid=(t_all // t,),
        in_specs=[pl.BlockSpec((t, width), lambda c: (c, 0)),
                  pl.BlockSpec((8, lanes), lambda c: (0, 0)),
                  wspec_b, wspec_b, wspec_c, wspec_c,
                  pl.BlockSpec((1, width), lambda c: (0, 0))],
        out_specs=pl.BlockSpec((t, width), lambda c: (c, 0)),
        out_shape=jax.ShapeDtypeStruct((t_all, width), BF16),
        scratch_shapes=[pltpu.VMEM((t, lanes), F32), pltpu.VMEM((t, lanes), F32),
                        pltpu.VMEM((8, lanes), F32), pltpu.VMEM((8, lanes), F32),
                        pltpu.VMEM((8, lanes), F32), pltpu.VMEM((8, lanes), F32),
                        pltpu.VMEM((sub, lanes), F32), pltpu.VMEM((sub, lanes), F32)],
        compiler_params=_cparams(1),
    )(u_p, a8, blk_b(bb_r), blk_b(bb_i), blk_c(c_re), blk_c(-c_im), d_skip.reshape(1, width))


def _s5_rows_to_subseq(x, inverse=False):
    t_all, d = x.shape
    sub = S5_CHUNK // S5_SUBSEQ
    shape = (t_all // S5_CHUNK, sub, S5_SUBSEQ, d) if inverse else (t_all // S5_CHUNK, S5_SUBSEQ, sub, d)
    return x.reshape(shape).transpose(0, 2, 1, 3).reshape(t_all, d)


def _s5_layer(x, g, sh, sc, gate, w_in, lam_re, lam_im, log_dt, b_re, b_im, c_re, c_im, d_skip, w_out):
    d = x.shape[1]
    x_p = _s5_rows_to_subseq(x)
    ones = jnp.ones((1, w_in.shape[1]), F32)
    u_p = _norm_mod_matmul(x_p, g, sh, sc, w_in.astype(BF16), ones, F32)
    ar, ai, bb_r, bb_i = _s5_params(lam_re, lam_im, log_dt, b_re, b_im)
    gy_p = _s5_scan(u_p, ar, ai, bb_r, bb_i, c_re, c_im, d_skip)
    xn_p = _matmul_residual(gy_p, w_out.astype(BF16), x_p, gate, glu=True)
    return _s5_rows_to_subseq(xn_p, inverse=True)


def _cmp_kernel(x_ref, w1a_ref, w1b_ref, pos_ref, w1_ref, w2_ref, o_ref, acc_a, acc_b):
    l = pl.program_id(2)

    @pl.when(l == 0)
    def _():
        acc_a[...] = jnp.zeros_like(acc_a)
        acc_b[...] = jnp.zeros_like(acc_b)

    x = x_ref[...]
    acc_a[...] += jnp.dot(x, w1a_ref[0, 0], preferred_element_type=F32)
    acc_b[...] += jnp.dot(x, w1b_ref[0, 0], preferred_element_type=F32)

    @pl.when(l == pl.num_programs(2) - 1)
    def _():
        m = acc_a.shape[0]
        posc = jnp.zeros((8, NSA_HEAD_DIM), F32)
        for ll in range(CMP_LEN):
            prow = jnp.broadcast_to(pos_ref[0, ll:ll + 1, :], (8, NSA_HEAD_DIM)).astype(BF16)
            posc = posc + jnp.dot(prow, w1_ref[0, ll], preferred_element_type=F32)
        pre = acc_a[...] + pltpu.roll(acc_b[...], m - 1, 0) + posc[0:1, :]
        hid = jax.nn.gelu(pre)
        o_ref[0, 0] = jnp.dot(hid.astype(BF16), w2_ref[0], preferred_element_type=F32).astype(o_ref.dtype)


def _compress(qkv, kv_col0, pos, w1, w2):
    l_all, c = qkv.shape
    half = CMP_LEN // 2
    m = l_all // half
    cb = c // NSA_HEAD_DIM
    x2 = qkv.reshape(m, half * c)
    g = NSA_KV_GROUPS
    dh = NSA_HEAD_DIM
    return pl.pallas_call(
        _cmp_kernel,
        grid=(2, g, half),
        in_specs=[pl.BlockSpec((m, dh), lambda s, gg, l: (0, l * cb + kv_col0 + g * s + gg)),
                  pl.BlockSpec((1, 1, dh, dh), lambda s, gg, l: (s, l, 0, 0)),
                  pl.BlockSpec((1, 1, dh, dh), lambda s, gg, l: (s, l + half, 0, 0)),
                  pl.BlockSpec((1, CMP_LEN, dh), lambda s, gg, l: (s, 0, 0)),
                  pl.BlockSpec((1, CMP_LEN, dh, dh), lambda s, gg, l: (s, 0, 0, 0)),
                  pl.BlockSpec((1, dh, dh), lambda s, gg, l: (s, 0, 0))],
        out_specs=pl.BlockSpec((1, 1, m, dh), lambda s, gg, l: (s, gg, 0, 0)),
        out_shape=jax.ShapeDtypeStruct((2, g, m, dh), BF16),
        scratch_shapes=[pltpu.VMEM((m, dh), F32), pltpu.VMEM((m, dh), F32)],
        compiler_params=_cparams(3),
    )(x2, w1, w1, pos, w1, w2)


def _softmax2_rows(s, mask):
    s = jnp.where(mask, s, -1e30)
    m = jnp.max(s, axis=-1, keepdims=True)
    p = jnp.exp2(s - m)
    return p, 1.0 / jnp.maximum(jnp.sum(p, axis=-1, keepdims=True), 1e-30)


def _shifted_softmax(s, mask, shift):
    p = jnp.exp2(jnp.where(mask, s - shift, -1e30))
    return p, jnp.sum(p, axis=-1, keepdims=True)


def _any_out_of_range(den, needed):
    ok = (den > 0.0) & (den < SEL_DEN_LIMIT)
    return jnp.max(jnp.where(needed & jnp.logical_not(ok), 1.0, 0.0)) > 0.0


def _nsa_attn_kernel(*refs):
    redo = _nsa_attn_body(True, *refs)

    @pl.when(redo)
    def _():
        _nsa_attn_body(False, *refs)


def _nsa_attn_body(fast, q_ref, gt_ref, kc_ref, vc_ref, agg_ref, ksa_ref, vsa_ref, kw_ref, vw_ref,
                   o_ref, qaug_scr, s_a, s_b, acc_scr, m_scr, shift_scr):
    b = pl.program_id(1)
    qb = q_ref.shape[0]
    dh = NSA_HEAD_DIM
    rep = NSA_REP
    rows = rep * qb
    n_cmp = kc_ref.shape[2]
    n_sel = agg_ref.shape[1]
    top_n = min(SEL_TOPN, n_sel)
    t0 = pl.multiple_of(b * qb, qb)

    q_blk = q_ref[...]
    qs = jnp.concatenate([q_blk[:, r * dh:(r + 1) * dh] for r in range(rep)], axis=0)
    tpos = t0 + lax.broadcasted_iota(jnp.int32, (rows, 1), 0) % qb
    all_rows = jnp.ones((rows, 1), jnp.bool_)

    kc = kc_ref[0, 0]
    s_c = lax.dot_general(qs, kc, NT_DIMS, preferred_element_type=F32)
    cmp_end = lax.broadcasted_iota(jnp.int32, (1, n_cmp), 1) * CMP_STRIDE + (CMP_LEN - 1)
    has_c = tpos >= CMP_LEN - 1
    if fast:
        p_c, den_c = _shifted_softmax(s_c, cmp_end <= tpos, s_c[:, 0:1])
        redo = _any_out_of_range(den_c, has_c)
        inv_c = 1.0 / jnp.maximum(den_c, 1e-30)
    else:
        p_c, inv_c = _softmax2_rows(s_c, cmp_end <= tpos)
    p_c = p_c * jnp.where(has_c, inv_c, 0.0)
    o_c = jnp.dot(p_c.astype(BF16), vc_ref[0, 0], preferred_element_type=F32)

    psum = p_c[0:qb]
    for r in range(1, rep):
        psum = psum + p_c[r * qb:(r + 1) * qb]
    p_hi = psum.astype(BF16)
    p_lo = (psum - p_hi.astype(F32)).astype(BF16)
    agg = agg_ref[...]
    imp = (jnp.dot(p_hi, agg, preferred_element_type=F32)
           + jnp.dot(p_lo, agg, preferred_element_type=F32))
    imp_t = imp.T
    tq = t0 + lax.broadcasted_iota(jnp.int32, (1, qb), 1)
    blk = lax.broadcasted_iota(jnp.int32, (n_sel, qb), 0)
    blk_f = blk.astype(F32)
    cur = tq // SEL_LEN
    valid = blk * SEL_LEN <= tq
    if fast:
        score = jnp.where(valid, imp_t, -1.0 - blk_f)
        score = jnp.where(blk == cur - 1, 1e9, score)
        score = jnp.where(blk == cur, 2e9, score)
        score = jnp.where(blk == 0, 3e9, score)
    else:
        forced = (blk == 0) | (blk == cur) | (blk == cur - 1)
        score = jnp.where(forced, 1e9, jnp.where(valid, imp_t, -1e9))
    for _ in range(top_n):
        mx = jnp.max(score, axis=0, keepdims=True)
        hit = score == mx
        if not fast:
            idx = jnp.min(jnp.where(hit, blk_f, float(n_sel)), axis=0, keepdims=True)
            hit = blk_f == idx
        score = jnp.where(hit, -jnp.inf, score)
    took = score == -jnp.inf
    if fast:
        taken = jnp.sum(jnp.where(took, 1.0, 0.0), axis=0, keepdims=True)
        redo = redo | (jnp.max(jnp.where(taken != float(top_n), 1.0, 0.0)) > 0.0)
    pen = jnp.where(took, 0.0, MASK_PENALTY).T.astype(BF16)
    n_half = qaug_scr.shape[0]
    for hh in range(n_half):
        if n_sel >= ONEHOT_PERIOD:
            ph = pen[:, hh * ONEHOT_PERIOD:(hh + 1) * ONEHOT_PERIOD]
        else:
            ph = jnp.concatenate(
                [pen, jnp.full((qb, ONEHOT_PERIOD - n_sel), MASK_PENALTY, BF16)], axis=1)
        qaug_scr[hh] = jnp.concatenate([qs, jnp.concatenate([ph] * rep, axis=0)], axis=1)

    kt = SEL_KV_TILE
    period_keys = ONEHOT_PERIOD * SEL_LEN

    def sel_scores(i, s_out):
        k0 = pl.multiple_of(i * kt, kt)
        s_out[...] = lax.dot_general(qaug_scr[k0 // period_keys], ksa_ref[pl.ds(k0, kt), :], NT_DIMS,
                                     preferred_element_type=F32)

    def causal_mask(i, s):
        kpos = i * kt + lax.broadcasted_iota(jnp.int32, (1, kt), 1)
        return jnp.where(kpos <= tpos, s, MASK_PENALTY)

    def update_running_max(i, s_in, causal):
        k0 = pl.multiple_of(i * kt, kt)
        s = causal_mask(i, s_in[...]) if causal else s_in[...]
        m_run = m_scr[...]
        m_new = jnp.maximum(m_run, jnp.max(s, axis=-1, keepdims=True))
        p = jnp.exp2(s - m_new)
        acc_scr[...] = (jnp.exp2(m_run - m_new) * acc_scr[...]
                        + jnp.dot(p.astype(BF16), vsa_ref[pl.ds(k0, kt), :], preferred_element_type=F32))
        m_scr[...] = m_new

    def update_fixed_shift(i, s_in, causal):
        k0 = pl.multiple_of(i * kt, kt)
        s = causal_mask(i, s_in[...]) if causal else s_in[...]
        shift = shift_scr[...]
        p = jnp.concatenate([jnp.exp2(s[:, c * dh:(c + 1) * dh] - shift).astype(BF16)
                             for c in range(kt // dh)], axis=1)
        acc_scr[...] += jnp.dot(p, vsa_ref[pl.ds(k0, kt), :], preferred_element_type=F32)

    def sweep(update):
        def pair(j, carry):
            sel_scores(2 * j + 1, s_b)
            update(2 * j, s_a, False)
            sel_scores(2 * j + 2, s_a)
            update(2 * j + 1, s_b, False)
            return carry

        acc_scr[...] = jnp.zeros_like(acc_scr)
        lax.fori_loop(0, n_full // 2, pair, 0)
        odd = n_full % 2 == 1

        @pl.when(odd)
        def _():
            sel_scores(n_full, s_b)
            update(n_full - 1, s_a, False)
            update(n_full, s_b, True)

        @pl.when(jnp.logical_not(odd))
        def _():
            update(n_full, s_a, True)

    n_full = t0 // kt
    sel_scores(0, s_a)
    if fast:
        first = jnp.max(causal_mask(0, s_a[...]), axis=-1, keepdims=True)
        shift_scr[...] = jnp.broadcast_to(first, shift_scr.shape)
        sweep(update_fixed_shift)
    else:
        m_scr[...] = jnp.full(m_scr.shape, -1e30, F32)
        sweep(update_running_max)
    acc_s = acc_scr[...]
    den_s = acc_s[:, dh:dh + 1]
    o_s = acc_s[:, 0:dh] * (1.0 / den_s)

    wlen = WIN + qb
    w0 = pl.multiple_of(jnp.maximum(t0 - WIN, 0), qb)
    s_w = lax.dot_general(qs, kw_ref[pl.ds(w0, wlen), :], NT_DIMS, preferred_element_type=F32)
    diff = tpos - (w0 + lax.broadcasted_iota(jnp.int32, (1, wlen), 1))
    mask_w = (diff >= 0) & (diff < WIN)
    if fast:
        k_diag = jnp.concatenate([kw_ref[pl.ds(t0, qb), :]] * rep, axis=0).astype(F32)
        diag = jnp.sum(qs.astype(F32) * k_diag, axis=-1, keepdims=True)
        p_w, den_w = _shifted_softmax(s_w, mask_w, diag)
        o_w = jnp.dot(p_w.astype(BF16), vw_ref[pl.ds(w0, wlen), :], preferred_element_type=F32)
        o_w = o_w * (1.0 / den_w)
        redo = redo | _any_out_of_range(den_s, all_rows) | _any_out_of_range(den_w, all_rows)
    else:
        p_w, inv_w = _softmax2_rows(s_w, mask_w)
        o_w = jnp.dot((p_w * inv_w).astype(BF16), vw_ref[pl.ds(w0, wlen), :], preferred_element_type=F32)

    gt = gt_ref[...]
    for r in range(rep):
        rs = slice(r * qb, (r + 1) * qb)
        o = (gt[:, r:r + 1] * o_c[rs]
             + gt[:, rep + r:rep + r + 1] * o_s[rs]
             + gt[:, 2 * rep + r:2 * rep + r + 1] * o_w[rs])
        o_ref[:, r * dh:(r + 1) * dh] = o.astype(o_ref.dtype)
    return redo if fast else None


def _nsa_agg(n_cmp_pad, n_cmp, n_sel):
    ratio, span = SEL_LEN // CMP_STRIDE, CMP_LEN // CMP_STRIDE
    agg = np.zeros((n_cmp_pad, n_sel), np.float32)
    jj = np.arange(n_sel)
    for m in range(ratio):
        for n in range(span):
            ii = ratio * jj + m - n
            ok = (ii >= 0) & (ii < n_cmp)
            agg[ii[ok], jj[ok]] += 1.0
    return jnp.asarray(agg, dtype=BF16)


def _nsa_attention(qkv, gates, kv_cmp):
    l_all = qkv.shape[0]
    dh = NSA_HEAD_DIM
    g = NSA_KV_GROUPS
    qb = Q_BLOCK
    qw = NSA_REP * dh
    n_cmp_pad = l_all // CMP_STRIDE
    n_cmp = (l_all - CMP_LEN) // CMP_STRIDE + 1
    n_sel = l_all // SEL_LEN
    agg = _nsa_agg(n_cmp_pad, n_cmp, n_sel)
    n_half = max(n_sel // ONEHOT_PERIOD, 1)
    q_blocks = (NSA_HEADS * dh) // dh
    key = np.arange(l_all)
    onehot = jnp.asarray((key[:, None] // SEL_LEN) % ONEHOT_PERIOD == np.arange(ONEHOT_PERIOD)[None, :],
                         dtype=BF16)
    ones_col = jnp.asarray(np.arange(dh)[None, :] == 0, dtype=BF16)
    grouped = lambda base: qkv[:, (q_blocks + base) * dh:(q_blocks + base + g) * dh].reshape(l_all, g, dh)
    ks_aug = jnp.concatenate([grouped(2 * g), jnp.broadcast_to(onehot[:, None, :], (l_all, g, ONEHOT_PERIOD))],
                             axis=-1).reshape(l_all, g * (dh + ONEHOT_PERIOD))
    vs_aug = jnp.concatenate([grouped(3 * g), jnp.broadcast_to(ones_col[:, None, :], (l_all, g, dh))],
                             axis=-1).reshape(l_all, g * 2 * dh)
    once = dict(pipeline_mode=pl.Buffered(1))
    kv_spec = lambda base: pl.BlockSpec((l_all, dh), lambda gg, b: (0, q_blocks + base + gg), **once)
    aug_spec = pl.BlockSpec((l_all, 2 * dh), lambda gg, b: (0, gg), **once)
    rows = NSA_REP * qb
    return pl.pallas_call(
        _nsa_attn_kernel,
        grid=(g, l_all // qb),
        in_specs=[pl.BlockSpec((qb, qw), lambda gg, b: (b, gg)),
                  pl.BlockSpec((qb, dh), lambda gg, b: (b, gg)),
                  pl.BlockSpec((1, 1, n_cmp_pad, dh), lambda gg, b: (0, gg, 0, 0)),
                  pl.BlockSpec((1, 1, n_cmp_pad, dh), lambda gg, b: (1, gg, 0, 0)),
                  pl.BlockSpec((n_cmp_pad, n_sel), lambda gg, b: (0, 0)),
                  aug_spec, aug_spec, kv_spec(4 * g), kv_spec(5 * g)],
        out_specs=pl.BlockSpec((qb, qw), lambda gg, b: (b, gg)),
        out_shape=jax.ShapeDtypeStruct((l_all, NSA_HEADS * dh), BF16),
        scratch_shapes=[pltpu.VMEM((n_half, rows, dh + ONEHOT_PERIOD), BF16),
                        pltpu.VMEM((rows, SEL_KV_TILE), F32), pltpu.VMEM((rows, SEL_KV_TILE), F32),
                        pltpu.VMEM((rows, 2 * dh), F32), pltpu.VMEM((rows, 1), F32),
                        pltpu.VMEM((rows, dh), F32)],
        compiler_params=_cparams(2),
    )(qkv, gates, kv_cmp, kv_cmp, agg, ks_aug, vs_aug, qkv, qkv)


def _nsa_layer(x, g, sh, sc, gate, w_in, k_pos, k_w1, k_w2, v_pos, v_w1, v_w2, w_out):
    dh = NSA_HEAD_DIM
    qd = NSA_HEADS * dh
    kvd = 2 * N_BRANCH * NSA_KV_GROUPS * dh
    w_main = w_in[:, :qd + kvd].astype(BF16)
    colscale = jnp.concatenate([jnp.full((1, qd), dh ** -0.5 * math.log2(math.e), F32),
                                jnp.ones((1, kvd), F32)], axis=1)
    qkv = _norm_mod_matmul(x, g, sh, sc, w_main, colscale, BF16)
    wg = w_in[:, qd + kvd:].reshape(-1, NSA_KV_GROUPS, NSA_REP, N_BRANCH).transpose(0, 1, 3, 2)
    wg = wg.reshape(-1, NSA_KV_GROUPS, N_BRANCH * NSA_REP)
    wg = jnp.pad(wg, ((0, 0), (0, 0), (0, dh - N_BRANCH * NSA_REP))).reshape(-1, NSA_KV_GROUPS * dh)
    gates = _norm_mod_matmul(x, g, sh, sc, wg.astype(BF16), jnp.ones((1, wg.shape[1]), F32), F32,
                             act="sigmoid")
    kv_cmp = _compress(qkv, qd // dh, jnp.stack([k_pos, v_pos]),
                       jnp.stack([k_w1, v_w1]).astype(BF16), jnp.stack([k_w2, v_w2]).astype(BF16))
    o = _nsa_attention(qkv, gates, kv_cmp)
    return _matmul_residual(o, w_out.astype(BF16), x, gate, glu=False)


def _peer_query_kernel(x_ref, g_ref, sh_ref, sc_ref, wq_ref, keys_ref, ht_ref, st_ref):
    h32 = _norm_mod(x_ref[...], g_ref[...], sh_ref[...], sc_ref[...])
    ht_ref[...] = h32.T.astype(BF16)
    q = jnp.dot(h32.astype(BF16), wq_ref[...], preferred_element_type=F32).astype(BF16)
    for hc in range(keys_ref.shape[0]):
        rows = slice(hc * PEER_NKEYS, (hc + 1) * PEER_NKEYS)
        st_ref[rows, :] = lax.dot_general(keys_ref[hc], q[:, hc * PEER_HALF:(hc + 1) * PEER_HALF],
                                          NT_DIMS, preferred_element_type=F32)


def _peer_query(x, g, sh, sc, w_q, sub_keys, tm=512):
    t, d = x.shape
    nq = w_q.shape[1]
    keys = sub_keys.reshape(-1, PEER_NKEYS, PEER_HALF).astype(BF16)
    row = lambda i: (0, 0)
    return pl.pallas_call(
        _peer_query_kernel,
        grid=(t // tm,),
        in_specs=[pl.BlockSpec((tm, d), lambda i: (i, 0)),
                  pl.BlockSpec((1, d), row), pl.BlockSpec((1, d), row), pl.BlockSpec((1, d), row),
                  pl.BlockSpec((d, nq), row),
                  pl.BlockSpec(keys.shape, lambda i: (0, 0, 0))],
        out_specs=[pl.BlockSpec((d, tm), lambda i: (0, i)),
                   pl.BlockSpec((keys.shape[0] * PEER_NKEYS, tm), lambda i: (0, i))],
        out_shape=[jax.ShapeDtypeStruct((d, t), BF16),
                   jax.ShapeDtypeStruct((keys.shape[0] * PEER_NKEYS, t), F32)],
        compiler_params=_cparams(1),
    )(x, g, sh, sc, w_q.astype(BF16), keys)


def _peer_cells():
    return [(a, b) for a in range(PEER_TOPK) for b in range(PEER_TOPK) if (a + 1) * (b + 1) <= PEER_TOPK]


def _take_max(cur, iota, exact):
    v = jnp.max(cur, axis=0, keepdims=True)
    hit = cur == v
    if exact:
        idx = jnp.min(jnp.where(hit, iota, cur.shape[0]), axis=0, keepdims=True)
        hit = iota == idx
    return v, hit


def _top_ranks(s, iota_k, exact):
    vals = []
    cur = s
    if exact:
        rank = jnp.full(s.shape, RANK_NONE, F32)
        for a in range(PEER_TOPK):
            v, hit = _take_max(cur, iota_k, exact)
            rank = jnp.where(hit, float(a), rank)
            cur = jnp.where(hit, -jnp.inf, cur)
            vals.append(v)
        return vals, rank
    for a in range(PEER_TOPK):
        v, hit = _take_max(cur, iota_k, exact)
        cur = jnp.where(hit, -RANK_CODE * (1.0 + a / RANK_CODE_STEPS), cur)
        vals.append(v)
    rank = jnp.where(cur <= -RANK_CODE, (cur * (-1.0 / RANK_CODE) - 1.0) * RANK_CODE_STEPS, RANK_NONE)
    return vals, rank


def _peer_route_body(st_ref, seg_ref, r2_ref, ln_ref, g1_ref, g2_ref, exact):
    tn = st_ref.shape[1]
    nk = PEER_NKEYS
    cells = _peer_cells()
    n_cell = len(cells)
    n_pad = -(-n_cell // 8) * 8
    n_seg = seg_ref.shape[1]
    iota_k = lax.broadcasted_iota(jnp.int32, (nk, tn), 0)
    iota_c = lax.broadcasted_iota(jnp.int32, (n_pad, tn), 0)
    count = lambda m: jnp.sum(m.astype(F32), axis=0, keepdims=True)
    tied = jnp.zeros((1, tn), jnp.bool_)
    for h in range(PEER_HEADS):
        s1 = st_ref[(2 * h) * nk:(2 * h + 1) * nk, :]
        s2 = st_ref[(2 * h + 1) * nk:(2 * h + 2) * nk, :]
        v1, rank1 = _top_ranks(s1, iota_k, exact)
        v2, rank2 = _top_ranks(s2, iota_k, exact)
        cand = jnp.concatenate([v1[a] + v2[b] for a, b in cells]
                               + [jnp.full((n_pad - n_cell, tn), -jnp.inf, F32)], axis=0)
        top = v1[0] + v2[0]
        e_c = jnp.exp(cand - top)
        chosen = jnp.zeros((n_pad, tn), jnp.bool_)
        cur = cand
        for _ in range(PEER_TOPK):
            _, hit = _take_max(cur, iota_c, exact)
            chosen = chosen | hit
            cur = jnp.where(hit, -jnp.inf, cur)
        chosen_f = chosen.astype(F32)
        if not exact:
            k = float(PEER_TOPK)
            tied = (tied | (count(rank1 != RANK_NONE) != k) | (count(rank2 != RANK_NONE) != k)
                    | (jnp.sum(chosen_f, axis=0, keepdims=True) != k))
        z = jnp.sum(chosen_f * e_c, axis=0, keepdims=True)
        chosen_pad = jnp.concatenate([chosen_f, jnp.zeros((n_seg - n_pad, tn), F32)], axis=0)
        rowlen = jnp.dot(seg_ref[...], chosen_pad.astype(BF16), preferred_element_type=F32)
        ln = jnp.zeros((nk, tn), F32)
        for a in range(PEER_TOPK):
            ln = jnp.where(rank1 == float(a), rowlen[a:a + 1, :], ln)
        rows = slice(h * nk, (h + 1) * nk)
        r2_ref[rows, :] = rank2.astype(r2_ref.dtype)
        ln_ref[h] = ln
        g1_ref[h] = jnp.exp(s1 - v1[0]) / z
        g2_ref[rows, :] = jnp.exp(s2 - v2[0]).astype(g2_ref.dtype)
    return tied


def _peer_route_kernel(st_ref, seg_ref, r2_ref, ln_ref, g1_ref, g2_ref):
    refs = (st_ref, seg_ref, r2_ref, ln_ref, g1_ref, g2_ref)
    tied = _peer_route_body(*refs, exact=False)

    @pl.when(jnp.max(tied.astype(F32)) > 0.0)
    def _():
        _peer_route_body(*refs, exact=True)


def _peer_route(st, tn=128):
    n_rows, t = st.shape
    cells = _peer_cells()
    seg = np.zeros((PEER_TOPK, PEER_NKEYS), np.float32)
    for c, (a, _) in enumerate(cells):
        seg[a, c] = 1.0
    out_rows = PEER_HEADS * PEER_NKEYS
    shp = jax.ShapeDtypeStruct((out_rows, t), BF16)
    shp_row = jax.ShapeDtypeStruct((PEER_HEADS, PEER_NKEYS, t), F32)
    spec = pl.BlockSpec((out_rows, tn), lambda i: (0, i))
    spec_row = pl.BlockSpec((PEER_HEADS, PEER_NKEYS, tn), lambda i: (0, 0, i))
    return pl.pallas_call(
        _peer_route_kernel,
        grid=(t // tn,),
        in_specs=[pl.BlockSpec((n_rows, tn), lambda i: (0, i)),
                  pl.BlockSpec((PEER_TOPK, PEER_NKEYS), lambda i: (0, 0))],
        out_specs=[spec, spec_row, spec_row, spec],
        out_shape=[shp, shp_row, shp_row, shp],
        compiler_params=_cparams(1),
    )(st, jnp.asarray(seg, dtype=BF16))


def _peer_expert_kernel(h_ref, u_ref, vt_ref, r2_ref, ln_ref, g1_ref, g2_ref, x_ref, gate_ref, nf_ref,
                        o_ref, acc_t, p_scr, pre_a, pre_b, *, final_norm):
    c = pl.program_id(1)
    last = pl.num_programs(1) - 1
    ec = u_ref.shape[0]
    nk = PEER_NKEYS
    tm = h_ref.shape[1]
    zero = jnp.zeros((), BF16)

    def score(pre_out):
        pre_out[...] = jnp.dot(u_ref[...], h_ref[...], preferred_element_type=F32)

    def finish(pre_in):
        for ii in range(ec // nk):
            w = jnp.zeros((nk, tm), BF16)
            for h in range(PEER_HEADS):
                rows = slice(h * nk, (h + 1) * nk)
                ln_row = ln_ref[h, ii:ii + 1, :].astype(BF16)
                g1_row = g1_ref[h, ii:ii + 1, :].astype(BF16)
                w = w + jnp.where(r2_ref[rows, :] < ln_row, g2_ref[rows, :], zero) * g1_row
            act = jax.nn.gelu(pre_in[ii * nk:(ii + 1) * nk, :])
            p_scr[ii * nk:(ii + 1) * nk, :] = w * act.astype(BF16)
        acc_t[...] += jnp.dot(vt_ref[0], p_scr[...], preferred_element_type=F32)

    @pl.when(c == 0)
    def _():
        acc_t[...] = jnp.zeros_like(acc_t)
        score(pre_a)

    @pl.when((c > 0) & (c < last) & (c % 2 == 1))
    def _():
        score(pre_b)
        finish(pre_a)

    @pl.when((c > 0) & (c < last) & (c % 2 == 0))
    def _():
        score(pre_a)
        finish(pre_b)

    @pl.when(c == last)
    def _():
        finish(pre_b if PEER_LAST_IS_EVEN else pre_a)
        xo = x_ref[...] + gate_ref[...] * acc_t[...].T
        if final_norm:
            ms = jnp.mean(xo * xo, axis=-1, keepdims=True)
            xo = (xo * lax.rsqrt(ms + RMS_EPS)) * nf_ref[...]
        o_ref[...] = xo


def _peer_experts(ht, u_bf, vt_bf, route, x, gate, norm_final, final_norm, tm=512, ec=PEER_EXPERT_CHUNK):
    t, d = x.shape
    e = u_bf.shape[0]
    n_chunks = e // ec
    assert (n_chunks % 2 == 0) == PEER_LAST_IS_EVEN
    r2, ln, g1, g2 = route
    rspec = pl.BlockSpec((r2.shape[0], tm), lambda i, c: (0, i))
    prev = lambda c: jnp.maximum(c - 1, 0)
    kspec = pl.BlockSpec((PEER_HEADS, ec // PEER_NKEYS, tm), lambda i, c: (0, prev(c), i))
    row = lambda i, c: (0, 0)
    return pl.pallas_call(
        functools.partial(_peer_expert_kernel, final_norm=final_norm),
        grid=(t // tm, n_chunks + 1),
        in_specs=[pl.BlockSpec((d, tm), lambda i, c: (0, i)),
                  pl.BlockSpec((ec, d), lambda i, c: (jnp.minimum(c, n_chunks - 1), 0)),
                  pl.BlockSpec((1, d, ec), lambda i, c: (prev(c), 0, 0)),
                  rspec, kspec, kspec, rspec,
                  pl.BlockSpec((tm, d), lambda i, c: (i, 0)),
                  pl.BlockSpec((1, d), row), pl.BlockSpec((1, d), row)],
        out_specs=pl.BlockSpec((tm, d), lambda i, c: (i, 0)),
        out_shape=jax.ShapeDtypeStruct((t, d), F32),
        scratch_shapes=[pltpu.VMEM((d, tm), F32), pltpu.VMEM((ec, tm), BF16),
                        pltpu.VMEM((ec, tm), F32), pltpu.VMEM((ec, tm), F32)],
        compiler_params=_cparams(2),
    )(ht, u_bf, vt_bf, r2, ln, g1, g2, x, gate, norm_final)


def _peer_layer(x, g, sh, sc, gate, w_q, sub_keys, u_tab, v_tab, norm_final, final_norm):
    ht, st = _peer_query(x, g, sh, sc, w_q, sub_keys)
    route = _peer_route(st)
    e, d = v_tab.shape
    vt = v_tab.astype(BF16).reshape(e // PEER_EXPERT_CHUNK, PEER_EXPERT_CHUNK, d).transpose(0, 2, 1)
    return _peer_experts(ht, u_tab.astype(BF16), vt, route, x, gate, norm_final, final_norm)


def kernel(x, c, ada_w, ada_b, norm_mix, norm_ffn, norm_final, ssm_w_in, ssm_lambda_re, ssm_lambda_im, ssm_log_dt, ssm_b_re, ssm_b_im, ssm_c_re, ssm_c_im, ssm_d, ssm_w_out, nsa_w_in, nsa_cmp_k_pos, nsa_cmp_k_w1, nsa_cmp_k_w2, nsa_cmp_v_pos, nsa_cmp_v_w1, nsa_cmp_v_w2, nsa_w_out, peer_w_q, peer_sub_keys, peer_u, peer_v):
    bsz, l_all, d = x.shape
    assert bsz == 1
    depth = ada_w.shape[0]
    mod = _adaln(c, ada_w, ada_b)
    xt = x.reshape(l_all, d)
    nf = norm_final.reshape(1, d)
    for i in range(depth):
        sh1, sc1, g1, sh2, sc2, g2 = [mod[i, :, k * d:(k + 1) * d] for k in range(6)]
        j = i // 2
        gm = norm_mix[i].reshape(1, d)
        if i % 2 == 0:
            xt = _s5_layer(xt, gm, sh1, sc1, g1, ssm_w_in[j], ssm_lambda_re[j], ssm_lambda_im[j],
                           ssm_log_dt[j], ssm_b_re[j], ssm_b_im[j], ssm_c_re[j], ssm_c_im[j],
                           ssm_d[j], ssm_w_out[j])
        else:
            xt = _nsa_layer(xt, gm, sh1, sc1, g1, nsa_w_in[j], nsa_cmp_k_pos[j], nsa_cmp_k_w1[j],
                            nsa_cmp_k_w2[j], nsa_cmp_v_pos[j], nsa_cmp_v_w1[j], nsa_cmp_v_w2[j],
                            nsa_w_out[j])
        xt = _peer_layer(xt, norm_ffn[i].reshape(1, d), sh2, sc2, g2, peer_w_q[i], peer_sub_keys[i],
                         peer_u[i], peer_v[i], nf, final_norm=(i == depth - 1))
    return xt.reshape(bsz, l_all, d)
```

```python
import functools
import math

import numpy as np
import jax
import jax.numpy as jnp
from jax import lax
from jax.experimental import pallas as pl
from jax.experimental.pallas import tpu as pltpu

F32 = jnp.float32
BF16 = jnp.bfloat16

RMS_EPS = 1e-6

SSM_GROUP = 16
SSM_STATE = 64
SSM_BLOCK_GROUPS = 8
S5_SUBSEQ = 8
S5_CHUNK = 128
S5_LANE_BLOCK = 1024

NSA_HEADS = 16
NSA_HEAD_DIM = 128
NSA_KV_GROUPS = 4
NSA_REP = NSA_HEADS // NSA_KV_GROUPS
N_BRANCH = 3
CMP_LEN = 32
CMP_STRIDE = 16
SEL_LEN = 64
SEL_TOPN = 16
WIN = 512
Q_BLOCK = 256
SEL_KV_TILE = 512
MASK_PENALTY = -(2.0 ** 30)
SEL_DEN_LIMIT = 1e30
ONEHOT_PERIOD = 128

PEER_HEADS = 8
PEER_NKEYS = 128
PEER_TOPK = 16
PEER_HALF = 128
PEER_EXPERT_CHUNK = 1024
PEER_LAST_IS_EVEN = (PEER_NKEYS ** 2 // PEER_EXPERT_CHUNK) % 2 == 0
RANK_NONE = 255.0
RANK_CODE = 2.0 ** 100
RANK_CODE_STEPS = 32.0

VMEM_LIMIT = 56 * 1024 * 1024
NT_DIMS = (((1,), (1,)), ((), ()))


def _cparams(n_axes):
    return pltpu.CompilerParams(
        dimension_semantics=("arbitrary",) * n_axes, vmem_limit_bytes=VMEM_LIMIT)


def _norm_mod(x, g, sh, sc):
    ms = jnp.mean(x * x, axis=-1, keepdims=True)
    xn = x * lax.rsqrt(ms + RMS_EPS)
    return (xn * g) * (1.0 + sc) + sh


def _adaln_kernel(c_ref, w_ref, b_ref, o_ref):
    c = c_ref[...]
    cond = c * jax.nn.sigmoid(c)
    o_ref[0] = jnp.dot(cond.astype(BF16), w_ref[0].astype(BF16),
                       preferred_element_type=F32) + b_ref[0]


def _adaln(c, ada_w, ada_b):
    depth, d, n = ada_w.shape
    tn = 1024
    c8 = jnp.broadcast_to(c, (8, d))
    out = pl.pallas_call(
        _adaln_kernel,
        grid=(depth, n // tn),
        in_specs=[pl.BlockSpec((8, d), lambda i, j: (0, 0)),
                  pl.BlockSpec((1, d, tn), lambda i, j: (i, 0, j)),
                  pl.BlockSpec((1, 1, tn), lambda i, j: (i, 0, j))],
        out_specs=pl.BlockSpec((1, 8, tn), lambda i, j: (i, 0, j)),
        out_shape=jax.ShapeDtypeStruct((depth, 8, n), F32),
        compiler_params=_cparams(2),
    )(c8, ada_w, ada_b.reshape(depth, 1, n))
    return out[:, 0:1, :]


def _nmm_kernel(x_ref, g_ref, sh_ref, sc_ref, w_ref, cs_ref, o_ref, h_scr, *, act):
    @pl.when(pl.program_id(1) == 0)
    def _():
        h_scr[...] = _norm_mod(x_ref[...], g_ref[...], sh_ref[...], sc_ref[...]).astype(BF16)

    acc = jnp.dot(h_scr[...], w_ref[...], preferred_element_type=F32)
    if act == "sigmoid":
        acc = jax.nn.sigmoid(acc)
    else:
        acc = acc * cs_ref[...]
    o_ref[...] = acc.astype(o_ref.dtype)


def _norm_mod_matmul(x, g, sh, sc, w, colscale, out_dtype, act="scale", tm=1024, tn=1024):
    t, d = x.shape
    n = w.shape[1]
    tm, tn = min(tm, t), min(tn, n)
    row = lambda i, j: (0, 0)
    return pl.pallas_call(
        functools.partial(_nmm_kernel, act=act),
        grid=(t // tm, n // tn),
        in_specs=[pl.BlockSpec((tm, d), lambda i, j: (i, 0)),
                  pl.BlockSpec((1, d), row), pl.BlockSpec((1, d), row), pl.BlockSpec((1, d), row),
                  pl.BlockSpec((d, tn), lambda i, j: (0, j)),
                  pl.BlockSpec((1, tn), lambda i, j: (0, j))],
        out_specs=pl.BlockSpec((tm, tn), lambda i, j: (i, j)),
        out_shape=jax.ShapeDtypeStruct((t, n), out_dtype),
        scratch_shapes=[pltpu.VMEM((tm, d), BF16)],
        compiler_params=_cparams(2),
    )(x, g, sh, sc, w, colscale)


def _mm_res_kernel(a_ref, w_ref, x_ref, g_ref, o_ref):
    y = jnp.dot(a_ref[...], w_ref[...], preferred_element_type=F32)
    o_ref[...] = x_ref[...] + g_ref[...] * y


def _mm_glu_res_kernel(a_ref, wa_ref, wb_ref, x_ref, g_ref, o_ref):
    a = a_ref[...]
    ya = jnp.dot(a, wa_ref[...], preferred_element_type=F32)
    yb = jnp.dot(a, wb_ref[...], preferred_element_type=F32)
    o_ref[...] = x_ref[...] + g_ref[...] * (ya * jax.nn.sigmoid(yb))


def _matmul_residual(a, w, x, gate, glu, tm=1024, tn=1024):
    t, k = a.shape
    n = x.shape[1]
    tm = min(tm, t)
    nb = n // tn
    a_spec = pl.BlockSpec((tm, k), lambda i, j: (i, 0))
    w_spec = pl.BlockSpec((k, tn), lambda i, j: (0, j))
    tail = [pl.BlockSpec((tm, tn), lambda i, j: (i, j)), pl.BlockSpec((1, tn), lambda i, j: (0, j))]
    if glu:
        kern = _mm_glu_res_kernel
        in_specs = [a_spec, w_spec, pl.BlockSpec((k, tn), lambda i, j: (0, j + nb))] + tail
        args = (a, w, w, x, gate)
    else:
        kern = _mm_res_kernel
        in_specs = [a_spec, w_spec] + tail
        args = (a, w, x, gate)
    return pl.pallas_call(
        kern,
        grid=(t // tm, nb),
        in_specs=in_specs,
        out_specs=pl.BlockSpec((tm, tn), lambda i, j: (i, j)),
        out_shape=jax.ShapeDtypeStruct((t, n), F32),
        compiler_params=_cparams(2),
    )(*args)


def _s5_param_kernel(lre_ref, lim_ref, ldt_ref, bre_ref, bim_ref, ar_ref, ai_ref, bbr_ref, bbi_ref):
    lre = lre_ref[...]
    lim = lim_ref[...]
    dt = jnp.exp(ldt_ref[...])
    mag = jnp.exp(lre * dt)
    ar = mag * jnp.cos(lim * dt)
    ai = mag * jnp.sin(lim * dt)
    den = lre * lre + lim * lim
    cr = ((ar - 1.0) * lre + ai * lim) / den
    ci = (ai * lre - (ar - 1.0) * lim) / den
    ar_ref[...] = ar
    ai_ref[...] = ai
    bbr_ref[...] = cr * bre_ref[...] - ci * bim_ref[...]
    bbi_ref[...] = cr * bim_ref[...] + ci * bre_ref[...]


def _s5_params(lam_re, lam_im, log_dt, b_re, b_im):
    g, n, p = b_re.shape
    rep = lambda a: jnp.repeat(a, p, axis=1)
    shp = jax.ShapeDtypeStruct((g, n * p), F32)
    ar, ai, bbr, bbi = pl.pallas_call(
        _s5_param_kernel, out_shape=(shp, shp, shp, shp),
    )(rep(lam_re), rep(lam_im), log_dt.reshape(g, 1), b_re.reshape(g, n * p), b_im.reshape(g, n * p))
    return ar[:, ::p], ai[:, ::p], bbr.reshape(g, n, p), bbi.reshape(g, n, p)


def _s5_scan_kernel(u_ref, a_ref, bbr_ref, bbi_ref, cr_ref, cin_ref, d_ref, o_ref,
                    bur, bui, hin_r, hin_i, car_r, car_i, apr, api):
    t, width = u_ref.shape
    sub = t // S5_SUBSEQ
    lanes = bur.shape[1]
    nblk = bbr_ref.shape[0]
    kin = width // nblk
    kst = lanes // nblk

    @pl.when(pl.program_id(0) == 0)
    def _init():
        car_r[...] = jnp.zeros_like(car_r)
        car_i[...] = jnp.zeros_like(car_i)
        ar = a_ref[0:1, :]
        ai = a_ref[1:2, :]
        pr, pi = ar, ai
        apr[0:1, :] = pr
        api[0:1, :] = pi
        for i in range(1, sub):
            pr, pi = pr * ar - pi * ai, pr * ai + pi * ar
            apr[i:i + 1, :] = pr
            api[i:i + 1, :] = pi

    for k in range(nblk):
        uk = u_ref[:, kin * k:kin * (k + 1)].astype(BF16)
        bur[:, kst * k:kst * (k + 1)] = jnp.dot(uk, bbr_ref[k], preferred_element_type=F32)
        bui[:, kst * k:kst * (k + 1)] = jnp.dot(uk, bbi_ref[k], preferred_element_type=F32)

    lb = S5_LANE_BLOCK
    for b in range(lanes // lb):
        sl = slice(b * lb, (b + 1) * lb)
        ar = jnp.broadcast_to(a_ref[0:1, sl], (S5_SUBSEQ, lb))
        ai = jnp.broadcast_to(a_ref[1:2, sl], (S5_SUBSEQ, lb))
        sr = jnp.zeros((S5_SUBSEQ, lb), F32)
        si = jnp.zeros((S5_SUBSEQ, lb), F32)
        for i in range(sub):
            rows = slice(S5_SUBSEQ * i, S5_SUBSEQ * (i + 1))
            sr, si = (ar * sr - ai * si + bur[rows, sl], ar * si + ai * sr + bui[rows, sl])
            bur[rows, sl] = sr
            bui[rows, sl] = si
        asr = apr[sub - 1:sub, sl]
        asi = api[sub - 1:sub, sl]
        hr = car_r[0:1, sl]
        hi = car_i[0:1, sl]
        for j in range(S5_SUBSEQ):
            hin_r[j:j + 1, sl] = hr
            hin_i[j:j + 1, sl] = hi
            er = sr[j:j + 1, :]
            ei = si[j:j + 1, :]
            hr, hi = er + asr * hr - asi * hi, ei + asr * hi + asi * hr
        car_r[0:1, sl] = hr
        car_i[0:1, sl] = hi
        hinr = hin_r[:, sl]
        hini = hin_i[:, sl]
        for i in range(sub):
            rows = slice(S5_SUBSEQ * i, S5_SUBSEQ * (i + 1))
            pr = jnp.broadcast_to(apr[i:i + 1, sl], (S5_SUBSEQ, lb))
            pi = jnp.broadcast_to(api[i:i + 1, sl], (S5_SUBSEQ, lb))
            bur[rows, sl] = bur[rows, sl] + pr * hinr - pi * hini
            bui[rows, sl] = bui[rows, sl] + pr * hini + pi * hinr

    for k in range(nblk):
        sr = bur[:, kst * k:kst * (k + 1)].astype(BF16)
        si = bui[:, kst * k:kst * (k + 1)].astype(BF16)
        y = (jnp.dot(sr, cr_ref[k], preferred_element_type=F32)
             + jnp.dot(si, cin_ref[k], preferred_element_type=F32))
        cols = slice(kin * k, kin * (k + 1))
        y = y + d_ref[:, cols] * u_ref[:, cols]
        o_ref[:, cols] = jax.nn.gelu(y).astype(o_ref.dtype)


def _s5_scan(u_p, ar, ai, bb_r, bb_i, c_re, c_im, d_skip):
    t_all, width = u_p.shape
    g, n, p = bb_r.shape
    bg = SSM_BLOCK_GROUPS
    nblk = g // bg
    eye = jnp.eye(bg, dtype=F32)
    blk_b = lambda bb: jnp.einsum('kgnp,gh->kgphn', bb.reshape(nblk, bg, n, p), eye
                                  ).reshape(nblk, bg * p, bg * n).astype(BF16)
    blk_c = lambda cc: jnp.einsum('kgpn,gh->kgnhp', cc.reshape(nblk, bg, p, n), eye
                                  ).reshape(nblk, bg * n, bg * p).astype(BF16)
    lanes = g * n
    a8 = jnp.zeros((8, lanes), F32).at[0].set(ar.reshape(lanes)).at[1].set(ai.reshape(lanes))
    t = S5_CHUNK
    sub = t // S5_SUBSEQ
    full3 = lambda c: (0, 0, 0)
    wspec_b = pl.BlockSpec((nblk, bg * p, bg * n), full3)
    wspec_c = pl.BlockSpec((nblk, bg * n, bg * p), full3)
    return pl.pallas_call(
        _s5_scan_kernel,
        grid=(t_all // t,),
        in_specs=[pl.BlockSpec((t, width), lambda c: (c, 0)),
                  pl.BlockSpec((8, lanes), lambda c: (0, 0)),
                  wspec_b, wspec_b, wspec_c, wspec_c,
                  pl.BlockSpec((1, width), lambda c: (0, 0))],
        out_specs=pl.BlockSpec((t, width), lambda c: (c, 0)),
        out_shape=jax.ShapeDtypeStruct((t_all, width), BF16),
        scratch_shapes=[pltpu.VMEM((t, lanes), F32), pltpu.VMEM((t, lanes), F32),
                        pltpu.VMEM((8, lanes), F32), pltpu.VMEM((8, lanes), F32),
                        pltpu.VMEM((8, lanes), F32), pltpu.VMEM((8, lanes), F32),
                        pltpu.VMEM((sub, lanes), F32), pltpu.VMEM((sub, lanes), F32)],
        compiler_params=_cparams(1),
    )(u_p, a8, blk_b(bb_r), blk_b(bb_i), blk_c(c_re), blk_c(-c_im), d_skip.reshape(1, width))


def _s5_rows_to_subseq(x, inverse=False):
    t_all, d = x.shape
    sub = S5_CHUNK // S5_SUBSEQ
    shape = (t_all // S5_CHUNK, sub, S5_SUBSEQ, d) if inverse else (t_all // S5_CHUNK, S5_SUBSEQ, sub, d)
    return x.reshape(shape).transpose(0, 2, 1, 3).reshape(t_all, d)


def _s5_layer(x, g, sh, sc, gate, w_in, lam_re, lam_im, log_dt, b_re, b_im, c_re, c_im, d_skip, w_out):
    d = x.shape[1]
    x_p = _s5_rows_to_subseq(x)
    ones = jnp.ones((1, w_in.shape[1]), F32)
    u_p = _norm_mod_matmul(x_p, g, sh, sc, w_in.astype(BF16), ones, F32)
    ar, ai, bb_r, bb_i = _s5_params(lam_re, lam_im, log_dt, b_re, b_im)
    gy_p = _s5_scan(u_p, ar, ai, bb_r, bb_i, c_re, c_im, d_skip)
    xn_p = _matmul_residual(gy_p, w_out.astype(BF16), x_p, gate, glu=True)
    return _s5_rows_to_subseq(xn_p, inverse=True)


def _cmp_kernel(x_ref, w1a_ref, w1b_ref, pos_ref, w1_ref, w2_ref, o_ref, acc_a, acc_b):
    l = pl.program_id(2)

    @pl.when(l == 0)
    def _():
        acc_a[...] = jnp.zeros_like(acc_a)
        acc_b[...] = jnp.zeros_like(acc_b)

    x = x_ref[...]
    acc_a[...] += jnp.dot(x, w1a_ref[0, 0], preferred_element_type=F32)
    acc_b[...] += jnp.dot(x, w1b_ref[0, 0], preferred_element_type=F32)

    @pl.when(l == pl.num_programs(2) - 1)
    def _():
        m = acc_a.shape[0]
        posc = jnp.zeros((8, NSA_HEAD_DIM), F32)
        for ll in range(CMP_LEN):
            prow = jnp.broadcast_to(pos_ref[0, ll:ll + 1, :], (8, NSA_HEAD_DIM)).astype(BF16)
            posc = posc + jnp.dot(prow, w1_ref[0, ll], preferred_element_type=F32)
        pre = acc_a[...] + pltpu.roll(acc_b[...], m - 1, 0) + posc[0:1, :]
        hid = jax.nn.gelu(pre)
        o_ref[0, 0] = jnp.dot(hid.astype(BF16), w2_ref[0], preferred_element_type=F32).astype(o_ref.dtype)


def _compress(qkv, kv_col0, pos, w1, w2):
    l_all, c = qkv.shape
    half = CMP_LEN // 2
    m = l_all // half
    cb = c // NSA_HEAD_DIM
    x2 = qkv.reshape(m, half * c)
    g = NSA_KV_GROUPS
    dh = NSA_HEAD_DIM
    return pl.pallas_call(
        _cmp_kernel,
        grid=(2, g, half),
        in_specs=[pl.BlockSpec((m, dh), lambda s, gg, l: (0, l * cb + kv_col0 + g * s + gg)),
                  pl.BlockSpec((1, 1, dh, dh), lambda s, gg, l: (s, l, 0, 0)),
                  pl.BlockSpec((1, 1, dh, dh), lambda s, gg, l: (s, l + half, 0, 0)),
                  pl.BlockSpec((1, CMP_LEN, dh), lambda s, gg, l: (s, 0, 0)),
                  pl.BlockSpec((1, CMP_LEN, dh, dh), lambda s, gg, l: (s, 0, 0, 0)),
                  pl.BlockSpec((1, dh, dh), lambda s, gg, l: (s, 0, 0))],
        out_specs=pl.BlockSpec((1, 1, m, dh), lambda s, gg, l: (s, gg, 0, 0)),
        out_shape=jax.ShapeDtypeStruct((2, g, m, dh), BF16),
        scratch_shapes=[pltpu.VMEM((m, dh), F32), pltpu.VMEM((m, dh), F32)],
        compiler_params=_cparams(3),
    )(x2, w1, w1, pos, w1, w2)


def _softmax2_rows(s, mask):
    s = jnp.where(mask, s, -1e30)
    m = jnp.max(s, axis=-1, keepdims=True)
    p = jnp.exp2(s - m)
    return p, 1.0 / jnp.maximum(jnp.sum(p, axis=-1, keepdims=True), 1e-30)


def _shifted_softmax(s, mask, shift):
    p = jnp.exp2(jnp.where(mask, s - shift, -1e30))
    return p, jnp.sum(p, axis=-1, keepdims=True)


def _any_out_of_range(den, needed):
    ok = (den > 0.0) & (den < SEL_DEN_LIMIT)
    return jnp.max(jnp.where(needed & jnp.logical_not(ok), 1.0, 0.0)) > 0.0


def _nsa_attn_kernel(*refs):
    redo = _nsa_attn_body(True, *refs)

    @pl.when(redo)
    def _():
        _nsa_attn_body(False, *refs)


def _nsa_attn_body(fast, q_ref, gt_ref, kc_ref, vc_ref, agg_ref, ksa_ref, vsa_ref, kw_ref, vw_ref,
                   o_ref, qaug_scr, s_a, s_b, acc_scr, m_scr, shift_scr):
    b = pl.program_id(1)
    qb = q_ref.shape[0]
    dh = NSA_HEAD_DIM
    rep = NSA_REP
    rows = rep * qb
    n_cmp = kc_ref.shape[2]
    n_sel = agg_ref.shape[1]
    top_n = min(SEL_TOPN, n_sel)
    t0 = pl.multiple_of(b * qb, qb)

    q_blk = q_ref[...]
    qs = jnp.concatenate([q_blk[:, r * dh:(r + 1) * dh] for r in range(rep)], axis=0)
    tpos = t0 + lax.broadcasted_iota(jnp.int32, (rows, 1), 0) % qb
    all_rows = jnp.ones((rows, 1), jnp.bool_)

    kc = kc_ref[0, 0]
    s_c = lax.dot_general(qs, kc, NT_DIMS, preferred_element_type=F32)
    cmp_end = lax.broadcasted_iota(jnp.int32, (1, n_cmp), 1) * CMP_STRIDE + (CMP_LEN - 1)
    has_c = tpos >= CMP_LEN - 1
    if fast:
        p_c, den_c = _shifted_softmax(s_c, cmp_end <= tpos, s_c[:, 0:1])
        redo = _any_out_of_range(den_c, has_c)
        inv_c = 1.0 / jnp.maximum(den_c, 1e-30)
    else:
        p_c, inv_c = _softmax2_rows(s_c, cmp_end <= tpos)
    p_c = p_c * jnp.where(has_c, inv_c, 0.0)
    o_c = jnp.dot(p_c.astype(BF16), vc_ref[0, 0], preferred_element_type=F32)

    psum = p_c[0:qb]
    for r in range(1, rep):
        psum = psum + p_c[r * qb:(r + 1) * qb]
    p_hi = psum.astype(BF16)
    p_lo = (psum - p_hi.astype(F32)).astype(BF16)
    agg = agg_ref[...]
    imp = (jnp.dot(p_hi, agg, preferred_element_type=F32)
           + jnp.dot(p_lo, agg, preferred_element_type=F32))
    imp_t = imp.T
    tq = t0 + lax.broadcasted_iota(jnp.int32, (1, qb), 1)
    blk = lax.broadcasted_iota(jnp.int32, (n_sel, qb), 0)
    blk_f = blk.astype(F32)
    cur = tq // SEL_LEN
    valid = blk * SEL_LEN <= tq
    if fast:
        score = jnp.where(valid, imp_t, -1.0 - blk_f)
        score = jnp.where(blk == cur - 1, 1e9, score)
        score = jnp.where(blk == cur, 2e9, score)
        score = jnp.where(blk == 0, 3e9, score)
    else:
        forced = (blk == 0) | (blk == cur) | (blk == cur - 1)
        score = jnp.where(forced, 1e9, jnp.where(valid, imp_t, -1e9))
    for _ in range(top_n):
        mx = jnp.max(score, axis=0, keepdims=True)
        hit = score == mx
        if not fast:
            idx = jnp.min(jnp.where(hit, blk_f, float(n_sel)), axis=0, keepdims=True)
            hit = blk_f == idx
        score = jnp.where(hit, -jnp.inf, score)
    took = score == -jnp.inf
    if fast:
        taken = jnp.sum(jnp.where(took, 1.0, 0.0), axis=0, keepdims=True)
        redo = redo | (jnp.max(jnp.where(taken != float(top_n), 1.0, 0.0)) > 0.0)
    pen = jnp.where(took, 0.0, MASK_PENALTY).T.astype(BF16)
    n_half = qaug_scr.shape[0]
    for hh in range(n_half):
        if n_sel >= ONEHOT_PERIOD:
            ph = pen[:, hh * ONEHOT_PERIOD:(hh + 1) * ONEHOT_PERIOD]
        else:
            ph = jnp.concatenate(
                [pen, jnp.full((qb, ONEHOT_PERIOD - n_sel), MASK_PENALTY, BF16)], axis=1)
        qaug_scr[hh] = jnp.concatenate([qs, jnp.concatenate([ph] * rep, axis=0)], axis=1)

    kt = SEL_KV_TILE
    period_keys = ONEHOT_PERIOD * SEL_LEN

    def sel_scores(i, s_out):
        k0 = pl.multiple_of(i * kt, kt)
        s_out[...] = lax.dot_general(qaug_scr[k0 // period_keys], ksa_ref[pl.ds(k0, kt), :], NT_DIMS,
                                     preferred_element_type=F32)

    def causal_mask(i, s):
        kpos = i * kt + lax.broadcasted_iota(jnp.int32, (1, kt), 1)
        return jnp.where(kpos <= tpos, s, MASK_PENALTY)

    def update_running_max(i, s_in, causal):
        k0 = pl.multiple_of(i * kt, kt)
        s = causal_mask(i, s_in[...]) if causal else s_in[...]
        m_run = m_scr[...]
        m_new = jnp.maximum(m_run, jnp.max(s, axis=-1, keepdims=True))
        p = jnp.exp2(s - m_new)
        acc_scr[...] = (jnp.exp2(m_run - m_new) * acc_scr[...]
                        + jnp.dot(p.astype(BF16), vsa_ref[pl.ds(k0, kt), :], preferred_element_type=F32))
        m_scr[...] = m_new

    def update_fixed_shift(i, s_in, causal):
        k0 = pl.multiple_of(i * kt, kt)
        s = causal_mask(i, s_in[...]) if causal else s_in[...]
        shift = shift_scr[...]
        p = jnp.concatenate([jnp.exp2(s[:, c * dh:(c + 1) * dh] - shift).astype(BF16)
                             for c in range(kt // dh)], axis=1)
        acc_scr[...] += jnp.dot(p, vsa_ref[pl.ds(k0, kt), :], preferred_element_type=F32)

    def sweep(update):
        def pair(j, carry):
            sel_scores(2 * j + 1, s_b)
            update(2 * j, s_a, False)
            sel_scores(2 * j + 2, s_a)
            update(2 * j + 1, s_b, False)
            return carry

        acc_scr[...] = jnp.zeros_like(acc_scr)
        lax.fori_loop(0, n_full // 2, pair, 0)
        odd = n_full % 2 == 1

        @pl.when(odd)
        def _():
            sel_scores(n_full, s_b)
            update(n_full - 1, s_a, False)
            update(n_full, s_b, True)

        @pl.when(jnp.logical_not(odd))
        def _():
            update(n_full, s_a, True)

    n_full = t0 // kt
    sel_scores(0, s_a)
    if fast:
        first = jnp.max(causal_mask(0, s_a[...]), axis=-1, keepdims=True)
        shift_scr[...] = jnp.broadcast_to(first, shift_scr.shape)
        sweep(update_fixed_shift)
    else:
        m_scr[...] = jnp.full(m_scr.shape, -1e30, F32)
        sweep(update_running_max)
    acc_s = acc_scr[...]
    den_s = acc_s[:, dh:dh + 1]
    o_s = acc_s[:, 0:dh] * (1.0 / den_s)

    wlen = WIN + qb
    w0 = pl.multiple_of(jnp.maximum(t0 - WIN, 0), qb)
    s_w = lax.dot_general(qs, kw_ref[pl.ds(w0, wlen), :], NT_DIMS, preferred_element_type=F32)
    diff = tpos - (w0 + lax.broadcasted_iota(jnp.int32, (1, wlen), 1))
    mask_w = (diff >= 0) & (diff < WIN)
    if fast:
        k_diag = jnp.concatenate([kw_ref[pl.ds(t0, qb), :]] * rep, axis=0).astype(F32)
        diag = jnp.sum(qs.astype(F32) * k_diag, axis=-1, keepdims=True)
        p_w, den_w = _shifted_softmax(s_w, mask_w, diag)
        o_w = jnp.dot(p_w.astype(BF16), vw_ref[pl.ds(w0, wlen), :], preferred_element_type=F32)
        o_w = o_w * (1.0 / den_w)
        redo = redo | _any_out_of_range(den_s, all_rows) | _any_out_of_range(den_w, all_rows)
    else:
        p_w, inv_w = _softmax2_rows(s_w, mask_w)
        o_w = jnp.dot((p_w * inv_w).astype(BF16), vw_ref[pl.ds(w0, wlen), :], preferred_element_type=F32)

    gt = gt_ref[...]
    for r in range(rep):
        rs = slice(r * qb, (r + 1) * qb)
        o = (gt[:, r:r + 1] * o_c[rs]
             + gt[:, rep + r:rep + r + 1] * o_s[rs]
             + gt[:, 2 * rep + r:2 * rep + r + 1] * o_w[rs])
        o_ref[:, r * dh:(r + 1) * dh] = o.astype(o_ref.dtype)
    return redo if fast else None


def _nsa_agg(n_cmp_pad, n_cmp, n_sel):
    ratio, span = SEL_LEN // CMP_STRIDE, CMP_LEN // CMP_STRIDE
    agg = np.zeros((n_cmp_pad, n_sel), np.float32)
    jj = np.arange(n_sel)
    for m in range(ratio):
        for n in range(span):
            ii = ratio * jj + m - n
            ok = (ii >= 0) & (ii < n_cmp)
            agg[ii[ok], jj[ok]] += 1.0
    return jnp.asarray(agg, dtype=BF16)


def _nsa_attention(qkv, gates, kv_cmp):
    l_all = qkv.shape[0]
    dh = NSA_HEAD_DIM
    g = NSA_KV_GROUPS
    qb = Q_BLOCK
    qw = NSA_REP * dh
    n_cmp_pad = l_all // CMP_STRIDE
    n_cmp = (l_all - CMP_LEN) // CMP_STRIDE + 1
    n_sel = l_all // SEL_LEN
    agg = _nsa_agg(n_cmp_pad, n_cmp, n_sel)
    n_half = max(n_sel // ONEHOT_PERIOD, 1)
    q_blocks = (NSA_HEADS * dh) // dh
    key = np.arange(l_all)
    onehot = jnp.asarray((key[:, None] // SEL_LEN) % ONEHOT_PERIOD == np.arange(ONEHOT_PERIOD)[None, :],
                         dtype=BF16)
    ones_col = jnp.asarray(np.arange(dh)[None, :] == 0, dtype=BF16)
    grouped = lambda base: qkv[:, (q_blocks + base) * dh:(q_blocks + base + g) * dh].reshape(l_all, g, dh)
    ks_aug = jnp.concatenate([grouped(2 * g), jnp.broadcast_to(onehot[:, None, :], (l_all, g, ONEHOT_PERIOD))],
                             axis=-1).reshape(l_all, g * (dh + ONEHOT_PERIOD))
    vs_aug = jnp.concatenate([grouped(3 * g), jnp.broadcast_to(ones_col[:, None, :], (l_all, g, dh))],
                             axis=-1).reshape(l_all, g * 2 * dh)
    once = dict(pipeline_mode=pl.Buffered(1))
    kv_spec = lambda base: pl.BlockSpec((l_all, dh), lambda gg, b: (0, q_blocks + base + gg), **once)
    aug_spec = pl.BlockSpec((l_all, 2 * dh), lambda gg, b: (0, gg), **once)
    rows = NSA_REP * qb
    return pl.pallas_call(
        _nsa_attn_kernel,
        grid=(g, l_all // qb),
        in_specs=[pl.BlockSpec((qb, qw), lambda gg, b: (b, gg)),
                  pl.BlockSpec((qb, dh), lambda gg, b: (b, gg)),
                  pl.BlockSpec((1, 1, n_cmp_pad, dh), lambda gg, b: (0, gg, 0, 0)),
                  pl.BlockSpec((1, 1, n_cmp_pad, dh), lambda gg, b: (1, gg, 0, 0)),
                  pl.BlockSpec((n_cmp_pad, n_sel), lambda gg, b: (0, 0)),
                  aug_spec, aug_spec, kv_spec(4 * g), kv_spec(5 * g)],
        out_specs=pl.BlockSpec((qb, qw), lambda gg, b: (b, gg)),
        out_shape=jax.ShapeDtypeStruct((l_all, NSA_HEADS * dh), BF16),
        scratch_shapes=[pltpu.VMEM((n_half, rows, dh + ONEHOT_PERIOD), BF16),
                        pltpu.VMEM((rows, SEL_KV_TILE), F32), pltpu.VMEM((rows, SEL_KV_TILE), F32),
                        pltpu.VMEM((rows, 2 * dh), F32), pltpu.VMEM((rows, 1), F32),
                        pltpu.VMEM((rows, dh), F32)],
        compiler_params=_cparams(2),
    )(qkv, gates, kv_cmp, kv_cmp, agg, ks_aug, vs_aug, qkv, qkv)


def _nsa_layer(x, g, sh, sc, gate, w_in, k_pos, k_w1, k_w2, v_pos, v_w1, v_w2, w_out):
    dh = NSA_HEAD_DIM
    qd = NSA_HEADS * dh
    kvd = 2 * N_BRANCH * NSA_KV_GROUPS * dh
    w_main = w_in[:, :qd + kvd].astype(BF16)
    colscale = jnp.concatenate([jnp.full((1, qd), dh ** -0.5 * math.log2(math.e), F32),
                                jnp.ones((1, kvd), F32)], axis=1)
    qkv = _norm_mod_matmul(x, g, sh, sc, w_main, colscale, BF16)
    wg = w_in[:, qd + kvd:].reshape(-1, NSA_KV_GROUPS, NSA_REP, N_BRANCH).transpose(0, 1, 3, 2)
    wg = wg.reshape(-1, NSA_KV_GROUPS, N_BRANCH * NSA_REP)
    wg = jnp.pad(wg, ((0, 0), (0, 0), (0, dh - N_BRANCH * NSA_REP))).reshape(-1, NSA_KV_GROUPS * dh)
    gates = _norm_mod_matmul(x, g, sh, sc, wg.astype(BF16), jnp.ones((1, wg.shape[1]), F32), F32,
                             act="sigmoid")
    kv_cmp = _compress(qkv, qd // dh, jnp.stack([k_pos, v_pos]),
                       jnp.stack([k_w1, v_w1]).astype(BF16), jnp.stack([k_w2, v_w2]).astype(BF16))
    o = _nsa_attention(qkv, gates, kv_cmp)
    return _matmul_residual(o, w_out.astype(BF16), x, gate, glu=False)


def _peer_query_kernel(x_ref, g_ref, sh_ref, sc_ref, wq_ref, keys_ref, ht_ref, st_ref):
    h32 = _norm_mod(x_ref[...], g_ref[...], sh_ref[...], sc_ref[...])
    ht_ref[...] = h32.T.astype(BF16)
    q = jnp.dot(h32.astype(BF16), wq_ref[...], preferred_element_type=F32).astype(BF16)
    for hc in range(keys_ref.shape[0]):
        rows = slice(hc * PEER_NKEYS, (hc + 1) * PEER_NKEYS)
        st_ref[rows, :] = lax.dot_general(keys_ref[hc], q[:, hc * PEER_HALF:(hc + 1) * PEER_HALF],
                                          NT_DIMS, preferred_element_type=F32)


def _peer_query(x, g, sh, sc, w_q, sub_keys, tm=512):
    t, d = x.shape
    nq = w_q.shape[1]
    keys = sub_keys.reshape(-1, PEER_NKEYS, PEER_HALF).astype(BF16)
    row = lambda i: (0, 0)
    return pl.pallas_call(
        _peer_query_kernel,
        grid=(t // tm,),
        in_specs=[pl.BlockSpec((tm, d), lambda i: (i, 0)),
                  pl.BlockSpec((1, d), row), pl.BlockSpec((1, d), row), pl.BlockSpec((1, d), row),
                  pl.BlockSpec((d, nq), row),
                  pl.BlockSpec(keys.shape, lambda i: (0, 0, 0))],
        out_specs=[pl.BlockSpec((d, tm), lambda i: (0, i)),
                   pl.BlockSpec((keys.shape[0] * PEER_NKEYS, tm), lambda i: (0, i))],
        out_shape=[jax.ShapeDtypeStruct((d, t), BF16),
                   jax.ShapeDtypeStruct((keys.shape[0] * PEER_NKEYS, t), F32)],
        compiler_params=_cparams(1),
    )(x, g, sh, sc, w_q.astype(BF16), keys)


def _peer_cells():
    return [(a, b) for a in range(PEER_TOPK) for b in range(PEER_TOPK) if (a + 1) * (b + 1) <= PEER_TOPK]


def _take_max(cur, iota, exact):
    v = jnp.max(cur, axis=0, keepdims=True)
    hit = cur == v
    if exact:
        idx = jnp.min(jnp.where(hit, iota, cur.shape[0]), axis=0, keepdims=True)
        hit = iota == idx
    return v, hit


def _top_ranks(s, iota_k, exact):
    vals = []
    cur = s
    if exact:
        rank = jnp.full(s.shape, RANK_NONE, F32)
        for a in range(PEER_TOPK):
            v, hit = _take_max(cur, iota_k, exact)
            rank = jnp.where(hit, float(a), rank)
            cur = jnp.where(hit, -jnp.inf, cur)
            vals.append(v)
        return vals, rank
    for a in range(PEER_TOPK):
        v, hit = _take_max(cur, iota_k, exact)
        cur = jnp.where(hit, -RANK_CODE * (1.0 + a / RANK_CODE_STEPS), cur)
        vals.append(v)
    rank = jnp.where(cur <= -RANK_CODE, (cur * (-1.0 / RANK_CODE) - 1.0) * RANK_CODE_STEPS, RANK_NONE)
    return vals, rank


def _peer_route_body(st_ref, seg_ref, r2_ref, ln_ref, g1_ref, g2_ref, exact):
    tn = st_ref.shape[1]
    nk = PEER_NKEYS
    cells = _peer_cells()
    n_cell = len(cells)
    n_pad = -(-n_cell // 8) * 8
    n_seg = seg_ref.shape[1]
    iota_k = lax.broadcasted_iota(jnp.int32, (nk, tn), 0)
    iota_c = lax.broadcasted_iota(jnp.int32, (n_pad, tn), 0)
    count = lambda m: jnp.sum(m.astype(F32), axis=0, keepdims=True)
    tied = jnp.zeros((1, tn), jnp.bool_)
    for h in range(PEER_HEADS):
        s1 = st_ref[(2 * h) * nk:(2 * h + 1) * nk, :]
        s2 = st_ref[(2 * h + 1) * nk:(2 * h + 2) * nk, :]
        v1, rank1 = _top_ranks(s1, iota_k, exact)
        v2, rank2 = _top_ranks(s2, iota_k, exact)
        cand = jnp.concatenate([v1[a] + v2[b] for a, b in cells]
                               + [jnp.full((n_pad - n_cell, tn), -jnp.inf, F32)], axis=0)
        top = v1[0] + v2[0]
        e_c = jnp.exp(cand - top)
        chosen = jnp.zeros((n_pad, tn), jnp.bool_)
        cur = cand
        for _ in range(PEER_TOPK):
            _, hit = _take_max(cur, iota_c, exact)
            chosen = chosen | hit
            cur = jnp.where(hit, -jnp.inf, cur)
        chosen_f = chosen.astype(F32)
        if not exact:
            k = float(PEER_TOPK)
            tied = (tied | (count(rank1 != RANK_NONE) != k) | (count(rank2 != RANK_NONE) != k)
                    | (jnp.sum(chosen_f, axis=0, keepdims=True) != k))
        z = jnp.sum(chosen_f * e_c, axis=0, keepdims=True)
        chosen_pad = jnp.concatenate([chosen_f, jnp.zeros((n_seg - n_pad, tn), F32)], axis=0)
        rowlen = jnp.dot(seg_ref[...], chosen_pad.astype(BF16), preferred_element_type=F32)
        ln = jnp.zeros((nk, tn), F32)
        for a in range(PEER_TOPK):
            ln = jnp.where(rank1 == float(a), rowlen[a:a + 1, :], ln)
        rows = slice(h * nk, (h + 1) * nk)
        r2_ref[rows, :] = rank2.astype(r2_ref.dtype)
        ln_ref[h] = ln
        g1_ref[h] = jnp.exp(s1 - v1[0]) / z
        g2_ref[rows, :] = jnp.exp(s2 - v2[0]).astype(g2_ref.dtype)
    return tied


def _peer_route_kernel(st_ref, seg_ref, r2_ref, ln_ref, g1_ref, g2_ref):
    refs = (st_ref, seg_ref, r2_ref, ln_ref, g1_ref, g2_ref)
    tied = _peer_route_body(*refs, exact=False)

    @pl.when(jnp.max(tied.astype(F32)) > 0.0)
    def _():
        _peer_route_body(*refs, exact=True)


def _peer_route(st, tn=128):
    n_rows, t = st.shape
    cells = _peer_cells()
    seg = np.zeros((PEER_TOPK, PEER_NKEYS), np.float32)
    for c, (a, _) in enumerate(cells):
        seg[a, c] = 1.0
    out_rows = PEER_HEADS * PEER_NKEYS
    shp = jax.ShapeDtypeStruct((out_rows, t), BF16)
    shp_row = jax.ShapeDtypeStruct((PEER_HEADS, PEER_NKEYS, t), F32)
    spec = pl.BlockSpec((out_rows, tn), lambda i: (0, i))
    spec_row = pl.BlockSpec((PEER_HEADS, PEER_NKEYS, tn), lambda i: (0, 0, i))
    return pl.pallas_call(
        _peer_route_kernel,
        grid=(t // tn,),
        in_specs=[pl.BlockSpec((n_rows, tn), lambda i: (0, i)),
                  pl.BlockSpec((PEER_TOPK, PEER_NKEYS), lambda i: (0, 0))],
        out_specs=[spec, spec_row, spec_row, spec],
        out_shape=[shp, shp_row, shp_row, shp],
        compiler_params=_cparams(1),
    )(st, jnp.asarray(seg, dtype=BF16))


def _peer_expert_kernel(h_ref, u_ref, vt_ref, r2_ref, ln_ref, g1_ref, g2_ref, x_ref, gate_ref, nf_ref,
                        o_ref, acc_t, p_scr, pre_a, pre_b, *, final_norm):
    c = pl.program_id(1)
    last = pl.num_programs(1) - 1
    ec = u_ref.shape[0]
    nk = PEER_NKEYS
    tm = h_ref.shape[1]
    zero = jnp.zeros((), BF16)

    def score(pre_out):
        pre_out[...] = jnp.dot(u_ref[...], h_ref[...], preferred_element_type=F32)

    def finish(pre_in):
        for ii in range(ec // nk):
            w = jnp.zeros((nk, tm), BF16)
            for h in range(PEER_HEADS):
                rows = slice(h * nk, (h + 1) * nk)
                ln_row = ln_ref[h, ii:ii + 1, :].astype(BF16)
                g1_row = g1_ref[h, ii:ii + 1, :].astype(BF16)
                w = w + jnp.where(r2_ref[rows, :] < ln_row, g2_ref[rows, :], zero) * g1_row
            act = jax.nn.gelu(pre_in[ii * nk:(ii + 1) * nk, :])
            p_scr[ii * nk:(ii + 1) * nk, :] = w * act.astype(BF16)
        acc_t[...] += lax.dot_general(p_scr[...], vt_ref[...], (((0,), (0,)), ((), ())),
                                      preferred_element_type=F32)

    @pl.when(c == 0)
    def _():
        acc_t[...] = jnp.zeros_like(acc_t)
        score(pre_a)

    @pl.when((c > 0) & (c < last) & (c % 2 == 1))
    def _():
        score(pre_b)
        finish(pre_a)

    @pl.when((c > 0) & (c < last) & (c % 2 == 0))
    def _():
        score(pre_a)
        finish(pre_b)

    @pl.when(c == last)
    def _():
        finish(pre_b if PEER_LAST_IS_EVEN else pre_a)
        xo = x_ref[...] + gate_ref[...] * acc_t[...]
        if final_norm:
            ms = jnp.mean(xo * xo, axis=-1, keepdims=True)
            xo = (xo * lax.rsqrt(ms + RMS_EPS)) * nf_ref[...]
        o_ref[...] = xo


def _peer_experts(ht, u_bf, vt_bf, route, x, gate, norm_final, final_norm, tm=512, ec=PEER_EXPERT_CHUNK):
    t, d = x.shape
    e = u_bf.shape[0]
    n_chunks = e // ec
    assert (n_chunks % 2 == 0) == PEER_LAST_IS_EVEN
    r2, ln, g1, g2 = route
    rspec = pl.BlockSpec((r2.shape[0], tm), lambda i, c: (0, i))
    prev = lambda c: jnp.maximum(c - 1, 0)
    kspec = pl.BlockSpec((PEER_HEADS, ec // PEER_NKEYS, tm), lambda i, c: (0, prev(c), i))
    row = lambda i, c: (0, 0)
    return pl.pallas_call(
        functools.partial(_peer_expert_kernel, final_norm=final_norm),
        grid=(t // tm, n_chunks + 1),
        in_specs=[pl.BlockSpec((d, tm), lambda i, c: (0, i)),
                  pl.BlockSpec((ec, d), lambda i, c: (jnp.minimum(c, n_chunks - 1), 0)),
                  pl.BlockSpec((ec, d), lambda i, c: (prev(c), 0)),
                  rspec, kspec, kspec, rspec,
                  pl.BlockSpec((tm, d), lambda i, c: (i, 0)),
                  pl.BlockSpec((1, d), row), pl.BlockSpec((1, d), row)],
        out_specs=pl.BlockSpec((tm, d), lambda i, c: (i, 0)),
        out_shape=jax.ShapeDtypeStruct((t, d), F32),
        scratch_shapes=[pltpu.VMEM((tm, d), F32), pltpu.VMEM((ec, tm), BF16),
                        pltpu.VMEM((ec, tm), F32), pltpu.VMEM((ec, tm), F32)],
        compiler_params=_cparams(2),
    )(ht, u_bf, vt_bf, r2, ln, g1, g2, x, gate, norm_final)


def _peer_layer(x, g, sh, sc, gate, w_q, sub_keys, u_tab, v_tab, norm_final, final_norm):
    ht, st = _peer_query(x, g, sh, sc, w_q, sub_keys)
    route = _peer_route(st)
    return _peer_experts(ht, u_tab.astype(BF16), v_tab.astype(BF16), route, x, gate, norm_final,
                         final_norm)


def kernel(x, c, ada_w, ada_b, norm_mix, norm_ffn, norm_final, ssm_w_in, ssm_lambda_re, ssm_lambda_im, ssm_log_dt, ssm_b_re, ssm_b_im, ssm_c_re, ssm_c_im, ssm_d, ssm_w_out, nsa_w_in, nsa_cmp_k_pos, nsa_cmp_k_w1, nsa_cmp_k_w2, nsa_cmp_v_pos, nsa_cmp_v_w1, nsa_cmp_v_w2, nsa_w_out, peer_w_q, peer_sub_keys, peer_u, peer_v):
    bsz, l_all, d = x.shape
    assert bsz == 1
    depth = ada_w.shape[0]
    mod = _adaln(c, ada_w, ada_b)
    xt = x.reshape(l_all, d)
    nf = norm_final.reshape(1, d)
    for i in range(depth):
        sh1, sc1, g1, sh2, sc2, g2 = [mod[i, :, k * d:(k + 1) * d] for k in range(6)]
        j = i // 2
        gm = norm_mix[i].reshape(1, d)
        if i % 2 == 0:
            xt = _s5_layer(xt, gm, sh1, sc1, g1, ssm_w_in[j], ssm_lambda_re[j], ssm_lambda_im[j],
                           ssm_log_dt[j], ssm_b_re[j], ssm_b_im[j], ssm_c_re[j], ssm_c_im[j],
                           ssm_d[j], ssm_w_out[j])
        else:
            xt = _nsa_layer(xt, gm, sh1, sc1, g1, nsa_w_in[j], nsa_cmp_k_pos[j], nsa_cmp_k_w1[j],
                            nsa_cmp_k_w2[j], nsa_cmp_v_pos[j], nsa_cmp_v_w1[j], nsa_cmp_v_w2[j],
                            nsa_w_out[j])
        xt = _peer_layer(xt, norm_ffn[i].reshape(1, d), sh2, sc2, g2, peer_w_q[i], peer_sub_keys[i],
                         peer_u[i], peer_v[i], nf, final_norm=(i == depth - 1))
    return xt.reshape(bsz, l_all, d)
```

```python
import functools
import math

import numpy as np
import jax
import jax.numpy as jnp
from jax import lax
from jax.experimental import pallas as pl
from jax.experimental.pallas import tpu as pltpu

F32 = jnp.float32
BF16 = jnp.bfloat16

RMS_EPS = 1e-6

SSM_GROUP = 16
SSM_STATE = 64
SSM_BLOCK_GROUPS = 8
S5_SUBSEQ = 8
S5_CHUNK = 128
S5_LANE_BLOCK = 1024

NSA_HEADS = 16
NSA_HEAD_DIM = 128
NSA_KV_GROUPS = 4
NSA_REP = NSA_HEADS // NSA_KV_GROUPS
N_BRANCH = 3
CMP_LEN = 32
CMP_STRIDE = 16
SEL_LEN = 64
SEL_TOPN = 16
WIN = 512
Q_BLOCK = 256
SEL_KV_TILE = 512
MASK_PENALTY = -(2.0 ** 30)
SEL_DEN_LIMIT = 1e30
ONEHOT_PERIOD = 128

PEER_HEADS = 8
PEER_NKEYS = 128
PEER_TOPK = 16
PEER_HALF = 128
PEER_EXPERT_CHUNK = 1024
PEER_LAST_IS_EVEN = (PEER_NKEYS ** 2 // PEER_EXPERT_CHUNK) % 2 == 0
RANK_NONE = 255.0
RANK_CODE = 2.0 ** 100
RANK_CODE_STEPS = 32.0

VMEM_LIMIT = 56 * 1024 * 1024
NT_DIMS = (((1,), (1,)), ((), ()))


def _cparams(n_axes):
    return pltpu.CompilerParams(
        dimension_semantics=("arbitrary",) * n_axes, vmem_limit_bytes=VMEM_LIMIT)


def _norm_mod(x, g, sh, sc):
    ms = jnp.mean(x * x, axis=-1, keepdims=True)
    xn = x * lax.rsqrt(ms + RMS_EPS)
    return (xn * g) * (1.0 + sc) + sh


def _adaln_kernel(c_ref, w_ref, b_ref, o_ref):
    c = c_ref[...]
    cond = c * jax.nn.sigmoid(c)
    o_ref[0] = jnp.dot(cond.astype(BF16), w_ref[0].astype(BF16),
                       preferred_element_type=F32) + b_ref[0]


def _adaln(c, ada_w, ada_b):
    depth, d, n = ada_w.shape
    tn = 1024
    c8 = jnp.broadcast_to(c, (8, d))
    out = pl.pallas_call(
        _adaln_kernel,
        grid=(depth, n // tn),
        in_specs=[pl.BlockSpec((8, d), lambda i, j: (0, 0)),
                  pl.BlockSpec((1, d, tn), lambda i, j: (i, 0, j)),
                  pl.BlockSpec((1, 1, tn), lambda i, j: (i, 0, j))],
        out_specs=pl.BlockSpec((1, 8, tn), lambda i, j: (i, 0, j)),
        out_shape=jax.ShapeDtypeStruct((depth, 8, n), F32),
        compiler_params=_cparams(2),
    )(c8, ada_w, ada_b.reshape(depth, 1, n))
    return out[:, 0:1, :]


def _nmm_kernel(x_ref, g_ref, sh_ref, sc_ref, w_ref, cs_ref, o_ref, h_scr, *, act):
    @pl.when(pl.program_id(1) == 0)
    def _():
        h_scr[...] = _norm_mod(x_ref[...], g_ref[...], sh_ref[...], sc_ref[...]).astype(BF16)

    acc = jnp.dot(h_scr[...], w_ref[...], preferred_element_type=F32)
    if act == "sigmoid":
        acc = jax.nn.sigmoid(acc)
    else:
        acc = acc * cs_ref[...]
    o_ref[...] = acc.astype(o_ref.dtype)


def _norm_mod_matmul(x, g, sh, sc, w, colscale, out_dtype, act="scale", tm=1024, tn=1024):
    t, d = x.shape
    n = w.shape[1]
    tm, tn = min(tm, t), min(tn, n)
    row = lambda i, j: (0, 0)
    return pl.pallas_call(
        functools.partial(_nmm_kernel, act=act),
        grid=(t // tm, n // tn),
        in_specs=[pl.BlockSpec((tm, d), lambda i, j: (i, 0)),
                  pl.BlockSpec((1, d), row), pl.BlockSpec((1, d), row), pl.BlockSpec((1, d), row),
                  pl.BlockSpec((d, tn), lambda i, j: (0, j)),
                  pl.BlockSpec((1, tn), lambda i, j: (0, j))],
        out_specs=pl.BlockSpec((tm, tn), lambda i, j: (i, j)),
        out_shape=jax.ShapeDtypeStruct((t, n), out_dtype),
        scratch_shapes=[pltpu.VMEM((tm, d), BF16)],
        compiler_params=_cparams(2),
    )(x, g, sh, sc, w, colscale)


def _nsa_proj_kernel(x_ref, g_ref, sh_ref, sc_ref, w_ref, cs_ref, perm_ref, o_ref, o_cmp_ref, h_scr, *,
                     cmp_tile):
    j = pl.program_id(1)

    @pl.when(j == 0)
    def _():
        h_scr[...] = _norm_mod(x_ref[...], g_ref[...], sh_ref[...], sc_ref[...]).astype(BF16)

    out = (jnp.dot(h_scr[...], w_ref[...], preferred_element_type=F32) * cs_ref[...]).astype(o_ref.dtype)
    o_ref[...] = out

    @pl.when(j == cmp_tile)
    def _():
        moved = jnp.dot(perm_ref[...], out, preferred_element_type=F32).astype(o_cmp_ref.dtype)
        per = o_cmp_ref.shape[1]
        for l in range(o_cmp_ref.shape[0]):
            o_cmp_ref[l] = moved[l * per:(l + 1) * per, :]


def _nsa_projection(x, g, sh, sc, w, colscale, cmp_tile, tm=1024, tn=1024):
    t, d = x.shape
    n = w.shape[1]
    tm = min(tm, t)
    half = CMP_LEN // 2
    per = tm // half
    r = np.arange(tm)
    perm = np.zeros((tm, tm), np.float32)
    perm[r, (r % per) * half + r // per] = 1.0
    row = lambda i, j: (0, 0)
    return pl.pallas_call(
        functools.partial(_nsa_proj_kernel, cmp_tile=cmp_tile),
        grid=(t // tm, n // tn),
        in_specs=[pl.BlockSpec((tm, d), lambda i, j: (i, 0)),
                  pl.BlockSpec((1, d), row), pl.BlockSpec((1, d), row), pl.BlockSpec((1, d), row),
                  pl.BlockSpec((d, tn), lambda i, j: (0, j)),
                  pl.BlockSpec((1, tn), lambda i, j: (0, j)),
                  pl.BlockSpec((tm, tm), row)],
        out_specs=[pl.BlockSpec((tm, tn), lambda i, j: (i, j)),
                   pl.BlockSpec((half, per, tn), lambda i, j: (0, i, 0))],
        out_shape=[jax.ShapeDtypeStruct((t, n), BF16),
                   jax.ShapeDtypeStruct((half, t // half, tn), BF16)],
        scratch_shapes=[pltpu.VMEM((tm, d), BF16)],
        compiler_params=_cparams(2),
    )(x, g, sh, sc, w, colscale, jnp.asarray(perm, dtype=BF16))


def _mm_res_kernel(a_ref, w_ref, x_ref, g_ref, o_ref):
    y = jnp.dot(a_ref[...], w_ref[...], preferred_element_type=F32)
    o_ref[...] = x_ref[...] + g_ref[...] * y


def _mm_glu_res_kernel(a_ref, wa_ref, wb_ref, x_ref, g_ref, o_ref):
    a = a_ref[...]
    ya = jnp.dot(a, wa_ref[...], preferred_element_type=F32)
    yb = jnp.dot(a, wb_ref[...], preferred_element_type=F32)
    o_ref[...] = x_ref[...] + g_ref[...] * (ya * jax.nn.sigmoid(yb))


def _matmul_residual(a, w, x, gate, glu, tm=1024, tn=1024):
    t, k = a.shape
    n = x.shape[1]
    tm = min(tm, t)
    nb = n // tn
    a_spec = pl.BlockSpec((tm, k), lambda i, j: (i, 0))
    w_spec = pl.BlockSpec((k, tn), lambda i, j: (0, j))
    tail = [pl.BlockSpec((tm, tn), lambda i, j: (i, j)), pl.BlockSpec((1, tn), lambda i, j: (0, j))]
    if glu:
        kern = _mm_glu_res_kernel
        in_specs = [a_spec, w_spec, pl.BlockSpec((k, tn), lambda i, j: (0, j + nb))] + tail
        args = (a, w, w, x, gate)
    else:
        kern = _mm_res_kernel
        in_specs = [a_spec, w_spec] + tail
        args = (a, w, x, gate)
    return pl.pallas_call(
        kern,
        grid=(t // tm, nb),
        in_specs=in_specs,
        out_specs=pl.BlockSpec((tm, tn), lambda i, j: (i, j)),
        out_shape=jax.ShapeDtypeStruct((t, n), F32),
        compiler_params=_cparams(2),
    )(*args)


def _s5_param_kernel(lre_ref, lim_ref, ldt_ref, bre_ref, bim_ref, ar_ref, ai_ref, bbr_ref, bbi_ref):
    lre = lre_ref[...]
    lim = lim_ref[...]
    dt = jnp.exp(ldt_ref[...])
    mag = jnp.exp(lre * dt)
    ar = mag * jnp.cos(lim * dt)
    ai = mag * jnp.sin(lim * dt)
    den = lre * lre + lim * lim
    cr = ((ar - 1.0) * lre + ai * lim) / den
    ci = (ai * lre - (ar - 1.0) * lim) / den
    ar_ref[...] = ar
    ai_ref[...] = ai
    bbr_ref[...] = cr * bre_ref[...] - ci * bim_ref[...]
    bbi_ref[...] = cr * bim_ref[...] + ci * bre_ref[...]


def _s5_params(lam_re, lam_im, log_dt, b_re, b_im):
    g, n, p = b_re.shape
    rep = lambda a: jnp.repeat(a, p, axis=1)
    shp = jax.ShapeDtypeStruct((g, n * p), F32)
    ar, ai, bbr, bbi = pl.pallas_call(
        _s5_param_kernel, out_shape=(shp, shp, shp, shp),
    )(rep(lam_re), rep(lam_im), log_dt.reshape(g, 1), b_re.reshape(g, n * p), b_im.reshape(g, n * p))
    return ar[:, ::p], ai[:, ::p], bbr.reshape(g, n, p), bbi.reshape(g, n, p)


def _s5_scan_kernel(u_ref, a_ref, bbr_ref, bbi_ref, cr_ref, cin_ref, d_ref, o_ref,
                    bur, bui, hin_r, hin_i, car_r, car_i, apr, api):
    t, width = u_ref.shape
    sub = t // S5_SUBSEQ
    lanes = bur.shape[1]
    nblk = bbr_ref.shape[0]
    kin = width // nblk
    kst = lanes // nblk

    @pl.when(pl.program_id(0) == 0)
    def _init():
        car_r[...] = jnp.zeros_like(car_r)
        car_i[...] = jnp.zeros_like(car_i)
        ar = a_ref[0:1, :]
        ai = a_ref[1:2, :]
        pr, pi = ar, ai
        apr[0:1, :] = pr
        api[0:1, :] = pi
        for i in range(1, sub):
            pr, pi = pr * ar - pi * ai, pr * ai + pi * ar
            apr[i:i + 1, :] = pr
            api[i:i + 1, :] = pi

    for k in range(nblk):
        uk = u_ref[:, kin * k:kin * (k + 1)].astype(BF16)
        bur[:, kst * k:kst * (k + 1)] = jnp.dot(uk, bbr_ref[k], preferred_element_type=F32)
        bui[:, kst * k:kst * (k + 1)] = jnp.dot(uk, bbi_ref[k], preferred_element_type=F32)

    lb = S5_LANE_BLOCK
    for b in range(lanes // lb):
        sl = slice(b * lb, (b + 1) * lb)
        ar = jnp.broadcast_to(a_ref[0:1, sl], (S5_SUBSEQ, lb))
        ai = jnp.broadcast_to(a_ref[1:2, sl], (S5_SUBSEQ, lb))
        sr = jnp.zeros((S5_SUBSEQ, lb), F32)
        si = jnp.zeros((S5_SUBSEQ, lb), F32)
        for i in range(sub):
            rows = slice(S5_SUBSEQ * i, S5_SUBSEQ * (i + 1))
            sr, si = (ar * sr - ai * si + bur[rows, sl], ar * si + ai * sr + bui[rows, sl])
            bur[rows, sl] = sr
            bui[rows, sl] = si
        asr = apr[sub - 1:sub, sl]
        asi = api[sub - 1:sub, sl]
        hr = car_r[0:1, sl]
        hi = car_i[0:1, sl]
        for j in range(S5_SUBSEQ):
            hin_r[j:j + 1, sl] = hr
            hin_i[j:j + 1, sl] = hi
            er = sr[j:j + 1, :]
            ei = si[j:j + 1, :]
            hr, hi = er + asr * hr - asi * hi, ei + asr * hi + asi * hr
        car_r[0:1, sl] = hr
        car_i[0:1, sl] = hi
        hinr = hin_r[:, sl]
        hini = hin_i[:, sl]
        for i in range(sub):
            rows = slice(S5_SUBSEQ * i, S5_SUBSEQ * (i + 1))
            pr = jnp.broadcast_to(apr[i:i + 1, sl], (S5_SUBSEQ, lb))
            pi = jnp.broadcast_to(api[i:i + 1, sl], (S5_SUBSEQ, lb))
            bur[rows, sl] = bur[rows, sl] + pr * hinr - pi * hini
            bui[rows, sl] = bui[rows, sl] + pr * hini + pi * hinr

    for k in range(nblk):
        sr = bur[:, kst * k:kst * (k + 1)].astype(BF16)
        si = bui[:, kst * k:kst * (k + 1)].astype(BF16)
        y = (jnp.dot(sr, cr_ref[k], preferred_element_type=F32)
             + jnp.dot(si, cin_ref[k], preferred_element_type=F32))
        cols = slice(kin * k, kin * (k + 1))
        y = y + d_ref[:, cols] * u_ref[:, cols]
        o_ref[:, cols] = jax.nn.gelu(y).astype(o_ref.dtype)


def _s5_scan(u_p, ar, ai, bb_r, bb_i, c_re, c_im, d_skip):
    t_all, width = u_p.shape
    g, n, p = bb_r.shape
    bg = SSM_BLOCK_GROUPS
    nblk = g // bg
    eye = jnp.eye(bg, dtype=F32)
    blk_b = lambda bb: jnp.einsum('kgnp,gh->kgphn', bb.reshape(nblk, bg, n, p), eye
                                  ).reshape(nblk, bg * p, bg * n).astype(BF16)
    blk_c = lambda cc: jnp.einsum('kgpn,gh->kgnhp', cc.reshape(nblk, bg, p, n), eye
                                  ).reshape(nblk, bg * n, bg * p).astype(BF16)
    lanes = g * n
    a8 = jnp.zeros((8, lanes), F32).at[0].set(ar.reshape(lanes)).at[1].set(ai.reshape(lanes))
    t = S5_CHUNK
    sub = t // S5_SUBSEQ
    full3 = lambda c: (0, 0, 0)
    wspec_b = pl.BlockSpec((nblk, bg * p, bg * n), full3)
    wspec_c = pl.BlockSpec((nblk, bg * n, bg * p), full3)
    return pl.pallas_call(
        _s5_scan_kernel,
        grid=(t_all // t,),
        in_specs=[pl.BlockSpec((t, width), lambda c: (c, 0)),
                  pl.BlockSpec((8, lanes), lambda c: (0, 0)),
                  wspec_b, wspec_b, wspec_c, wspec_c,
                  pl.BlockSpec((1, width), lambda c: (0, 0))],
        out_specs=pl.BlockSpec((t, width), lambda c: (c, 0)),
        out_shape=jax.ShapeDtypeStruct((t_all, width), BF16),
        scratch_shapes=[pltpu.VMEM((t, lanes), F32), pltpu.VMEM((t, lanes), F32),
                        pltpu.VMEM((8, lanes), F32), pltpu.VMEM((8, lanes), F32),
                        pltpu.VMEM((8, lanes), F32), pltpu.VMEM((8, lanes), F32),
                        pltpu.VMEM((sub, lanes), F32), pltpu.VMEM((sub, lanes), F32)],
        compiler_params=_cparams(1),
    )(u_p, a8, blk_b(bb_r), blk_b(bb_i), blk_c(c_re), blk_c(-c_im), d_skip.reshape(1, width))


def _s5_rows_to_subseq(x, inverse=False):
    t_all, d = x.shape
    sub = S5_CHUNK // S5_SUBSEQ
    shape = (t_all // S5_CHUNK, sub, S5_SUBSEQ, d) if inverse else (t_all // S5_CHUNK, S5_SUBSEQ, sub, d)
    return x.reshape(shape).transpose(0, 2, 1, 3).reshape(t_all, d)


def _s5_layer(x, g, sh, sc, gate, w_in, lam_re, lam_im, log_dt, b_re, b_im, c_re, c_im, d_skip, w_out):
    d = x.shape[1]
    x_p = _s5_rows_to_subseq(x)
    ones = jnp.ones((1, w_in.shape[1]), F32)
    u_p = _norm_mod_matmul(x_p, g, sh, sc, w_in.astype(BF16), ones, F32)
    ar, ai, bb_r, bb_i = _s5_params(lam_re, lam_im, log_dt, b_re, b_im)
    gy_p = _s5_scan(u_p, ar, ai, bb_r, bb_i, c_re, c_im, d_skip)
    xn_p = _matmul_residual(gy_p, w_out.astype(BF16), x_p, gate, glu=True)
    return _s5_rows_to_subseq(xn_p, inverse=True)


def _cmp_kernel(x_ref, w1a_ref, w1b_ref, pos_ref, w1_ref, w2_ref, o_ref, acc_a, acc_b):
    l = pl.program_id(2)

    @pl.when(l == 0)
    def _():
        acc_a[...] = jnp.zeros_like(acc_a)
        acc_b[...] = jnp.zeros_like(acc_b)

    x = x_ref[0]
    acc_a[...] += jnp.dot(x, w1a_ref[0, 0], preferred_element_type=F32)
    acc_b[...] += jnp.dot(x, w1b_ref[0, 0], preferred_element_type=F32)

    @pl.when(l == pl.num_programs(2) - 1)
    def _():
        m = acc_a.shape[0]
        posc = jnp.zeros((8, NSA_HEAD_DIM), F32)
        for ll in range(CMP_LEN):
            prow = jnp.broadcast_to(pos_ref[0, ll:ll + 1, :], (8, NSA_HEAD_DIM)).astype(BF16)
            posc = posc + jnp.dot(prow, w1_ref[0, ll], preferred_element_type=F32)
        pre = acc_a[...] + pltpu.roll(acc_b[...], m - 1, 0) + posc[0:1, :]
        hid = jax.nn.gelu(pre)
        o_ref[0, 0] = jnp.dot(hid.astype(BF16), w2_ref[0], preferred_element_type=F32).astype(o_ref.dtype)


def _compress(kv_by_pos, pos, w1, w2):
    half, m, _ = kv_by_pos.shape
    g = NSA_KV_GROUPS
    dh = NSA_HEAD_DIM
    return pl.pallas_call(
        _cmp_kernel,
        grid=(2, g, half),
        in_specs=[pl.BlockSpec((1, m, dh), lambda s, gg, l: (l, 0, g * s + gg)),
                  pl.BlockSpec((1, 1, dh, dh), lambda s, gg, l: (s, l, 0, 0)),
                  pl.BlockSpec((1, 1, dh, dh), lambda s, gg, l: (s, l + half, 0, 0)),
                  pl.BlockSpec((1, CMP_LEN, dh), lambda s, gg, l: (s, 0, 0)),
                  pl.BlockSpec((1, CMP_LEN, dh, dh), lambda s, gg, l: (s, 0, 0, 0)),
                  pl.BlockSpec((1, dh, dh), lambda s, gg, l: (s, 0, 0))],
        out_specs=pl.BlockSpec((1, 1, m, dh), lambda s, gg, l: (s, gg, 0, 0)),
        out_shape=jax.ShapeDtypeStruct((2, g, m, dh), BF16),
        scratch_shapes=[pltpu.VMEM((m, dh), F32), pltpu.VMEM((m, dh), F32)],
        compiler_params=_cparams(3),
    )(kv_by_pos, w1, w1, pos, w1, w2)


def _softmax2_rows(s, mask):
    s = jnp.where(mask, s, -1e30)
    m = jnp.max(s, axis=-1, keepdims=True)
    p = jnp.exp2(s - m)
    return p, 1.0 / jnp.maximum(jnp.sum(p, axis=-1, keepdims=True), 1e-30)


def _shifted_softmax(s, mask, shift):
    p = jnp.exp2(jnp.where(mask, s - shift, -1e30))
    return p, jnp.sum(p, axis=-1, keepdims=True)


def _any_out_of_range(den, needed):
    ok = (den > 0.0) & (den < SEL_DEN_LIMIT)
    return jnp.max(jnp.where(needed & jnp.logical_not(ok), 1.0, 0.0)) > 0.0


def _nsa_attn_kernel(*refs):
    redo = _nsa_attn_body(True, *refs)

    @pl.when(redo)
    def _():
        _nsa_attn_body(False, *refs)


def _nsa_attn_body(fast, q_ref, gt_ref, kc_ref, vc_ref, agg_ref, ksa_ref, vsa_ref, kw_ref, vw_ref,
                   o_ref, qaug_scr, s_a, s_b, acc_scr, m_scr, shift_scr):
    b = pl.program_id(1)
    qb = q_ref.shape[0]
    dh = NSA_HEAD_DIM
    rep = NSA_REP
    rows = rep * qb
    n_cmp = kc_ref.shape[2]
    n_sel = agg_ref.shape[1]
    top_n = min(SEL_TOPN, n_sel)
    t0 = pl.multiple_of(b * qb, qb)

    q_blk = q_ref[...]
    qs = jnp.concatenate([q_blk[:, r * dh:(r + 1) * dh] for r in range(rep)], axis=0)
    tpos = t0 + lax.broadcasted_iota(jnp.int32, (rows, 1), 0) % qb
    all_rows = jnp.ones((rows, 1), jnp.bool_)

    kc = kc_ref[0, 0]
    s_c = lax.dot_general(qs, kc, NT_DIMS, preferred_element_type=F32)
    cmp_end = lax.broadcasted_iota(jnp.int32, (1, n_cmp), 1) * CMP_STRIDE + (CMP_LEN - 1)
    has_c = tpos >= CMP_LEN - 1
    if fast:
        p_c, den_c = _shifted_softmax(s_c, cmp_end <= tpos, s_c[:, 0:1])
        redo = _any_out_of_range(den_c, has_c)
        inv_c = 1.0 / jnp.maximum(den_c, 1e-30)
    else:
        p_c, inv_c = _softmax2_rows(s_c, cmp_end <= tpos)
    p_c = p_c * jnp.where(has_c, inv_c, 0.0)
    o_c = jnp.dot(p_c.astype(BF16), vc_ref[0, 0], preferred_element_type=F32)

    psum = p_c[0:qb]
    for r in range(1, rep):
        psum = psum + p_c[r * qb:(r + 1) * qb]
    p_hi = psum.astype(BF16)
    p_lo = (psum - p_hi.astype(F32)).astype(BF16)
    agg = agg_ref[...]
    imp = (jnp.dot(p_hi, agg, preferred_element_type=F32)
           + jnp.dot(p_lo, agg, preferred_element_type=F32))
    imp_t = imp.T
    tq = t0 + lax.broadcasted_iota(jnp.int32, (1, qb), 1)
    blk = lax.broadcasted_iota(jnp.int32, (n_sel, qb), 0)
    blk_f = blk.astype(F32)
    cur = tq // SEL_LEN
    valid = blk * SEL_LEN <= tq
    if fast:
        score = jnp.where(valid, imp_t, -1.0 - blk_f)
        score = jnp.where(blk == cur - 1, 1e9, score)
        score = jnp.where(blk == cur, 2e9, score)
        score = jnp.where(blk == 0, 3e9, score)
    else:
        forced = (blk == 0) | (blk == cur) | (blk == cur - 1)
        score = jnp.where(forced, 1e9, jnp.where(valid, imp_t, -1e9))
    for _ in range(top_n):
        mx = jnp.max(score, axis=0, keepdims=True)
        hit = score == mx
        if not fast:
            idx = jnp.min(jnp.where(hit, blk_f, float(n_sel)), axis=0, keepdims=True)
            hit = blk_f == idx
        score = jnp.where(hit, -jnp.inf, score)
    took = score == -jnp.inf
    if fast:
        taken = jnp.sum(jnp.where(took, 1.0, 0.0), axis=0, keepdims=True)
        redo = redo | (jnp.max(jnp.where(taken != float(top_n), 1.0, 0.0)) > 0.0)
    pen = jnp.where(took, 0.0, MASK_PENALTY).T.astype(BF16)
    n_half = qaug_scr.shape[0]
    for hh in range(n_half):
        if n_sel >= ONEHOT_PERIOD:
            ph = pen[:, hh * ONEHOT_PERIOD:(hh + 1) * ONEHOT_PERIOD]
        else:
            ph = jnp.concatenate(
                [pen, jnp.full((qb, ONEHOT_PERIOD - n_sel), MASK_PENALTY, BF16)], axis=1)
        qaug_scr[hh] = jnp.concatenate([qs, jnp.concatenate([ph] * rep, axis=0)], axis=1)

    kt = SEL_KV_TILE
    period_keys = ONEHOT_PERIOD * SEL_LEN

    def sel_scores(i, s_out):
        k0 = pl.multiple_of(i * kt, kt)
        s_out[...] = lax.dot_general(qaug_scr[k0 // period_keys], ksa_ref[pl.ds(k0, kt), :], NT_DIMS,
                                     preferred_element_type=F32)

    def causal_mask(i, s):
        kpos = i * kt + lax.broadcasted_iota(jnp.int32, (1, kt), 1)
        return jnp.where(kpos <= tpos, s, MASK_PENALTY)

    def update_running_max(i, s_in, causal):
        k0 = pl.multiple_of(i * kt, kt)
        s = causal_mask(i, s_in[...]) if causal else s_in[...]
        m_run = m_scr[...]
        m_new = jnp.maximum(m_run, jnp.max(s, axis=-1, keepdims=True))
        p = jnp.exp2(s - m_new)
        acc_scr[...] = (jnp.exp2(m_run - m_new) * acc_scr[...]
                        + jnp.dot(p.astype(BF16), vsa_ref[pl.ds(k0, kt), :], preferred_element_type=F32))
        m_scr[...] = m_new

    def update_fixed_shift(i, s_in, causal):
        k0 = pl.multiple_of(i * kt, kt)
        s = causal_mask(i, s_in[...]) if causal else s_in[...]
        shift = shift_scr[...]
        p = jnp.concatenate([jnp.exp2(s[:, c * dh:(c + 1) * dh] - shift).astype(BF16)
                             for c in range(kt // dh)], axis=1)
        acc_scr[...] += jnp.dot(p, vsa_ref[pl.ds(k0, kt), :], preferred_element_type=F32)

    def sweep(update):
        def pair(j, carry):
            sel_scores(2 * j + 1, s_b)
            update(2 * j, s_a, False)
            sel_scores(2 * j + 2, s_a)
            update(2 * j + 1, s_b, False)
            return carry

        acc_scr[...] = jnp.zeros_like(acc_scr)
        lax.fori_loop(0, n_full // 2, pair, 0)
        odd = n_full % 2 == 1

        @pl.when(odd)
        def _():
            sel_scores(n_full, s_b)
            update(n_full - 1, s_a, False)
            update(n_full, s_b, True)

        @pl.when(jnp.logical_not(odd))
        def _():
            update(n_full, s_a, True)

    n_full = t0 // kt
    sel_scores(0, s_a)
    if fast:
        first = jnp.max(causal_mask(0, s_a[...]), axis=-1, keepdims=True)
        shift_scr[...] = jnp.broadcast_to(first, shift_scr.shape)
        sweep(update_fixed_shift)
    else:
        m_scr[...] = jnp.full(m_scr.shape, -1e30, F32)
        sweep(update_running_max)
    acc_s = acc_scr[...]
    den_s = acc_s[:, dh:dh + 1]
    o_s = acc_s[:, 0:dh] * (1.0 / den_s)

    wlen = WIN + qb
    w0 = pl.multiple_of(jnp.maximum(t0 - WIN, 0), qb)
    s_w = lax.dot_general(qs, kw_ref[pl.ds(w0, wlen), :], NT_DIMS, preferred_element_type=F32)
    diff = tpos - (w0 + lax.broadcasted_iota(jnp.int32, (1, wlen), 1))
    mask_w = (diff >= 0) & (diff < WIN)
    if fast:
        k_diag = jnp.concatenate([kw_ref[pl.ds(t0, qb), :]] * rep, axis=0).astype(F32)
        diag = jnp.sum(qs.astype(F32) * k_diag, axis=-1, keepdims=True)
        p_w, den_w = _shifted_softmax(s_w, mask_w, diag)
        o_w = jnp.dot(p_w.astype(BF16), vw_ref[pl.ds(w0, wlen), :], preferred_element_type=F32)
        o_w = o_w * (1.0 / den_w)
        redo = redo | _any_out_of_range(den_s, all_rows) | _any_out_of_range(den_w, all_rows)
    else:
        p_w, inv_w = _softmax2_rows(s_w, mask_w)
        o_w = jnp.dot((p_w * inv_w).astype(BF16), vw_ref[pl.ds(w0, wlen), :], preferred_element_type=F32)

    gt = gt_ref[...]
    for r in range(rep):
        rs = slice(r * qb, (r + 1) * qb)
        o = (gt[:, r:r + 1] * o_c[rs]
             + gt[:, rep + r:rep + r + 1] * o_s[rs]
             + gt[:, 2 * rep + r:2 * rep + r + 1] * o_w[rs])
        o_ref[:, r * dh:(r + 1) * dh] = o.astype(o_ref.dtype)
    return redo if fast else None


def _nsa_agg(n_cmp_pad, n_cmp, n_sel):
    ratio, span = SEL_LEN // CMP_STRIDE, CMP_LEN // CMP_STRIDE
    agg = np.zeros((n_cmp_pad, n_sel), np.float32)
    jj = np.arange(n_sel)
    for m in range(ratio):
        for n in range(span):
            ii = ratio * jj + m - n
            ok = (ii >= 0) & (ii < n_cmp)
            agg[ii[ok], jj[ok]] += 1.0
    return jnp.asarray(agg, dtype=BF16)


def _nsa_attention(qkv, gates, kv_cmp):
    l_all = qkv.shape[0]
    dh = NSA_HEAD_DIM
    g = NSA_KV_GROUPS
    qb = Q_BLOCK
    qw = NSA_REP * dh
    n_cmp_pad = l_all // CMP_STRIDE
    n_cmp = (l_all - CMP_LEN) // CMP_STRIDE + 1
    n_sel = l_all // SEL_LEN
    agg = _nsa_agg(n_cmp_pad, n_cmp, n_sel)
    n_half = max(n_sel // ONEHOT_PERIOD, 1)
    q_blocks = (NSA_HEADS * dh) // dh
    key = np.arange(l_all)
    onehot = jnp.asarray((key[:, None] // SEL_LEN) % ONEHOT_PERIOD == np.arange(ONEHOT_PERIOD)[None, :],
                         dtype=BF16)
    ones_col = jnp.asarray(np.arange(dh)[None, :] == 0, dtype=BF16)
    grouped = lambda base: qkv[:, (q_blocks + base) * dh:(q_blocks + base + g) * dh].reshape(l_all, g, dh)
    ks_aug = jnp.concatenate([grouped(2 * g), jnp.broadcast_to(onehot[:, None, :], (l_all, g, ONEHOT_PERIOD))],
                             axis=-1).reshape(l_all, g * (dh + ONEHOT_PERIOD))
    vs_aug = jnp.concatenate([grouped(3 * g), jnp.broadcast_to(ones_col[:, None, :], (l_all, g, dh))],
                             axis=-1).reshape(l_all, g * 2 * dh)
    once = dict(pipeline_mode=pl.Buffered(1))
    kv_spec = lambda base: pl.BlockSpec((l_all, dh), lambda gg, b: (0, q_blocks + base + gg), **once)
    aug_spec = pl.BlockSpec((l_all, 2 * dh), lambda gg, b: (0, gg), **once)
    rows = NSA_REP * qb
    return pl.pallas_call(
        _nsa_attn_kernel,
        grid=(g, l_all // qb),
        in_specs=[pl.BlockSpec((qb, qw), lambda gg, b: (b, gg)),
                  pl.BlockSpec((qb, dh), lambda gg, b: (b, gg)),
                  pl.BlockSpec((1, 1, n_cmp_pad, dh), lambda gg, b: (0, gg, 0, 0)),
                  pl.BlockSpec((1, 1, n_cmp_pad, dh), lambda gg, b: (1, gg, 0, 0)),
                  pl.BlockSpec((n_cmp_pad, n_sel), lambda gg, b: (0, 0)),
                  aug_spec, aug_spec, kv_spec(4 * g), kv_spec(5 * g)],
        out_specs=pl.BlockSpec((qb, qw), lambda gg, b: (b, gg)),
        out_shape=jax.ShapeDtypeStruct((l_all, NSA_HEADS * dh), BF16),
        scratch_shapes=[pltpu.VMEM((n_half, rows, dh + ONEHOT_PERIOD), BF16),
                        pltpu.VMEM((rows, SEL_KV_TILE), F32), pltpu.VMEM((rows, SEL_KV_TILE), F32),
                        pltpu.VMEM((rows, 2 * dh), F32), pltpu.VMEM((rows, 1), F32),
                        pltpu.VMEM((rows, dh), F32)],
        compiler_params=_cparams(2),
    )(qkv, gates, kv_cmp, kv_cmp, agg, ks_aug, vs_aug, qkv, qkv)


def _nsa_layer(x, g, sh, sc, gate, w_in, k_pos, k_w1, k_w2, v_pos, v_w1, v_w2, w_out):
    dh = NSA_HEAD_DIM
    qd = NSA_HEADS * dh
    kvd = 2 * N_BRANCH * NSA_KV_GROUPS * dh
    w_main = w_in[:, :qd + kvd].astype(BF16)
    colscale = jnp.concatenate([jnp.full((1, qd), dh ** -0.5 * math.log2(math.e), F32),
                                jnp.ones((1, kvd), F32)], axis=1)
    tn = 2 * NSA_KV_GROUPS * dh
    qkv, kv_by_pos = _nsa_projection(x, g, sh, sc, w_main, colscale, cmp_tile=qd // tn, tn=tn)
    wg = w_in[:, qd + kvd:].reshape(-1, NSA_KV_GROUPS, NSA_REP, N_BRANCH).transpose(0, 1, 3, 2)
    wg = wg.reshape(-1, NSA_KV_GROUPS, N_BRANCH * NSA_REP)
    wg = jnp.pad(wg, ((0, 0), (0, 0), (0, dh - N_BRANCH * NSA_REP))).reshape(-1, NSA_KV_GROUPS * dh)
    gates = _norm_mod_matmul(x, g, sh, sc, wg.astype(BF16), jnp.ones((1, wg.shape[1]), F32), F32,
                             act="sigmoid")
    kv_cmp = _compress(kv_by_pos, jnp.stack([k_pos, v_pos]),
                       jnp.stack([k_w1, v_w1]).astype(BF16), jnp.stack([k_w2, v_w2]).astype(BF16))
    o = _nsa_attention(qkv, gates, kv_cmp)
    return _matmul_residual(o, w_out.astype(BF16), x, gate, glu=False)


def _peer_query_kernel(x_ref, g_ref, sh_ref, sc_ref, wq_ref, keys_ref, ht_ref, st_ref):
    h32 = _norm_mod(x_ref[...], g_ref[...], sh_ref[...], sc_ref[...])
    ht_ref[...] = h32.T.astype(BF16)
    q = jnp.dot(h32.astype(BF16), wq_ref[...], preferred_element_type=F32).astype(BF16)
    for hc in range(keys_ref.shape[0]):
        rows = slice(hc * PEER_NKEYS, (hc + 1) * PEER_NKEYS)
        st_ref[rows, :] = lax.dot_general(keys_ref[hc], q[:, hc * PEER_HALF:(hc + 1) * PEER_HALF],
                                          NT_DIMS, preferred_element_type=F32)


def _peer_query(x, g, sh, sc, w_q, sub_keys, tm=512):
    t, d = x.shape
    nq = w_q.shape[1]
    keys = sub_keys.reshape(-1, PEER_NKEYS, PEER_HALF).astype(BF16)
    row = lambda i: (0, 0)
    return pl.pallas_call(
        _peer_query_kernel,
        grid=(t // tm,),
        in_specs=[pl.BlockSpec((tm, d), lambda i: (i, 0)),
                  pl.BlockSpec((1, d), row), pl.BlockSpec((1, d), row), pl.BlockSpec((1, d), row),
                  pl.BlockSpec((d, nq), row),
                  pl.BlockSpec(keys.shape, lambda i: (0, 0, 0))],
        out_specs=[pl.BlockSpec((d, tm), lambda i: (0, i)),
                   pl.BlockSpec((keys.shape[0] * PEER_NKEYS, tm), lambda i: (0, i))],
        out_shape=[jax.ShapeDtypeStruct((d, t), BF16),
                   jax.ShapeDtypeStruct((keys.shape[0] * PEER_NKEYS, t), F32)],
        compiler_params=_cparams(1),
    )(x, g, sh, sc, w_q.astype(BF16), keys)


def _peer_cells():
    return [(a, b) for a in range(PEER_TOPK) for b in range(PEER_TOPK) if (a + 1) * (b + 1) <= PEER_TOPK]


def _take_max(cur, iota, exact):
    v = jnp.max(cur, axis=0, keepdims=True)
    hit = cur == v
    if exact:
        idx = jnp.min(jnp.where(hit, iota, cur.shape[0]), axis=0, keepdims=True)
        hit = iota == idx
    return v, hit


def _top_ranks(s, iota_k, exact):
    vals = []
    cur = s
    if exact:
        rank = jnp.full(s.shape, RANK_NONE, F32)
        for a in range(PEER_TOPK):
            v, hit = _take_max(cur, iota_k, exact)
            rank = jnp.where(hit, float(a), rank)
            cur = jnp.where(hit, -jnp.inf, cur)
            vals.append(v)
        return vals, rank
    for a in range(PEER_TOPK):
        v, hit = _take_max(cur, iota_k, exact)
        cur = jnp.where(hit, -RANK_CODE * (1.0 + a / RANK_CODE_STEPS), cur)
        vals.append(v)
    rank = jnp.where(cur <= -RANK_CODE, (cur * (-1.0 / RANK_CODE) - 1.0) * RANK_CODE_STEPS, RANK_NONE)
    return vals, rank


def _peer_route_body(st_ref, seg_ref, r2_ref, ln_ref, g1_ref, g2_ref, exact):
    tn = st_ref.shape[1]
    nk = PEER_NKEYS
    cells = _peer_cells()
    n_cell = len(cells)
    n_pad = -(-n_cell // 8) * 8
    n_seg = seg_ref.shape[1]
    iota_k = lax.broadcasted_iota(jnp.int32, (nk, tn), 0)
    iota_c = lax.broadcasted_iota(jnp.int32, (n_pad, tn), 0)
    count = lambda m: jnp.sum(m.astype(F32), axis=0, keepdims=True)
    tied = jnp.zeros((1, tn), jnp.bool_)
    for h in range(PEER_HEADS):
        s1 = st_ref[(2 * h) * nk:(2 * h + 1) * nk, :]
        s2 = st_ref[(2 * h + 1) * nk:(2 * h + 2) * nk, :]
        v1, rank1 = _top_ranks(s1, iota_k, exact)
        v2, rank2 = _top_ranks(s2, iota_k, exact)
        cand = jnp.concatenate([v1[a] + v2[b] for a, b in cells]
                               + [jnp.full((n_pad - n_cell, tn), -jnp.inf, F32)], axis=0)
        top = v1[0] + v2[0]
        e_c = jnp.exp(cand - top)
        chosen = jnp.zeros((n_pad, tn), jnp.bool_)
        cur = cand
        for _ in range(PEER_TOPK):
            _, hit = _take_max(cur, iota_c, exact)
            chosen = chosen | hit
            cur = jnp.where(hit, -jnp.inf, cur)
        chosen_f = chosen.astype(F32)
        if not exact:
            k = float(PEER_TOPK)
            tied = (tied | (count(rank1 != RANK_NONE) != k) | (count(rank2 != RANK_NONE) != k)
                    | (jnp.sum(chosen_f, axis=0, keepdims=True) != k))
        z = jnp.sum(chosen_f * e_c, axis=0, keepdims=True)
        chosen_pad = jnp.concatenate([chosen_f, jnp.zeros((n_seg - n_pad, tn), F32)], axis=0)
        rowlen = jnp.dot(seg_ref[...], chosen_pad.astype(BF16), preferred_element_type=F32)
        ln = jnp.zeros((nk, tn), F32)
        for a in range(PEER_TOPK):
            ln = jnp.where(rank1 == float(a), rowlen[a:a + 1, :], ln)
        rows = slice(h * nk, (h + 1) * nk)
        r2_ref[rows, :] = rank2.astype(r2_ref.dtype)
        ln_ref[h] = ln
        g1_ref[h] = jnp.exp(s1 - v1[0]) / z
        g2_ref[rows, :] = jnp.exp(s2 - v2[0]).astype(g2_ref.dtype)
    return tied


def _peer_route_kernel(st_ref, seg_ref, r2_ref, ln_ref, g1_ref, g2_ref):
    refs = (st_ref, seg_ref, r2_ref, ln_ref, g1_ref, g2_ref)
    tied = _peer_route_body(*refs, exact=False)

    @pl.when(jnp.max(tied.astype(F32)) > 0.0)
    def _():
        _peer_route_body(*refs, exact=True)


def _peer_route(st, tn=128):
    n_rows, t = st.shape
    cells = _peer_cells()
    seg = np.zeros((PEER_TOPK, PEER_NKEYS), np.float32)
    for c, (a, _) in enumerate(cells):
        seg[a, c] = 1.0
    out_rows = PEER_HEADS * PEER_NKEYS
    shp = jax.ShapeDtypeStruct((out_rows, t), BF16)
    shp_row = jax.ShapeDtypeStruct((PEER_HEADS, PEER_NKEYS, t), F32)
    spec = pl.BlockSpec((out_rows, tn), lambda i: (0, i))
    spec_row = pl.BlockSpec((PEER_HEADS, PEER_NKEYS, tn), lambda i: (0, 0, i))
    return pl.pallas_call(
        _peer_route_kernel,
        grid=(t // tn,),
        in_specs=[pl.BlockSpec((n_rows, tn), lambda i: (0, i)),
                  pl.BlockSpec((PEER_TOPK, PEER_NKEYS), lambda i: (0, 0))],
        out_specs=[spec, spec_row, spec_row, spec],
        out_shape=[shp, shp_row, shp_row, shp],
        compiler_params=_cparams(1),
    )(st, jnp.asarray(seg, dtype=BF16))


def _peer_expert_kernel(h_ref, u_ref, vt_ref, r2_ref, ln_ref, g1_ref, g2_ref, x_ref, gate_ref, nf_ref,
                        o_ref, acc_t, p_scr, pre_a, pre_b, *, final_norm):
    c = pl.program_id(1)
    last = pl.num_programs(1) - 1
    ec = u_ref.shape[0]
    nk = PEER_NKEYS
    tm = h_ref.shape[1]
    zero = jnp.zeros((), BF16)

    def score(pre_out):
        pre_out[...] = jnp.dot(u_ref[...], h_ref[...], preferred_element_type=F32)

    def finish(pre_in):
        for ii in range(ec // nk):
            w = jnp.zeros((nk, tm), BF16)
            for h in range(PEER_HEADS):
                rows = slice(h * nk, (h + 1) * nk)
                ln_row = ln_ref[h, ii:ii + 1, :].astype(BF16)
                g1_row = g1_ref[h, ii:ii + 1, :].astype(BF16)
                w = w + jnp.where(r2_ref[rows, :] < ln_row, g2_ref[rows, :], zero) * g1_row
            act = jax.nn.gelu(pre_in[ii * nk:(ii + 1) * nk, :])
            p_scr[ii * nk:(ii + 1) * nk, :] = w * act.astype(BF16)
        acc_t[...] += jnp.dot(vt_ref[0], p_scr[...], preferred_element_type=F32)

    @pl.when(c == 0)
    def _():
        acc_t[...] = jnp.zeros_like(acc_t)
        score(pre_a)

    @pl.when((c > 0) & (c < last) & (c % 2 == 1))
    def _():
        score(pre_b)
        finish(pre_a)

    @pl.when((c > 0) & (c < last) & (c % 2 == 0))
    def _():
        score(pre_a)
        finish(pre_b)

    @pl.when(c == last)
    def _():
        finish(pre_b if PEER_LAST_IS_EVEN else pre_a)
        xo = x_ref[...] + gate_ref[...] * acc_t[...].T
        if final_norm:
            ms = jnp.mean(xo * xo, axis=-1, keepdims=True)
            xo = (xo * lax.rsqrt(ms + RMS_EPS)) * nf_ref[...]
        o_ref[...] = xo


def _peer_experts(ht, u_bf, vt_bf, route, x, gate, norm_final, final_norm, tm=512, ec=PEER_EXPERT_CHUNK):
    t, d = x.shape
    e = u_bf.shape[0]
    n_chunks = e // ec
    assert (n_chunks % 2 == 0) == PEER_LAST_IS_EVEN
    r2, ln, g1, g2 = route
    rspec = pl.BlockSpec((r2.shape[0], tm), lambda i, c: (0, i))
    prev = lambda c: jnp.maximum(c - 1, 0)
    kspec = pl.BlockSpec((PEER_HEADS, ec // PEER_NKEYS, tm), lambda i, c: (0, prev(c), i))
    row = lambda i, c: (0, 0)
    return pl.pallas_call(
        functools.partial(_peer_expert_kernel, final_norm=final_norm),
        grid=(t // tm, n_chunks + 1),
        in_specs=[pl.BlockSpec((d, tm), lambda i, c: (0, i)),
                  pl.BlockSpec((ec, d), lambda i, c: (jnp.minimum(c, n_chunks - 1), 0)),
                  pl.BlockSpec((1, d, ec), lambda i, c: (prev(c), 0, 0)),
                  rspec, kspec, kspec, rspec,
                  pl.BlockSpec((tm, d), lambda i, c: (i, 0)),
                  pl.BlockSpec((1, d), row), pl.BlockSpec((1, d), row)],
        out_specs=pl.BlockSpec((tm, d), lambda i, c: (i, 0)),
        out_shape=jax.ShapeDtypeStruct((t, d), F32),
        scratch_shapes=[pltpu.VMEM((d, tm), F32), pltpu.VMEM((ec, tm), BF16),
                        pltpu.VMEM((ec, tm), F32), pltpu.VMEM((ec, tm), F32)],
        compiler_params=_cparams(2),
    )(ht, u_bf, vt_bf, r2, ln, g1, g2, x, gate, norm_final)


def _peer_layer(x, g, sh, sc, gate, w_q, sub_keys, u_tab, v_tab, norm_final, final_norm):
    ht, st = _peer_query(x, g, sh, sc, w_q, sub_keys)
    route = _peer_route(st)
    e, d = v_tab.shape
    vt = v_tab.astype(BF16).reshape(e // PEER_EXPERT_CHUNK, PEER_EXPERT_CHUNK, d).transpose(0, 2, 1)
    return _peer_experts(ht, u_tab.astype(BF16), vt, route, x, gate, norm_final, final_norm)


def kernel(x, c, ada_w, ada_b, norm_mix, norm_ffn, norm_final, ssm_w_in, ssm_lambda_re, ssm_lambda_im, ssm_log_dt, ssm_b_re, ssm_b_im, ssm_c_re, ssm_c_im, ssm_d, ssm_w_out, nsa_w_in, nsa_cmp_k_pos, nsa_cmp_k_w1, nsa_cmp_k_w2, nsa_cmp_v_pos, nsa_cmp_v_w1, nsa_cmp_v_w2, nsa_w_out, peer_w_q, peer_sub_keys, peer_u, peer_v):
    bsz, l_all, d = x.shape
    assert bsz == 1
    depth = ada_w.shape[0]
    mod = _adaln(c, ada_w, ada_b)
    xt = x.reshape(l_all, d)
    nf = norm_final.reshape(1, d)
    for i in range(depth):
        sh1, sc1, g1, sh2, sc2, g2 = [mod[i, :, k * d:(k + 1) * d] for k in range(6)]
        j = i // 2
        gm = norm_mix[i].reshape(1, d)
        if i % 2 == 0:
            xt = _s5_layer(xt, gm, sh1, sc1, g1, ssm_w_in[j], ssm_lambda_re[j], ssm_lambda_im[j],
                           ssm_log_dt[j], ssm_b_re[j], ssm_b_im[j], ssm_c_re[j], ssm_c_im[j],
                           ssm_d[j], ssm_w_out[j])
        else:
            xt = _nsa_layer(xt, gm, sh1, sc1, g1, nsa_w_in[j], nsa_cmp_k_pos[j], nsa_cmp_k_w1[j],
                            nsa_cmp_k_w2[j], nsa_cmp_v_pos[j], nsa_cmp_v_w1[j], nsa_cmp_v_w2[j],
                            nsa_w_out[j])
        xt = _peer_layer(xt, norm_ffn[i].reshape(1, d), sh2, sc2, g2, peer_w_q[i], peer_sub_keys[i],
                         peer_u[i], peer_v[i], nf, final_norm=(i == depth - 1))
    return xt.reshape(bsz, l_all, d)
```

```python
import functools
import math

import numpy as np
import jax
import jax.numpy as jnp
from jax import lax
from jax.experimental import pallas as pl
from jax.experimental.pallas import tpu as pltpu

F32 = jnp.float32
BF16 = jnp.bfloat16

RMS_EPS = 1e-6

SSM_GROUP = 16
SSM_STATE = 64
SSM_BLOCK_GROUPS = 8
S5_SUBSEQ = 8
S5_CHUNK = 256
S5_LANE_BLOCK = 1024

NSA_HEADS = 16
NSA_HEAD_DIM = 128
NSA_KV_GROUPS = 4
NSA_REP = NSA_HEADS // NSA_KV_GROUPS
N_BRANCH = 3
CMP_LEN = 32
CMP_STRIDE = 16
SEL_LEN = 64
SEL_TOPN = 16
WIN = 512
Q_BLOCK = 256
SEL_KV_TILE = 512
MASK_PENALTY = -(2.0 ** 30)
SEL_DEN_LIMIT = 1e30
ONEHOT_PERIOD = 128

PEER_HEADS = 8
PEER_NKEYS = 128
PEER_TOPK = 16
PEER_HALF = 128
PEER_EXPERT_CHUNK = 1024
PEER_LAST_IS_EVEN = (PEER_NKEYS ** 2 // PEER_EXPERT_CHUNK) % 2 == 0
RANK_NONE = 255.0
RANK_CODE = 2.0 ** 100
RANK_CODE_STEPS = 32.0

VMEM_LIMIT = 56 * 1024 * 1024
NT_DIMS = (((1,), (1,)), ((), ()))


def _cparams(n_axes):
    return pltpu.CompilerParams(
        dimension_semantics=("arbitrary",) * n_axes, vmem_limit_bytes=VMEM_LIMIT)


def _norm_mod(x, g, sh, sc):
    ms = jnp.mean(x * x, axis=-1, keepdims=True)
    xn = x * lax.rsqrt(ms + RMS_EPS)
    return (xn * g) * (1.0 + sc) + sh


def _adaln_kernel(c_ref, w_ref, b_ref, o_ref):
    c = c_ref[...]
    cond = c * jax.nn.sigmoid(c)
    o_ref[0] = jnp.dot(cond.astype(BF16), w_ref[0].astype(BF16),
                       preferred_element_type=F32) + b_ref[0]


def _adaln(c, ada_w, ada_b):
    depth, d, n = ada_w.shape
    tn = 1024
    c8 = jnp.broadcast_to(c, (8, d))
    out = pl.pallas_call(
        _adaln_kernel,
        grid=(depth, n // tn),
        in_specs=[pl.BlockSpec((8, d), lambda i, j: (0, 0)),
                  pl.BlockSpec((1, d, tn), lambda i, j: (i, 0, j)),
                  pl.BlockSpec((1, 1, tn), lambda i, j: (i, 0, j))],
        out_specs=pl.BlockSpec((1, 8, tn), lambda i, j: (i, 0, j)),
        out_shape=jax.ShapeDtypeStruct((depth, 8, n), F32),
        compiler_params=_cparams(2),
    )(c8, ada_w, ada_b.reshape(depth, 1, n))
    return out[:, 0:1, :]


def _nmm_kernel(x_ref, g_ref, sh_ref, sc_ref, w_ref, cs_ref, o_ref, h_scr, *, act):
    @pl.when(pl.program_id(1) == 0)
    def _():
        h_scr[...] = _norm_mod(x_ref[...], g_ref[...], sh_ref[...], sc_ref[...]).astype(BF16)

    acc = jnp.dot(h_scr[...], w_ref[...], preferred_element_type=F32)
    if act == "sigmoid":
        acc = jax.nn.sigmoid(acc)
    else:
        acc = acc * cs_ref[...]
    o_ref[...] = acc.astype(o_ref.dtype)


def _norm_mod_matmul(x, g, sh, sc, w, colscale, out_dtype, act="scale", tm=1024, tn=1024):
    t, d = x.shape
    n = w.shape[1]
    tm, tn = min(tm, t), min(tn, n)
    row = lambda i, j: (0, 0)
    return pl.pallas_call(
        functools.partial(_nmm_kernel, act=act),
        grid=(t // tm, n // tn),
        in_specs=[pl.BlockSpec((tm, d), lambda i, j: (i, 0)),
                  pl.BlockSpec((1, d), row), pl.BlockSpec((1, d), row), pl.BlockSpec((1, d), row),
                  pl.BlockSpec((d, tn), lambda i, j: (0, j)),
                  pl.BlockSpec((1, tn), lambda i, j: (0, j))],
        out_specs=pl.BlockSpec((tm, tn), lambda i, j: (i, j)),
        out_shape=jax.ShapeDtypeStruct((t, n), out_dtype),
        scratch_shapes=[pltpu.VMEM((tm, d), BF16)],
        compiler_params=_cparams(2),
    )(x, g, sh, sc, w, colscale)


def _nsa_proj_kernel(x_ref, g_ref, sh_ref, sc_ref, w_ref, cs_ref, perm_ref, o_ref, o_cmp_ref, h_scr, *,
                     cmp_tile):
    j = pl.program_id(1)

    @pl.when(j == 0)
    def _():
        h_scr[...] = _norm_mod(x_ref[...], g_ref[...], sh_ref[...], sc_ref[...]).astype(BF16)

    out = (jnp.dot(h_scr[...], w_ref[...], preferred_element_type=F32) * cs_ref[...]).astype(o_ref.dtype)
    o_ref[...] = out

    @pl.when(j == cmp_tile)
    def _():
        moved = jnp.dot(perm_ref[...], out, preferred_element_type=F32).astype(o_cmp_ref.dtype)
        per = o_cmp_ref.shape[1]
        for l in range(o_cmp_ref.shape[0]):
            o_cmp_ref[l] = moved[l * per:(l + 1) * per, :]


def _nsa_projection(x, g, sh, sc, w, colscale, cmp_tile, tm=1024, tn=1024):
    t, d = x.shape
    n = w.shape[1]
    tm = min(tm, t)
    half = CMP_LEN // 2
    per = tm // half
    r = np.arange(tm)
    perm = np.zeros((tm, tm), np.float32)
    perm[r, (r % per) * half + r // per] = 1.0
    row = lambda i, j: (0, 0)
    return pl.pallas_call(
        functools.partial(_nsa_proj_kernel, cmp_tile=cmp_tile),
        grid=(t // tm, n // tn),
        in_specs=[pl.BlockSpec((tm, d), lambda i, j: (i, 0)),
                  pl.BlockSpec((1, d), row), pl.BlockSpec((1, d), row), pl.BlockSpec((1, d), row),
                  pl.BlockSpec((d, tn), lambda i, j: (0, j)),
                  pl.BlockSpec((1, tn), lambda i, j: (0, j)),
                  pl.BlockSpec((tm, tm), row)],
        out_specs=[pl.BlockSpec((tm, tn), lambda i, j: (i, j)),
                   pl.BlockSpec((half, per, tn), lambda i, j: (0, i, 0))],
        out_shape=[jax.ShapeDtypeStruct((t, n), BF16),
                   jax.ShapeDtypeStruct((half, t // half, tn), BF16)],
        scratch_shapes=[pltpu.VMEM((tm, d), BF16)],
        compiler_params=_cparams(2),
    )(x, g, sh, sc, w, colscale, jnp.asarray(perm, dtype=BF16))


def _mm_res_kernel(a_ref, w_ref, x_ref, g_ref, o_ref):
    y = jnp.dot(a_ref[...], w_ref[...], preferred_element_type=F32)
    o_ref[...] = x_ref[...] + g_ref[...] * y


def _mm_glu_res_kernel(a_ref, wa_ref, wb_ref, x_ref, g_ref, o_ref):
    a = a_ref[...]
    ya = jnp.dot(a, wa_ref[...], preferred_element_type=F32)
    yb = jnp.dot(a, wb_ref[...], preferred_element_type=F32)
    o_ref[...] = x_ref[...] + g_ref[...] * (ya * jax.nn.sigmoid(yb))


def _matmul_residual(a, w, x, gate, glu, tm=1024, tn=1024):
    t, k = a.shape
    n = x.shape[1]
    tm = min(tm, t)
    nb = n // tn
    a_spec = pl.BlockSpec((tm, k), lambda i, j: (i, 0))
    w_spec = pl.BlockSpec((k, tn), lambda i, j: (0, j))
    tail = [pl.BlockSpec((tm, tn), lambda i, j: (i, j)), pl.BlockSpec((1, tn), lambda i, j: (0, j))]
    if glu:
        kern = _mm_glu_res_kernel
        in_specs = [a_spec, w_spec, pl.BlockSpec((k, tn), lambda i, j: (0, j + nb))] + tail
        args = (a, w, w, x, gate)
    else:
        kern = _mm_res_kernel
        in_specs = [a_spec, w_spec] + tail
        args = (a, w, x, gate)
    return pl.pallas_call(
        kern,
        grid=(t // tm, nb),
        in_specs=in_specs,
        out_specs=pl.BlockSpec((tm, tn), lambda i, j: (i, j)),
        out_shape=jax.ShapeDtypeStruct((t, n), F32),
        compiler_params=_cparams(2),
    )(*args)


def _s5_param_kernel(lre_ref, lim_ref, ldt_ref, bre_ref, bim_ref, ar_ref, ai_ref, bbr_ref, bbi_ref):
    lre = lre_ref[...]
    lim = lim_ref[...]
    dt = jnp.exp(ldt_ref[...])
    mag = jnp.exp(lre * dt)
    ar = mag * jnp.cos(lim * dt)
    ai = mag * jnp.sin(lim * dt)
    den = lre * lre + lim * lim
    cr = ((ar - 1.0) * lre + ai * lim) / den
    ci = (ai * lre - (ar - 1.0) * lim) / den
    ar_ref[...] = ar
    ai_ref[...] = ai
    bbr_ref[...] = cr * bre_ref[...] - ci * bim_ref[...]
    bbi_ref[...] = cr * bim_ref[...] + ci * bre_ref[...]


def _s5_params(lam_re, lam_im, log_dt, b_re, b_im):
    g, n, p = b_re.shape
    rep = lambda a: jnp.repeat(a, p, axis=1)
    shp = jax.ShapeDtypeStruct((g, n * p), F32)
    ar, ai, bbr, bbi = pl.pallas_call(
        _s5_param_kernel, out_shape=(shp, shp, shp, shp),
    )(rep(lam_re), rep(lam_im), log_dt.reshape(g, 1), b_re.reshape(g, n * p), b_im.reshape(g, n * p))
    return ar[:, ::p], ai[:, ::p], bbr.reshape(g, n, p), bbi.reshape(g, n, p)


def _s5_scan_kernel(u_ref, a_ref, bbr_ref, bbi_ref, cr_ref, cin_ref, d_ref, o_ref,
                    bur, bui, hin_r, hin_i, car_r, car_i, apr, api):
    t, width = u_ref.shape
    sub = t // S5_SUBSEQ
    lanes = bur.shape[1]
    nblk = bbr_ref.shape[0]
    kin = width // nblk
    kst = lanes // nblk

    @pl.when(pl.program_id(0) == 0)
    def _init():
        car_r[...] = jnp.zeros_like(car_r)
        car_i[...] = jnp.zeros_like(car_i)
        ar = a_ref[0:1, :]
        ai = a_ref[1:2, :]
        pr, pi = ar, ai
        apr[0:1, :] = pr
        api[0:1, :] = pi
        for i in range(1, sub):
            pr, pi = pr * ar - pi * ai, pr * ai + pi * ar
            apr[i:i + 1, :] = pr
            api[i:i + 1, :] = pi

    for k in range(nblk):
        uk = u_ref[:, kin * k:kin * (k + 1)].astype(BF16)
        bur[:, kst * k:kst * (k + 1)] = jnp.dot(uk, bbr_ref[k], preferred_element_type=F32)
        bui[:, kst * k:kst * (k + 1)] = jnp.dot(uk, bbi_ref[k], preferred_element_type=F32)

    lb = S5_LANE_BLOCK
    for b in range(lanes // lb):
        sl = slice(b * lb, (b + 1) * lb)
        ar = jnp.broadcast_to(a_ref[0:1, sl], (S5_SUBSEQ, lb))
        ai = jnp.broadcast_to(a_ref[1:2, sl], (S5_SUBSEQ, lb))
        sr = jnp.zeros((S5_SUBSEQ, lb), F32)
        si = jnp.zeros((S5_SUBSEQ, lb), F32)
        for i in range(sub):
            rows = slice(S5_SUBSEQ * i, S5_SUBSEQ * (i + 1))
            sr, si = (ar * sr - ai * si + bur[rows, sl], ar * si + ai * sr + bui[rows, sl])
            bur[rows, sl] = sr
            bui[rows, sl] = si
        asr = apr[sub - 1:sub, sl]
        asi = api[sub - 1:sub, sl]
        hr = car_r[0:1, sl]
        hi = car_i[0:1, sl]
        for j in range(S5_SUBSEQ):
            hin_r[j:j + 1, sl] = hr
            hin_i[j:j + 1, sl] = hi
            er = sr[j:j + 1, :]
            ei = si[j:j + 1, :]
            hr, hi = er + asr * hr - asi * hi, ei + asr * hi + asi * hr
        car_r[0:1, sl] = hr
        car_i[0:1, sl] = hi
        hinr = hin_r[:, sl]
        hini = hin_i[:, sl]
        for i in range(sub):
            rows = slice(S5_SUBSEQ * i, S5_SUBSEQ * (i + 1))
            pr = jnp.broadcast_to(apr[i:i + 1, sl], (S5_SUBSEQ, lb))
            pi = jnp.broadcast_to(api[i:i + 1, sl], (S5_SUBSEQ, lb))
            bur[rows, sl] = bur[rows, sl] + pr * hinr - pi * hini
            bui[rows, sl] = bui[rows, sl] + pr * hini + pi * hinr

    for k in range(nblk):
        sr = bur[:, kst * k:kst * (k + 1)].astype(BF16)
        si = bui[:, kst * k:kst * (k + 1)].astype(BF16)
        y = (jnp.dot(sr, cr_ref[k], preferred_element_type=F32)
             + jnp.dot(si, cin_ref[k], preferred_element_type=F32))
        cols = slice(kin * k, kin * (k + 1))
        y = y + d_ref[:, cols] * u_ref[:, cols]
        o_ref[:, cols] = jax.nn.gelu(y).astype(o_ref.dtype)


def _s5_scan(u_p, ar, ai, bb_r, bb_i, c_re, c_im, d_skip):
    t_all, width = u_p.shape
    g, n, p = bb_r.shape
    bg = SSM_BLOCK_GROUPS
    nblk = g // bg
    eye = jnp.eye(bg, dtype=F32)
    blk_b = lambda bb: jnp.einsum('kgnp,gh->kgphn', bb.reshape(nblk, bg, n, p), eye
                                  ).reshape(nblk, bg * p, bg * n).astype(BF16)
    blk_c = lambda cc: jnp.einsum('kgpn,gh->kgnhp', cc.reshape(nblk, bg, p, n), eye
                                  ).reshape(nblk, bg * n, bg * p).astype(BF16)
    lanes = g * n
    a8 = jnp.zeros((8, lanes), F32).at[0].set(ar.reshape(lanes)).at[1].set(ai.reshape(lanes))
    t = S5_CHUNK
    sub = t // S5_SUBSEQ
    full3 = lambda c: (0, 0, 0)
    wspec_b = pl.BlockSpec((nblk, bg * p, bg * n), full3)
    wspec_c = pl.BlockSpec((nblk, bg * n, bg * p), full3)
    return pl.pallas_call(
        _s5_scan_kernel,
        grid=(t_all // t,),
        in_specs=[pl.BlockSpec((t, width), lambda c: (c, 0)),
                  pl.BlockSpec((8, lanes), lambda c: (0, 0)),
                  wspec_b, wspec_b, wspec_c, wspec_c,
                  pl.BlockSpec((1, width), lambda c: (0, 0))],
        out_specs=pl.BlockSpec((t, width), lambda c: (c, 0)),
        out_shape=jax.ShapeDtypeStruct((t_all, width), BF16),
        scratch_shapes=[pltpu.VMEM((t, lanes), F32), pltpu.VMEM((t, lanes), F32),
                        pltpu.VMEM((8, lanes), F32), pltpu.VMEM((8, lanes), F32),
                        pltpu.VMEM((8, lanes), F32), pltpu.VMEM((8, lanes), F32),
                        pltpu.VMEM((sub, lanes), F32), pltpu.VMEM((sub, lanes), F32)],
        compiler_params=_cparams(1),
    )(u_p, a8, blk_b(bb_r), blk_b(bb_i), blk_c(c_re), blk_c(-c_im), d_skip.reshape(1, width))


def _s5_rows_to_subseq(x, inverse=False):
    t_all, d = x.shape
    sub = S5_CHUNK // S5_SUBSEQ
    shape = (t_all // S5_CHUNK, sub, S5_SUBSEQ, d) if inverse else (t_all // S5_CHUNK, S5_SUBSEQ, sub, d)
    return x.reshape(shape).transpose(0, 2, 1, 3).reshape(t_all, d)


def _s5_layer(x, g, sh, sc, gate, w_in, lam_re, lam_im, log_dt, b_re, b_im, c_re, c_im, d_skip, w_out):
    d = x.shape[1]
    x_p = _s5_rows_to_subseq(x)
    ones = jnp.ones((1, w_in.shape[1]), F32)
    u_p = _norm_mod_matmul(x_p, g, sh, sc, w_in.astype(BF16), ones, F32)
    ar, ai, bb_r, bb_i = _s5_params(lam_re, lam_im, log_dt, b_re, b_im)
    gy_p = _s5_scan(u_p, ar, ai, bb_r, bb_i, c_re, c_im, d_skip)
    xn_p = _matmul_residual(gy_p, w_out.astype(BF16), x_p, gate, glu=True)
    return _s5_rows_to_subseq(xn_p, inverse=True)


def _cmp_kernel(x_ref, w1a_ref, w1b_ref, pos_ref, w1_ref, w2_ref, o_ref, acc_a, acc_b):
    l = pl.program_id(2)

    @pl.when(l == 0)
    def _():
        acc_a[...] = jnp.zeros_like(acc_a)
        acc_b[...] = jnp.zeros_like(acc_b)

    x = x_ref[0]
    acc_a[...] += jnp.dot(x, w1a_ref[0, 0], preferred_element_type=F32)
    acc_b[...] += jnp.dot(x, w1b_ref[0, 0], preferred_element_type=F32)

    @pl.when(l == pl.num_programs(2) - 1)
    def _():
        m = acc_a.shape[0]
        posc = jnp.zeros((8, NSA_HEAD_DIM), F32)
        for ll in range(CMP_LEN):
            prow = jnp.broadcast_to(pos_ref[0, ll:ll + 1, :], (8, NSA_HEAD_DIM)).astype(BF16)
            posc = posc + jnp.dot(prow, w1_ref[0, ll], preferred_element_type=F32)
        pre = acc_a[...] + pltpu.roll(acc_b[...], m - 1, 0) + posc[0:1, :]
        hid = jax.nn.gelu(pre)
        o_ref[0, 0] = jnp.dot(hid.astype(BF16), w2_ref[0], preferred_element_type=F32).astype(o_ref.dtype)


def _compress(kv_by_pos, pos, w1, w2):
    half, m, _ = kv_by_pos.shape
    g = NSA_KV_GROUPS
    dh = NSA_HEAD_DIM
    return pl.pallas_call(
        _cmp_kernel,
        grid=(2, g, half),
        in_specs=[pl.BlockSpec((1, m, dh), lambda s, gg, l: (l, 0, g * s + gg)),
                  pl.BlockSpec((1, 1, dh, dh), lambda s, gg, l: (s, l, 0, 0)),
                  pl.BlockSpec((1, 1, dh, dh), lambda s, gg, l: (s, l + half, 0, 0)),
                  pl.BlockSpec((1, CMP_LEN, dh), lambda s, gg, l: (s, 0, 0)),
                  pl.BlockSpec((1, CMP_LEN, dh, dh), lambda s, gg, l: (s, 0, 0, 0)),
                  pl.BlockSpec((1, dh, dh), lambda s, gg, l: (s, 0, 0))],
        out_specs=pl.BlockSpec((1, 1, m, dh), lambda s, gg, l: (s, gg, 0, 0)),
        out_shape=jax.ShapeDtypeStruct((2, g, m, dh), BF16),
        scratch_shapes=[pltpu.VMEM((m, dh), F32), pltpu.VMEM((m, dh), F32)],
        compiler_params=_cparams(3),
    )(kv_by_pos, w1, w1, pos, w1, w2)


def _softmax2_rows(s, mask):
    s = jnp.where(mask, s, -1e30)
    m = jnp.max(s, axis=-1, keepdims=True)
    p = jnp.exp2(s - m)
    return p, 1.0 / jnp.maximum(jnp.sum(p, axis=-1, keepdims=True), 1e-30)


def _shifted_softmax(s, mask, shift):
    p = jnp.exp2(jnp.where(mask, s - shift, -1e30))
    return p, jnp.sum(p, axis=-1, keepdims=True)


def _any_out_of_range(den, needed):
    ok = (den > 0.0) & (den < SEL_DEN_LIMIT)
    return jnp.max(jnp.where(needed & jnp.logical_not(ok), 1.0, 0.0)) > 0.0


def _nsa_attn_kernel(*refs):
    redo = _nsa_attn_body(True, *refs)

    @pl.when(redo)
    def _():
        _nsa_attn_body(False, *refs)


def _nsa_attn_body(fast, q_ref, gt_ref, kc_ref, vc_ref, agg_ref, ksa_ref, vsa_ref, kw_ref, vw_ref,
                   o_ref, qaug_scr, s_a, s_b, acc_scr, m_scr, shift_scr):
    b = pl.program_id(1)
    qb = q_ref.shape[0]
    dh = NSA_HEAD_DIM
    rep = NSA_REP
    rows = rep * qb
    n_cmp = kc_ref.shape[2]
    n_sel = agg_ref.shape[1]
    top_n = min(SEL_TOPN, n_sel)
    t0 = pl.multiple_of(b * qb, qb)

    q_blk = q_ref[...]
    qs = jnp.concatenate([q_blk[:, r * dh:(r + 1) * dh] for r in range(rep)], axis=0)
    tpos = t0 + lax.broadcasted_iota(jnp.int32, (rows, 1), 0) % qb
    all_rows = jnp.ones((rows, 1), jnp.bool_)

    kc = kc_ref[0, 0]
    s_c = lax.dot_general(qs, kc, NT_DIMS, preferred_element_type=F32)
    cmp_end = lax.broadcasted_iota(jnp.int32, (1, n_cmp), 1) * CMP_STRIDE + (CMP_LEN - 1)
    has_c = tpos >= CMP_LEN - 1
    if fast:
        p_c, den_c = _shifted_softmax(s_c, cmp_end <= tpos, s_c[:, 0:1])
        redo = _any_out_of_range(den_c, has_c)
        inv_c = 1.0 / jnp.maximum(den_c, 1e-30)
    else:
        p_c, inv_c = _softmax2_rows(s_c, cmp_end <= tpos)
    p_c = p_c * jnp.where(has_c, inv_c, 0.0)
    o_c = jnp.dot(p_c.astype(BF16), vc_ref[0, 0], preferred_element_type=F32)

    psum = p_c[0:qb]
    for r in range(1, rep):
        psum = psum + p_c[r * qb:(r + 1) * qb]
    p_hi = psum.astype(BF16)
    p_lo = (psum - p_hi.astype(F32)).astype(BF16)
    agg = agg_ref[...]
    imp = (jnp.dot(p_hi, agg, preferred_element_type=F32)
           + jnp.dot(p_lo, agg, preferred_element_type=F32))
    imp_t = imp.T
    tq = t0 + lax.broadcasted_iota(jnp.int32, (1, qb), 1)
    blk = lax.broadcasted_iota(jnp.int32, (n_sel, qb), 0)
    blk_f = blk.astype(F32)
    cur = tq // SEL_LEN
    valid = blk * SEL_LEN <= tq
    if fast:
        score = jnp.where(valid, imp_t, -1.0 - blk_f)
        score = jnp.where(blk == cur - 1, 1e9, score)
        score = jnp.where(blk == cur, 2e9, score)
        score = jnp.where(blk == 0, 3e9, score)
    else:
        forced = (blk == 0) | (blk == cur) | (blk == cur - 1)
        score = jnp.where(forced, 1e9, jnp.where(valid, imp_t, -1e9))
    for _ in range(top_n):
        mx = jnp.max(score, axis=0, keepdims=True)
        hit = score == mx
        if not fast:
            idx = jnp.min(jnp.where(hit, blk_f, float(n_sel)), axis=0, keepdims=True)
            hit = blk_f == idx
        score = jnp.where(hit, -jnp.inf, score)
    took = score == -jnp.inf
    if fast:
        taken = jnp.sum(jnp.where(took, 1.0, 0.0), axis=0, keepdims=True)
        redo = redo | (jnp.max(jnp.where(taken != float(top_n), 1.0, 0.0)) > 0.0)
    pen = jnp.where(took, 0.0, MASK_PENALTY).T.astype(BF16)
    n_half = qaug_scr.shape[0]
    for hh in range(n_half):
        if n_sel >= ONEHOT_PERIOD:
            ph = pen[:, hh * ONEHOT_PERIOD:(hh + 1) * ONEHOT_PERIOD]
        else:
            ph = jnp.concatenate(
                [pen, jnp.full((qb, ONEHOT_PERIOD - n_sel), MASK_PENALTY, BF16)], axis=1)
        qaug_scr[hh] = jnp.concatenate([qs, jnp.concatenate([ph] * rep, axis=0)], axis=1)

    kt = SEL_KV_TILE
    period_keys = ONEHOT_PERIOD * SEL_LEN

    def sel_scores(i, s_out):
        k0 = pl.multiple_of(i * kt, kt)
        s_out[...] = lax.dot_general(qaug_scr[k0 // period_keys], ksa_ref[pl.ds(k0, kt), :], NT_DIMS,
                                     preferred_element_type=F32)

    def causal_mask(i, s):
        kpos = i * kt + lax.broadcasted_iota(jnp.int32, (1, kt), 1)
        return jnp.where(kpos <= tpos, s, MASK_PENALTY)

    def update_running_max(i, s_in, causal):
        k0 = pl.multiple_of(i * kt, kt)
        s = causal_mask(i, s_in[...]) if causal else s_in[...]
        m_run = m_scr[...]
        m_new = jnp.maximum(m_run, jnp.max(s, axis=-1, keepdims=True))
        p = jnp.exp2(s - m_new)
        acc_scr[...] = (jnp.exp2(m_run - m_new) * acc_scr[...]
                        + jnp.dot(p.astype(BF16), vsa_ref[pl.ds(k0, kt), :], preferred_element_type=F32))
        m_scr[...] = m_new

    def update_fixed_shift(i, s_in, causal):
        k0 = pl.multiple_of(i * kt, kt)
        s = causal_mask(i, s_in[...]) if causal else s_in[...]
        shift = shift_scr[...]
        p = jnp.concatenate([jnp.exp2(s[:, c * dh:(c + 1) * dh] - shift).astype(BF16)
                             for c in range(kt // dh)], axis=1)
        acc_scr[...] += jnp.dot(p, vsa_ref[pl.ds(k0, kt), :], preferred_element_type=F32)

    def sweep(update):
        def pair(j, carry):
            sel_scores(2 * j + 1, s_b)
            update(2 * j, s_a, False)
            sel_scores(2 * j + 2, s_a)
            update(2 * j + 1, s_b, False)
            return carry

        acc_scr[...] = jnp.zeros_like(acc_scr)
        lax.fori_loop(0, n_full // 2, pair, 0)
        odd = n_full % 2 == 1

        @pl.when(odd)
        def _():
            sel_scores(n_full, s_b)
            update(n_full - 1, s_a, False)
            update(n_full, s_b, True)

        @pl.when(jnp.logical_not(odd))
        def _():
            update(n_full, s_a, True)

    n_full = t0 // kt
    sel_scores(0, s_a)
    if fast:
        first = jnp.max(causal_mask(0, s_a[...]), axis=-1, keepdims=True)
        shift_scr[...] = jnp.broadcast_to(first, shift_scr.shape)
        sweep(update_fixed_shift)
    else:
        m_scr[...] = jnp.full(m_scr.shape, -1e30, F32)
        sweep(update_running_max)
    acc_s = acc_scr[...]
    den_s = acc_s[:, dh:dh + 1]
    o_s = acc_s[:, 0:dh] * (1.0 / den_s)

    wlen = WIN + qb
    w0 = pl.multiple_of(jnp.maximum(t0 - WIN, 0), qb)
    s_w = lax.dot_general(qs, kw_ref[pl.ds(w0, wlen), :], NT_DIMS, preferred_element_type=F32)
    diff = tpos - (w0 + lax.broadcasted_iota(jnp.int32, (1, wlen), 1))
    mask_w = (diff >= 0) & (diff < WIN)
    if fast:
        k_diag = jnp.concatenate([kw_ref[pl.ds(t0, qb), :]] * rep, axis=0).astype(F32)
        diag = jnp.sum(qs.astype(F32) * k_diag, axis=-1, keepdims=True)
        p_w, den_w = _shifted_softmax(s_w, mask_w, diag)
        o_w = jnp.dot(p_w.astype(BF16), vw_ref[pl.ds(w0, wlen), :], preferred_element_type=F32)
        o_w = o_w * (1.0 / den_w)
        redo = redo | _any_out_of_range(den_s, all_rows) | _any_out_of_range(den_w, all_rows)
    else:
        p_w, inv_w = _softmax2_rows(s_w, mask_w)
        o_w = jnp.dot((p_w * inv_w).astype(BF16), vw_ref[pl.ds(w0, wlen), :], preferred_element_type=F32)

    gt = gt_ref[...]
    for r in range(rep):
        rs = slice(r * qb, (r + 1) * qb)
        o = (gt[:, r:r + 1] * o_c[rs]
             + gt[:, rep + r:rep + r + 1] * o_s[rs]
             + gt[:, 2 * rep + r:2 * rep + r + 1] * o_w[rs])
        o_ref[:, r * dh:(r + 1) * dh] = o.astype(o_ref.dtype)
    return redo if fast else None


def _nsa_agg(n_cmp_pad, n_cmp, n_sel):
    ratio, span = SEL_LEN // CMP_STRIDE, CMP_LEN // CMP_STRIDE
    agg = np.zeros((n_cmp_pad, n_sel), np.float32)
    jj = np.arange(n_sel)
    for m in range(ratio):
        for n in range(span):
            ii = ratio * jj + m - n
            ok = (ii >= 0) & (ii < n_cmp)
            agg[ii[ok], jj[ok]] += 1.0
    return jnp.asarray(agg, dtype=BF16)


def _nsa_attention(qkv, gates, kv_cmp):
    l_all = qkv.shape[0]
    dh = NSA_HEAD_DIM
    g = NSA_KV_GROUPS
    qb = Q_BLOCK
    qw = NSA_REP * dh
    n_cmp_pad = l_all // CMP_STRIDE
    n_cmp = (l_all - CMP_LEN) // CMP_STRIDE + 1
    n_sel = l_all // SEL_LEN
    agg = _nsa_agg(n_cmp_pad, n_cmp, n_sel)
    n_half = max(n_sel // ONEHOT_PERIOD, 1)
    q_blocks = (NSA_HEADS * dh) // dh
    key = np.arange(l_all)
    onehot = jnp.asarray((key[:, None] // SEL_LEN) % ONEHOT_PERIOD == np.arange(ONEHOT_PERIOD)[None, :],
                         dtype=BF16)
    ones_col = jnp.asarray(np.arange(dh)[None, :] == 0, dtype=BF16)
    grouped = lambda base: qkv[:, (q_blocks + base) * dh:(q_blocks + base + g) * dh].reshape(l_all, g, dh)
    ks_aug = jnp.concatenate([grouped(2 * g), jnp.broadcast_to(onehot[:, None, :], (l_all, g, ONEHOT_PERIOD))],
                             axis=-1).reshape(l_all, g * (dh + ONEHOT_PERIOD))
    vs_aug = jnp.concatenate([grouped(3 * g), jnp.broadcast_to(ones_col[:, None, :], (l_all, g, dh))],
                             axis=-1).reshape(l_all, g * 2 * dh)
    once = dict(pipeline_mode=pl.Buffered(1))
    kv_spec = lambda base: pl.BlockSpec((l_all, dh), lambda gg, b: (0, q_blocks + base + gg), **once)
    aug_spec = pl.BlockSpec((l_all, 2 * dh), lambda gg, b: (0, gg), **once)
    rows = NSA_REP * qb
    return pl.pallas_call(
        _nsa_attn_kernel,
        grid=(g, l_all // qb),
        in_specs=[pl.BlockSpec((qb, qw), lambda gg, b: (b, gg)),
                  pl.BlockSpec((qb, dh), lambda gg, b: (b, gg)),
                  pl.BlockSpec((1, 1, n_cmp_pad, dh), lambda gg, b: (0, gg, 0, 0)),
                  pl.BlockSpec((1, 1, n_cmp_pad, dh), lambda gg, b: (1, gg, 0, 0)),
                  pl.BlockSpec((n_cmp_pad, n_sel), lambda gg, b: (0, 0)),
                  aug_spec, aug_spec, kv_spec(4 * g), kv_spec(5 * g)],
        out_specs=pl.BlockSpec((qb, qw), lambda gg, b: (b, gg)),
        out_shape=jax.ShapeDtypeStruct((l_all, NSA_HEADS * dh), BF16),
        scratch_shapes=[pltpu.VMEM((n_half, rows, dh + ONEHOT_PERIOD), BF16),
                        pltpu.VMEM((rows, SEL_KV_TILE), F32), pltpu.VMEM((rows, SEL_KV_TILE), F32),
                        pltpu.VMEM((rows, 2 * dh), F32), pltpu.VMEM((rows, 1), F32),
                        pltpu.VMEM((rows, dh), F32)],
        compiler_params=_cparams(2),
    )(qkv, gates, kv_cmp, kv_cmp, agg, ks_aug, vs_aug, qkv, qkv)


def _nsa_layer(x, g, sh, sc, gate, w_in, k_pos, k_w1, k_w2, v_pos, v_w1, v_w2, w_out):
    dh = NSA_HEAD_DIM
    qd = NSA_HEADS * dh
    kvd = 2 * N_BRANCH * NSA_KV_GROUPS * dh
    w_main = w_in[:, :qd + kvd].astype(BF16)
    colscale = jnp.concatenate([jnp.full((1, qd), dh ** -0.5 * math.log2(math.e), F32),
                                jnp.ones((1, kvd), F32)], axis=1)
    tn = 2 * NSA_KV_GROUPS * dh
    qkv, kv_by_pos = _nsa_projection(x, g, sh, sc, w_main, colscale, cmp_tile=qd // tn, tn=tn)
    wg = w_in[:, qd + kvd:].reshape(-1, NSA_KV_GROUPS, NSA_REP, N_BRANCH).transpose(0, 1, 3, 2)
    wg = wg.reshape(-1, NSA_KV_GROUPS, N_BRANCH * NSA_REP)
    wg = jnp.pad(wg, ((0, 0), (0, 0), (0, dh - N_BRANCH * NSA_REP))).reshape(-1, NSA_KV_GROUPS * dh)
    gates = _norm_mod_matmul(x, g, sh, sc, wg.astype(BF16), jnp.ones((1, wg.shape[1]), F32), F32,
                             act="sigmoid")
    kv_cmp = _compress(kv_by_pos, jnp.stack([k_pos, v_pos]),
                       jnp.stack([k_w1, v_w1]).astype(BF16), jnp.stack([k_w2, v_w2]).astype(BF16))
    o = _nsa_attention(qkv, gates, kv_cmp)
    return _matmul_residual(o, w_out.astype(BF16), x, gate, glu=False)


def _peer_query_kernel(x_ref, g_ref, sh_ref, sc_ref, wq_ref, keys_ref, ht_ref, st_ref):
    h32 = _norm_mod(x_ref[...], g_ref[...], sh_ref[...], sc_ref[...])
    ht_ref[...] = h32.T.astype(BF16)
    q = jnp.dot(h32.astype(BF16), wq_ref[...], preferred_element_type=F32).astype(BF16)
    for hc in range(keys_ref.shape[0]):
        rows = slice(hc * PEER_NKEYS, (hc + 1) * PEER_NKEYS)
        st_ref[rows, :] = lax.dot_general(keys_ref[hc], q[:, hc * PEER_HALF:(hc + 1) * PEER_HALF],
                                          NT_DIMS, preferred_element_type=F32)


def _peer_query(x, g, sh, sc, w_q, sub_keys, tm=512):
    t, d = x.shape
    nq = w_q.shape[1]
    keys = sub_keys.reshape(-1, PEER_NKEYS, PEER_HALF).astype(BF16)
    row = lambda i: (0, 0)
    return pl.pallas_call(
        _peer_query_kernel,
        grid=(t // tm,),
        in_specs=[pl.BlockSpec((tm, d), lambda i: (i, 0)),
                  pl.BlockSpec((1, d), row), pl.BlockSpec((1, d), row), pl.BlockSpec((1, d), row),
                  pl.BlockSpec((d, nq), row),
                  pl.BlockSpec(keys.shape, lambda i: (0, 0, 0))],
        out_specs=[pl.BlockSpec((d, tm), lambda i: (0, i)),
                   pl.BlockSpec((keys.shape[0] * PEER_NKEYS, tm), lambda i: (0, i))],
        out_shape=[jax.ShapeDtypeStruct((d, t), BF16),
                   jax.ShapeDtypeStruct((keys.shape[0] * PEER_NKEYS, t), F32)],
        compiler_params=_cparams(1),
    )(x, g, sh, sc, w_q.astype(BF16), keys)


def _peer_cells():
    return [(a, b) for a in range(PEER_TOPK) for b in range(PEER_TOPK) if (a + 1) * (b + 1) <= PEER_TOPK]


def _take_max(cur, iota, exact):
    v = jnp.max(cur, axis=0, keepdims=True)
    hit = cur == v
    if exact:
        idx = jnp.min(jnp.where(hit, iota, cur.shape[0]), axis=0, keepdims=True)
        hit = iota == idx
    return v, hit


def _top_ranks(s, iota_k, exact):
    vals = []
    cur = s
    if exact:
        rank = jnp.full(s.shape, RANK_NONE, F32)
        for a in range(PEER_TOPK):
            v, hit = _take_max(cur, iota_k, exact)
            rank = jnp.where(hit, float(a), rank)
            cur = jnp.where(hit, -jnp.inf, cur)
            vals.append(v)
        return vals, rank
    for a in range(PEER_TOPK):
        v, hit = _take_max(cur, iota_k, exact)
        cur = jnp.where(hit, -RANK_CODE * (1.0 + a / RANK_CODE_STEPS), cur)
        vals.append(v)
    rank = jnp.where(cur <= -RANK_CODE, (cur * (-1.0 / RANK_CODE) - 1.0) * RANK_CODE_STEPS, RANK_NONE)
    return vals, rank


def _peer_route_body(st_ref, seg_ref, r2_ref, ln_ref, g1_ref, g2_ref, exact):
    tn = st_ref.shape[1]
    nk = PEER_NKEYS
    cells = _peer_cells()
    n_cell = len(cells)
    n_pad = -(-n_cell // 8) * 8
    n_seg = seg_ref.shape[1]
    iota_k = lax.broadcasted_iota(jnp.int32, (nk, tn), 0)
    iota_c = lax.broadcasted_iota(jnp.int32, (n_pad, tn), 0)
    count = lambda m: jnp.sum(m.astype(F32), axis=0, keepdims=True)
    tied = jnp.zeros((1, tn), jnp.bool_)
    for h in range(PEER_HEADS):
        s1 = st_ref[(2 * h) * nk:(2 * h + 1) * nk, :]
        s2 = st_ref[(2 * h + 1) * nk:(2 * h + 2) * nk, :]
        v1, rank1 = _top_ranks(s1, iota_k, exact)
        v2, rank2 = _top_ranks(s2, iota_k, exact)
        cand = jnp.concatenate([v1[a] + v2[b] for a, b in cells]
                               + [jnp.full((n_pad - n_cell, tn), -jnp.inf, F32)], axis=0)
        top = v1[0] + v2[0]
        e_c = jnp.exp(cand - top)
        chosen = jnp.zeros((n_pad, tn), jnp.bool_)
        cur = cand
        for _ in range(PEER_TOPK):
            _, hit = _take_max(cur, iota_c, exact)
            chosen = chosen | hit
            cur = jnp.where(hit, -jnp.inf, cur)
        chosen_f = chosen.astype(F32)
        if not exact:
            k = float(PEER_TOPK)
            tied = (tied | (count(rank1 != RANK_NONE) != k) | (count(rank2 != RANK_NONE) != k)
                    | (jnp.sum(chosen_f, axis=0, keepdims=True) != k))
        z = jnp.sum(chosen_f * e_c, axis=0, keepdims=True)
        chosen_pad = jnp.concatenate([chosen_f, jnp.zeros((n_seg - n_pad, tn), F32)], axis=0)
        rowlen = jnp.dot(seg_ref[...], chosen_pad.astype(BF16), preferred_element_type=F32)
        ln = jnp.zeros((nk, tn), F32)
        for a in range(PEER_TOPK):
            ln = jnp.where(rank1 == float(a), rowlen[a:a + 1, :], ln)
        rows = slice(h * nk, (h + 1) * nk)
        r2_ref[rows, :] = rank2.astype(r2_ref.dtype)
        ln_ref[h] = ln
        g1_ref[h] = jnp.exp(s1 - v1[0]) / z
        g2_ref[rows, :] = jnp.exp(s2 - v2[0]).astype(g2_ref.dtype)
    return tied


def _peer_route_kernel(st_ref, seg_ref, r2_ref, ln_ref, g1_ref, g2_ref):
    refs = (st_ref, seg_ref, r2_ref, ln_ref, g1_ref, g2_ref)
    tied = _peer_route_body(*refs, exact=False)

    @pl.when(jnp.max(tied.astype(F32)) > 0.0)
    def _():
        _peer_route_body(*refs, exact=True)


def _peer_route(st, tn=128):
    n_rows, t = st.shape
    cells = _peer_cells()
    seg = np.zeros((PEER_TOPK, PEER_NKEYS), np.float32)
    for c, (a, _) in enumerate(cells):
        seg[a, c] = 1.0
    out_rows = PEER_HEADS * PEER_NKEYS
    shp = jax.ShapeDtypeStruct((out_rows, t), BF16)
    shp_row = jax.ShapeDtypeStruct((PEER_HEADS, PEER_NKEYS, t), F32)
    spec = pl.BlockSpec((out_rows, tn), lambda i: (0, i))
    spec_row = pl.BlockSpec((PEER_HEADS, PEER_NKEYS, tn), lambda i: (0, 0, i))
    return pl.pallas_call(
        _peer_route_kernel,
        grid=(t // tn,),
        in_specs=[pl.BlockSpec((n_rows, tn), lambda i: (0, i)),
                  pl.BlockSpec((PEER_TOPK, PEER_NKEYS), lambda i: (0, 0))],
        out_specs=[spec, spec_row, spec_row, spec],
        out_shape=[shp, shp_row, shp_row, shp],
        compiler_params=_cparams(1),
    )(st, jnp.asarray(seg, dtype=BF16))


def _peer_expert_kernel(h_ref, u_ref, vt_ref, r2_ref, ln_ref, g1_ref, g2_ref, x_ref, gate_ref, nf_ref,
                        o_ref, acc_t, p_scr, pre_a, pre_b, *, final_norm):
    c = pl.program_id(1)
    last = pl.num_programs(1) - 1
    ec = u_ref.shape[0]
    nk = PEER_NKEYS
    tm = h_ref.shape[1]
    zero = jnp.zeros((), BF16)

    def score(pre_out):
        pre_out[...] = jnp.dot(u_ref[...], h_ref[...], preferred_element_type=F32)

    def finish(pre_in):
        for ii in range(ec // nk):
            w = jnp.zeros((nk, tm), BF16)
            for h in range(PEER_HEADS):
                rows = slice(h * nk, (h + 1) * nk)
                ln_row = ln_ref[h, ii:ii + 1, :].astype(BF16)
                g1_row = g1_ref[h, ii:ii + 1, :].astype(BF16)
                w = w + jnp.where(r2_ref[rows, :] < ln_row, g2_ref[rows, :], zero) * g1_row
            act = jax.nn.gelu(pre_in[ii * nk:(ii + 1) * nk, :])
            p_scr[ii * nk:(ii + 1) * nk, :] = w * act.astype(BF16)
        acc_t[...] += jnp.dot(vt_ref[0], p_scr[...], preferred_element_type=F32)

    @pl.when(c == 0)
    def _():
        acc_t[...] = jnp.zeros_like(acc_t)
        score(pre_a)

    @pl.when((c > 0) & (c < last) & (c % 2 == 1))
    def _():
        score(pre_b)
        finish(pre_a)

    @pl.when((c > 0) & (c < last) & (c % 2 == 0))
    def _():
        score(pre_a)
        finish(pre_b)

    @pl.when(c == last)
    def _():
        finish(pre_b if PEER_LAST_IS_EVEN else pre_a)
        xo = x_ref[...] + gate_ref[...] * acc_t[...].T
        if final_norm:
            ms = jnp.mean(xo * xo, axis=-1, keepdims=True)
            xo = (xo * lax.rsqrt(ms + RMS_EPS)) * nf_ref[...]
        o_ref[...] = xo


def _peer_experts(ht, u_bf, vt_bf, route, x, gate, norm_final, final_norm, tm=512, ec=PEER_EXPERT_CHUNK):
    t, d = x.shape
    e = u_bf.shape[0]
    n_chunks = e // ec
    assert (n_chunks % 2 == 0) == PEER_LAST_IS_EVEN
    r2, ln, g1, g2 = route
    rspec = pl.BlockSpec((r2.shape[0], tm), lambda i, c: (0, i))
    prev = lambda c: jnp.maximum(c - 1, 0)
    kspec = pl.BlockSpec((PEER_HEADS, ec // PEER_NKEYS, tm), lambda i, c: (0, prev(c), i))
    row = lambda i, c: (0, 0)
    return pl.pallas_call(
        functools.partial(_peer_expert_kernel, final_norm=final_norm),
        grid=(t // tm, n_chunks + 1),
        in_specs=[pl.BlockSpec((d, tm), lambda i, c: (0, i)),
                  pl.BlockSpec((ec, d), lambda i, c: (jnp.minimum(c, n_chunks - 1), 0)),
                  pl.BlockSpec((1, d, ec), lambda i, c: (prev(c), 0, 0)),
                  rspec, kspec, kspec, rspec,
                  pl.BlockSpec((tm, d), lambda i, c: (i, 0)),
                  pl.BlockSpec((1, d), row), pl.BlockSpec((1, d), row)],
        out_specs=pl.BlockSpec((tm, d), lambda i, c: (i, 0)),
        out_shape=jax.ShapeDtypeStruct((t, d), F32),
        scratch_shapes=[pltpu.VMEM((d, tm), F32), pltpu.VMEM((ec, tm), BF16),
                        pltpu.VMEM((ec, tm), F32), pltpu.VMEM((ec, tm), F32)],
        compiler_params=_cparams(2),
    )(ht, u_bf, vt_bf, r2, ln, g1, g2, x, gate, norm_final)


def _peer_layer(x, g, sh, sc, gate, w_q, sub_keys, u_tab, v_tab, norm_final, final_norm):
    ht, st = _peer_query(x, g, sh, sc, w_q, sub_keys)
    route = _peer_route(st)
    e, d = v_tab.shape
    vt = v_tab.astype(BF16).reshape(e // PEER_EXPERT_CHUNK, PEER_EXPERT_CHUNK, d).transpose(0, 2, 1)
    return _peer_experts(ht, u_tab.astype(BF16), vt, route, x, gate, norm_final, final_norm)


def kernel(x, c, ada_w, ada_b, norm_mix, norm_ffn, norm_final, ssm_w_in, ssm_lambda_re, ssm_lambda_im, ssm_log_dt, ssm_b_re, ssm_b_im, ssm_c_re, ssm_c_im, ssm_d, ssm_w_out, nsa_w_in, nsa_cmp_k_pos, nsa_cmp_k_w1, nsa_cmp_k_w2, nsa_cmp_v_pos, nsa_cmp_v_w1, nsa_cmp_v_w2, nsa_w_out, peer_w_q, peer_sub_keys, peer_u, peer_v):
    bsz, l_all, d = x.shape
    assert bsz == 1
    depth = ada_w.shape[0]
    mod = _adaln(c, ada_w, ada_b)
    xt = x.reshape(l_all, d)
    nf = norm_final.reshape(1, d)
    for i in range(depth):
        sh1, sc1, g1, sh2, sc2, g2 = [mod[i, :, k * d:(k + 1) * d] for k in range(6)]
        j = i // 2
        gm = norm_mix[i].reshape(1, d)
        if i % 2 == 0:
            xt = _s5_layer(xt, gm, sh1, sc1, g1, ssm_w_in[j], ssm_lambda_re[j], ssm_lambda_im[j],
                           ssm_log_dt[j], ssm_b_re[j], ssm_b_im[j], ssm_c_re[j], ssm_c_im[j],
                           ssm_d[j], ssm_w_out[j])
        else:
            xt = _nsa_layer(xt, gm, sh1, sc1, g1, nsa_w_in[j], nsa_cmp_k_pos[j], nsa_cmp_k_w1[j],
                            nsa_cmp_k_w2[j], nsa_cmp_v_pos[j], nsa_cmp_v_w1[j], nsa_cmp_v_w2[j],
                            nsa_w_out[j])
        xt = _peer_layer(xt, norm_ffn[i].reshape(1, d), sh2, sc2, g2, peer_w_q[i], peer_sub_keys[i],
                         peer_u[i], peer_v[i], nf, final_norm=(i == depth - 1))
    return xt.reshape(bsz, l_all, d)
```
